```python
import jax, jax.numpy as jnp
from jax import lax
import numpy as np

D_MODEL = 1024
BATCH = 2
SEQ = 8192
DEPTH = 1

N_HEADS = 8
HEAD_DIM = 64
N_KV = 2
D_ATTN = N_HEADS * HEAD_DIM
D_KV = N_KV * HEAD_DIM
N_GM_GROUPS = 8
GM_GROUP_DIM = 64
D_GM = N_GM_GROUPS * GM_GROUP_DIM
D_MIX = D_ATTN + D_GM
SPLIT_SIZES = (D_ATTN, D_KV, D_KV, D_KV, D_KV, D_KV, D_KV, 3 * N_HEADS, D_GM, D_GM)
D_IN = D_ATTN + 6 * D_KV + 3 * N_HEADS + 2 * D_GM
CMP_LEN = 32
CMP_STRIDE = 16
CMP_HIDDEN = 128
SEL_BLOCK = 64
N_SEL = 16
WINDOW = 512
Q_BLOCK = 128
FORCE_BONUS = 1.0e4
GM_CHUNK = 128
N_EXPERTS = 32
TOP_K = 4
D_EXPERT = 1024
SWIGLU_LIMIT = 7.0
SWIGLU_ALPHA = 1.702
MOE_BLOCK = 128

EPS = 1e-6
NEG = -1.0e30

kernel_name = "hymba_nsa_gmlp_moe_layer"


def rmsnorm(x, g):
    xf = x.astype(jnp.float32)
    r = lax.rsqrt(jnp.mean(xf * xf, axis=-1, keepdims=True) + EPS)
    return (xf * r).astype(x.dtype) * g


def masked_softmax(s, mask):
    s = jnp.where(mask, s.astype(jnp.float32), NEG)
    p = jax.nn.softmax(s, axis=-1)
    return jnp.where(mask, p, 0.0)


def alibi_slopes(n_heads):
    return jnp.exp2(-8.0 * jnp.arange(1, n_heads + 1, dtype=jnp.float32) / n_heads)


def compress(kv, pos_emb, w1, b1, w2, b2):
    B, G, T, DH = kv.shape
    seg = kv.reshape(B, G, T // CMP_STRIDE, CMP_STRIDE, DH)
    r = CMP_LEN // CMP_STRIDE
    nc = T // CMP_STRIDE - r + 1
    blocks = jnp.concatenate([seg[:, :, i:i + nc] for i in range(r)], axis=3)
    flat = (blocks + pos_emb).reshape(B, G, nc, CMP_LEN * DH)
    hid = jax.nn.gelu(flat @ w1 + b1)
    return hid @ w2 + b2


def block_overlap(nc, ns):
    c0 = jnp.arange(nc)[:, None] * CMP_STRIDE
    n0 = jnp.arange(ns)[None, :] * SEL_BLOCK
    ov = jnp.clip(jnp.minimum(c0 + CMP_LEN, n0 + SEL_BLOCK) - jnp.maximum(c0, n0), 0, None)
    return ov.astype(jnp.float32) / CMP_LEN


def nsa_attention(q, kc, vc, ks, vs, kw, vw, gates):
    B, H, T, DH = q.shape
    G = ks.shape[1]
    R = H // G
    NC = kc.shape[2]
    NS = T // SEL_BLOCK
    n_sel = min(N_SEL, NS)
    scale = DH ** -0.5
    slopes = alibi_slopes(H).reshape(G, R, 1, 1)
    c_end = jnp.arange(NC) * CMP_STRIDE + CMP_LEN - 1
    overlap = block_overlap(NC, NS)
    n_start = jnp.arange(NS) * SEL_BLOCK
    ks_blocks = ks.reshape(B, G, NS, SEL_BLOCK, DH)
    vs_blocks = vs.reshape(B, G, NS, SEL_BLOCK, DH)
    pad = ((0, 0), (0, 0), (WINDOW, 0), (0, 0))
    kw_pad = jnp.pad(kw, pad)
    vw_pad = jnp.pad(vw, pad)
    b_ix = jnp.arange(B)[:, None, None, None]
    g_ix = jnp.arange(G)[None, :, None, None]
    in_block = jnp.arange(SEL_BLOCK)
    win_off = jnp.arange(WINDOW + Q_BLOCK) - WINDOW

    def one_block(q0):
        t = q0 + jnp.arange(Q_BLOCK)
        qg = lax.dynamic_slice_in_dim(q, q0, Q_BLOCK, axis=2).reshape(B, G, R, Q_BLOCK, DH)
        gt = lax.dynamic_slice_in_dim(gates, q0, Q_BLOCK, axis=2).reshape(B, G, R, Q_BLOCK, 3)
        dist_c = t[:, None] - c_end[None, :]
        s = jnp.einsum('bgrqd,bgcd->bgrqc', qg, kc) * scale - slopes * dist_c.astype(jnp.float32)
        p_cmp = masked_softmax(s, dist_c >= 0)
        o_cmp = jnp.einsum('bgrqc,bgcd->bgrqd', p_cmp.astype(vc.dtype), vc)
        imp = jnp.einsum('bgrqc,cn->bgqn', p_cmp, overlap)
        forced = (n_start[None, :] == (t[:, None] // SEL_BLOCK) * SEL_BLOCK) | (n_start[None, :] == 0)
        imp = jnp.where(forced, imp + FORCE_BONUS, imp)
        imp = jnp.where(n_start[None, :] <= t[:, None], imp, NEG)
        _, idx = lax.top_k(imp, n_sel)
        k_g = ks_blocks[b_ix, g_ix, idx].reshape(B, G, Q_BLOCK, n_sel * SEL_BLOCK, DH)
        v_g = vs_blocks[b_ix, g_ix, idx].reshape(B, G, Q_BLOCK, n_sel * SEL_BLOCK, DH)
        pos = (idx[..., None] * SEL_BLOCK + in_block).reshape(B, G, Q_BLOCK, n_sel * SEL_BLOCK)
        dist_s = (t[None, None, :, None] - pos)[:, :, None]
        s = jnp.einsum('bgrqd,bgqkd->bgrqk', qg, k_g) * scale - slopes * dist_s.astype(jnp.float32)
        p = masked_softmax(s, dist_s >= 0)
        o_sel = jnp.einsum('bgrqk,bgqkd->bgrqd', p.astype(v_g.dtype), v_g)
        k_w = lax.dynamic_slice_in_dim(kw_pad, q0, WINDOW + Q_BLOCK, axis=2)
        v_w = lax.dynamic_slice_in_dim(vw_pad, q0, WINDOW + Q_BLOCK, axis=2)
        pos_w = q0 + win_off
        dist_w = t[:, None] - pos_w[None, :]
        mask_w = (dist_w >= 0) & (dist_w < WINDOW) & (pos_w[None, :] >= 0)
        s = jnp.einsum('bgrqd,bgkd->bgrqk', qg, k_w) * scale - slopes * dist_w.astype(jnp.float32)
        p = masked_softmax(s, mask_w)
        o_win = jnp.einsum('bgrqk,bgkd->bgrqd', p.astype(v_w.dtype), v_w)
        o = gt[..., 0:1] * o_cmp + gt[..., 1:2] * o_sel + gt[..., 2:3] * o_win
        return o.reshape(B, H, Q_BLOCK, DH)

    out = lax.map(one_block, jnp.arange(T // Q_BLOCK) * Q_BLOCK)
    return out.transpose(1, 0, 3, 2, 4).reshape(B, T, H * DH)


def spatial_gating(u, v, v_gain, w_s, b_s):
    B, T, _ = u.shape
    u = jax.nn.gelu(u)
    v = rmsnorm(jax.nn.gelu(v), v_gain)
    vc = v.reshape(B, T // GM_CHUNK, GM_CHUNK, N_GM_GROUPS, GM_GROUP_DIM)
    ws = w_s * jnp.tril(jnp.ones((GM_CHUNK, GM_CHUNK), w_s.dtype))
    y = jnp.einsum('gts,bcsgd->bctgd', ws, vc) + b_s.T[None, None, :, :, None]
    return u * y.reshape(B, T, D_GM)


def moe(x, w_router, b_router, w_gate_up, b_gate_up, w_down, b_down):
    B, T, D = x.shape
    N = B * T
    xt = x.reshape(N, D)
    logits = (xt @ w_router + b_router).astype(jnp.float32)
    top_val, top_idx = lax.top_k(logits, TOP_K)
    gate = jax.nn.softmax(top_val, axis=-1)
    S = N * TOP_K
    e_flat = top_idx.reshape(S)
    tok_flat = jnp.repeat(jnp.arange(N, dtype=jnp.int32), TOP_K)
    g_flat = gate.reshape(S)
    order = jnp.argsort(e_flat)
    e_s, tok_s, g_s = e_flat[order], tok_flat[order], g_flat[order]
    counts = jnp.bincount(e_flat, length=N_EXPERTS)
    padded = ((counts + MOE_BLOCK - 1) // MOE_BLOCK) * MOE_BLOCK
    pad_end = jnp.cumsum(padded)
    pad_start = pad_end - padded
    cnt_start = jnp.cumsum(counts) - counts
    dest = pad_start[e_s] + jnp.arange(S) - cnt_start[e_s]
    P = ((S + MOE_BLOCK - 1) // MOE_BLOCK) * MOE_BLOCK + N_EXPERTS * MOE_BLOCK
    NB = P // MOE_BLOCK
    tok_buf = jnp.full((P,), N, dtype=jnp.int32).at[dest].set(tok_s)
    g_buf = jnp.zeros((P,), jnp.float32).at[dest].set(g_s)
    blk_expert = jnp.minimum(jnp.searchsorted(pad_end, jnp.arange(NB) * MOE_BLOCK, side='right'), N_EXPERTS - 1)
    x_pad = jnp.concatenate([xt, jnp.zeros((1, D), xt.dtype)], axis=0)

    def expert_block(args):
        e, tok, g = args
        xb = x_pad[tok]
        h = xb @ w_gate_up[e] + b_gate_up[e]
        h_glu = jnp.minimum(h[:, 0::2], SWIGLU_LIMIT)
        h_lin = jnp.clip(h[:, 1::2], -SWIGLU_LIMIT, SWIGLU_LIMIT)
        a = h_glu * jax.nn.sigmoid(SWIGLU_ALPHA * h_glu) * (h_lin + 1.0)
        return ((a @ w_down[e] + b_down[e]) * g[:, None]).astype(x.dtype)

    out = lax.map(expert_block, (blk_expert, tok_buf.reshape(NB, MOE_BLOCK), g_buf.reshape(NB, MOE_BLOCK)))
    y = jnp.zeros((N + 1, D), x.dtype).at[tok_buf].add(out.reshape(P, D))
    return y[:N].reshape(B, T, D)


def setup_inputs(seed: int = 0) -> dict:
    key = jax.random.key(seed)
    ks = jax.random.split(key, 23)
    f = jnp.float32

    def nrm(k, shape, scale):
        return jax.random.normal(k, shape, f) * scale

    def gain(k, shape):
        return 1.0 + 0.01 * jax.random.normal(k, shape, f)

    return {
        "x": nrm(ks[0], (BATCH, SEQ, D_MODEL), 1.0),
        "norm1_g": gain(ks[1], (D_MODEL,)),
        "w_in": nrm(ks[2], (D_MODEL, D_IN), D_MODEL ** -0.5),
        "q_norm_g": gain(ks[3], (HEAD_DIM,)),
        "k_norm_g": gain(ks[4], (3, HEAD_DIM)),
        "cmp_pos": nrm(ks[5], (2, CMP_LEN, HEAD_DIM), 0.02),
        "w_cmp1": nrm(ks[6], (2, CMP_LEN * HEAD_DIM, CMP_HIDDEN), (CMP_LEN * HEAD_DIM) ** -0.5),
        "b_cmp1": nrm(ks[7], (2, CMP_HIDDEN), 0.01),
        "w_cmp2": nrm(ks[8], (2, CMP_HIDDEN, HEAD_DIM), CMP_HIDDEN ** -0.5),
        "b_cmp2": nrm(ks[9], (2, HEAD_DIM), 0.01),
        "gm_v_norm_g": gain(ks[10], (D_GM,)),
        "gm_w_s": nrm(ks[11], (N_GM_GROUPS, GM_CHUNK, GM_CHUNK), GM_CHUNK ** -0.5),
        "gm_b_s": gain(ks[12], (N_GM_GROUPS, GM_CHUNK)),
        "out_norm_attn_g": gain(ks[13], (D_ATTN,)),
        "out_norm_gm_g": gain(ks[14], (D_GM,)),
        "w_out": nrm(ks[15], (D_MIX, D_MODEL), D_MIX ** -0.5),
        "norm2_g": gain(ks[16], (D_MODEL,)),
        "w_router": nrm(ks[17], (D_MODEL, N_EXPERTS), D_MODEL ** -0.5),
        "b_router": nrm(ks[18], (N_EXPERTS,), 0.01),
        "w_gate_up": nrm(ks[19], (N_EXPERTS, D_MODEL, 2 * D_EXPERT), D_MODEL ** -0.5),
        "b_gate_up": nrm(ks[20], (N_EXPERTS, 2 * D_EXPERT), 0.01),
        "w_down": nrm(ks[21], (N_EXPERTS, D_EXPERT, D_MODEL), D_EXPERT ** -0.5),
        "b_down": nrm(ks[22], (N_EXPERTS, D_MODEL), 0.01),
    }


def reference(x, norm1_g, w_in, q_norm_g, k_norm_g, cmp_pos, w_cmp1, b_cmp1, w_cmp2, b_cmp2,
              gm_v_norm_g, gm_w_s, gm_b_s, out_norm_attn_g, out_norm_gm_g, w_out, norm2_g,
              w_router, b_router, w_gate_up, b_gate_up, w_down, b_down):
    B, T, _ = x.shape

    def heads(a, n):
        return a.reshape(B, T, n, HEAD_DIM).transpose(0, 2, 1, 3)

    for _layer in range(DEPTH):
        h = rmsnorm(x, norm1_g)
        z = h @ w_in
        split_at = np.cumsum(SPLIT_SIZES)[:-1].tolist()
        zq, zkc, zvc, zks, zvs, zkw, zvw, zg, zu, zv = jnp.split(z, split_at, axis=-1)
        q = rmsnorm(heads(zq, N_HEADS), q_norm_g)
        kc = rmsnorm(compress(heads(zkc, N_KV), cmp_pos[0], w_cmp1[0], b_cmp1[0], w_cmp2[0], b_cmp2[0]), k_norm_g[0])
        vc = compress(heads(zvc, N_KV), cmp_pos[1], w_cmp1[1], b_cmp1[1], w_cmp2[1], b_cmp2[1])
        k_sel = rmsnorm(heads(zks, N_KV), k_norm_g[1])
        k_win = rmsnorm(heads(zkw, N_KV), k_norm_g[2])
        gates = jax.nn.sigmoid(zg).reshape(B, T, N_HEADS, 3).transpose(0, 2, 1, 3)
        o_attn = nsa_attention(q, kc, vc, k_sel, heads(zvs, N_KV), k_win, heads(zvw, N_KV), gates)
        o_gm = spatial_gating(zu, zv, gm_v_norm_g, gm_w_s, gm_b_s)
        mixed = jnp.concatenate([rmsnorm(o_attn, out_norm_attn_g), rmsnorm(o_gm, out_norm_gm_g)], axis=-1)
        x = x + mixed @ w_out
        x = x + moe(rmsnorm(x, norm2_g), w_router, b_router, w_gate_up, b_gate_up, w_down, b_down)
    return x
```

```python
import functools

import jax
import jax.numpy as jnp
from jax import lax
from jax.experimental import pallas as pl
from jax.experimental.pallas import tpu as pltpu

F32 = jnp.float32
BF16 = jnp.bfloat16
HIGHEST = lax.Precision.HIGHEST

D_MODEL = 1024
N_HEADS = 8
HEAD_DIM = 64
N_KV = 2
N_REP = N_HEADS // N_KV
D_ATTN = N_HEADS * HEAD_DIM
D_KV = N_KV * HEAD_DIM
N_GM_GROUPS = 8
GM_GROUP_DIM = 64
D_GM = N_GM_GROUPS * GM_GROUP_DIM
N_GATE = 3 * N_HEADS
CMP_LEN = 32
CMP_STRIDE = 16
CMP_HIDDEN = 128
SEL_BLOCK = 64
N_SEL = 16
WINDOW = 512
Q_BLOCK = 128
FORCE_BONUS = 1.0e4
GM_CHUNK = 128
N_EXPERTS = 32
TOP_K = 4
D_EXPERT = 1024
SWIGLU_LIMIT = 7.0
SWIGLU_ALPHA = 1.702
EPS = 1e-6
NEG = -1.0e30

LANE = 128
VMEM_LIMIT = 48 * 1024 * 1024

_C_Q = 0
_C_KC = _C_Q + D_ATTN
_C_VC = _C_KC + D_KV
_C_KS = _C_VC + D_KV
_C_VS = _C_KS + D_KV
_C_KW = _C_VS + D_KV
_C_VW = _C_KW + D_KV
_C_U = _C_VW + D_KV
_C_V = _C_U + D_GM
_C_G = _C_V + D_GM
D_IN_PAD = _C_G + LANE

TM_IN = 256
TM_MIX = 256
KC_SEL = 256
KC_WIN = 128
BM_MOE = 256
TM_CMB = 128


def _rms(x, eps=EPS):
    return lax.rsqrt(jnp.mean(x * x, axis=-1, keepdims=True) + eps)


def _inproj_kernel(x_ref, g1_ref, w_ref, qg_ref, kg_ref, vg_ref,
                   q_ref, kc_ref, vc_ref, ks_ref, vs_ref, kw_ref, vw_ref, gate_ref, u_ref, v_ref):
    x = x_ref[...]
    h = (x * _rms(x)) * g1_ref[...]
    z = jnp.dot(h.astype(BF16), w_ref[...], preferred_element_type=F32)

    def head_norm(col0, n, gain, scale):
        outs = []
        for i in range(n):
            sl = z[:, col0 + i * HEAD_DIM: col0 + (i + 1) * HEAD_DIM]
            outs.append((sl * _rms(sl)) * gain * scale)
        return jnp.concatenate(outs, axis=-1)

    q_ref[...] = head_norm(_C_Q, N_HEADS, qg_ref[...], HEAD_DIM ** -0.5)
    kc_ref[...] = z[:, _C_KC:_C_KC + D_KV]
    vc_ref[...] = z[:, _C_VC:_C_VC + D_KV]
    ks_ref[...] = head_norm(_C_KS, N_KV, kg_ref[1:2, :], 1.0)
    vs_ref[...] = z[:, _C_VS:_C_VS + D_KV]
    kw_ref[...] = head_norm(_C_KW, N_KV, kg_ref[2:3, :], 1.0)
    vw_ref[...] = z[:, _C_VW:_C_VW + D_KV]
    gate_ref[...] = jax.nn.sigmoid(z[:, _C_G:_C_G + LANE])
    u_ref[...] = jax.nn.gelu(z[:, _C_U:_C_U + D_GM])
    gv = jax.nn.gelu(z[:, _C_V:_C_V + D_GM])
    v_ref[...] = (gv * _rms(gv)) * vg_ref[...]


def _inproj(x2, norm1_g, w_r, q_norm_g, k_norm_g, gm_v_norm_g):
    n = x2.shape[0]
    row = lambda c: pl.BlockSpec((TM_IN, c), lambda i: (i, 0))
    full = lambda a: pl.BlockSpec(a.shape, lambda i: (0,) * a.ndim)
    g1 = norm1_g.reshape(1, D_MODEL)
    qg = q_norm_g.reshape(1, HEAD_DIM)
    vg = gm_v_norm_g.reshape(1, D_GM)
    widths = (D_ATTN, D_KV, D_KV, D_KV, D_KV, D_KV, D_KV, LANE, D_GM, D_GM)
    return pl.pallas_call(
        _inproj_kernel,
        grid=(n // TM_IN,),
        in_specs=[row(D_MODEL), full(g1), full(w_r), full(qg), full(k_norm_g), full(vg)],
        out_specs=[row(c) for c in widths],
        out_shape=[jax.ShapeDtypeStruct((n, c), F32) for c in widths],
        compiler_params=pltpu.CompilerParams(dimension_semantics=("arbitrary",), vmem_limit_bytes=VMEM_LIMIT),
        name="inproj",
    )(x2, g1, w_r, qg, k_norm_g, vg)


def _compress_kernel(a_ref, pos_ref, w1_ref, w1a_ref, w1b_ref, b1_ref, w2_ref, b2_ref, kg_ref, o_ref, *, norm):
    a = a_ref[0]
    nseg = a.shape[0]
    c = jnp.dot(pos_ref[...], w1_ref[...], precision=HIGHEST, preferred_element_type=F32)[0:1] + b1_ref[...]
    row = lax.broadcasted_iota(jnp.int32, (nseg, 1), 0)
    for g in range(N_KV):
        pa = jnp.dot(a, w1a_ref[g], precision=HIGHEST, preferred_element_type=F32)
        pb = jnp.dot(a, w1b_ref[g], precision=HIGHEST, preferred_element_type=F32)
        hid = jax.nn.gelu(pa + pltpu.roll(pb, nseg - 1, 0) + c)
        out = jnp.dot(hid, w2_ref[...], precision=HIGHEST, preferred_element_type=F32) + b2_ref[...]
        if norm:
            out = (out * _rms(out)) * kg_ref[...]
        o_ref[0, g] = jnp.where(row < nseg - 1, out, 0.0)


def _compress(raw, pos, w1, b1, w2, b2, gain, batch, seq, norm):
    nseg = seq // CMP_STRIDE
    half = CMP_STRIDE * HEAD_DIM
    a = raw.reshape(batch, nseg, CMP_STRIDE * D_KV)
    pos8 = jnp.broadcast_to(pos.reshape(1, CMP_LEN * HEAD_DIM), (8, CMP_LEN * HEAD_DIM))

    def expand(wh):
        wh = wh.reshape(CMP_STRIDE, HEAD_DIM, CMP_HIDDEN)
        z = jnp.zeros((N_KV, CMP_STRIDE, N_KV, HEAD_DIM, CMP_HIDDEN), F32)
        for g in range(N_KV):
            z = z.at[g, :, g].set(wh)
        return z.reshape(N_KV, CMP_STRIDE * D_KV, CMP_HIDDEN)

    w1a, w1b = expand(w1[:half]), expand(w1[half:])
    b1r, b2r, gr = b1.reshape(1, CMP_HIDDEN), b2.reshape(1, HEAD_DIM), gain.reshape(1, HEAD_DIM)
    full = lambda t: pl.BlockSpec(t.shape, lambda i: (0,) * t.ndim)
    return pl.pallas_call(
        functools.partial(_compress_kernel, norm=norm),
        grid=(batch,),
        in_specs=[pl.BlockSpec((1, nseg, CMP_STRIDE * D_KV), lambda i: (i, 0, 0)),
                  full(pos8), full(w1), full(w1a), full(w1b), full(b1r), full(w2), full(b2r), full(gr)],
        out_specs=pl.BlockSpec((1, N_KV, nseg, HEAD_DIM), lambda i: (i, 0, 0, 0)),
        out_shape=jax.ShapeDtypeStruct((batch, N_KV, nseg, HEAD_DIM), F32),
        compiler_params=pltpu.CompilerParams(dimension_semantics=("arbitrary",), vmem_limit_bytes=VMEM_LIMIT),
        name="compress_k" if norm else "compress_v",
    )(a, pos8, w1, w1a, w1b, b1r, w2, b2r, gr)


def _attn_kernel(qt_ref, kc_ref, vct_ref, ks_ref, vst_ref, kw_ref, vwt_ref, g_ref, o_ref, sel_ref):
    bg = pl.program_id(0)
    qb = pl.program_id(1)
    nq = N_REP * Q_BLOCK
    q0 = qb * Q_BLOCK
    qt = qt_ref[0, 0]
    lane = lax.broadcasted_iota(jnp.int32, (1, nq), 1)
    head = (bg % N_KV) * N_REP + lane // Q_BLOCK
    slope = jnp.exp2(-(head + 1).astype(F32))
    t_row = (q0 + lane % Q_BLOCK).astype(F32)

    def softmax_step(s, mask, m, l):
        s = jnp.where(mask, s, NEG)
        m_new = jnp.maximum(m, jnp.max(s, axis=0, keepdims=True))
        alpha = jnp.exp(m - m_new)
        p = jnp.where(mask, jnp.exp(s - m_new), 0.0)
        return p, m_new, alpha, alpha * l + jnp.sum(p, axis=0, keepdims=True)

    def inv(l):
        return jnp.where(l > 0.0, 1.0 / l, 0.0)

    m0 = jnp.full((1, nq), NEG, F32)
    l0 = jnp.zeros((1, nq), F32)
    a0 = jnp.zeros((HEAD_DIM, nq), F32)

    ncmp = kc_ref.shape[1]
    s = jnp.dot(kc_ref[0], qt, preferred_element_type=F32)
    c_end = (lax.broadcasted_iota(jnp.int32, (ncmp, 1), 0) * CMP_STRIDE + (CMP_LEN - 1)).astype(F32)
    dist = t_row - c_end
    p, _, _, l = softmax_step(s - slope * dist, dist >= 0.0, m0, l0)
    p = p * inv(l)
    o_cmp = jnp.dot(vct_ref[0], p.astype(BF16), preferred_element_type=F32)

    psum = p[:, 0:Q_BLOCK]
    for r in range(1, N_REP):
        psum = psum + p[:, r * Q_BLOCK:(r + 1) * Q_BLOCK]
    nsel = sel_ref.shape[0]
    n_i = lax.broadcasted_iota(jnp.int32, (nsel, ncmp), 0) * SEL_BLOCK
    c_i = lax.broadcasted_iota(jnp.int32, (nsel, ncmp), 1) * CMP_STRIDE
    ov = jnp.clip(jnp.minimum(c_i + CMP_LEN, n_i + SEL_BLOCK) - jnp.maximum(c_i, n_i), 0, None)
    ovt = ov.astype(F32) * (1.0 / CMP_LEN)
    imp = jnp.dot(ovt, psum, precision=HIGHEST, preferred_element_type=F32)
    n_col = lax.broadcasted_iota(jnp.int32, (nsel, 1), 0).astype(F32)
    n_start = n_col * SEL_BLOCK
    tq = t_row[:, 0:Q_BLOCK]
    cur = jnp.floor(tq * (1.0 / SEL_BLOCK)) * SEL_BLOCK
    forced = (n_start == cur) | (n_start == 0.0)
    valid = n_start <= tq
    imp = jnp.where(forced, imp + FORCE_BONUS, imp)
    imp = jnp.where(valid, imp, NEG)
    sel = jnp.zeros((nsel, Q_BLOCK), F32)
    for _ in range(min(N_SEL, nsel)):
        mx = jnp.max(imp, axis=0, keepdims=True)
        first = jnp.min(jnp.where(imp == mx, n_col, float(nsel)), axis=0, keepdims=True)
        hit = n_col == first
        sel = jnp.where(hit, 1.0, sel)
        imp = jnp.where(hit, -jnp.inf, imp)
    sel_ref[...] = jnp.where(valid, sel, 0.0)

    bpc = KC_SEL // SEL_BLOCK
    pos_s = lax.broadcasted_iota(jnp.int32, (KC_SEL, 1), 0).astype(F32)

    def sel_body(j, carry):
        m, l, acc = carry
        k0 = pl.multiple_of(j * KC_SEL, KC_SEL)
        s = jnp.dot(ks_ref[0, pl.ds(k0, KC_SEL), :], qt, preferred_element_type=F32)
        dist = t_row - (pos_s + k0.astype(F32))
        rows = sel_ref[pl.ds(pl.multiple_of(j * bpc, bpc), bpc), :]
        blk = jnp.concatenate(
            [jnp.broadcast_to(rows[i:i + 1, :], (SEL_BLOCK, Q_BLOCK)) for i in range(bpc)], axis=0)
        mask = (jnp.concatenate([blk] * N_REP, axis=1) > 0.0) & (dist >= 0.0)
        p, m, alpha, l = softmax_step(s - slope * dist, mask, m, l)
        pv = jnp.dot(vst_ref[0, :, pl.ds(k0, KC_SEL)], p.astype(BF16), preferred_element_type=F32)
        return m, l, alpha * acc + pv

    n_sel_steps = (q0 + Q_BLOCK + KC_SEL - 1) // KC_SEL
    _, l_sel, o_sel = lax.fori_loop(0, n_sel_steps, sel_body, (m0, l0, a0))

    pos_w = lax.broadcasted_iota(jnp.int32, (KC_WIN, 1), 0).astype(F32)
    n_win = (WINDOW + Q_BLOCK) // KC_WIN

    def win_body(j, carry):
        m, l, acc = carry
        k0 = pl.multiple_of(q0 - WINDOW + j * KC_WIN, KC_WIN)
        s = jnp.dot(kw_ref[0, pl.ds(k0, KC_WIN), :], qt, preferred_element_type=F32)
        dist = t_row - (pos_w + k0.astype(F32))
        mask = (dist >= 0.0) & (dist < float(WINDOW))
        p, m, alpha, l = softmax_step(s - slope * dist, mask, m, l)
        pv = jnp.dot(vwt_ref[0, :, pl.ds(k0, KC_WIN)], p.astype(BF16), preferred_element_type=F32)
        return m, l, alpha * acc + pv

    first_win = jnp.maximum(0, (WINDOW - q0) // KC_WIN)
    _, l_win, o_win = lax.fori_loop(first_win, n_win, win_body, (m0, l0, a0))

    gt = g_ref[0, 0]
    o_ref[0, 0] = (gt[0:1] * o_cmp + gt[1:2] * (o_sel * inv(l_sel)) + gt[2:3] * (o_win * inv(l_win)))


def _attention(qt, kc, vct, ks, vst, kw, vwt, gt):
    bgn, nqb = qt.shape[0], qt.shape[1]
    seq = ks.shape[1]
    ncmp = kc.shape[1]
    nq = N_REP * Q_BLOCK
    per_bg = lambda a: pl.BlockSpec((1,) + a.shape[1:], lambda b, i: (b,) + (0,) * (a.ndim - 1))
    return pl.pallas_call(
        _attn_kernel,
        grid=(bgn, nqb),
        in_specs=[pl.BlockSpec((1, 1, HEAD_DIM, nq), lambda b, i: (b, i, 0, 0)),
                  per_bg(kc), per_bg(vct), per_bg(ks), per_bg(vst), per_bg(kw), per_bg(vwt),
                  pl.BlockSpec((1, 1, 3, nq), lambda b, i: (b, i, 0, 0))],
        out_specs=pl.BlockSpec((1, 1, HEAD_DIM, nq), lambda b, i: (b, i, 0, 0)),
        out_shape=jax.ShapeDtypeStruct((bgn, nqb, HEAD_DIM, nq), F32),
        scratch_shapes=[pltpu.VMEM((seq // SEL_BLOCK, Q_BLOCK), F32)],
        compiler_params=pltpu.CompilerParams(dimension_semantics=("arbitrary", "arbitrary"),
                                             vmem_limit_bytes=VMEM_LIMIT),
        name="nsa_attention",
    )(qt, kc, vct, ks, vst, kw, vwt, gt)


def _mix_kernel(x_ref, oa_ref, u_ref, v_ref, ws_ref, bs_ref, ga_ref, gg_ref, wo_ref, g2_ref, wr_ref, br_ref,
                x1_ref, xn_ref, idx_ref, gate_ref):
    tm = x_ref.shape[0]
    rr = lax.broadcasted_iota(jnp.int32, (GM_CHUNK, GM_CHUNK), 0)
    cc = lax.broadcasted_iota(jnp.int32, (GM_CHUNK, GM_CHUNK), 1)
    grp = lax.broadcasted_iota(jnp.int32, (1, D_GM), 1) // GM_GROUP_DIM
    ws = [jnp.where(rr >= cc, ws_ref[g], 0.0).astype(BF16) for g in range(N_GM_GROUPS)]
    ys = []
    for c in range(tm // GM_CHUNK):
        vch = v_ref[c * GM_CHUNK:(c + 1) * GM_CHUNK, :].astype(BF16)
        y = bs_ref[...]
        for g in range(N_GM_GROUPS):
            y = y + jnp.where(grp == g, jnp.dot(ws[g], vch, preferred_element_type=F32), 0.0)
        ys.append(y)
    o_gm = u_ref[...] * jnp.concatenate(ys, axis=0)
    o_at = oa_ref[...]
    mixed = jnp.concatenate([(o_at * _rms(o_at)) * ga_ref[...], (o_gm * _rms(o_gm)) * gg_ref[...]], axis=-1)
    x1 = x_ref[...] + jnp.dot(mixed.astype(BF16), wo_ref[...], preferred_element_type=F32)
    x1_ref[...] = x1
    xn = (x1 * _rms(x1)) * g2_ref[...]
    xn_ref[...] = xn
    logits = jnp.dot(xn, wr_ref[...], precision=HIGHEST, preferred_element_type=F32) + br_ref[...]
    lane = lax.broadcasted_iota(jnp.int32, (1, LANE), 1).astype(F32)
    idx_out = jnp.zeros((tm, LANE), F32)
    val_out = jnp.zeros((tm, LANE), F32)
    vals = []
    for k in range(TOP_K):
        mx = jnp.max(logits, axis=-1, keepdims=True)
        first = jnp.min(jnp.where(logits == mx, lane, float(LANE)), axis=-1, keepdims=True)
        logits = jnp.where(lane == first, -jnp.inf, logits)
        idx_out = jnp.where(lane == float(k), first, idx_out)
        vals.append(mx)
    es = [jnp.exp(v - vals[0]) for v in vals]
    den = es[0] + es[1] + es[2] + es[3]
    for k in range(TOP_K):
        val_out = jnp.where(lane == float(k), es[k] / den, val_out)
    idx_ref[...] = idx_out.astype(jnp.int32)
    gate_ref[...] = val_out


def _mix(x2, o_attn, u_act, v_act, gm_w_s, bias_full, ga, gg, w_out_b, g2, wr_pad, br_pad):
    n = x2.shape[0]
    row = lambda c: pl.BlockSpec((TM_MIX, c), lambda i: (i, 0))
    full = lambda a: pl.BlockSpec(a.shape, lambda i: (0,) * a.ndim)
    return pl.pallas_call(
        _mix_kernel,
        grid=(n // TM_MIX,),
        in_specs=[row(D_MODEL), row(D_ATTN), row(D_GM), row(D_GM), full(gm_w_s), full(bias_full), full(ga), full(gg),
                  full(w_out_b), full(g2), full(wr_pad), full(br_pad)],
        out_specs=[row(D_MODEL), row(D_MODEL), row(LANE), row(LANE)],
        out_shape=[jax.ShapeDtypeStruct((n, D_MODEL), F32), jax.ShapeDtypeStruct((n, D_MODEL), F32),
                   jax.ShapeDtypeStruct((n, LANE), jnp.int32), jax.ShapeDtypeStruct((n, LANE), F32)],
        compiler_params=pltpu.CompilerParams(dimension_semantics=("arbitrary",), vmem_limit_bytes=VMEM_LIMIT),
        name="mix_outproj_router",
    )(x2, o_attn, u_act, v_act, gm_w_s, bias_full, ga, gg, w_out_b, g2, wr_pad, br_pad)


def _gather_rows(idx_ref, n_rows, src_hbm, dst_ref, sem):
    def issue(r, carry):
        pltpu.make_async_copy(src_hbm.at[pl.ds(idx_ref[0, 0, r], 1), :], dst_ref.at[pl.ds(r, 1), :], sem).start()
        return carry
    lax.fori_loop(0, n_rows, issue, 0)
    pltpu.make_async_copy(src_hbm.at[pl.ds(0, n_rows), :], dst_ref, sem).wait()


def _moe_kernel(be_ref, bv_ref, tok_ref, x_hbm, g_ref, wg_ref, wl_ref, bg_ref, bl_ref, wd_ref, bd_ref,
                o_ref, xbuf, sem):
    i = pl.program_id(0)

    @pl.when(bv_ref[i] == 1)
    def _():
        _gather_rows(tok_ref, BM_MOE, x_hbm, xbuf, sem)
        xb = xbuf[...].astype(BF16)
        hg = jnp.dot(xb, wg_ref[0], preferred_element_type=F32) + bg_ref[0]
        hl = jnp.dot(xb, wl_ref[0], preferred_element_type=F32) + bl_ref[0]
        hg = jnp.minimum(hg, SWIGLU_LIMIT)
        hl = jnp.clip(hl, -SWIGLU_LIMIT, SWIGLU_LIMIT)
        a = hg * jax.nn.sigmoid(SWIGLU_ALPHA * hg) * (hl + 1.0)
        out = jnp.dot(a.astype(BF16), wd_ref[0], preferred_element_type=F32) + bd_ref[0]
        o_ref[...] = out * g_ref[...]

    @pl.when(bv_ref[i] == 0)
    def _():
        o_ref[...] = jnp.zeros(o_ref.shape, F32)


def _moe(blk_expert, blk_valid, tok_blocks, xn, g_buf, wg, wl, bg, bl, wd, bd):
    nb = blk_expert.shape[0]
    per_e = lambda a: pl.BlockSpec((1,) + a.shape[1:], lambda i, be, bv: (be[i],) + (0,) * (a.ndim - 1))
    grid_spec = pltpu.PrefetchScalarGridSpec(
        num_scalar_prefetch=2,
        grid=(nb,),
        in_specs=[pl.BlockSpec((1, 1, BM_MOE), lambda i, be, bv: (i, 0, 0), memory_space=pltpu.SMEM),
                  pl.BlockSpec(memory_space=pl.ANY),
                  pl.BlockSpec((BM_MOE, 1), lambda i, be, bv: (i, 0)),
                  per_e(wg), per_e(wl), per_e(bg), per_e(bl), per_e(wd), per_e(bd)],
        out_specs=pl.BlockSpec((BM_MOE, D_MODEL), lambda i, be, bv: (i, 0)),
        scratch_shapes=[pltpu.VMEM((BM_MOE, D_MODEL), F32), pltpu.SemaphoreType.DMA(())],
    )
    return pl.pallas_call(
        _moe_kernel,
        grid_spec=grid_spec,
        out_shape=jax.ShapeDtypeStruct((nb * BM_MOE, D_MODEL), F32),
        compiler_params=pltpu.CompilerParams(dimension_semantics=("arbitrary",), vmem_limit_bytes=VMEM_LIMIT),
        name="moe_experts",
    )(blk_expert, blk_valid, tok_blocks, xn, g_buf, wg, wl, bg, bl, wd, bd)


def _combine_kernel(dest_ref, x1_ref, y_hbm, o_ref, buf, sem):
    _gather_rows(dest_ref, TOP_K * TM_CMB, y_hbm, buf, sem)
    acc = x1_ref[...]
    for k in range(TOP_K):
        acc = acc + buf[k * TM_CMB:(k + 1) * TM_CMB, :]
    o_ref[...] = acc


def _combine(dest_blocks, x1, y_rows):
    n = x1.shape[0]
    return pl.pallas_call(
        _combine_kernel,
        grid=(n // TM_CMB,),
        in_specs=[pl.BlockSpec((1, 1, TOP_K * TM_CMB), lambda i: (i, 0, 0), memory_space=pltpu.SMEM),
                  pl.BlockSpec((TM_CMB, D_MODEL), lambda i: (i, 0)),
                  pl.BlockSpec(memory_space=pl.ANY)],
        out_specs=pl.BlockSpec((TM_CMB, D_MODEL), lambda i: (i, 0)),
        out_shape=jax.ShapeDtypeStruct((n, D_MODEL), F32),
        scratch_shapes=[pltpu.VMEM((TOP_K * TM_CMB, D_MODEL), F32), pltpu.SemaphoreType.DMA(())],
        compiler_params=pltpu.CompilerParams(dimension_semantics=("arbitrary",), vmem_limit_bytes=VMEM_LIMIT),
        name="moe_combine",
    )(dest_blocks, x1, y_rows)


def kernel(x, norm1_g, w_in, q_norm_g, k_norm_g, cmp_pos, w_cmp1, b_cmp1, w_cmp2, b_cmp2, gm_v_norm_g, gm_w_s,
           gm_b_s, out_norm_attn_g, out_norm_gm_g, w_out, norm2_g, w_router, b_router, w_gate_up, b_gate_up,
           w_down, b_down):
    batch, seq, _ = x.shape
    n = batch * seq
    nqb = seq // Q_BLOCK
    bgn = batch * N_KV
    x2 = x.reshape(n, D_MODEL)

    c_gate = D_ATTN + 6 * D_KV
    w_r = jnp.concatenate([w_in[:, :c_gate], w_in[:, c_gate + N_GATE:], w_in[:, c_gate:c_gate + N_GATE],
                           jnp.zeros((D_MODEL, LANE - N_GATE), F32)], axis=1).astype(BF16)
    q, kc_raw, vc_raw, ks, vs, kw, vw, gates, u_act, v_act = _inproj(x2, norm1_g, w_r, q_norm_g, k_norm_g,
                                                                     gm_v_norm_g)

    kc = _compress(kc_raw, cmp_pos[0], w_cmp1[0], b_cmp1[0], w_cmp2[0], b_cmp2[0], k_norm_g[0], batch, seq, True)
    vc = _compress(vc_raw, cmp_pos[1], w_cmp1[1], b_cmp1[1], w_cmp2[1], b_cmp2[1], k_norm_g[0], batch, seq, False)

    def tok_major(a):
        return a.reshape(batch, seq, N_KV, HEAD_DIM).transpose(0, 2, 1, 3).reshape(bgn, seq, HEAD_DIM).astype(BF16)

    def feat_major(a):
        return a.reshape(batch, seq, N_KV, HEAD_DIM).transpose(0, 2, 3, 1).reshape(bgn, HEAD_DIM, seq).astype(BF16)

    qt = (q.reshape(batch, nqb, Q_BLOCK, N_KV, N_REP, HEAD_DIM).transpose(0, 3, 1, 5, 4, 2)
          .reshape(bgn, nqb, HEAD_DIM, N_REP * Q_BLOCK).astype(BF16))
    gt = (gates[:, :N_GATE].reshape(batch, nqb, Q_BLOCK, N_KV, N_REP, 3).transpose(0, 3, 1, 5, 4, 2)
          .reshape(bgn, nqb, 3, N_REP * Q_BLOCK))
    ncmp = seq // CMP_STRIDE
    kc_b = kc.reshape(bgn, ncmp, HEAD_DIM).astype(BF16)
    vct = vc.reshape(bgn, ncmp, HEAD_DIM).transpose(0, 2, 1).astype(BF16)
    ot = _attention(qt, kc_b, vct, tok_major(ks), feat_major(vs), tok_major(kw), feat_major(vw), gt)
    o_attn = (ot.reshape(batch, N_KV, nqb, HEAD_DIM, N_REP, Q_BLOCK).transpose(0, 2, 5, 1, 4, 3)
              .reshape(n, D_ATTN))

    bias_full = jnp.repeat(gm_b_s.T, GM_GROUP_DIM, axis=1)
    wr_pad = jnp.concatenate([w_router, jnp.zeros((D_MODEL, LANE - N_EXPERTS), F32)], axis=1)
    br_pad = jnp.concatenate([b_router, jnp.full((LANE - N_EXPERTS,), NEG, F32)]).reshape(1, LANE)
    x1, xn, idx_pad, gate_pad = _mix(x2, o_attn, u_act, v_act, gm_w_s, bias_full,
                                     out_norm_attn_g.reshape(1, D_ATTN), out_norm_gm_g.reshape(1, D_GM),
                                     w_out.astype(BF16), norm2_g.reshape(1, D_MODEL), wr_pad, br_pad)

    s_tot = n * TOP_K
    nb = s_tot // BM_MOE + N_EXPERTS
    e_flat = idx_pad[:, :TOP_K].reshape(s_tot)
    g_flat = gate_pad[:, :TOP_K].reshape(s_tot)
    onehot = (e_flat[:, None] == jnp.arange(N_EXPERTS, dtype=jnp.int32)[None, :]).astype(jnp.int32)
    csum = jnp.cumsum(onehot, axis=0)
    rank = jnp.sum(csum * onehot, axis=1) - 1
    counts = csum[-1]
    padded = ((counts + BM_MOE - 1) // BM_MOE) * BM_MOE
    pad_end = jnp.cumsum(padded)
    pad_start = pad_end - padded
    dest = pad_start[e_flat] + rank
    tok_flat = jnp.arange(s_tot, dtype=jnp.int32) // TOP_K
    tok_buf = jnp.zeros((nb * BM_MOE,), jnp.int32).at[dest].set(tok_flat)
    g_buf = jnp.zeros((nb * BM_MOE,), F32).at[dest].set(g_flat)
    blk_start = jnp.arange(nb, dtype=jnp.int32) * BM_MOE
    blk_expert = jnp.minimum(jnp.searchsorted(pad_end, blk_start, side='right'), N_EXPERTS - 1).astype(jnp.int32)
    blk_valid = (blk_start < pad_end[-1]).astype(jnp.int32)

    wg = w_gate_up[:, :, 0::2].astype(BF16)
    wl = w_gate_up[:, :, 1::2].astype(BF16)
    bg = b_gate_up[:, 0::2].reshape(N_EXPERTS, 1, D_EXPERT)
    bl = b_gate_up[:, 1::2].reshape(N_EXPERTS, 1, D_EXPERT)
    y_rows = _moe(blk_expert, blk_valid, tok_buf.reshape(nb, 1, BM_MOE), xn, g_buf.reshape(nb * BM_MOE, 1),
                  wg, wl, bg, bl, w_down.astype(BF16), b_down.reshape(N_EXPERTS, 1, D_MODEL))

    dest_blocks = (dest.reshape(n // TM_CMB, TM_CMB, TOP_K).transpose(0, 2, 1)
                   .reshape(n // TM_CMB, 1, TOP_K * TM_CMB).astype(jnp.int32))
    out = _combine(dest_blocks, x1, y_rows)
    return out.reshape(batch, seq, D_MODEL)
```

```python
import functools

import jax
import jax.numpy as jnp
import numpy as np
from jax import lax
from jax.experimental import pallas as pl
from jax.experimental.pallas import tpu as pltpu

F32 = jnp.float32
BF16 = jnp.bfloat16
HIGHEST = lax.Precision.HIGHEST
_NT = (((1,), (1,)), ((), ()))

D_MODEL = 1024
N_HEADS = 8
HEAD_DIM = 64
N_KV = 2
N_REP = N_HEADS // N_KV
D_ATTN = N_HEADS * HEAD_DIM
D_KV = N_KV * HEAD_DIM
N_GM_GROUPS = 8
GM_GROUP_DIM = 64
D_GM = N_GM_GROUPS * GM_GROUP_DIM
N_GATE = 3 * N_HEADS
CMP_LEN = 32
CMP_STRIDE = 16
CMP_HIDDEN = 128
SEL_BLOCK = 64
N_SEL = 16
WINDOW = 512
Q_BLOCK = 128
FORCE_BONUS = 1.0e4
GM_CHUNK = 128
N_EXPERTS = 32
TOP_K = 4
D_EXPERT = 1024
SWIGLU_LIMIT = 7.0
SWIGLU_ALPHA = 1.702
EPS = 1e-6
NEG = -1.0e30
LOG2E = 1.4426950408889634

LANE = 128
VMEM_LIMIT = 48 * 1024 * 1024

_C_Q = 0
_C_KC = _C_Q + D_ATTN
_C_VC = _C_KC + D_KV
_C_KS = _C_VC + D_KV
_C_VS = _C_KS + D_KV
_C_KW = _C_VS + D_KV
_C_VW = _C_KW + D_KV
_C_U = _C_VW + D_KV
_C_V = _C_U + D_GM
_C_G = _C_V + D_GM
D_IN_PAD = _C_G + LANE

TM_IN = 256
TM_MIX = 256
KC_SEL = 512
BM_MOE = 256
TM_CMB = 128


def _rms(x, eps=EPS):
    return lax.rsqrt(jnp.mean(x * x, axis=-1, keepdims=True) + eps)


def _inproj_kernel(x_ref, g1_ref, w_ref, qg_ref, kg_ref, vg_ref,
                   q_ref, kc_ref, vc_ref, ks_ref, vs_ref, kw_ref, vw_ref, gate_ref, u_ref, v_ref):
    x = x_ref[...]
    h = (x * _rms(x)) * g1_ref[...]
    z = jnp.dot(h.astype(BF16), w_ref[...], preferred_element_type=F32)

    def head_norm(col0, n, gain, scale):
        outs = []
        for i in range(n):
            sl = z[:, col0 + i * HEAD_DIM: col0 + (i + 1) * HEAD_DIM]
            outs.append((sl * _rms(sl)) * gain * scale)
        return jnp.concatenate(outs, axis=-1)

    q_ref[...] = head_norm(_C_Q, N_HEADS, qg_ref[...], HEAD_DIM ** -0.5 * LOG2E)
    kc_ref[...] = z[:, _C_KC:_C_KC + D_KV]
    vc_ref[...] = z[:, _C_VC:_C_VC + D_KV]
    ks_ref[...] = head_norm(_C_KS, N_KV, kg_ref[1:2, :], 1.0)
    vs_ref[...] = z[:, _C_VS:_C_VS + D_KV]
    kw_ref[...] = head_norm(_C_KW, N_KV, kg_ref[2:3, :], 1.0)
    vw_ref[...] = z[:, _C_VW:_C_VW + D_KV]
    gate_ref[...] = jax.nn.sigmoid(z[:, _C_G:_C_G + LANE])
    u_ref[...] = jax.nn.gelu(z[:, _C_U:_C_U + D_GM])
    gv = jax.nn.gelu(z[:, _C_V:_C_V + D_GM])
    v_ref[...] = (gv * _rms(gv)) * vg_ref[...]


def _inproj(x2, norm1_g, w_r, q_norm_g, k_norm_g, gm_v_norm_g):
    n = x2.shape[0]
    row = lambda c: pl.BlockSpec((TM_IN, c), lambda i: (i, 0))
    full = lambda a: pl.BlockSpec(a.shape, lambda i: (0,) * a.ndim)
    g1 = norm1_g.reshape(1, D_MODEL)
    qg = q_norm_g.reshape(1, HEAD_DIM)
    vg = gm_v_norm_g.reshape(1, D_GM)
    widths = (D_ATTN, D_KV, D_KV, D_KV, D_KV, D_KV, D_KV, LANE, D_GM, D_GM)
    return pl.pallas_call(
        _inproj_kernel,
        grid=(n // TM_IN,),
        in_specs=[row(D_MODEL), full(g1), full(w_r), full(qg), full(k_norm_g), full(vg)],
        out_specs=[row(c) for c in widths],
        out_shape=[jax.ShapeDtypeStruct((n, c), F32) for c in widths],
        compiler_params=pltpu.CompilerParams(dimension_semantics=("arbitrary",), vmem_limit_bytes=VMEM_LIMIT),
        name="inproj",
    )(x2, g1, w_r, qg, k_norm_g, vg)


def _compress_kernel(a_ref, pos_ref, w1_ref, w1a_ref, w1b_ref, b1_ref, w2_ref, b2_ref, kg_ref, o_ref, *, norm):
    a = a_ref[0]
    nseg = a.shape[0]
    c = jnp.dot(pos_ref[...], w1_ref[...], precision=HIGHEST, preferred_element_type=F32)[0:1] + b1_ref[...]
    row = lax.broadcasted_iota(jnp.int32, (nseg, 1), 0)
    for g in range(N_KV):
        pa = jnp.dot(a, w1a_ref[g], precision=HIGHEST, preferred_element_type=F32)
        pb = jnp.dot(a, w1b_ref[g], precision=HIGHEST, preferred_element_type=F32)
        hid = jax.nn.gelu(pa + pltpu.roll(pb, nseg - 1, 0) + c)
        out = jnp.dot(hid, w2_ref[...], precision=HIGHEST, preferred_element_type=F32) + b2_ref[...]
        if norm:
            out = (out * _rms(out)) * kg_ref[...]
        o_ref[0, g] = jnp.where(row < nseg - 1, out, 0.0)


def _compress(raw, pos, w1, b1, w2, b2, gain, batch, seq, norm):
    nseg = seq // CMP_STRIDE
    half = CMP_STRIDE * HEAD_DIM
    a = raw.reshape(batch, nseg, CMP_STRIDE * D_KV)
    pos8 = jnp.broadcast_to(pos.reshape(1, CMP_LEN * HEAD_DIM), (8, CMP_LEN * HEAD_DIM))

    def expand(wh):
        wh = wh.reshape(CMP_STRIDE, HEAD_DIM, CMP_HIDDEN)
        z = jnp.zeros((N_KV, CMP_STRIDE, N_KV, HEAD_DIM, CMP_HIDDEN), F32)
        for g in range(N_KV):
            z = z.at[g, :, g].set(wh)
        return z.reshape(N_KV, CMP_STRIDE * D_KV, CMP_HIDDEN)

    w1a, w1b = expand(w1[:half]), expand(w1[half:])
    b1r, b2r, gr = b1.reshape(1, CMP_HIDDEN), b2.reshape(1, HEAD_DIM), gain.reshape(1, HEAD_DIM)
    full = lambda t: pl.BlockSpec(t.shape, lambda i: (0,) * t.ndim)
    return pl.pallas_call(
        functools.partial(_compress_kernel, norm=norm),
        grid=(batch,),
        in_specs=[pl.BlockSpec((1, nseg, CMP_STRIDE * D_KV), lambda i: (i, 0, 0)),
                  full(pos8), full(w1), full(w1a), full(w1b), full(b1r), full(w2), full(b2r), full(gr)],
        out_specs=pl.BlockSpec((1, N_KV, nseg, HEAD_DIM), lambda i: (i, 0, 0, 0)),
        out_shape=jax.ShapeDtypeStruct((batch, N_KV, nseg, HEAD_DIM), F32),
        compiler_params=pltpu.CompilerParams(dimension_semantics=("arbitrary",), vmem_limit_bytes=VMEM_LIMIT),
        name="compress_k" if norm else "compress_v",
    )(a, pos8, w1, w1a, w1b, b1r, w2, b2r, gr)


def _attn_kernel(qt_ref, kc_ref, vct_ref, ks_ref, vst_ref, kw_ref, vwt_ref, g_ref, ovt_ref, o_ref, qs_ref):
    qb = pl.program_id(1)
    nq = N_REP * Q_BLOCK
    q0 = qb * Q_BLOCK
    qt = qt_ref[0, 0]
    ql = lax.broadcasted_iota(jnp.int32, (1, nq), 1) % Q_BLOCK
    t_row = (q0 + ql).astype(F32)
    m_init = 0.5 * NEG

    def online(s, m, l):
        m_new = jnp.maximum(m, jnp.max(s, axis=0, keepdims=True))
        alpha = jnp.exp2(m - m_new)
        p = jnp.exp2(s - m_new)
        return p, m_new, alpha, alpha * l + jnp.sum(p, axis=0, keepdims=True)

    def inv(l):
        return jnp.where(l > 0.0, 1.0 / l, 0.0)

    m0 = jnp.full((1, nq), m_init, F32)
    l0 = jnp.zeros((1, nq), F32)
    a0 = jnp.zeros((HEAD_DIM, nq), F32)

    ncmp = kc_ref.shape[1]
    s = jnp.dot(kc_ref[0], qt, preferred_element_type=F32)
    c_end = (lax.broadcasted_iota(jnp.int32, (ncmp, 1), 0) * CMP_STRIDE + (CMP_LEN - 1)).astype(F32)
    p, _, _, l = online(jnp.where(c_end <= t_row, s, NEG), m0, l0)
    p = p * inv(l)
    o_cmp = jnp.dot(vct_ref[0], p.astype(BF16), preferred_element_type=F32)

    psum = p[:, 0:Q_BLOCK]
    for r in range(1, N_REP):
        psum = psum + p[:, r * Q_BLOCK:(r + 1) * Q_BLOCK]
    nsel = ovt_ref.shape[0]
    p_hi = psum.astype(BF16)
    p_lo = (psum - p_hi.astype(F32)).astype(BF16)
    imp = (jnp.dot(ovt_ref[...], p_hi, preferred_element_type=F32)
           + jnp.dot(ovt_ref[...], p_lo, preferred_element_type=F32))
    n_col = lax.broadcasted_iota(jnp.int32, (nsel, 1), 0).astype(F32)
    n_start = n_col * SEL_BLOCK
    tq = t_row[:, 0:Q_BLOCK]
    cur = jnp.floor(tq * (1.0 / SEL_BLOCK)) * SEL_BLOCK
    forced = (n_start == cur) | (n_start == 0.0)
    valid = n_start <= tq
    imp = jnp.where(forced, imp + FORCE_BONUS, imp)
    imp = jnp.where(valid, imp, NEG)
    sel = jnp.zeros((nsel, Q_BLOCK), F32)
    for _ in range(min(N_SEL, nsel)):
        mx = jnp.max(imp, axis=0, keepdims=True)
        first = jnp.min(jnp.where(imp == mx, n_col, float(nsel)), axis=0, keepdims=True)
        hit = n_col == first
        sel = jnp.where(hit, 1.0, sel)
        imp = jnp.where(hit, -jnp.inf, imp)
    selb = jnp.where(valid & (sel > 0.0), 0.0, NEG).astype(BF16)
    qs_ref[0:LANE, :] = qt
    qs_ref[LANE:LANE + nsel, :] = jnp.concatenate([selb] * N_REP, axis=1)
    if qs_ref.shape[0] > LANE + nsel:
        qs_ref[LANE + nsel:, :] = jnp.zeros((qs_ref.shape[0] - LANE - nsel, nq), BF16)

    def attend(k_blk, vt_blk, q_op, bias, carry):
        m, l, acc = carry
        s = jnp.dot(k_blk, q_op, preferred_element_type=F32)
        if bias is not None:
            s = s + bias
        p, m, alpha, l = online(s, m, l)
        pv = jnp.dot(vt_blk, p.astype(BF16), preferred_element_type=F32)
        return m, l, alpha * acc + pv

    def sel_chunk(j, bias, carry):
        k0 = pl.multiple_of(j * KC_SEL, KC_SEL)
        return attend(ks_ref[0, pl.ds(k0, KC_SEL), :], vst_ref[0, :, pl.ds(k0, KC_SEL)], qs_ref[...], bias, carry)

    n_full = q0 // KC_SEL
    carry = lax.fori_loop(0, n_full, lambda j, c: sel_chunk(j, None, c), (m0, l0, a0))
    pos_last = (n_full * KC_SEL + lax.broadcasted_iota(jnp.int32, (KC_SEL, 1), 0)).astype(F32)
    _, l_sel, o_sel = sel_chunk(n_full, jnp.where(pos_last <= t_row, 0.0, NEG), carry)

    n_wk = WINDOW + Q_BLOCK
    kk = lax.broadcasted_iota(jnp.int32, (n_wk, 1), 0)
    in_win = (kk - WINDOW <= ql) & (kk > ql) & (kk >= WINDOW - q0)
    _, l_win, o_win = attend(kw_ref[0, pl.ds(pl.multiple_of(q0, Q_BLOCK), n_wk), :],
                             vwt_ref[0, :, pl.ds(pl.multiple_of(q0, Q_BLOCK), n_wk)], qt,
                             jnp.where(in_win, 0.0, NEG), (m0, l0, a0))

    gt = g_ref[0, 0]
    o_ref[0, 0] = (gt[0:1] * o_cmp + gt[1:2] * (o_sel * inv(l_sel)) + gt[2:3] * (o_win * inv(l_win)))


def _attention(qt, kc, vct, ks, vst, kw, vwt, gt, ovt):
    bgn, nqb = qt.shape[0], qt.shape[1]
    seq = ks.shape[1]
    nq = N_REP * Q_BLOCK
    per_bg = lambda a: pl.BlockSpec((1,) + a.shape[1:], lambda b, i: (b,) + (0,) * (a.ndim - 1))
    return pl.pallas_call(
        _attn_kernel,
        grid=(bgn, nqb),
        in_specs=[pl.BlockSpec((1, 1, LANE, nq), lambda b, i: (b, i, 0, 0)),
                  per_bg(kc), per_bg(vct), per_bg(ks), per_bg(vst), per_bg(kw), per_bg(vwt),
                  pl.BlockSpec((1, 1, 3, nq), lambda b, i: (b, i, 0, 0)),
                  pl.BlockSpec(ovt.shape, lambda b, i: (0, 0))],
        out_specs=pl.BlockSpec((1, 1, HEAD_DIM, nq), lambda b, i: (b, i, 0, 0)),
        out_shape=jax.ShapeDtypeStruct((bgn, nqb, HEAD_DIM, nq), F32),
        scratch_shapes=[pltpu.VMEM((ks.shape[2], nq), BF16)],
        compiler_params=pltpu.CompilerParams(dimension_semantics=("arbitrary", "arbitrary"),
                                             vmem_limit_bytes=VMEM_LIMIT),
        name="nsa_attention",
    )(qt, kc, vct, ks, vst, kw, vwt, gt, ovt)


def _mix_kernel(x_ref, oa_ref, u_ref, v_ref, ws_ref, bs_ref, ga_ref, gg_ref, wo_ref, g2_ref, wr_ref, br_ref,
                x1_ref, xn_ref, idx_ref, gate_ref):
    tm = x_ref.shape[0]
    rr = lax.broadcasted_iota(jnp.int32, (GM_CHUNK, GM_CHUNK), 0)
    cc = lax.broadcasted_iota(jnp.int32, (GM_CHUNK, GM_CHUNK), 1)
    grp = lax.broadcasted_iota(jnp.int32, (1, D_GM), 1) // GM_GROUP_DIM
    ws = [jnp.where(rr >= cc, ws_ref[g], 0.0).astype(BF16) for g in range(N_GM_GROUPS)]
    ys = []
    for c in range(tm // GM_CHUNK):
        vch = v_ref[c * GM_CHUNK:(c + 1) * GM_CHUNK, :].astype(BF16)
        y = bs_ref[...]
        for g in range(N_GM_GROUPS):
            y = y + jnp.where(grp == g, jnp.dot(ws[g], vch, preferred_element_type=F32), 0.0)
        ys.append(y)
    o_gm = u_ref[...] * jnp.concatenate(ys, axis=0)
    o_at = oa_ref[...]
    mixed = jnp.concatenate([(o_at * _rms(o_at)) * ga_ref[...], (o_gm * _rms(o_gm)) * gg_ref[...]], axis=-1)
    x1 = x_ref[...] + jnp.dot(mixed.astype(BF16), wo_ref[...], preferred_element_type=F32)
    x1_ref[...] = x1
    xn = (x1 * _rms(x1)) * g2_ref[...]
    xn_ref[...] = xn
    logits = jnp.dot(xn, wr_ref[...], precision=HIGHEST, preferred_element_type=F32) + br_ref[...]
    lane = lax.broadcasted_iota(jnp.int32, (1, LANE), 1).astype(F32)
    idx_out = jnp.zeros((tm, LANE), F32)
    val_out = jnp.zeros((tm, LANE), F32)
    vals = []
    for k in range(TOP_K):
        mx = jnp.max(logits, axis=-1, keepdims=True)
        first = jnp.min(jnp.where(logits == mx, lane, float(LANE)), axis=-1, keepdims=True)
        logits = jnp.where(lane == first, -jnp.inf, logits)
        idx_out = jnp.where(lane == float(k), first, idx_out)
        vals.append(mx)
    es = [jnp.exp(v - vals[0]) for v in vals]
    den = es[0] + es[1] + es[2] + es[3]
    for k in range(TOP_K):
        val_out = jnp.where(lane == float(k), es[k] / den, val_out)
    idx_ref[...] = idx_out.astype(jnp.int32)
    gate_ref[...] = val_out


def _mix(x2, o_attn, u_act, v_act, gm_w_s, bias_full, ga, gg, w_out_b, g2, wr_pad, br_pad):
    n = x2.shape[0]
    row = lambda c: pl.BlockSpec((TM_MIX, c), lambda i: (i, 0))
    full = lambda a: pl.BlockSpec(a.shape, lambda i: (0,) * a.ndim)
    return pl.pallas_call(
        _mix_kernel,
        grid=(n // TM_MIX,),
        in_specs=[row(D_MODEL), row(D_ATTN), row(D_GM), row(D_GM), full(gm_w_s), full(bias_full), full(ga), full(gg),
                  full(w_out_b), full(g2), full(wr_pad), full(br_pad)],
        out_specs=[row(D_MODEL), row(D_MODEL), row(LANE), row(LANE)],
        out_shape=[jax.ShapeDtypeStruct((n, D_MODEL), F32), jax.ShapeDtypeStruct((n, D_MODEL), F32),
                   jax.ShapeDtypeStruct((n, LANE), jnp.int32), jax.ShapeDtypeStruct((n, LANE), F32)],
        compiler_params=pltpu.CompilerParams(dimension_semantics=("arbitrary",), vmem_limit_bytes=VMEM_LIMIT),
        name="mix_outproj_router",
    )(x2, o_attn, u_act, v_act, gm_w_s, bias_full, ga, gg, w_out_b, g2, wr_pad, br_pad)


def _gather_rows(idx_ref, n_rows, src_hbm, dst_ref, sem):
    def issue(r, carry):
        pltpu.make_async_copy(src_hbm.at[pl.ds(idx_ref[0, 0, r], 1), :], dst_ref.at[pl.ds(r, 1), :], sem).start()
        return carry
    lax.fori_loop(0, n_rows, issue, 0)
    pltpu.make_async_copy(src_hbm.at[pl.ds(0, n_rows), :], dst_ref, sem).wait()


def _moe_kernel(be_ref, bv_ref, tok_ref, x_hbm, g_ref, wg_ref, wl_ref, bg_ref, bl_ref, wd_ref, bd_ref,
                o_ref, xbuf, sem):
    i = pl.program_id(0)

    @pl.when(bv_ref[i] == 1)
    def _():
        _gather_rows(tok_ref, BM_MOE, x_hbm, xbuf, sem)
        xb = xbuf[...].astype(BF16)
        hg = lax.dot_general(xb, wg_ref[0], _NT, preferred_element_type=F32) + bg_ref[0]
        hl = lax.dot_general(xb, wl_ref[0], _NT, preferred_element_type=F32) + bl_ref[0]
        hg = jnp.minimum(hg, SWIGLU_LIMIT)
        hl = jnp.clip(hl, -SWIGLU_LIMIT, SWIGLU_LIMIT)
        a = hg * jax.nn.sigmoid(SWIGLU_ALPHA * hg) * (hl + 1.0)
        out = jnp.dot(a.astype(BF16), wd_ref[0], preferred_element_type=F32) + bd_ref[0]
        o_ref[...] = out * g_ref[...]

    @pl.when(bv_ref[i] == 0)
    def _():
        o_ref[...] = jnp.zeros(o_ref.shape, F32)


def _moe(blk_expert, blk_valid, tok_blocks, xn, g_buf, wg, wl, bg, bl, wd, bd):
    nb = blk_expert.shape[0]
    per_e = lambda a: pl.BlockSpec((1,) + a.shape[1:], lambda i, be, bv: (be[i],) + (0,) * (a.ndim - 1))
    grid_spec = pltpu.PrefetchScalarGridSpec(
        num_scalar_prefetch=2,
        grid=(nb,),
        in_specs=[pl.BlockSpec((1, 1, BM_MOE), lambda i, be, bv: (i, 0, 0), memory_space=pltpu.SMEM),
                  pl.BlockSpec(memory_space=pl.ANY),
                  pl.BlockSpec((BM_MOE, 1), lambda i, be, bv: (i, 0)),
                  per_e(wg), per_e(wl), per_e(bg), per_e(bl), per_e(wd), per_e(bd)],
        out_specs=pl.BlockSpec((BM_MOE, D_MODEL), lambda i, be, bv: (i, 0)),
        scratch_shapes=[pltpu.VMEM((BM_MOE, D_MODEL), F32), pltpu.SemaphoreType.DMA(())],
    )
    return pl.pallas_call(
        _moe_kernel,
        grid_spec=grid_spec,
        out_shape=jax.ShapeDtypeStruct((nb * BM_MOE, D_MODEL), F32),
        compiler_params=pltpu.CompilerParams(dimension_semantics=("arbitrary",), vmem_limit_bytes=VMEM_LIMIT),
        name="moe_experts",
    )(blk_expert, blk_valid, tok_blocks, xn, g_buf, wg, wl, bg, bl, wd, bd)


def _combine_kernel(dest_ref, x1_ref, y_hbm, o_ref, buf, sem):
    _gather_rows(dest_ref, TOP_K * TM_CMB, y_hbm, buf, sem)
    acc = x1_ref[...]
    for k in range(TOP_K):
        acc = acc + buf[k * TM_CMB:(k + 1) * TM_CMB, :]
    o_ref[...] = acc


def _combine(dest_blocks, x1, y_rows):
    n = x1.shape[0]
    return pl.pallas_call(
        _combine_kernel,
        grid=(n // TM_CMB,),
        in_specs=[pl.BlockSpec((1, 1, TOP_K * TM_CMB), lambda i: (i, 0, 0), memory_space=pltpu.SMEM),
                  pl.BlockSpec((TM_CMB, D_MODEL), lambda i: (i, 0)),
                  pl.BlockSpec(memory_space=pl.ANY)],
        out_specs=pl.BlockSpec((TM_CMB, D_MODEL), lambda i: (i, 0)),
        out_shape=jax.ShapeDtypeStruct((n, D_MODEL), F32),
        scratch_shapes=[pltpu.VMEM((TOP_K * TM_CMB, D_MODEL), F32), pltpu.SemaphoreType.DMA(())],
        compiler_params=pltpu.CompilerParams(dimension_semantics=("arbitrary",), vmem_limit_bytes=VMEM_LIMIT),
        name="moe_combine",
    )(dest_blocks, x1, y_rows)


def kernel(x, norm1_g, w_in, q_norm_g, k_norm_g, cmp_pos, w_cmp1, b_cmp1, w_cmp2, b_cmp2, gm_v_norm_g, gm_w_s,
           gm_b_s, out_norm_attn_g, out_norm_gm_g, w_out, norm2_g, w_router, b_router, w_gate_up, b_gate_up,
           w_down, b_down):
    batch, seq, _ = x.shape
    n = batch * seq
    nqb = seq // Q_BLOCK
    bgn = batch * N_KV
    x2 = x.reshape(n, D_MODEL)

    c_gate = D_ATTN + 6 * D_KV
    w_r = jnp.concatenate([w_in[:, :c_gate], w_in[:, c_gate + N_GATE:], w_in[:, c_gate:c_gate + N_GATE],
                           jnp.zeros((D_MODEL, LANE - N_GATE), F32)], axis=1).astype(BF16)
    q, kc_raw, vc_raw, ks, vs, kw, vw, gates, u_act, v_act = _inproj(x2, norm1_g, w_r, q_norm_g, k_norm_g,
                                                                     gm_v_norm_g)

    kc = _compress(kc_raw, cmp_pos[0], w_cmp1[0], b_cmp1[0], w_cmp2[0], b_cmp2[0], k_norm_g[0], batch, seq, True)
    vc = _compress(vc_raw, cmp_pos[1], w_cmp1[1], b_cmp1[1], w_cmp2[1], b_cmp2[1], k_norm_g[0], batch, seq, False)

    nq = N_REP * Q_BLOCK
    ncmp = seq // CMP_STRIDE

    nsel = seq // SEL_BLOCK
    oh_w = -(-nsel // LANE) * LANE

    def with_pos(k, pos, one_hot=False):
        feat = np.zeros((pos.shape[0], LANE - HEAD_DIM + (oh_w if one_hot else 0)), np.float32)
        feat[:, 0] = feat[:, 1] = pos // SEL_BLOCK
        feat[:, 2] = feat[:, 3] = pos % SEL_BLOCK
        if one_hot:
            feat[np.arange(pos.shape[0]), LANE - HEAD_DIM + pos // SEL_BLOCK] = 1.0
        feat = jnp.broadcast_to(jnp.asarray(feat, BF16)[None], (bgn,) + feat.shape)
        return jnp.concatenate([k.astype(BF16), feat], axis=-1)

    def front_pad(a, axis):
        pad = [(0, 0)] * a.ndim
        pad[axis] = (WINDOW, 0)
        return jnp.pad(a, pad)

    def tok_major(a):
        return a.reshape(batch, seq, N_KV, HEAD_DIM).transpose(0, 2, 1, 3).reshape(bgn, seq, HEAD_DIM)

    def feat_major(a):
        return a.reshape(batch, seq, N_KV, HEAD_DIM).transpose(0, 2, 3, 1).reshape(bgn, HEAD_DIM, seq).astype(BF16)

    head = np.arange(N_KV)[:, None] * N_REP + np.arange(nq)[None, :] // Q_BLOCK
    coef = np.exp2(-(head + 1.0)) * LOG2E
    c_hi = coef.astype(BF16).astype(np.float64)
    c_lo = (coef - c_hi).astype(BF16).astype(np.float64)
    qrows = np.zeros((N_KV, LANE - HEAD_DIM, nq), np.float32)
    qrows[:, 0], qrows[:, 1], qrows[:, 2], qrows[:, 3] = SEL_BLOCK * c_hi, SEL_BLOCK * c_lo, c_hi, c_lo
    qrows = jnp.broadcast_to(jnp.asarray(qrows, BF16)[None, :, None], (batch, N_KV, nqb, LANE - HEAD_DIM, nq))
    qt = (q.reshape(batch, nqb, Q_BLOCK, N_KV, N_REP, HEAD_DIM).transpose(0, 3, 1, 5, 4, 2)
          .reshape(batch, N_KV, nqb, HEAD_DIM, nq).astype(BF16))
    qt = jnp.concatenate([qt, qrows], axis=3).reshape(bgn, nqb, LANE, nq)
    gt = (gates[:, :N_GATE].reshape(batch, nqb, Q_BLOCK, N_KV, N_REP, 3).transpose(0, 3, 1, 5, 4, 2)
          .reshape(bgn, nqb, 3, nq))
    pos_t = np.arange(seq)
    pos_c = np.arange(ncmp) * CMP_STRIDE + (CMP_LEN - 1)
    kc_b = with_pos(kc.reshape(bgn, ncmp, HEAD_DIM), pos_c)
    vct = vc.reshape(bgn, ncmp, HEAD_DIM).transpose(0, 2, 1).astype(BF16)
    c0 = np.arange(ncmp)[None, :] * CMP_STRIDE
    n0 = np.arange(seq // SEL_BLOCK)[:, None] * SEL_BLOCK
    ovt = np.clip(np.minimum(c0 + CMP_LEN, n0 + SEL_BLOCK) - np.maximum(c0, n0), 0, None) / CMP_LEN
    pos_w = np.maximum(np.arange(seq + WINDOW) - WINDOW, 0)
    ot = _attention(qt, kc_b, vct, with_pos(tok_major(ks), pos_t, one_hot=True), feat_major(vs),
                    with_pos(front_pad(tok_major(kw), 1), pos_w), front_pad(feat_major(vw), 2), gt,
                    jnp.asarray(ovt, BF16))
    o_attn = (ot.reshape(batch, N_KV, nqb, HEAD_DIM, N_REP, Q_BLOCK).transpose(0, 2, 5, 1, 4, 3)
              .reshape(n, D_ATTN))

    bias_full = jnp.repeat(gm_b_s.T, GM_GROUP_DIM, axis=1)
    wr_pad = jnp.concatenate([w_router, jnp.zeros((D_MODEL, LANE - N_EXPERTS), F32)], axis=1)
    br_pad = jnp.concatenate([b_router, jnp.full((LANE - N_EXPERTS,), NEG, F32)]).reshape(1, LANE)
    x1, xn, idx_pad, gate_pad = _mix(x2, o_attn, u_act, v_act, gm_w_s, bias_full,
                                     out_norm_attn_g.reshape(1, D_ATTN), out_norm_gm_g.reshape(1, D_GM),
                                     w_out.astype(BF16), norm2_g.reshape(1, D_MODEL), wr_pad, br_pad)

    s_tot = n * TOP_K
    nb = s_tot // BM_MOE + N_EXPERTS
    e_flat = idx_pad[:, :TOP_K].reshape(s_tot)
    g_flat = gate_pad[:, :TOP_K].reshape(s_tot)
    onehot = (e_flat[:, None] == jnp.arange(N_EXPERTS, dtype=jnp.int32)[None, :]).astype(jnp.int32)
    csum = jnp.cumsum(onehot, axis=0)
    rank = jnp.sum(csum * onehot, axis=1) - 1
    counts = csum[-1]
    padded = ((counts + BM_MOE - 1) // BM_MOE) * BM_MOE
    pad_end = jnp.cumsum(padded)
    pad_start = pad_end - padded
    dest = pad_start[e_flat] + rank
    tok_flat = jnp.arange(s_tot, dtype=jnp.int32) // TOP_K
    tok_buf = jnp.zeros((nb * BM_MOE,), jnp.int32).at[dest].set(tok_flat)
    g_buf = jnp.zeros((nb * BM_MOE,), F32).at[dest].set(g_flat)
    blk_start = jnp.arange(nb, dtype=jnp.int32) * BM_MOE
    blk_expert = jnp.minimum(jnp.sum((blk_start[:, None] >= pad_end[None, :]).astype(jnp.int32), axis=1),
                             N_EXPERTS - 1)
    blk_valid = (blk_start < pad_end[-1]).astype(jnp.int32)

    wgu_t = w_gate_up.transpose(0, 2, 1)
    wg = wgu_t[:, 0::2, :].astype(BF16)
    wl = wgu_t[:, 1::2, :].astype(BF16)
    bg = b_gate_up[:, 0::2].reshape(N_EXPERTS, 1, D_EXPERT)
    bl = b_gate_up[:, 1::2].reshape(N_EXPERTS, 1, D_EXPERT)
    y_rows = _moe(blk_expert, blk_valid, tok_buf.reshape(nb, 1, BM_MOE), xn, g_buf.reshape(nb * BM_MOE, 1),
                  wg, wl, bg, bl, w_down.astype(BF16), b_down.reshape(N_EXPERTS, 1, D_MODEL))

    dest_blocks = (dest.reshape(n // TM_CMB, TM_CMB, TOP_K).transpose(0, 2, 1)
                   .reshape(n // TM_CMB, 1, TOP_K * TM_CMB).astype(jnp.int32))
    out = _combine(dest_blocks, x1, y_rows)
    return out.reshape(batch, seq, D_MODEL)
```

```python
import functools

import jax
import jax.numpy as jnp
import numpy as np
from jax import lax
from jax.experimental import pallas as pl
from jax.experimental.pallas import tpu as pltpu

F32 = jnp.float32
BF16 = jnp.bfloat16
HIGHEST = lax.Precision.HIGHEST
_NT = (((1,), (1,)), ((), ()))

D_MODEL = 1024
N_HEADS = 8
HEAD_DIM = 64
N_KV = 2
N_REP = N_HEADS // N_KV
D_ATTN = N_HEADS * HEAD_DIM
D_KV = N_KV * HEAD_DIM
N_GM_GROUPS = 8
GM_GROUP_DIM = 64
D_GM = N_GM_GROUPS * GM_GROUP_DIM
N_GATE = 3 * N_HEADS
CMP_LEN = 32
CMP_STRIDE = 16
CMP_HIDDEN = 128
SEL_BLOCK = 64
N_SEL = 16
WINDOW = 512
Q_BLOCK = 128
FORCE_BONUS = 1.0e4
GM_CHUNK = 128
N_EXPERTS = 32
TOP_K = 4
D_EXPERT = 1024
SWIGLU_LIMIT = 7.0
SWIGLU_ALPHA = 1.702
EPS = 1e-6
NEG = -1.0e30
LOG2E = 1.4426950408889634

LANE = 128
ROW_SUB = D_MODEL // LANE
VMEM_LIMIT = 48 * 1024 * 1024
VMEM_LIMIT_MOE = 56 * 1024 * 1024

_C_Q = 0
_C_KC = _C_Q + D_ATTN
_C_VC = _C_KC + D_KV
_C_KS = _C_VC + D_KV
_C_VS = _C_KS + D_KV
_C_KW = _C_VS + D_KV
_C_VW = _C_KW + D_KV
_C_U = _C_VW + D_KV
_C_V = _C_U + D_GM
_C_G = _C_V + D_GM
D_IN_PAD = _C_G + LANE

TM_IN = 256
TM_MIX = 256
KC_SEL = 512
BM_MOE = 256
TM_CMB = 128


def _rms(x, eps=EPS):
    return lax.rsqrt(jnp.mean(x * x, axis=-1, keepdims=True) + eps)


def _inproj_kernel(x_ref, g1_ref, w_ref, qg_ref, kg_ref, vg_ref,
                   q_ref, kc_ref, vc_ref, ks_ref, vs_ref, kw_ref, vw_ref, gate_ref, u_ref, v_ref):
    x = x_ref[...]
    h = (x * _rms(x)) * g1_ref[...]
    z = jnp.dot(h.astype(BF16), w_ref[...], preferred_element_type=F32)

    def head_norm(col0, n, gain, scale):
        outs = []
        for i in range(n):
            sl = z[:, col0 + i * HEAD_DIM: col0 + (i + 1) * HEAD_DIM]
            outs.append((sl * _rms(sl)) * gain * scale)
        return jnp.concatenate(outs, axis=-1)

    q_ref[...] = head_norm(_C_Q, N_HEADS, qg_ref[...], HEAD_DIM ** -0.5 * LOG2E)
    kc_ref[...] = z[:, _C_KC:_C_KC + D_KV]
    vc_ref[...] = z[:, _C_VC:_C_VC + D_KV]
    ks_ref[...] = head_norm(_C_KS, N_KV, kg_ref[1:2, :], 1.0)
    vs_ref[...] = z[:, _C_VS:_C_VS + D_KV]
    kw_ref[...] = head_norm(_C_KW, N_KV, kg_ref[2:3, :], 1.0)
    vw_ref[...] = z[:, _C_VW:_C_VW + D_KV]
    gate_ref[...] = jax.nn.sigmoid(z[:, _C_G:_C_G + LANE])
    u_ref[...] = jax.nn.gelu(z[:, _C_U:_C_U + D_GM])
    gv = jax.nn.gelu(z[:, _C_V:_C_V + D_GM])
    v_ref[...] = (gv * _rms(gv)) * vg_ref[...]


def _inproj(x2, norm1_g, w_r, q_norm_g, k_norm_g, gm_v_norm_g):
    n = x2.shape[0]
    row = lambda c: pl.BlockSpec((TM_IN, c), lambda i: (i, 0))
    full = lambda a: pl.BlockSpec(a.shape, lambda i: (0,) * a.ndim)
    g1 = norm1_g.reshape(1, D_MODEL)
    qg = q_norm_g.reshape(1, HEAD_DIM)
    vg = gm_v_norm_g.reshape(1, D_GM)
    widths = (D_ATTN, D_KV, D_KV, D_KV, D_KV, D_KV, D_KV, LANE, D_GM, D_GM)
    return pl.pallas_call(
        _inproj_kernel,
        grid=(n // TM_IN,),
        in_specs=[row(D_MODEL), full(g1), full(w_r), full(qg), full(k_norm_g), full(vg)],
        out_specs=[row(c) for c in widths],
        out_shape=[jax.ShapeDtypeStruct((n, c), F32) for c in widths],
        compiler_params=pltpu.CompilerParams(dimension_semantics=("arbitrary",), vmem_limit_bytes=VMEM_LIMIT),
        name="inproj",
    )(x2, g1, w_r, qg, k_norm_g, vg)


def _compress_kernel(a_ref, pos_ref, w1_ref, w1a_ref, w1b_ref, b1_ref, w2_ref, b2_ref, kg_ref, o_ref, *, norm):
    a = a_ref[0]
    nseg = a.shape[0]
    c = jnp.dot(pos_ref[...], w1_ref[...], precision=HIGHEST, preferred_element_type=F32)[0:1] + b1_ref[...]
    row = lax.broadcasted_iota(jnp.int32, (nseg, 1), 0)
    for g in range(N_KV):
        pa = jnp.dot(a, w1a_ref[g], precision=HIGHEST, preferred_element_type=F32)
        pb = jnp.dot(a, w1b_ref[g], precision=HIGHEST, preferred_element_type=F32)
        hid = jax.nn.gelu(pa + pltpu.roll(pb, nseg - 1, 0) + c)
        out = jnp.dot(hid, w2_ref[...], precision=HIGHEST, preferred_element_type=F32) + b2_ref[...]
        if norm:
            out = (out * _rms(out)) * kg_ref[...]
        o_ref[0, g] = jnp.where(row < nseg - 1, out, 0.0)


def _compress(raw, pos, w1, b1, w2, b2, gain, batch, seq, norm):
    nseg = seq // CMP_STRIDE
    half = CMP_STRIDE * HEAD_DIM
    a = raw.reshape(batch, nseg, CMP_STRIDE * D_KV)
    pos8 = jnp.broadcast_to(pos.reshape(1, CMP_LEN * HEAD_DIM), (8, CMP_LEN * HEAD_DIM))

    def expand(wh):
        wh = wh.reshape(CMP_STRIDE, HEAD_DIM, CMP_HIDDEN)
        z = jnp.zeros((N_KV, CMP_STRIDE, N_KV, HEAD_DIM, CMP_HIDDEN), F32)
        for g in range(N_KV):
            z = z.at[g, :, g].set(wh)
        return z.reshape(N_KV, CMP_STRIDE * D_KV, CMP_HIDDEN)

    w1a, w1b = expand(w1[:half]), expand(w1[half:])
    b1r, b2r, gr = b1.reshape(1, CMP_HIDDEN), b2.reshape(1, HEAD_DIM), gain.reshape(1, HEAD_DIM)
    full = lambda t: pl.BlockSpec(t.shape, lambda i: (0,) * t.ndim)
    return pl.pallas_call(
        functools.partial(_compress_kernel, norm=norm),
        grid=(batch,),
        in_specs=[pl.BlockSpec((1, nseg, CMP_STRIDE * D_KV), lambda i: (i, 0, 0)),
                  full(pos8), full(w1), full(w1a), full(w1b), full(b1r), full(w2), full(b2r), full(gr)],
        out_specs=pl.BlockSpec((1, N_KV, nseg, HEAD_DIM), lambda i: (i, 0, 0, 0)),
        out_shape=jax.ShapeDtypeStruct((batch, N_KV, nseg, HEAD_DIM), F32),
        compiler_params=pltpu.CompilerParams(dimension_semantics=("arbitrary",), vmem_limit_bytes=VMEM_LIMIT),
        name="compress_k" if norm else "compress_v",
    )(a, pos8, w1, w1a, w1b, b1r, w2, b2r, gr)


def _attn_kernel(qt_ref, kc_ref, vct_ref, ks_ref, vst_ref, kw_ref, vwt_ref, g_ref, ovt_ref, o_ref, qs_ref):
    qb = pl.program_id(1)
    nq = N_REP * Q_BLOCK
    q0 = qb * Q_BLOCK
    qt = qt_ref[0, 0]
    ql = lax.broadcasted_iota(jnp.int32, (1, nq), 1) % Q_BLOCK
    t_row = (q0 + ql).astype(F32)
    m_init = 0.5 * NEG

    def online(s, m, l):
        m_new = jnp.maximum(m, jnp.max(s, axis=0, keepdims=True))
        alpha = jnp.exp2(m - m_new)
        p = jnp.exp2(s - m_new)
        return p, m_new, alpha, alpha * l + jnp.sum(p, axis=0, keepdims=True)

    def inv(l):
        return jnp.where(l > 0.0, 1.0 / l, 0.0)

    m0 = jnp.full((1, nq), m_init, F32)
    l0 = jnp.zeros((1, nq), F32)
    a0 = jnp.zeros((HEAD_DIM, nq), F32)

    ncmp = kc_ref.shape[1]
    s = jnp.dot(kc_ref[0], qt, preferred_element_type=F32)
    c_end = (lax.broadcasted_iota(jnp.int32, (ncmp, 1), 0) * CMP_STRIDE + (CMP_LEN - 1)).astype(F32)
    p, _, _, l = online(jnp.where(c_end <= t_row, s, NEG), m0, l0)
    p = p * inv(l)
    o_cmp = jnp.dot(vct_ref[0], p.astype(BF16), preferred_element_type=F32)

    psum = p[:, 0:Q_BLOCK]
    for r in range(1, N_REP):
        psum = psum + p[:, r * Q_BLOCK:(r + 1) * Q_BLOCK]
    nsel = ovt_ref.shape[0]
    p_hi = psum.astype(BF16)
    p_lo = (psum - p_hi.astype(F32)).astype(BF16)
    imp = (jnp.dot(ovt_ref[...], p_hi, preferred_element_type=F32)
           + jnp.dot(ovt_ref[...], p_lo, preferred_element_type=F32))
    n_col = lax.broadcasted_iota(jnp.int32, (nsel, 1), 0).astype(F32)
    n_start = n_col * SEL_BLOCK
    tq = t_row[:, 0:Q_BLOCK]
    cur = jnp.floor(tq * (1.0 / SEL_BLOCK)) * SEL_BLOCK
    forced = (n_start == cur) | (n_start == 0.0)
    valid = n_start <= tq
    imp = jnp.where(forced, imp + FORCE_BONUS, imp)
    imp = jnp.where(valid, imp, NEG)
    sel = jnp.zeros((nsel, Q_BLOCK), F32)
    for _ in range(min(N_SEL, nsel)):
        mx = jnp.max(imp, axis=0, keepdims=True)
        first = jnp.min(jnp.where(imp == mx, n_col, float(nsel)), axis=0, keepdims=True)
        hit = n_col == first
        sel = jnp.where(hit, 1.0, sel)
        imp = jnp.where(hit, -jnp.inf, imp)
    selb = jnp.where(valid & (sel > 0.0), 0.0, NEG).astype(BF16)
    qs_ref[0:LANE, :] = qt
    qs_ref[LANE:LANE + nsel, :] = jnp.concatenate([selb] * N_REP, axis=1)
    if qs_ref.shape[0] > LANE + nsel:
        qs_ref[LANE + nsel:, :] = jnp.zeros((qs_ref.shape[0] - LANE - nsel, nq), BF16)

    def attend(k_blk, vt_blk, q_op, bias, carry):
        m, l, acc = carry
        s = jnp.dot(k_blk, q_op, preferred_element_type=F32)
        if bias is not None:
            s = s + bias
        p, m, alpha, l = online(s, m, l)
        pv = jnp.dot(vt_blk, p.astype(BF16), preferred_element_type=F32)
        return m, l, alpha * acc + pv

    def sel_chunk(j, bias, carry):
        k0 = pl.multiple_of(j * KC_SEL, KC_SEL)
        return attend(ks_ref[0, pl.ds(k0, KC_SEL), :], vst_ref[0, :, pl.ds(k0, KC_SEL)], qs_ref[...], bias, carry)

    n_full = q0 // KC_SEL
    carry = lax.fori_loop(0, n_full, lambda j, c: sel_chunk(j, None, c), (m0, l0, a0))
    pos_last = (n_full * KC_SEL + lax.broadcasted_iota(jnp.int32, (KC_SEL, 1), 0)).astype(F32)
    _, l_sel, o_sel = sel_chunk(n_full, jnp.where(pos_last <= t_row, 0.0, NEG), carry)

    n_wk = WINDOW + Q_BLOCK
    kk = lax.broadcasted_iota(jnp.int32, (n_wk, 1), 0)
    in_win = (kk - WINDOW <= ql) & (kk > ql) & (kk >= WINDOW - q0)
    _, l_win, o_win = attend(kw_ref[0, pl.ds(pl.multiple_of(q0, Q_BLOCK), n_wk), :],
                             vwt_ref[0, :, pl.ds(pl.multiple_of(q0, Q_BLOCK), n_wk)], qt,
                             jnp.where(in_win, 0.0, NEG), (m0, l0, a0))

    gt = g_ref[0, 0]
    o_ref[0, 0] = (gt[0:1] * o_cmp + gt[1:2] * (o_sel * inv(l_sel)) + gt[2:3] * (o_win * inv(l_win)))


def _attention(qt, kc, vct, ks, vst, kw, vwt, gt, ovt):
    bgn, nqb = qt.shape[0], qt.shape[1]
    seq = ks.shape[1]
    nq = N_REP * Q_BLOCK
    per_bg = lambda a: pl.BlockSpec((1,) + a.shape[1:], lambda b, i: (b,) + (0,) * (a.ndim - 1))
    return pl.pallas_call(
        _attn_kernel,
        grid=(bgn, nqb),
        in_specs=[pl.BlockSpec((1, 1, LANE, nq), lambda b, i: (b, i, 0, 0)),
                  per_bg(kc), per_bg(vct), per_bg(ks), per_bg(vst), per_bg(kw), per_bg(vwt),
                  pl.BlockSpec((1, 1, 3, nq), lambda b, i: (b, i, 0, 0)),
                  pl.BlockSpec(ovt.shape, lambda b, i: (0, 0))],
        out_specs=pl.BlockSpec((1, 1, HEAD_DIM, nq), lambda b, i: (b, i, 0, 0)),
        out_shape=jax.ShapeDtypeStruct((bgn, nqb, HEAD_DIM, nq), F32),
        scratch_shapes=[pltpu.VMEM((ks.shape[2], nq), BF16)],
        compiler_params=pltpu.CompilerParams(dimension_semantics=("arbitrary", "arbitrary"),
                                             vmem_limit_bytes=VMEM_LIMIT),
        name="nsa_attention",
    )(qt, kc, vct, ks, vst, kw, vwt, gt, ovt)


def _mix_kernel(x_ref, oa_ref, u_ref, v_ref, ws_ref, bs_ref, ga_ref, gg_ref, wo_ref, g2_ref, wr_ref, br_ref,
                x1_ref, xn_ref, idx_ref, gate_ref):
    tm = x_ref.shape[0]
    rr = lax.broadcasted_iota(jnp.int32, (GM_CHUNK, GM_CHUNK), 0)
    cc = lax.broadcasted_iota(jnp.int32, (GM_CHUNK, GM_CHUNK), 1)
    grp = lax.broadcasted_iota(jnp.int32, (1, D_GM), 1) // GM_GROUP_DIM
    ws = [jnp.where(rr >= cc, ws_ref[g], 0.0).astype(BF16) for g in range(N_GM_GROUPS)]
    ys = []
    for c in range(tm // GM_CHUNK):
        vch = v_ref[c * GM_CHUNK:(c + 1) * GM_CHUNK, :].astype(BF16)
        y = bs_ref[...]
        for g in range(N_GM_GROUPS):
            y = y + jnp.where(grp == g, jnp.dot(ws[g], vch, preferred_element_type=F32), 0.0)
        ys.append(y)
    o_gm = u_ref[...] * jnp.concatenate(ys, axis=0)
    o_at = oa_ref[...]
    mixed = jnp.concatenate([(o_at * _rms(o_at)) * ga_ref[...], (o_gm * _rms(o_gm)) * gg_ref[...]], axis=-1)
    x1 = x_ref[...] + jnp.dot(mixed.astype(BF16), wo_ref[...], preferred_element_type=F32)
    x1_ref[...] = x1
    xn = (x1 * _rms(x1)) * g2_ref[...]
    for s in range(ROW_SUB):
        xn_ref[:, s, :] = xn[:, s * LANE:(s + 1) * LANE]
    logits = jnp.dot(xn, wr_ref[...], precision=HIGHEST, preferred_element_type=F32) + br_ref[...]
    lane = lax.broadcasted_iota(jnp.int32, (1, LANE), 1).astype(F32)
    idx_out = jnp.zeros((tm, LANE), F32)
    val_out = jnp.zeros((tm, LANE), F32)
    vals = []
    for k in range(TOP_K):
        mx = jnp.max(logits, axis=-1, keepdims=True)
        first = jnp.min(jnp.where(logits == mx, lane, float(LANE)), axis=-1, keepdims=True)
        logits = jnp.where(lane == first, -jnp.inf, logits)
        idx_out = jnp.where(lane == float(k), first, idx_out)
        vals.append(mx)
    es = [jnp.exp(v - vals[0]) for v in vals]
    den = es[0] + es[1] + es[2] + es[3]
    for k in range(TOP_K):
        val_out = jnp.where(lane == float(k), es[k] / den, val_out)
    idx_ref[...] = idx_out.astype(jnp.int32)
    gate_ref[...] = val_out


def _mix(x2, o_attn, u_act, v_act, gm_w_s, bias_full, ga, gg, w_out_b, g2, wr_pad, br_pad):
    n = x2.shape[0]
    row = lambda c: pl.BlockSpec((TM_MIX, c), lambda i: (i, 0))
    full = lambda a: pl.BlockSpec(a.shape, lambda i: (0,) * a.ndim)
    return pl.pallas_call(
        _mix_kernel,
        grid=(n // TM_MIX,),
        in_specs=[row(D_MODEL), row(D_ATTN), row(D_GM), row(D_GM), full(gm_w_s), full(bias_full), full(ga), full(gg),
                  full(w_out_b), full(g2), full(wr_pad), full(br_pad)],
        out_specs=[row(D_MODEL), pl.BlockSpec((TM_MIX, ROW_SUB, LANE), lambda i: (i, 0, 0)), row(LANE), row(LANE)],
        out_shape=[jax.ShapeDtypeStruct((n, D_MODEL), F32), jax.ShapeDtypeStruct((n, ROW_SUB, LANE), F32),
                   jax.ShapeDtypeStruct((n, LANE), jnp.int32), jax.ShapeDtypeStruct((n, LANE), F32)],
        compiler_params=pltpu.CompilerParams(dimension_semantics=("arbitrary",), vmem_limit_bytes=VMEM_LIMIT),
        name="mix_outproj_router",
    )(x2, o_attn, u_act, v_act, gm_w_s, bias_full, ga, gg, w_out_b, g2, wr_pad, br_pad)


def _row_gather(idx_ref, n_rows, src_hbm, dst_ref, sem):
    def start():
        for r in range(n_rows):
            pltpu.make_async_copy(src_hbm.at[pl.ds(idx_ref[0, 0, r], 1)], dst_ref.at[pl.ds(r, 1)], sem).start()

    def wait():
        pltpu.make_async_copy(src_hbm.at[pl.ds(0, n_rows)], dst_ref, sem).wait()

    return start, wait


def _rows_to_matrix(ref):
    return jnp.concatenate([ref[:, s, :] for s in range(ROW_SUB)], axis=1)


def _moe_kernel(be_ref, bv_ref, tok_ref, tok_next_ref, x_hbm, wgu_ref, bg_ref, bl_ref, wd_ref, bd_ref,
                o_ref, xbuf, sems, wt_s, wg_s, wl_s, wd_s):
    i = pl.program_id(0)
    nb = pl.num_programs(0)
    slot = i % 2
    start_cur, wait_cur = _row_gather(tok_ref, BM_MOE, x_hbm, xbuf.at[slot], sems.at[slot])
    start_next, _ = _row_gather(tok_next_ref, BM_MOE, x_hbm, xbuf.at[1 - slot], sems.at[1 - slot])

    prev = jnp.maximum(i - 1, 0)

    @pl.when((i == 0) & (bv_ref[0] == 1))
    def _():
        start_cur()

    @pl.when((bv_ref[i] == 1) & ((i == 0) | (be_ref[i] != be_ref[prev])))
    def _():
        tc = wt_s.shape[1]
        for c in range(2 * D_EXPERT // tc):
            wt = wgu_ref[0, :, c * tc:(c + 1) * tc].T
            for j in range(ROW_SUB):
                wt_s[j] = wt[:, j * LANE:(j + 1) * LANE]
            for first, dst in ((0, wg_s), (1, wl_s)):
                half = jnp.concatenate([wt_s[j, pl.ds(first, tc // 2, stride=2), :] for j in range(ROW_SUB)], axis=1)
                dst[c * tc // 2:(c + 1) * tc // 2, :] = half.astype(BF16)
        wd_s[...] = wd_ref[0].astype(BF16)

    @pl.when(bv_ref[i] == 1)
    def _():
        start_next()
        wait_cur()
        xb = _rows_to_matrix(xbuf.at[slot]).astype(BF16)
        hg = lax.dot_general(xb, wg_s[...], _NT, preferred_element_type=F32) + bg_ref[0]
        hl = lax.dot_general(xb, wl_s[...], _NT, preferred_element_type=F32) + bl_ref[0]
        hg = jnp.minimum(hg, SWIGLU_LIMIT)
        hl = jnp.clip(hl, -SWIGLU_LIMIT, SWIGLU_LIMIT)
        a = hg * jax.nn.sigmoid(SWIGLU_ALPHA * hg) * (hl + 1.0)
        out = jnp.dot(a.astype(BF16), wd_s[...], preferred_element_type=F32) + bd_ref[0]
        for s in range(ROW_SUB):
            o_ref[:, s, :] = out[:, s * LANE:(s + 1) * LANE]

    @pl.when((bv_ref[i] == 0) & (i > 0) & (bv_ref[prev] == 1))
    def _():
        wait_cur()

    @pl.when(bv_ref[i] == 0)
    def _():
        o_ref[...] = jnp.zeros(o_ref.shape, F32)


def _moe(blk_expert, blk_valid, tok_blocks, xn3, w_gate_up, bg, bl, w_down, bd):
    nb = blk_expert.shape[0]
    per_e = lambda a: pl.BlockSpec((1,) + a.shape[1:], lambda i, be, bv: (be[i],) + (0,) * (a.ndim - 1))
    grid_spec = pltpu.PrefetchScalarGridSpec(
        num_scalar_prefetch=2,
        grid=(nb,),
        in_specs=[pl.BlockSpec((1, 1, BM_MOE), lambda i, be, bv: (i, 0, 0), memory_space=pltpu.SMEM),
                  pl.BlockSpec((1, 1, BM_MOE), lambda i, be, bv: (jnp.minimum(i + 1, nb - 1), 0, 0),
                               memory_space=pltpu.SMEM),
                  pl.BlockSpec(memory_space=pl.ANY),
                  per_e(w_gate_up), per_e(bg), per_e(bl), per_e(w_down), per_e(bd)],
        out_specs=pl.BlockSpec((BM_MOE, ROW_SUB, LANE), lambda i, be, bv: (i, 0, 0)),
        scratch_shapes=[pltpu.VMEM((2, BM_MOE, ROW_SUB, LANE), F32), pltpu.SemaphoreType.DMA((2,)),
                        pltpu.VMEM((ROW_SUB, 256, LANE), F32), pltpu.VMEM((D_EXPERT, D_MODEL), BF16),
                        pltpu.VMEM((D_EXPERT, D_MODEL), BF16), pltpu.VMEM((D_EXPERT, D_MODEL), BF16)],
    )
    return pl.pallas_call(
        _moe_kernel,
        grid_spec=grid_spec,
        out_shape=jax.ShapeDtypeStruct((nb * BM_MOE, ROW_SUB, LANE), F32),
        compiler_params=pltpu.CompilerParams(dimension_semantics=("arbitrary",), vmem_limit_bytes=VMEM_LIMIT_MOE),
        name="moe_experts",
    )(blk_expert, blk_valid, tok_blocks, tok_blocks, xn3, w_gate_up, bg, bl, w_down, bd)


def _combine_kernel(dest_ref, dest_next_ref, x1_ref, gate_ref, y_hbm, o_ref, buf, sems):
    i = pl.program_id(0)
    slot = i % 2
    n_rows = TOP_K * TM_CMB
    start_cur, wait_cur = _row_gather(dest_ref, n_rows, y_hbm, buf.at[slot], sems.at[slot])
    start_next, _ = _row_gather(dest_next_ref, n_rows, y_hbm, buf.at[1 - slot], sems.at[1 - slot])

    @pl.when(i == 0)
    def _():
        start_cur()

    @pl.when(i + 1 < pl.num_programs(0))
    def _():
        start_next()

    wait_cur()
    gate = gate_ref[...]
    gk = [jnp.broadcast_to(gate[:, k:k + 1], (TM_CMB, LANE)) for k in range(TOP_K)]
    rows = buf.at[slot]
    cols = []
    for s in range(ROW_SUB):
        acc = x1_ref[:, s * LANE:(s + 1) * LANE]
        for k in range(TOP_K):
            acc = acc + gk[k] * rows[k * TM_CMB:(k + 1) * TM_CMB, s, :]
        cols.append(acc)
    o_ref[...] = jnp.concatenate(cols, axis=1)


def _combine(dest_blocks, x1, gate_pad, y_rows):
    n = x1.shape[0]
    nt = n // TM_CMB
    n_rows = TOP_K * TM_CMB
    return pl.pallas_call(
        _combine_kernel,
        grid=(nt,),
        in_specs=[pl.BlockSpec((1, 1, n_rows), lambda i: (i, 0, 0), memory_space=pltpu.SMEM),
                  pl.BlockSpec((1, 1, n_rows), lambda i: (jnp.minimum(i + 1, nt - 1), 0, 0),
                               memory_space=pltpu.SMEM),
                  pl.BlockSpec((TM_CMB, D_MODEL), lambda i: (i, 0)),
                  pl.BlockSpec((TM_CMB, LANE), lambda i: (i, 0)),
                  pl.BlockSpec(memory_space=pl.ANY)],
        out_specs=pl.BlockSpec((TM_CMB, D_MODEL), lambda i: (i, 0)),
        out_shape=jax.ShapeDtypeStruct((n, D_MODEL), F32),
        scratch_shapes=[pltpu.VMEM((2, n_rows, ROW_SUB, LANE), F32), pltpu.SemaphoreType.DMA((2,))],
        compiler_params=pltpu.CompilerParams(dimension_semantics=("arbitrary",), vmem_limit_bytes=VMEM_LIMIT),
        name="moe_combine",
    )(dest_blocks, dest_blocks, x1, gate_pad, y_rows)


def kernel(x, norm1_g, w_in, q_norm_g, k_norm_g, cmp_pos, w_cmp1, b_cmp1, w_cmp2, b_cmp2, gm_v_norm_g, gm_w_s,
           gm_b_s, out_norm_attn_g, out_norm_gm_g, w_out, norm2_g, w_router, b_router, w_gate_up, b_gate_up,
           w_down, b_down):
    batch, seq, _ = x.shape
    n = batch * seq
    nqb = seq // Q_BLOCK
    bgn = batch * N_KV
    x2 = x.reshape(n, D_MODEL)

    c_gate = D_ATTN + 6 * D_KV
    w_r = jnp.concatenate([w_in[:, :c_gate], w_in[:, c_gate + N_GATE:], w_in[:, c_gate:c_gate + N_GATE],
                           jnp.zeros((D_MODEL, LANE - N_GATE), F32)], axis=1).astype(BF16)
    q, kc_raw, vc_raw, ks, vs, kw, vw, gates, u_act, v_act = _inproj(x2, norm1_g, w_r, q_norm_g, k_norm_g,
                                                                     gm_v_norm_g)

    kc = _compress(kc_raw, cmp_pos[0], w_cmp1[0], b_cmp1[0], w_cmp2[0], b_cmp2[0], k_norm_g[0], batch, seq, True)
    vc = _compress(vc_raw, cmp_pos[1], w_cmp1[1], b_cmp1[1], w_cmp2[1], b_cmp2[1], k_norm_g[0], batch, seq, False)

    nq = N_REP * Q_BLOCK
    ncmp = seq // CMP_STRIDE

    nsel = seq // SEL_BLOCK
    oh_w = -(-nsel // LANE) * LANE

    def with_pos(k, pos, one_hot=False):
        feat = np.zeros((pos.shape[0], LANE - HEAD_DIM + (oh_w if one_hot else 0)), np.float32)
        feat[:, 0] = feat[:, 1] = pos // SEL_BLOCK
        feat[:, 2] = feat[:, 3] = pos % SEL_BLOCK
        if one_hot:
            feat[np.arange(pos.shape[0]), LANE - HEAD_DIM + pos // SEL_BLOCK] = 1.0
        feat = jnp.broadcast_to(jnp.asarray(feat, BF16)[None], (bgn,) + feat.shape)
        return jnp.concatenate([k.astype(BF16), feat], axis=-1)

    def front_pad(a, axis):
        pad = [(0, 0)] * a.ndim
        pad[axis] = (WINDOW, 0)
        return jnp.pad(a, pad)

    def tok_major(a):
        return a.reshape(batch, seq, N_KV, HEAD_DIM).transpose(0, 2, 1, 3).reshape(bgn, seq, HEAD_DIM)

    def feat_major(a):
        return a.reshape(batch, seq, N_KV, HEAD_DIM).transpose(0, 2, 3, 1).reshape(bgn, HEAD_DIM, seq).astype(BF16)

    head = np.arange(N_KV)[:, None] * N_REP + np.arange(nq)[None, :] // Q_BLOCK
    coef = np.exp2(-(head + 1.0)) * LOG2E
    c_hi = coef.astype(BF16).astype(np.float64)
    c_lo = (coef - c_hi).astype(BF16).astype(np.float64)
    qrows = np.zeros((N_KV, LANE - HEAD_DIM, nq), np.float32)
    qrows[:, 0], qrows[:, 1], qrows[:, 2], qrows[:, 3] = SEL_BLOCK * c_hi, SEL_BLOCK * c_lo, c_hi, c_lo
    qrows = jnp.broadcast_to(jnp.asarray(qrows, BF16)[None, :, None], (batch, N_KV, nqb, LANE - HEAD_DIM, nq))
    qt = (q.reshape(batch, nqb, Q_BLOCK, N_KV, N_REP, HEAD_DIM).transpose(0, 3, 1, 5, 4, 2)
          .reshape(batch, N_KV, nqb, HEAD_DIM, nq).astype(BF16))
    qt = jnp.concatenate([qt, qrows], axis=3).reshape(bgn, nqb, LANE, nq)
    gt = (gates[:, :N_GATE].reshape(batch, nqb, Q_BLOCK, N_KV, N_REP, 3).transpose(0, 3, 1, 5, 4, 2)
          .reshape(bgn, nqb, 3, nq))
    pos_t = np.arange(seq)
    pos_c = np.arange(ncmp) * CMP_STRIDE + (CMP_LEN - 1)
    kc_b = with_pos(kc.reshape(bgn, ncmp, HEAD_DIM), pos_c)
    vct = vc.reshape(bgn, ncmp, HEAD_DIM).transpose(0, 2, 1).astype(BF16)
    c0 = np.arange(ncmp)[None, :] * CMP_STRIDE
    n0 = np.arange(seq // SEL_BLOCK)[:, None] * SEL_BLOCK
    ovt = np.clip(np.minimum(c0 + CMP_LEN, n0 + SEL_BLOCK) - np.maximum(c0, n0), 0, None) / CMP_LEN
    pos_w = np.maximum(np.arange(seq + WINDOW) - WINDOW, 0)
    ot = _attention(qt, kc_b, vct, with_pos(tok_major(ks), pos_t, one_hot=True), feat_major(vs),
                    with_pos(front_pad(tok_major(kw), 1), pos_w), front_pad(feat_major(vw), 2), gt,
                    jnp.asarray(ovt, BF16))
    o_attn = (ot.reshape(batch, N_KV, nqb, HEAD_DIM, N_REP, Q_BLOCK).transpose(0, 2, 5, 1, 4, 3)
              .reshape(n, D_ATTN))

    bias_full = jnp.repeat(gm_b_s.T, GM_GROUP_DIM, axis=1)
    wr_pad = jnp.concatenate([w_router, jnp.zeros((D_MODEL, LANE - N_EXPERTS), F32)], axis=1)
    br_pad = jnp.concatenate([b_router, jnp.full((LANE - N_EXPERTS,), NEG, F32)]).reshape(1, LANE)
    x1, xn3, idx_pad, gate_pad = _mix(x2, o_attn, u_act, v_act, gm_w_s, bias_full,
                                     out_norm_attn_g.reshape(1, D_ATTN), out_norm_gm_g.reshape(1, D_GM),
                                     w_out.astype(BF16), norm2_g.reshape(1, D_MODEL), wr_pad, br_pad)

    s_tot = n * TOP_K
    nb = s_tot // BM_MOE + N_EXPERTS
    e_flat = idx_pad[:, :TOP_K].reshape(s_tot)
    onehot = (e_flat[:, None] == jnp.arange(N_EXPERTS, dtype=jnp.int32)[None, :]).astype(jnp.int32)
    csum = jnp.cumsum(onehot, axis=0)
    rank = jnp.sum(csum * onehot, axis=1) - 1
    counts = csum[-1]
    padded = ((counts + BM_MOE - 1) // BM_MOE) * BM_MOE
    pad_end = jnp.cumsum(padded)
    pad_start = pad_end - padded
    dest = pad_start[e_flat] + rank
    tok_flat = jnp.arange(s_tot, dtype=jnp.int32) // TOP_K
    tok_buf = jnp.zeros((nb * BM_MOE,), jnp.int32).at[dest].set(tok_flat)
    blk_start = jnp.arange(nb, dtype=jnp.int32) * BM_MOE
    blk_expert = jnp.minimum(jnp.sum((blk_start[:, None] >= pad_end[None, :]).astype(jnp.int32), axis=1),
                             N_EXPERTS - 1)
    blk_valid = (blk_start < pad_end[-1]).astype(jnp.int32)

    bg = b_gate_up[:, 0::2].reshape(N_EXPERTS, 1, D_EXPERT)
    bl = b_gate_up[:, 1::2].reshape(N_EXPERTS, 1, D_EXPERT)
    y_rows = _moe(blk_expert, blk_valid, tok_buf.reshape(nb, 1, BM_MOE), xn3, w_gate_up, bg, bl, w_down,
                  b_down.reshape(N_EXPERTS, 1, D_MODEL))

    dest_blocks = (dest.reshape(n // TM_CMB, TM_CMB, TOP_K).transpose(0, 2, 1)
                   .reshape(n // TM_CMB, 1, TOP_K * TM_CMB).astype(jnp.int32))
    out = _combine(dest_blocks, x1, gate_pad, y_rows)
    return out.reshape(batch, seq, D_MODEL)
```

```python
import functools

import jax
import jax.numpy as jnp
import numpy as np
from jax import lax
from jax.experimental import pallas as pl
from jax.experimental.pallas import tpu as pltpu

F32 = jnp.float32
BF16 = jnp.bfloat16
HIGHEST = lax.Precision.HIGHEST
_NT = (((1,), (1,)), ((), ()))

D_MODEL = 1024
N_HEADS = 8
HEAD_DIM = 64
N_KV = 2
N_REP = N_HEADS // N_KV
D_ATTN = N_HEADS * HEAD_DIM
D_KV = N_KV * HEAD_DIM
N_GM_GROUPS = 8
GM_GROUP_DIM = 64
D_GM = N_GM_GROUPS * GM_GROUP_DIM
N_GATE = 3 * N_HEADS
CMP_LEN = 32
CMP_STRIDE = 16
CMP_HIDDEN = 128
SEL_BLOCK = 64
N_SEL = 16
WINDOW = 512
Q_BLOCK = 128
FORCE_BONUS = 1.0e4
GM_CHUNK = 128
N_EXPERTS = 32
TOP_K = 4
D_EXPERT = 1024
SWIGLU_LIMIT = 7.0
SWIGLU_ALPHA = 1.702
EPS = 1e-6
NEG = -1.0e30
LOG2E = 1.4426950408889634

LANE = 128
ROW_SUB = D_MODEL // LANE
VMEM_LIMIT = 48 * 1024 * 1024
VMEM_LIMIT_MOE = 56 * 1024 * 1024

_C_Q = 0
_C_KC = _C_Q + D_ATTN
_C_VC = _C_KC + D_KV
_C_KS = _C_VC + D_KV
_C_VS = _C_KS + D_KV
_C_KW = _C_VS + D_KV
_C_VW = _C_KW + D_KV
_C_U = _C_VW + D_KV
_C_V = _C_U + D_GM
_C_G = _C_V + D_GM
D_IN_PAD = _C_G + LANE

TM_IN = 256
TM_MIX = 256
KC_SEL = 512
BM_MOE = 256
TM_CMB = 128


def _rms(x, eps=EPS):
    return lax.rsqrt(jnp.mean(x * x, axis=-1, keepdims=True) + eps)


def _inproj_kernel(x_ref, g1_ref, w_ref, qg_ref, kg_ref, vg_ref,
                   q_ref, kc_ref, vc_ref, ks_ref, vs_ref, kw_ref, vw_ref, gate_ref, u_ref, v_ref):
    x = x_ref[...]
    h = (x * _rms(x)) * g1_ref[...]
    z = jnp.dot(h.astype(BF16), w_ref[...], preferred_element_type=F32)

    def head_norm(col0, n, gain, scale):
        outs = []
        for i in range(n):
            sl = z[:, col0 + i * HEAD_DIM: col0 + (i + 1) * HEAD_DIM]
            outs.append((sl * _rms(sl)) * gain * scale)
        return jnp.concatenate(outs, axis=-1)

    q_ref[...] = head_norm(_C_Q, N_HEADS, qg_ref[...], HEAD_DIM ** -0.5 * LOG2E)
    kc_ref[...] = z[:, _C_KC:_C_KC + D_KV]
    vc_ref[...] = z[:, _C_VC:_C_VC + D_KV]
    ks_ref[...] = head_norm(_C_KS, N_KV, kg_ref[1:2, :], 1.0)
    vs_ref[...] = z[:, _C_VS:_C_VS + D_KV]
    kw_ref[...] = head_norm(_C_KW, N_KV, kg_ref[2:3, :], 1.0)
    vw_ref[...] = z[:, _C_VW:_C_VW + D_KV]
    gate_ref[...] = jax.nn.sigmoid(z[:, _C_G:_C_G + LANE])
    u_ref[...] = jax.nn.gelu(z[:, _C_U:_C_U + D_GM])
    gv = jax.nn.gelu(z[:, _C_V:_C_V + D_GM])
    v_ref[...] = (gv * _rms(gv)) * vg_ref[...]


def _inproj(x2, norm1_g, w_r, q_norm_g, k_norm_g, gm_v_norm_g):
    n = x2.shape[0]
    row = lambda c: pl.BlockSpec((TM_IN, c), lambda i: (i, 0))
    full = lambda a: pl.BlockSpec(a.shape, lambda i: (0,) * a.ndim)
    g1 = norm1_g.reshape(1, D_MODEL)
    qg = q_norm_g.reshape(1, HEAD_DIM)
    vg = gm_v_norm_g.reshape(1, D_GM)
    widths = (D_ATTN, D_KV, D_KV, D_KV, D_KV, D_KV, D_KV, LANE, D_GM, D_GM)
    return pl.pallas_call(
        _inproj_kernel,
        grid=(n // TM_IN,),
        in_specs=[row(D_MODEL), full(g1), full(w_r), full(qg), full(k_norm_g), full(vg)],
        out_specs=[row(c) for c in widths],
        out_shape=[jax.ShapeDtypeStruct((n, c), F32) for c in widths],
        compiler_params=pltpu.CompilerParams(dimension_semantics=("arbitrary",), vmem_limit_bytes=VMEM_LIMIT),
        name="inproj",
    )(x2, g1, w_r, qg, k_norm_g, vg)


def _compress_kernel(a_ref, pos_ref, w1_ref, w1a_ref, w1b_ref, b1_ref, w2_ref, b2_ref, kg_ref, o_ref, *, norm):
    a = a_ref[0]
    nseg = a.shape[0]
    c = jnp.dot(pos_ref[...], w1_ref[...], precision=HIGHEST, preferred_element_type=F32)[0:1] + b1_ref[...]
    row = lax.broadcasted_iota(jnp.int32, (nseg, 1), 0)
    for g in range(N_KV):
        pa = jnp.dot(a, w1a_ref[g], precision=HIGHEST, preferred_element_type=F32)
        pb = jnp.dot(a, w1b_ref[g], precision=HIGHEST, preferred_element_type=F32)
        hid = jax.nn.gelu(pa + pltpu.roll(pb, nseg - 1, 0) + c)
        out = jnp.dot(hid, w2_ref[...], precision=HIGHEST, preferred_element_type=F32) + b2_ref[...]
        if norm:
            out = (out * _rms(out)) * kg_ref[...]
        o_ref[0, g] = jnp.where(row < nseg - 1, out, 0.0)


def _compress(raw, pos, w1, b1, w2, b2, gain, batch, seq, norm):
    nseg = seq // CMP_STRIDE
    half = CMP_STRIDE * HEAD_DIM
    a = raw.reshape(batch, nseg, CMP_STRIDE * D_KV)
    pos8 = jnp.broadcast_to(pos.reshape(1, CMP_LEN * HEAD_DIM), (8, CMP_LEN * HEAD_DIM))

    def expand(wh):
        wh = wh.reshape(CMP_STRIDE, HEAD_DIM, CMP_HIDDEN)
        z = jnp.zeros((N_KV, CMP_STRIDE, N_KV, HEAD_DIM, CMP_HIDDEN), F32)
        for g in range(N_KV):
            z = z.at[g, :, g].set(wh)
        return z.reshape(N_KV, CMP_STRIDE * D_KV, CMP_HIDDEN)

    w1a, w1b = expand(w1[:half]), expand(w1[half:])
    b1r, b2r, gr = b1.reshape(1, CMP_HIDDEN), b2.reshape(1, HEAD_DIM), gain.reshape(1, HEAD_DIM)
    full = lambda t: pl.BlockSpec(t.shape, lambda i: (0,) * t.ndim)
    return pl.pallas_call(
        functools.partial(_compress_kernel, norm=norm),
        grid=(batch,),
        in_specs=[pl.BlockSpec((1, nseg, CMP_STRIDE * D_KV), lambda i: (i, 0, 0)),
                  full(pos8), full(w1), full(w1a), full(w1b), full(b1r), full(w2), full(b2r), full(gr)],
        out_specs=pl.BlockSpec((1, N_KV, nseg, HEAD_DIM), lambda i: (i, 0, 0, 0)),
        out_shape=jax.ShapeDtypeStruct((batch, N_KV, nseg, HEAD_DIM), F32),
        compiler_params=pltpu.CompilerParams(dimension_semantics=("arbitrary",), vmem_limit_bytes=VMEM_LIMIT),
        name="compress_k" if norm else "compress_v",
    )(a, pos8, w1, w1a, w1b, b1r, w2, b2r, gr)


def _attn_kernel(qt_ref, kc_ref, vct_ref, ks_ref, vst_ref, kw_ref, vwt_ref, g_ref, ovt_ref, o_ref,
                 qs_ref, s0_ref, s1_ref, p0_ref, p1_ref, st_ref, acc_ref):
    qb = pl.program_id(1)
    nq = N_REP * Q_BLOCK
    q0 = qb * Q_BLOCK
    qt = qt_ref[0, 0]
    ql = lax.broadcasted_iota(jnp.int32, (1, nq), 1) % Q_BLOCK
    t_row = (q0 + ql).astype(F32)
    m_init = 0.5 * NEG

    def online(s, m, l):
        m_new = jnp.maximum(m, jnp.max(s, axis=0, keepdims=True))
        alpha = jnp.exp2(m - m_new)
        p = jnp.exp2(s - m_new)
        return p, m_new, alpha, alpha * l + jnp.sum(p, axis=0, keepdims=True)

    def inv(l):
        return jnp.where(l > 0.0, 1.0 / l, 0.0)

    m0 = jnp.full((1, nq), m_init, F32)
    l0 = jnp.zeros((1, nq), F32)
    a0 = jnp.zeros((HEAD_DIM, nq), F32)

    ncmp = kc_ref.shape[1]
    s = jnp.dot(kc_ref[0], qt, preferred_element_type=F32)
    c_end = (lax.broadcasted_iota(jnp.int32, (ncmp, 1), 0) * CMP_STRIDE + (CMP_LEN - 1)).astype(F32)
    p, _, _, l = online(jnp.where(c_end <= t_row, s, NEG), m0, l0)
    p = p * inv(l)
    o_cmp = jnp.dot(vct_ref[0], p.astype(BF16), preferred_element_type=F32)

    psum = p[:, 0:Q_BLOCK]
    for r in range(1, N_REP):
        psum = psum + p[:, r * Q_BLOCK:(r + 1) * Q_BLOCK]
    nsel = ovt_ref.shape[0]
    p_hi = psum.astype(BF16)
    p_lo = (psum - p_hi.astype(F32)).astype(BF16)
    imp = (jnp.dot(ovt_ref[...], p_hi, preferred_element_type=F32)
           + jnp.dot(ovt_ref[...], p_lo, preferred_element_type=F32))
    n_col = lax.broadcasted_iota(jnp.int32, (nsel, 1), 0).astype(F32)
    n_start = n_col * SEL_BLOCK
    tq = t_row[:, 0:Q_BLOCK]
    cur = jnp.floor(tq * (1.0 / SEL_BLOCK)) * SEL_BLOCK
    forced = (n_start == cur) | (n_start == 0.0)
    valid = n_start <= tq
    imp = jnp.where(forced, imp + FORCE_BONUS, imp)
    imp = jnp.where(valid, imp, NEG)
    sel = jnp.zeros((nsel, Q_BLOCK), F32)
    for _ in range(min(N_SEL, nsel)):
        mx = jnp.max(imp, axis=0, keepdims=True)
        first = jnp.min(jnp.where(imp == mx, n_col, float(nsel)), axis=0, keepdims=True)
        hit = n_col == first
        sel = jnp.where(hit, 1.0, sel)
        imp = jnp.where(hit, -jnp.inf, imp)
    selb = jnp.where(valid & (sel > 0.0), 0.0, NEG).astype(BF16)
    qs_ref[0:LANE, :] = qt
    qs_ref[LANE:LANE + nsel, :] = jnp.concatenate([selb] * N_REP, axis=1)
    if qs_ref.shape[0] > LANE + nsel:
        qs_ref[LANE + nsel:, :] = jnp.zeros((qs_ref.shape[0] - LANE - nsel, nq), BF16)

    def attend(k_blk, vt_blk, q_op, bias, carry):
        m, l, acc = carry
        s = jnp.dot(k_blk, q_op, preferred_element_type=F32)
        if bias is not None:
            s = s + bias
        p, m, alpha, l = online(s, m, l)
        pv = jnp.dot(vt_blk, p.astype(BF16), preferred_element_type=F32)
        return m, l, alpha * acc + pv

    seq = ks_ref.shape[1]

    def scores(j):
        k0 = pl.multiple_of(jnp.minimum(j * KC_SEL, seq - KC_SEL), KC_SEL)
        s = jnp.dot(ks_ref[0, pl.ds(k0, KC_SEL), :], qs_ref[...], preferred_element_type=F32)
        return s, jnp.max(s, axis=0, keepdims=True)

    def values(j, p):
        k0 = pl.multiple_of(jnp.maximum(j, 0) * KC_SEL, KC_SEL)
        return jnp.dot(vst_ref[0, :, pl.ds(k0, KC_SEL)], p, preferred_element_type=F32)

    def stage(j, s_cur, s_nxt, p_cur, p_prv):
        m, l, alpha_prev, mx = st_ref[0:1, :], st_ref[1:2, :], st_ref[2:3, :], st_ref[3:4, :]
        m_new = jnp.maximum(m, mx)
        alpha = jnp.exp2(m - m_new)
        p = jnp.exp2(s_cur[...] - m_new)
        acc_ref[...] = alpha_prev * acc_ref[...] + values(j - 1, p_prv[...])
        p_cur[...] = p.astype(BF16)
        s_next, mx_next = scores(j + 1)
        s_nxt[...] = s_next
        st_ref[0:1, :] = m_new
        st_ref[1:2, :] = alpha * l + jnp.sum(p, axis=0, keepdims=True)
        st_ref[2:3, :] = alpha
        st_ref[3:4, :] = mx_next

    n_full = q0 // KC_SEL
    s_first, mx_first = scores(0)

    @pl.when(n_full % 2 == 0)
    def _():
        s0_ref[...] = s_first

    @pl.when(n_full % 2 == 1)
    def _():
        s1_ref[...] = s_first

    p0_ref[...] = jnp.zeros(p0_ref.shape, BF16)
    p1_ref[...] = jnp.zeros(p1_ref.shape, BF16)
    st_ref[0:1, :] = m0
    st_ref[1:2, :] = l0
    st_ref[2:3, :] = jnp.ones((1, nq), F32)
    st_ref[3:4, :] = mx_first
    acc_ref[...] = a0

    def sel_body(j, carry):
        @pl.when((n_full - j) % 2 == 0)
        def _():
            stage(j, s0_ref, s1_ref, p0_ref, p1_ref)

        @pl.when((n_full - j) % 2 == 1)
        def _():
            stage(j, s1_ref, s0_ref, p1_ref, p0_ref)
        return carry

    lax.fori_loop(0, n_full, sel_body, 0)
    pos_last = (n_full * KC_SEL + lax.broadcasted_iota(jnp.int32, (KC_SEL, 1), 0)).astype(F32)
    p, _, alpha, l_sel = online(s0_ref[...] + jnp.where(pos_last <= t_row, 0.0, NEG), st_ref[0:1, :], st_ref[1:2, :])
    acc = st_ref[2:3, :] * acc_ref[...] + values(n_full - 1, p1_ref[...])
    o_sel = alpha * acc + values(n_full, p.astype(BF16))

    n_wk = WINDOW + Q_BLOCK
    kk = lax.broadcasted_iota(jnp.int32, (n_wk, 1), 0)
    in_win = (kk - WINDOW <= ql) & (kk > ql) & (kk >= WINDOW - q0)
    _, l_win, o_win = attend(kw_ref[0, pl.ds(pl.multiple_of(q0, Q_BLOCK), n_wk), :],
                             vwt_ref[0, :, pl.ds(pl.multiple_of(q0, Q_BLOCK), n_wk)], qt,
                             jnp.where(in_win, 0.0, NEG), (m0, l0, a0))

    gt = g_ref[0, 0]
    o_ref[0, 0] = (gt[0:1] * o_cmp + gt[1:2] * (o_sel * inv(l_sel)) + gt[2:3] * (o_win * inv(l_win)))


def _attention(qt, kc, vct, ks, vst, kw, vwt, gt, ovt):
    bgn, nqb = qt.shape[0], qt.shape[1]
    seq = ks.shape[1]
    nq = N_REP * Q_BLOCK
    per_bg = lambda a: pl.BlockSpec((1,) + a.shape[1:], lambda b, i: (b,) + (0,) * (a.ndim - 1))
    return pl.pallas_call(
        _attn_kernel,
        grid=(bgn, nqb),
        in_specs=[pl.BlockSpec((1, 1, LANE, nq), lambda b, i: (b, i, 0, 0)),
                  per_bg(kc), per_bg(vct), per_bg(ks), per_bg(vst), per_bg(kw), per_bg(vwt),
                  pl.BlockSpec((1, 1, 3, nq), lambda b, i: (b, i, 0, 0)),
                  pl.BlockSpec(ovt.shape, lambda b, i: (0, 0))],
        out_specs=pl.BlockSpec((1, 1, HEAD_DIM, nq), lambda b, i: (b, i, 0, 0)),
        out_shape=jax.ShapeDtypeStruct((bgn, nqb, HEAD_DIM, nq), F32),
        scratch_shapes=[pltpu.VMEM((ks.shape[2], nq), BF16),
                        pltpu.VMEM((KC_SEL, nq), F32), pltpu.VMEM((KC_SEL, nq), F32),
                        pltpu.VMEM((KC_SEL, nq), BF16), pltpu.VMEM((KC_SEL, nq), BF16),
                        pltpu.VMEM((8, nq), F32), pltpu.VMEM((HEAD_DIM, nq), F32)],
        compiler_params=pltpu.CompilerParams(dimension_semantics=("arbitrary", "arbitrary"),
                                             vmem_limit_bytes=VMEM_LIMIT),
        name="nsa_attention",
    )(qt, kc, vct, ks, vst, kw, vwt, gt, ovt)


def _mix_kernel(x_ref, oa_ref, u_ref, v_ref, ws_ref, bs_ref, ga_ref, gg_ref, wo_ref, g2_ref, wr_ref, br_ref,
                x1_ref, xn_ref, idx_ref, gate_ref):
    tm = x_ref.shape[0]
    rr = lax.broadcasted_iota(jnp.int32, (GM_CHUNK, GM_CHUNK), 0)
    cc = lax.broadcasted_iota(jnp.int32, (GM_CHUNK, GM_CHUNK), 1)
    grp = lax.broadcasted_iota(jnp.int32, (1, D_GM), 1) // GM_GROUP_DIM
    ws = [jnp.where(rr >= cc, ws_ref[g], 0.0).astype(BF16) for g in range(N_GM_GROUPS)]
    ys = []
    for c in range(tm // GM_CHUNK):
        vch = v_ref[c * GM_CHUNK:(c + 1) * GM_CHUNK, :].astype(BF16)
        y = bs_ref[...]
        for g in range(N_GM_GROUPS):
            y = y + jnp.where(grp == g, jnp.dot(ws[g], vch, preferred_element_type=F32), 0.0)
        ys.append(y)
    o_gm = u_ref[...] * jnp.concatenate(ys, axis=0)
    o_at = oa_ref[...]
    mixed = jnp.concatenate([(o_at * _rms(o_at)) * ga_ref[...], (o_gm * _rms(o_gm)) * gg_ref[...]], axis=-1)
    x1 = x_ref[...] + jnp.dot(mixed.astype(BF16), wo_ref[...], preferred_element_type=F32)
    x1_ref[...] = x1
    xn = (x1 * _rms(x1)) * g2_ref[...]
    for s in range(ROW_SUB):
        xn_ref[:, s, :] = xn[:, s * LANE:(s + 1) * LANE]
    logits = jnp.dot(xn, wr_ref[...], precision=HIGHEST, preferred_element_type=F32) + br_ref[...]
    lane = lax.broadcasted_iota(jnp.int32, (1, LANE), 1).astype(F32)
    idx_out = jnp.zeros((tm, LANE), F32)
    val_out = jnp.zeros((tm, LANE), F32)
    vals = []
    for k in range(TOP_K):
        mx = jnp.max(logits, axis=-1, keepdims=True)
        first = jnp.min(jnp.where(logits == mx, lane, float(LANE)), axis=-1, keepdims=True)
        logits = jnp.where(lane == first, -jnp.inf, logits)
        idx_out = jnp.where(lane == float(k), first, idx_out)
        vals.append(mx)
    es = [jnp.exp(v - vals[0]) for v in vals]
    den = es[0] + es[1] + es[2] + es[3]
    for k in range(TOP_K):
        val_out = jnp.where(lane == float(k), es[k] / den, val_out)
    idx_ref[...] = idx_out.astype(jnp.int32)
    gate_ref[...] = val_out


def _mix(x2, o_attn, u_act, v_act, gm_w_s, bias_full, ga, gg, w_out_b, g2, wr_pad, br_pad):
    n = x2.shape[0]
    row = lambda c: pl.BlockSpec((TM_MIX, c), lambda i: (i, 0))
    full = lambda a: pl.BlockSpec(a.shape, lambda i: (0,) * a.ndim)
    return pl.pallas_call(
        _mix_kernel,
        grid=(n // TM_MIX,),
        in_specs=[row(D_MODEL), row(D_ATTN), row(D_GM), row(D_GM), full(gm_w_s), full(bias_full), full(ga), full(gg),
                  full(w_out_b), full(g2), full(wr_pad), full(br_pad)],
        out_specs=[row(D_MODEL), pl.BlockSpec((TM_MIX, ROW_SUB, LANE), lambda i: (i, 0, 0)), row(LANE), row(LANE)],
        out_shape=[jax.ShapeDtypeStruct((n, D_MODEL), F32), jax.ShapeDtypeStruct((n, ROW_SUB, LANE), F32),
                   jax.ShapeDtypeStruct((n, LANE), jnp.int32), jax.ShapeDtypeStruct((n, LANE), F32)],
        compiler_params=pltpu.CompilerParams(dimension_semantics=("arbitrary",), vmem_limit_bytes=VMEM_LIMIT),
        name="mix_outproj_router",
    )(x2, o_attn, u_act, v_act, gm_w_s, bias_full, ga, gg, w_out_b, g2, wr_pad, br_pad)


def _row_gather(idx_ref, n_rows, src_hbm, dst_ref, sem):
    def start():
        for r in range(n_rows):
            pltpu.make_async_copy(src_hbm.at[pl.ds(idx_ref[0, 0, r], 1)], dst_ref.at[pl.ds(r, 1)], sem).start()

    def wait():
        pltpu.make_async_copy(src_hbm.at[pl.ds(0, n_rows)], dst_ref, sem).wait()

    return start, wait


def _rows_to_matrix(ref):
    return jnp.concatenate([ref[:, s, :] for s in range(ROW_SUB)], axis=1)


def _moe_kernel(be_ref, bv_ref, tok_ref, tok_next_ref, x_hbm, wgu_ref, bg_ref, bl_ref, wd_ref, bd_ref,
                o_ref, xbuf, sems, wt_s, wg_s, wl_s, wd_s):
    i = pl.program_id(0)
    nb = pl.num_programs(0)
    slot = i % 2
    start_cur, wait_cur = _row_gather(tok_ref, BM_MOE, x_hbm, xbuf.at[slot], sems.at[slot])
    start_next, _ = _row_gather(tok_next_ref, BM_MOE, x_hbm, xbuf.at[1 - slot], sems.at[1 - slot])

    prev = jnp.maximum(i - 1, 0)

    @pl.when((i == 0) & (bv_ref[0] == 1))
    def _():
        start_cur()

    @pl.when((bv_ref[i] == 1) & ((i == 0) | (be_ref[i] != be_ref[prev])))
    def _():
        tc = wt_s.shape[1]
        for c in range(2 * D_EXPERT // tc):
            wt = wgu_ref[0, :, c * tc:(c + 1) * tc].T
            for j in range(ROW_SUB):
                wt_s[j] = wt[:, j * LANE:(j + 1) * LANE]
            for first, dst in ((0, wg_s), (1, wl_s)):
                half = jnp.concatenate([wt_s[j, pl.ds(first, tc // 2, stride=2), :] for j in range(ROW_SUB)], axis=1)
                dst[c * tc // 2:(c + 1) * tc // 2, :] = half.astype(BF16)
        wd_s[...] = wd_ref[0].astype(BF16)

    @pl.when(bv_ref[i] == 1)
    def _():
        start_next()
        wait_cur()
        xb = _rows_to_matrix(xbuf.at[slot]).astype(BF16)
        hg = lax.dot_general(xb, wg_s[...], _NT, preferred_element_type=F32) + bg_ref[0]
        hl = lax.dot_general(xb, wl_s[...], _NT, preferred_element_type=F32) + bl_ref[0]
        hg = jnp.minimum(hg, SWIGLU_LIMIT)
        hl = jnp.clip(hl, -SWIGLU_LIMIT, SWIGLU_LIMIT)
        a = hg * jax.nn.sigmoid(SWIGLU_ALPHA * hg) * (hl + 1.0)
        out = jnp.dot(a.astype(BF16), wd_s[...], preferred_element_type=F32) + bd_ref[0]
        for s in range(ROW_SUB):
            o_ref[:, s, :] = out[:, s * LANE:(s + 1) * LANE]

    @pl.when((bv_ref[i] == 0) & (i > 0) & (bv_ref[prev] == 1))
    def _():
        wait_cur()

    @pl.when(bv_ref[i] == 0)
    def _():
        o_ref[...] = jnp.zeros(o_ref.shape, F32)


def _moe(blk_expert, blk_valid, tok_blocks, xn3, w_gate_up, bg, bl, w_down, bd):
    nb = blk_expert.shape[0]
    per_e = lambda a: pl.BlockSpec((1,) + a.shape[1:], lambda i, be, bv: (be[i],) + (0,) * (a.ndim - 1))
    grid_spec = pltpu.PrefetchScalarGridSpec(
        num_scalar_prefetch=2,
        grid=(nb,),
        in_specs=[pl.BlockSpec((1, 1, BM_MOE), lambda i, be, bv: (i, 0, 0), memory_space=pltpu.SMEM),
                  pl.BlockSpec((1, 1, BM_MOE), lambda i, be, bv: (jnp.minimum(i + 1, nb - 1), 0, 0),
                               memory_space=pltpu.SMEM),
                  pl.BlockSpec(memory_space=pl.ANY),
                  per_e(w_gate_up), per_e(bg), per_e(bl), per_e(w_down), per_e(bd)],
        out_specs=pl.BlockSpec((BM_MOE, ROW_SUB, LANE), lambda i, be, bv: (i, 0, 0)),
        scratch_shapes=[pltpu.VMEM((2, BM_MOE, ROW_SUB, LANE), F32), pltpu.SemaphoreType.DMA((2,)),
                        pltpu.VMEM((ROW_SUB, 256, LANE), F32), pltpu.VMEM((D_EXPERT, D_MODEL), BF16),
                        pltpu.VMEM((D_EXPERT, D_MODEL), BF16), pltpu.VMEM((D_EXPERT, D_MODEL), BF16)],
    )
    return pl.pallas_call(
        _moe_kernel,
        grid_spec=grid_spec,
        out_shape=jax.ShapeDtypeStruct((nb * BM_MOE, ROW_SUB, LANE), F32),
        compiler_params=pltpu.CompilerParams(dimension_semantics=("arbitrary",), vmem_limit_bytes=VMEM_LIMIT_MOE),
        name="moe_experts",
    )(blk_expert, blk_valid, tok_blocks, tok_blocks, xn3, w_gate_up, bg, bl, w_down, bd)


def _combine_kernel(dest_ref, dest_next_ref, x1_ref, gate_ref, y_hbm, o_ref, buf, sems):
    i = pl.program_id(0)
    slot = i % 2
    n_rows = TOP_K * TM_CMB
    start_cur, wait_cur = _row_gather(dest_ref, n_rows, y_hbm, buf.at[slot], sems.at[slot])
    start_next, _ = _row_gather(dest_next_ref, n_rows, y_hbm, buf.at[1 - slot], sems.at[1 - slot])

    @pl.when(i == 0)
    def _():
        start_cur()

    @pl.when(i + 1 < pl.num_programs(0))
    def _():
        start_next()

    wait_cur()
    gate = gate_ref[...]
    gk = [jnp.broadcast_to(gate[:, k:k + 1], (TM_CMB, LANE)) for k in range(TOP_K)]
    rows = buf.at[slot]
    cols = []
    for s in range(ROW_SUB):
        acc = x1_ref[:, s * LANE:(s + 1) * LANE]
        for k in range(TOP_K):
            acc = acc + gk[k] * rows[k * TM_CMB:(k + 1) * TM_CMB, s, :]
        cols.append(acc)
    o_ref[...] = jnp.concatenate(cols, axis=1)


def _combine(dest_blocks, x1, gate_pad, y_rows):
    n = x1.shape[0]
    nt = n // TM_CMB
    n_rows = TOP_K * TM_CMB
    return pl.pallas_call(
        _combine_kernel,
        grid=(nt,),
        in_specs=[pl.BlockSpec((1, 1, n_rows), lambda i: (i, 0, 0), memory_space=pltpu.SMEM),
                  pl.BlockSpec((1, 1, n_rows), lambda i: (jnp.minimum(i + 1, nt - 1), 0, 0),
                               memory_space=pltpu.SMEM),
                  pl.BlockSpec((TM_CMB, D_MODEL), lambda i: (i, 0)),
                  pl.BlockSpec((TM_CMB, LANE), lambda i: (i, 0)),
                  pl.BlockSpec(memory_space=pl.ANY)],
        out_specs=pl.BlockSpec((TM_CMB, D_MODEL), lambda i: (i, 0)),
        out_shape=jax.ShapeDtypeStruct((n, D_MODEL), F32),
        scratch_shapes=[pltpu.VMEM((2, n_rows, ROW_SUB, LANE), F32), pltpu.SemaphoreType.DMA((2,))],
        compiler_params=pltpu.CompilerParams(dimension_semantics=("arbitrary",), vmem_limit_bytes=VMEM_LIMIT),
        name="moe_combine",
    )(dest_blocks, dest_blocks, x1, gate_pad, y_rows)


def kernel(x, norm1_g, w_in, q_norm_g, k_norm_g, cmp_pos, w_cmp1, b_cmp1, w_cmp2, b_cmp2, gm_v_norm_g, gm_w_s,
           gm_b_s, out_norm_attn_g, out_norm_gm_g, w_out, norm2_g, w_router, b_router, w_gate_up, b_gate_up,
           w_down, b_down):
    batch, seq, _ = x.shape
    n = batch * seq
    nqb = seq // Q_BLOCK
    bgn = batch * N_KV
    x2 = x.reshape(n, D_MODEL)

    c_gate = D_ATTN + 6 * D_KV
    w_r = jnp.concatenate([w_in[:, :c_gate], w_in[:, c_gate + N_GATE:], w_in[:, c_gate:c_gate + N_GATE],
                           jnp.zeros((D_MODEL, LANE - N_GATE), F32)], axis=1).astype(BF16)
    q, kc_raw, vc_raw, ks, vs, kw, vw, gates, u_act, v_act = _inproj(x2, norm1_g, w_r, q_norm_g, k_norm_g,
                                                                     gm_v_norm_g)

    kc = _compress(kc_raw, cmp_pos[0], w_cmp1[0], b_cmp1[0], w_cmp2[0], b_cmp2[0], k_norm_g[0], batch, seq, True)
    vc = _compress(vc_raw, cmp_pos[1], w_cmp1[1], b_cmp1[1], w_cmp2[1], b_cmp2[1], k_norm_g[0], batch, seq, False)

    nq = N_REP * Q_BLOCK
    ncmp = seq // CMP_STRIDE

    nsel = seq // SEL_BLOCK
    oh_w = -(-nsel // LANE) * LANE

    def with_pos(k, pos, one_hot=False):
        feat = np.zeros((pos.shape[0], LANE - HEAD_DIM + (oh_w if one_hot else 0)), np.float32)
        feat[:, 0] = feat[:, 1] = pos // SEL_BLOCK
        feat[:, 2] = feat[:, 3] = pos % SEL_BLOCK
        if one_hot:
            feat[np.arange(pos.shape[0]), LANE - HEAD_DIM + pos // SEL_BLOCK] = 1.0
        feat = jnp.broadcast_to(jnp.asarray(feat, BF16)[None], (bgn,) + feat.shape)
        return jnp.concatenate([k.astype(BF16), feat], axis=-1)

    def front_pad(a, axis):
        pad = [(0, 0)] * a.ndim
        pad[axis] = (WINDOW, 0)
        return jnp.pad(a, pad)

    def tok_major(a):
        return a.reshape(batch, seq, N_KV, HEAD_DIM).transpose(0, 2, 1, 3).reshape(bgn, seq, HEAD_DIM)

    def feat_major(a):
        return a.reshape(batch, seq, N_KV, HEAD_DIM).transpose(0, 2, 3, 1).reshape(bgn, HEAD_DIM, seq).astype(BF16)

    head = np.arange(N_KV)[:, None] * N_REP + np.arange(nq)[None, :] // Q_BLOCK
    coef = np.exp2(-(head + 1.0)) * LOG2E
    c_hi = coef.astype(BF16).astype(np.float64)
    c_lo = (coef - c_hi).astype(BF16).astype(np.float64)
    qrows = np.zeros((N_KV, LANE - HEAD_DIM, nq), np.float32)
    qrows[:, 0], qrows[:, 1], qrows[:, 2], qrows[:, 3] = SEL_BLOCK * c_hi, SEL_BLOCK * c_lo, c_hi, c_lo
    qrows = jnp.broadcast_to(jnp.asarray(qrows, BF16)[None, :, None], (batch, N_KV, nqb, LANE - HEAD_DIM, nq))
    qt = (q.reshape(batch, nqb, Q_BLOCK, N_KV, N_REP, HEAD_DIM).transpose(0, 3, 1, 5, 4, 2)
          .reshape(batch, N_KV, nqb, HEAD_DIM, nq).astype(BF16))
    qt = jnp.concatenate([qt, qrows], axis=3).reshape(bgn, nqb, LANE, nq)
    gt = (gates[:, :N_GATE].reshape(batch, nqb, Q_BLOCK, N_KV, N_REP, 3).transpose(0, 3, 1, 5, 4, 2)
          .reshape(bgn, nqb, 3, nq))
    pos_t = np.arange(seq)
    pos_c = np.arange(ncmp) * CMP_STRIDE + (CMP_LEN - 1)
    kc_b = with_pos(kc.reshape(bgn, ncmp, HEAD_DIM), pos_c)
    vct = vc.reshape(bgn, ncmp, HEAD_DIM).transpose(0, 2, 1).astype(BF16)
    c0 = np.arange(ncmp)[None, :] * CMP_STRIDE
    n0 = np.arange(seq // SEL_BLOCK)[:, None] * SEL_BLOCK
    ovt = np.clip(np.minimum(c0 + CMP_LEN, n0 + SEL_BLOCK) - np.maximum(c0, n0), 0, None) / CMP_LEN
    pos_w = np.maximum(np.arange(seq + WINDOW) - WINDOW, 0)
    ot = _attention(qt, kc_b, vct, with_pos(tok_major(ks), pos_t, one_hot=True), feat_major(vs),
                    with_pos(front_pad(tok_major(kw), 1), pos_w), front_pad(feat_major(vw), 2), gt,
                    jnp.asarray(ovt, BF16))
    o_attn = (ot.reshape(batch, N_KV, nqb, HEAD_DIM, N_REP, Q_BLOCK).transpose(0, 2, 5, 1, 4, 3)
              .reshape(n, D_ATTN))

    bias_full = jnp.repeat(gm_b_s.T, GM_GROUP_DIM, axis=1)
    wr_pad = jnp.concatenate([w_router, jnp.zeros((D_MODEL, LANE - N_EXPERTS), F32)], axis=1)
    br_pad = jnp.concatenate([b_router, jnp.full((LANE - N_EXPERTS,), NEG, F32)]).reshape(1, LANE)
    x1, xn3, idx_pad, gate_pad = _mix(x2, o_attn, u_act, v_act, gm_w_s, bias_full,
                                     out_norm_attn_g.reshape(1, D_ATTN), out_norm_gm_g.reshape(1, D_GM),
                                     w_out.astype(BF16), norm2_g.reshape(1, D_MODEL), wr_pad, br_pad)

    s_tot = n * TOP_K
    nb = s_tot // BM_MOE + N_EXPERTS
    e_flat = idx_pad[:, :TOP_K].reshape(s_tot)
    onehot = (e_flat[:, None] == jnp.arange(N_EXPERTS, dtype=jnp.int32)[None, :]).astype(jnp.int32)
    csum = jnp.cumsum(onehot, axis=0)
    rank = jnp.sum(csum * onehot, axis=1) - 1
    counts = csum[-1]
    padded = ((counts + BM_MOE - 1) // BM_MOE) * BM_MOE
    pad_end = jnp.cumsum(padded)
    pad_start = pad_end - padded
    dest = pad_start[e_flat] + rank
    tok_flat = jnp.arange(s_tot, dtype=jnp.int32) // TOP_K
    tok_buf = jnp.zeros((nb * BM_MOE,), jnp.int32).at[dest].set(tok_flat)
    blk_start = jnp.arange(nb, dtype=jnp.int32) * BM_MOE
    blk_expert = jnp.minimum(jnp.sum((blk_start[:, None] >= pad_end[None, :]).astype(jnp.int32), axis=1),
                             N_EXPERTS - 1)
    blk_valid = (blk_start < pad_end[-1]).astype(jnp.int32)

    bg = b_gate_up[:, 0::2].reshape(N_EXPERTS, 1, D_EXPERT)
    bl = b_gate_up[:, 1::2].reshape(N_EXPERTS, 1, D_EXPERT)
    y_rows = _moe(blk_expert, blk_valid, tok_buf.reshape(nb, 1, BM_MOE), xn3, w_gate_up, bg, bl, w_down,
                  b_down.reshape(N_EXPERTS, 1, D_MODEL))

    dest_blocks = (dest.reshape(n // TM_CMB, TM_CMB, TOP_K).transpose(0, 2, 1)
                   .reshape(n // TM_CMB, 1, TOP_K * TM_CMB).astype(jnp.int32))
    out = _combine(dest_blocks, x1, gate_pad, y_rows)
    return out.reshape(batch, seq, D_MODEL)
```

```python
import functools

import jax
import jax.numpy as jnp
import numpy as np
from jax import lax
from jax.experimental import pallas as pl
from jax.experimental.pallas import tpu as pltpu

F32 = jnp.float32
BF16 = jnp.bfloat16
HIGHEST = lax.Precision.HIGHEST
_NT = (((1,), (1,)), ((), ()))

D_MODEL = 1024
N_HEADS = 8
HEAD_DIM = 64
N_KV = 2
N_REP = N_HEADS // N_KV
D_ATTN = N_HEADS * HEAD_DIM
D_KV = N_KV * HEAD_DIM
N_GM_GROUPS = 8
GM_GROUP_DIM = 64
D_GM = N_GM_GROUPS * GM_GROUP_DIM
N_GATE = 3 * N_HEADS
CMP_LEN = 32
CMP_STRIDE = 16
CMP_HIDDEN = 128
SEL_BLOCK = 64
N_SEL = 16
WINDOW = 512
Q_BLOCK = 128
FORCE_BONUS = 1.0e4
GM_CHUNK = 128
N_EXPERTS = 32
TOP_K = 4
D_EXPERT = 1024
SWIGLU_LIMIT = 7.0
SWIGLU_ALPHA = 1.702
EPS = 1e-6
NEG = -1.0e30
LOG2E = 1.4426950408889634

LANE = 128
ROW_SUB = D_MODEL // LANE
VMEM_LIMIT = 48 * 1024 * 1024
VMEM_LIMIT_MOE = 56 * 1024 * 1024

_C_Q = 0
_C_KC = _C_Q + D_ATTN
_C_VC = _C_KC + D_KV
_C_KS = _C_VC + D_KV
_C_VS = _C_KS + D_KV
_C_KW = _C_VS + D_KV
_C_VW = _C_KW + D_KV
_C_U = _C_VW + D_KV
_C_V = _C_U + D_GM
_C_G = _C_V + D_GM
D_IN_PAD = _C_G + LANE

TM_IN = 256
TM_MIX = 256
KC_SEL = 512
BM_MOE = 256
TM_CMB = 128


def _rms(x, eps=EPS):
    return lax.rsqrt(jnp.mean(x * x, axis=-1, keepdims=True) + eps)


def _inproj_kernel(x_ref, g1_ref, w_ref, qg_ref, kg_ref, vg_ref,
                   q_ref, kc_ref, vc_ref, ks_ref, vs_ref, kw_ref, vw_ref, gate_ref, u_ref, v_ref):
    x = x_ref[...]
    h = (x * _rms(x)) * g1_ref[...]
    z = jnp.dot(h.astype(BF16), w_ref[...], preferred_element_type=F32)

    def head_norm(col0, n, gain, scale):
        outs = []
        for i in range(n):
            sl = z[:, col0 + i * HEAD_DIM: col0 + (i + 1) * HEAD_DIM]
            outs.append((sl * _rms(sl)) * gain * scale)
        return jnp.concatenate(outs, axis=-1)

    q_ref[...] = head_norm(_C_Q, N_HEADS, qg_ref[...], HEAD_DIM ** -0.5 * LOG2E)
    kc_ref[...] = z[:, _C_KC:_C_KC + D_KV]
    vc_ref[...] = z[:, _C_VC:_C_VC + D_KV]
    ks_ref[...] = head_norm(_C_KS, N_KV, kg_ref[1:2, :], 1.0)
    vs_ref[...] = z[:, _C_VS:_C_VS + D_KV]
    kw_ref[...] = head_norm(_C_KW, N_KV, kg_ref[2:3, :], 1.0)
    vw_ref[...] = z[:, _C_VW:_C_VW + D_KV]
    gate_ref[...] = jax.nn.sigmoid(z[:, _C_G:_C_G + LANE])
    u_ref[...] = jax.nn.gelu(z[:, _C_U:_C_U + D_GM])
    gv = jax.nn.gelu(z[:, _C_V:_C_V + D_GM])
    v_ref[...] = (gv * _rms(gv)) * vg_ref[...]


def _inproj(x2, norm1_g, w_r, q_norm_g, k_norm_g, gm_v_norm_g):
    n = x2.shape[0]
    row = lambda c: pl.BlockSpec((TM_IN, c), lambda i: (i, 0))
    full = lambda a: pl.BlockSpec(a.shape, lambda i: (0,) * a.ndim)
    g1 = norm1_g.reshape(1, D_MODEL)
    qg = q_norm_g.reshape(1, HEAD_DIM)
    vg = gm_v_norm_g.reshape(1, D_GM)
    widths = (D_ATTN, D_KV, D_KV, D_KV, D_KV, D_KV, D_KV, LANE, D_GM, D_GM)
    return pl.pallas_call(
        _inproj_kernel,
        grid=(n // TM_IN,),
        in_specs=[row(D_MODEL), full(g1), full(w_r), full(qg), full(k_norm_g), full(vg)],
        out_specs=[row(c) for c in widths],
        out_shape=[jax.ShapeDtypeStruct((n, c), F32) for c in widths],
        compiler_params=pltpu.CompilerParams(dimension_semantics=("arbitrary",), vmem_limit_bytes=VMEM_LIMIT),
        name="inproj",
    )(x2, g1, w_r, qg, k_norm_g, vg)


def _compress_kernel(a_ref, pos_ref, w1_ref, w1a_ref, w1b_ref, b1_ref, w2_ref, b2_ref, kg_ref, o_ref, *, norm):
    a = a_ref[0]
    nseg = a.shape[0]
    c = jnp.dot(pos_ref[...], w1_ref[...], precision=HIGHEST, preferred_element_type=F32)[0:1] + b1_ref[...]
    row = lax.broadcasted_iota(jnp.int32, (nseg, 1), 0)
    for g in range(N_KV):
        pa = jnp.dot(a, w1a_ref[g], precision=HIGHEST, preferred_element_type=F32)
        pb = jnp.dot(a, w1b_ref[g], precision=HIGHEST, preferred_element_type=F32)
        hid = jax.nn.gelu(pa + pltpu.roll(pb, nseg - 1, 0) + c)
        out = jnp.dot(hid, w2_ref[...], precision=HIGHEST, preferred_element_type=F32) + b2_ref[...]
        if norm:
            out = (out * _rms(out)) * kg_ref[...]
        o_ref[0, g] = jnp.where(row < nseg - 1, out, 0.0)


def _compress(raw, pos, w1, b1, w2, b2, gain, batch, seq, norm):
    nseg = seq // CMP_STRIDE
    half = CMP_STRIDE * HEAD_DIM
    a = raw.reshape(batch, nseg, CMP_STRIDE * D_KV)
    pos8 = jnp.broadcast_to(pos.reshape(1, CMP_LEN * HEAD_DIM), (8, CMP_LEN * HEAD_DIM))

    def expand(wh):
        wh = wh.reshape(CMP_STRIDE, HEAD_DIM, CMP_HIDDEN)
        z = jnp.zeros((N_KV, CMP_STRIDE, N_KV, HEAD_DIM, CMP_HIDDEN), F32)
        for g in range(N_KV):
            z = z.at[g, :, g].set(wh)
        return z.reshape(N_KV, CMP_STRIDE * D_KV, CMP_HIDDEN)

    w1a, w1b = expand(w1[:half]), expand(w1[half:])
    b1r, b2r, gr = b1.reshape(1, CMP_HIDDEN), b2.reshape(1, HEAD_DIM), gain.reshape(1, HEAD_DIM)
    full = lambda t: pl.BlockSpec(t.shape, lambda i: (0,) * t.ndim)
    return pl.pallas_call(
        functools.partial(_compress_kernel, norm=norm),
        grid=(batch,),
        in_specs=[pl.BlockSpec((1, nseg, CMP_STRIDE * D_KV), lambda i: (i, 0, 0)),
                  full(pos8), full(w1), full(w1a), full(w1b), full(b1r), full(w2), full(b2r), full(gr)],
        out_specs=pl.BlockSpec((1, N_KV, nseg, HEAD_DIM), lambda i: (i, 0, 0, 0)),
        out_shape=jax.ShapeDtypeStruct((batch, N_KV, nseg, HEAD_DIM), F32),
        compiler_params=pltpu.CompilerParams(dimension_semantics=("arbitrary",), vmem_limit_bytes=VMEM_LIMIT),
        name="compress_k" if norm else "compress_v",
    )(a, pos8, w1, w1a, w1b, b1r, w2, b2r, gr)


def _attn_kernel(qt_ref, kc_ref, vct_ref, ks_ref, vst_ref, kw_ref, vwt_ref, g_ref, ovt_ref, o_ref,
                 qs_ref, s0_ref, s1_ref, p0_ref, p1_ref, st_ref, acc_ref):
    qb = pl.program_id(1)
    nq = N_REP * Q_BLOCK
    q0 = qb * Q_BLOCK
    qt = qt_ref[0, 0]
    ql = lax.broadcasted_iota(jnp.int32, (1, nq), 1) % Q_BLOCK
    t_row = (q0 + ql).astype(F32)
    m_init = 0.5 * NEG

    def online(s, m, l):
        m_new = jnp.maximum(m, jnp.max(s, axis=0, keepdims=True))
        alpha = jnp.exp2(m - m_new)
        p = jnp.exp2(s - m_new)
        return p, m_new, alpha, alpha * l + jnp.sum(p, axis=0, keepdims=True)

    def inv(l):
        return jnp.where(l > 0.0, 1.0 / l, 0.0)

    m0 = jnp.full((1, nq), m_init, F32)
    l0 = jnp.zeros((1, nq), F32)
    a0 = jnp.zeros((HEAD_DIM, nq), F32)

    ncmp = kc_ref.shape[1]
    s = jnp.dot(kc_ref[0], qt, preferred_element_type=F32)
    c_end = (lax.broadcasted_iota(jnp.int32, (ncmp, 1), 0) * CMP_STRIDE + (CMP_LEN - 1)).astype(F32)
    p, _, _, l = online(jnp.where(c_end <= t_row, s, NEG), m0, l0)
    p = p * inv(l)
    o_cmp = jnp.dot(vct_ref[0], p.astype(BF16), preferred_element_type=F32)

    psum = p[:, 0:Q_BLOCK]
    for r in range(1, N_REP):
        psum = psum + p[:, r * Q_BLOCK:(r + 1) * Q_BLOCK]
    nsel = ovt_ref.shape[0]
    p_hi = psum.astype(BF16)
    p_lo = (psum - p_hi.astype(F32)).astype(BF16)
    imp = (jnp.dot(ovt_ref[...], p_hi, preferred_element_type=F32)
           + jnp.dot(ovt_ref[...], p_lo, preferred_element_type=F32))
    n_col = lax.broadcasted_iota(jnp.int32, (nsel, 1), 0).astype(F32)
    n_start = n_col * SEL_BLOCK
    tq = t_row[:, 0:Q_BLOCK]
    cur = jnp.floor(tq * (1.0 / SEL_BLOCK)) * SEL_BLOCK
    forced = (n_start == cur) | (n_start == 0.0)
    valid = n_start <= tq
    imp = jnp.where(forced, imp + FORCE_BONUS, imp)
    imp = jnp.where(valid, imp, NEG)
    sel = jnp.zeros((nsel, Q_BLOCK), F32)
    for _ in range(min(N_SEL, nsel)):
        mx = jnp.max(imp, axis=0, keepdims=True)
        first = jnp.min(jnp.where(imp == mx, n_col, float(nsel)), axis=0, keepdims=True)
        hit = n_col == first
        sel = jnp.where(hit, 1.0, sel)
        imp = jnp.where(hit, -jnp.inf, imp)
    selb = jnp.where(valid & (sel > 0.0), 0.0, NEG).astype(BF16)
    qs_ref[0:LANE, :] = qt
    qs_ref[LANE:LANE + nsel, :] = jnp.concatenate([selb] * N_REP, axis=1)
    if qs_ref.shape[0] > LANE + nsel:
        qs_ref[LANE + nsel:, :] = jnp.zeros((qs_ref.shape[0] - LANE - nsel, nq), BF16)

    def attend(k_blk, vt_blk, q_op, bias, carry):
        m, l, acc = carry
        s = jnp.dot(k_blk, q_op, preferred_element_type=F32)
        if bias is not None:
            s = s + bias
        p, m, alpha, l = online(s, m, l)
        pv = jnp.dot(vt_blk, p.astype(BF16), preferred_element_type=F32)
        return m, l, alpha * acc + pv

    seq = ks_ref.shape[1]

    def scores(j):
        k0 = pl.multiple_of(jnp.minimum(j * KC_SEL, seq - KC_SEL), KC_SEL)
        s = jnp.dot(ks_ref[0, pl.ds(k0, KC_SEL), :], qs_ref[...], preferred_element_type=F32)
        return s, jnp.max(s, axis=0, keepdims=True)

    def values(j, p):
        k0 = pl.multiple_of(jnp.maximum(j, 0) * KC_SEL, KC_SEL)
        return jnp.dot(vst_ref[0, :, pl.ds(k0, KC_SEL)], p, preferred_element_type=F32)

    def stage(j, s_cur, s_nxt, p_cur, p_prv):
        m, l, alpha_prev, mx = st_ref[0:1, :], st_ref[1:2, :], st_ref[2:3, :], st_ref[3:4, :]
        m_new = jnp.maximum(m, mx)
        alpha = jnp.exp2(m - m_new)
        p = jnp.exp2(s_cur[...] - m_new)
        acc_ref[...] = alpha_prev * acc_ref[...] + values(j - 1, p_prv[...])
        p_cur[...] = p.astype(BF16)
        s_next, mx_next = scores(j + 1)
        s_nxt[...] = s_next
        st_ref[0:1, :] = m_new
        st_ref[1:2, :] = alpha * l + jnp.sum(p, axis=0, keepdims=True)
        st_ref[2:3, :] = alpha
        st_ref[3:4, :] = mx_next

    n_full = q0 // KC_SEL
    s_first, mx_first = scores(0)

    @pl.when(n_full % 2 == 0)
    def _():
        s0_ref[...] = s_first

    @pl.when(n_full % 2 == 1)
    def _():
        s1_ref[...] = s_first

    p0_ref[...] = jnp.zeros(p0_ref.shape, BF16)
    p1_ref[...] = jnp.zeros(p1_ref.shape, BF16)
    st_ref[0:1, :] = m0
    st_ref[1:2, :] = l0
    st_ref[2:3, :] = jnp.ones((1, nq), F32)
    st_ref[3:4, :] = mx_first
    acc_ref[...] = a0

    def sel_body(j, carry):
        @pl.when((n_full - j) % 2 == 0)
        def _():
            stage(j, s0_ref, s1_ref, p0_ref, p1_ref)

        @pl.when((n_full - j) % 2 == 1)
        def _():
            stage(j, s1_ref, s0_ref, p1_ref, p0_ref)
        return carry

    lax.fori_loop(0, n_full, sel_body, 0)
    pos_last = (n_full * KC_SEL + lax.broadcasted_iota(jnp.int32, (KC_SEL, 1), 0)).astype(F32)
    p, _, alpha, l_sel = online(s0_ref[...] + jnp.where(pos_last <= t_row, 0.0, NEG), st_ref[0:1, :], st_ref[1:2, :])
    acc = st_ref[2:3, :] * acc_ref[...] + values(n_full - 1, p1_ref[...])
    o_sel = alpha * acc + values(n_full, p.astype(BF16))

    n_wk = WINDOW + Q_BLOCK
    kk = lax.broadcasted_iota(jnp.int32, (n_wk, 1), 0)
    in_win = (kk - WINDOW <= ql) & (kk > ql) & (kk >= WINDOW - q0)
    _, l_win, o_win = attend(kw_ref[0, pl.ds(pl.multiple_of(q0, Q_BLOCK), n_wk), :],
                             vwt_ref[0, :, pl.ds(pl.multiple_of(q0, Q_BLOCK), n_wk)], qt,
                             jnp.where(in_win, 0.0, NEG), (m0, l0, a0))

    gt = g_ref[0, 0]
    o_ref[0, 0] = (gt[0:1] * o_cmp + gt[1:2] * (o_sel * inv(l_sel)) + gt[2:3] * (o_win * inv(l_win)))


def _attention(qt, kc, vct, ks, vst, kw, vwt, gt, ovt):
    bgn, nqb = qt.shape[0], qt.shape[1]
    seq = ks.shape[1]
    nq = N_REP * Q_BLOCK
    per_bg = lambda a: pl.BlockSpec((1,) + a.shape[1:], lambda b, i: (b,) + (0,) * (a.ndim - 1))
    return pl.pallas_call(
        _attn_kernel,
        grid=(bgn, nqb),
        in_specs=[pl.BlockSpec((1, 1, LANE, nq), lambda b, i: (b, i, 0, 0)),
                  per_bg(kc), per_bg(vct), per_bg(ks), per_bg(vst), per_bg(kw), per_bg(vwt),
                  pl.BlockSpec((1, 1, 3, nq), lambda b, i: (b, i, 0, 0)),
                  pl.BlockSpec(ovt.shape, lambda b, i: (0, 0))],
        out_specs=pl.BlockSpec((1, 1, HEAD_DIM, nq), lambda b, i: (b, i, 0, 0)),
        out_shape=jax.ShapeDtypeStruct((bgn, nqb, HEAD_DIM, nq), F32),
        scratch_shapes=[pltpu.VMEM((ks.shape[2], nq), BF16),
                        pltpu.VMEM((KC_SEL, nq), F32), pltpu.VMEM((KC_SEL, nq), F32),
                        pltpu.VMEM((KC_SEL, nq), BF16), pltpu.VMEM((KC_SEL, nq), BF16),
                        pltpu.VMEM((8, nq), F32), pltpu.VMEM((HEAD_DIM, nq), F32)],
        compiler_params=pltpu.CompilerParams(dimension_semantics=("arbitrary", "arbitrary"),
                                             vmem_limit_bytes=VMEM_LIMIT),
        name="nsa_attention",
    )(qt, kc, vct, ks, vst, kw, vwt, gt, ovt)


def _mix_kernel(x_ref, oa_ref, u_ref, v_ref, ws_ref, bs_ref, ga_ref, gg_ref, wo_ref, g2_ref, wr_ref, br_ref,
                x1_ref, xn_ref, idx_ref, gate_ref):
    tm = x_ref.shape[0]
    rr = lax.broadcasted_iota(jnp.int32, (GM_CHUNK, GM_CHUNK), 0)
    cc = lax.broadcasted_iota(jnp.int32, (GM_CHUNK, GM_CHUNK), 1)
    grp = lax.broadcasted_iota(jnp.int32, (1, D_GM), 1) // GM_GROUP_DIM
    ws = [jnp.where(rr >= cc, ws_ref[g], 0.0).astype(BF16) for g in range(N_GM_GROUPS)]
    ys = []
    for c in range(tm // GM_CHUNK):
        vch = v_ref[c * GM_CHUNK:(c + 1) * GM_CHUNK, :].astype(BF16)
        y = bs_ref[...]
        for g in range(N_GM_GROUPS):
            y = y + jnp.where(grp == g, jnp.dot(ws[g], vch, preferred_element_type=F32), 0.0)
        ys.append(y)
    o_gm = u_ref[...] * jnp.concatenate(ys, axis=0)
    o_at = oa_ref[...]
    mixed = jnp.concatenate([(o_at * _rms(o_at)) * ga_ref[...], (o_gm * _rms(o_gm)) * gg_ref[...]], axis=-1)
    x1 = x_ref[...] + jnp.dot(mixed.astype(BF16), wo_ref[...], preferred_element_type=F32)
    x1_ref[...] = x1
    xn = (x1 * _rms(x1)) * g2_ref[...]
    for s in range(ROW_SUB):
        xn_ref[:, s, :] = xn[:, s * LANE:(s + 1) * LANE]
    logits = jnp.dot(xn, wr_ref[...], precision=HIGHEST, preferred_element_type=F32) + br_ref[...]
    lane = lax.broadcasted_iota(jnp.int32, (1, LANE), 1).astype(F32)
    idx_out = jnp.zeros((tm, LANE), F32)
    val_out = jnp.zeros((tm, LANE), F32)
    vals = []
    for k in range(TOP_K):
        mx = jnp.max(logits, axis=-1, keepdims=True)
        first = jnp.min(jnp.where(logits == mx, lane, float(LANE)), axis=-1, keepdims=True)
        logits = jnp.where(lane == first, -jnp.inf, logits)
        idx_out = jnp.where(lane == float(k), first, idx_out)
        vals.append(mx)
    es = [jnp.exp(v - vals[0]) for v in vals]
    den = es[0] + es[1] + es[2] + es[3]
    for k in range(TOP_K):
        val_out = jnp.where(lane == float(k), es[k] / den, val_out)
    idx_ref[...] = idx_out.astype(jnp.int32)
    gate_ref[...] = val_out


def _mix(x2, o_attn, u_act, v_act, gm_w_s, bias_full, ga, gg, w_out_b, g2, wr_pad, br_pad):
    n = x2.shape[0]
    row = lambda c: pl.BlockSpec((TM_MIX, c), lambda i: (i, 0))
    full = lambda a: pl.BlockSpec(a.shape, lambda i: (0,) * a.ndim)
    return pl.pallas_call(
        _mix_kernel,
        grid=(n // TM_MIX,),
        in_specs=[row(D_MODEL), row(D_ATTN), row(D_GM), row(D_GM), full(gm_w_s), full(bias_full), full(ga), full(gg),
                  full(w_out_b), full(g2), full(wr_pad), full(br_pad)],
        out_specs=[row(D_MODEL), pl.BlockSpec((TM_MIX, ROW_SUB, LANE), lambda i: (i, 0, 0)), row(LANE), row(LANE)],
        out_shape=[jax.ShapeDtypeStruct((n, D_MODEL), F32), jax.ShapeDtypeStruct((n, ROW_SUB, LANE), F32),
                   jax.ShapeDtypeStruct((n, LANE), jnp.int32), jax.ShapeDtypeStruct((n, LANE), F32)],
        compiler_params=pltpu.CompilerParams(dimension_semantics=("arbitrary",), vmem_limit_bytes=VMEM_LIMIT),
        name="mix_outproj_router",
    )(x2, o_attn, u_act, v_act, gm_w_s, bias_full, ga, gg, w_out_b, g2, wr_pad, br_pad)


def _row_gather(idx_ref, n_rows, src_hbm, dst_ref, sem):
    def start():
        for r in range(n_rows):
            pltpu.make_async_copy(src_hbm.at[pl.ds(idx_ref[0, 0, r], 1), :], dst_ref.at[pl.ds(r, 1), :], sem).start()

    def wait():
        pltpu.make_async_copy(src_hbm.at[pl.ds(0, n_rows), :], dst_ref, sem).wait()

    return start, wait


def _tile_row_gather(idx_ref, n_rows, src_hbm, dst_ref, sem):
    def start():
        for r in range(n_rows):
            t = idx_ref[0, 0, r]
            pltpu.make_async_copy(src_hbm.at[lax.shift_right_logical(t, 3), t & (ROW_SUB - 1)],
                                  dst_ref.at[r // ROW_SUB, :, r % ROW_SUB, :], sem).start()

    def wait():
        pltpu.make_async_copy(src_hbm.at[pl.ds(0, n_rows // ROW_SUB)], dst_ref, sem).wait()

    return start, wait


def _tiles_to_matrix(ref):
    rows = ref.shape[0] * ROW_SUB
    return jnp.concatenate([ref[:, c].reshape(rows, LANE) for c in range(ROW_SUB)], axis=1)


def _moe_kernel(be_ref, bv_ref, tok_ref, tok_next_ref, x_hbm, wgu_ref, bg_ref, bl_ref, wd_ref, bd_ref,
                o_ref, xbuf, sems, wt_s, wg_s, wl_s, wd_s):
    i = pl.program_id(0)
    slot = i % 2
    start_cur, wait_cur = _tile_row_gather(tok_ref, BM_MOE, x_hbm, xbuf.at[slot], sems.at[slot])
    start_next, _ = _tile_row_gather(tok_next_ref, BM_MOE, x_hbm, xbuf.at[1 - slot], sems.at[1 - slot])

    prev = jnp.maximum(i - 1, 0)

    @pl.when((i == 0) & (bv_ref[0] == 1))
    def _():
        start_cur()

    @pl.when((bv_ref[i] == 1) & ((i == 0) | (be_ref[i] != be_ref[prev])))
    def _():
        tc = wt_s.shape[1]
        for c in range(2 * D_EXPERT // tc):
            wt = wgu_ref[0, :, c * tc:(c + 1) * tc].T
            for j in range(ROW_SUB):
                wt_s[j] = wt[:, j * LANE:(j + 1) * LANE]
            for first, dst in ((0, wg_s), (1, wl_s)):
                half = jnp.concatenate([wt_s[j, pl.ds(first, tc // 2, stride=2), :] for j in range(ROW_SUB)], axis=1)
                dst[c * tc // 2:(c + 1) * tc // 2, :] = half.astype(BF16)
        wd_s[...] = wd_ref[0].astype(BF16)

    @pl.when(bv_ref[i] == 1)
    def _():
        wait_cur()
        start_next()
        xb = _tiles_to_matrix(xbuf.at[slot]).astype(BF16)
        hg = lax.dot_general(xb, wg_s[...], _NT, preferred_element_type=F32) + bg_ref[0]
        hl = lax.dot_general(xb, wl_s[...], _NT, preferred_element_type=F32) + bl_ref[0]
        hg = jnp.minimum(hg, SWIGLU_LIMIT)
        hl = jnp.clip(hl, -SWIGLU_LIMIT, SWIGLU_LIMIT)
        a = hg * jax.nn.sigmoid(SWIGLU_ALPHA * hg) * (hl + 1.0)
        o_ref[...] = jnp.dot(a.astype(BF16), wd_s[...], preferred_element_type=F32) + bd_ref[0]

    @pl.when((bv_ref[i] == 0) & (i > 0) & (bv_ref[prev] == 1))
    def _():
        wait_cur()

    @pl.when(bv_ref[i] == 0)
    def _():
        o_ref[...] = jnp.zeros(o_ref.shape, F32)


def _moe(blk_expert, blk_valid, tok_blocks, xn3, w_gate_up, bg, bl, w_down, bd):
    nb = blk_expert.shape[0]
    per_e = lambda a: pl.BlockSpec((1,) + a.shape[1:], lambda i, be, bv: (be[i],) + (0,) * (a.ndim - 1))
    grid_spec = pltpu.PrefetchScalarGridSpec(
        num_scalar_prefetch=2,
        grid=(nb,),
        in_specs=[pl.BlockSpec((1, 1, BM_MOE), lambda i, be, bv: (i, 0, 0), memory_space=pltpu.SMEM),
                  pl.BlockSpec((1, 1, BM_MOE), lambda i, be, bv: (jnp.minimum(i + 1, nb - 1), 0, 0),
                               memory_space=pltpu.SMEM),
                  pl.BlockSpec(memory_space=pl.ANY),
                  per_e(w_gate_up), per_e(bg), per_e(bl), per_e(w_down), per_e(bd)],
        out_specs=pl.BlockSpec((BM_MOE, D_MODEL), lambda i, be, bv: (i, 0)),
        scratch_shapes=[pltpu.VMEM((2, BM_MOE // ROW_SUB, ROW_SUB, ROW_SUB, LANE), F32),
                        pltpu.SemaphoreType.DMA((2,)),
                        pltpu.VMEM((ROW_SUB, 256, LANE), F32), pltpu.VMEM((D_EXPERT, D_MODEL), BF16),
                        pltpu.VMEM((D_EXPERT, D_MODEL), BF16), pltpu.VMEM((D_EXPERT, D_MODEL), BF16)],
    )
    return pl.pallas_call(
        _moe_kernel,
        grid_spec=grid_spec,
        out_shape=jax.ShapeDtypeStruct((nb * BM_MOE, D_MODEL), F32),
        compiler_params=pltpu.CompilerParams(dimension_semantics=("arbitrary",), vmem_limit_bytes=VMEM_LIMIT_MOE),
        name="moe_experts",
    )(blk_expert, blk_valid, tok_blocks, tok_blocks,
      xn3.reshape(xn3.shape[0] // ROW_SUB, ROW_SUB, ROW_SUB, LANE), w_gate_up, bg, bl, w_down, bd)


def _combine_kernel(dest_ref, dest_next_ref, x1_ref, gate_ref, y_hbm, o_ref, buf, sems):
    i = pl.program_id(0)
    slot = i % 2
    n_rows = TOP_K * TM_CMB
    start_cur, wait_cur = _row_gather(dest_ref, n_rows, y_hbm, buf.at[slot], sems.at[slot])
    start_next, _ = _row_gather(dest_next_ref, n_rows, y_hbm, buf.at[1 - slot], sems.at[1 - slot])

    @pl.when(i == 0)
    def _():
        start_cur()

    @pl.when(i + 1 < pl.num_programs(0))
    def _():
        start_next()

    wait_cur()
    gate = gate_ref[...]
    acc = x1_ref[...]
    for k in range(TOP_K):
        acc = acc + gate[:, k:k + 1] * buf[slot, k * TM_CMB:(k + 1) * TM_CMB, :]
    o_ref[...] = acc


def _combine(dest_blocks, x1, gate_pad, y_rows):
    n = x1.shape[0]
    nt = n // TM_CMB
    n_rows = TOP_K * TM_CMB
    return pl.pallas_call(
        _combine_kernel,
        grid=(nt,),
        in_specs=[pl.BlockSpec((1, 1, n_rows), lambda i: (i, 0, 0), memory_space=pltpu.SMEM),
                  pl.BlockSpec((1, 1, n_rows), lambda i: (jnp.minimum(i + 1, nt - 1), 0, 0),
                               memory_space=pltpu.SMEM),
                  pl.BlockSpec((TM_CMB, D_MODEL), lambda i: (i, 0)),
                  pl.BlockSpec((TM_CMB, LANE), lambda i: (i, 0)),
                  pl.BlockSpec(memory_space=pl.ANY)],
        out_specs=pl.BlockSpec((TM_CMB, D_MODEL), lambda i: (i, 0)),
        out_shape=jax.ShapeDtypeStruct((n, D_MODEL), F32),
        scratch_shapes=[pltpu.VMEM((2, n_rows, D_MODEL), F32), pltpu.SemaphoreType.DMA((2,))],
        compiler_params=pltpu.CompilerParams(dimension_semantics=("arbitrary",), vmem_limit_bytes=VMEM_LIMIT),
        name="moe_combine",
    )(dest_blocks, dest_blocks, x1, gate_pad, y_rows)


def kernel(x, norm1_g, w_in, q_norm_g, k_norm_g, cmp_pos, w_cmp1, b_cmp1, w_cmp2, b_cmp2, gm_v_norm_g, gm_w_s,
           gm_b_s, out_norm_attn_g, out_norm_gm_g, w_out, norm2_g, w_router, b_router, w_gate_up, b_gate_up,
           w_down, b_down):
    batch, seq, _ = x.shape
    n = batch * seq
    nqb = seq // Q_BLOCK
    bgn = batch * N_KV
    x2 = x.reshape(n, D_MODEL)

    c_gate = D_ATTN + 6 * D_KV
    w_r = jnp.concatenate([w_in[:, :c_gate], w_in[:, c_gate + N_GATE:], w_in[:, c_gate:c_gate + N_GATE],
                           jnp.zeros((D_MODEL, LANE - N_GATE), F32)], axis=1).astype(BF16)
    q, kc_raw, vc_raw, ks, vs, kw, vw, gates, u_act, v_act = _inproj(x2, norm1_g, w_r, q_norm_g, k_norm_g,
                                                                     gm_v_norm_g)

    kc = _compress(kc_raw, cmp_pos[0], w_cmp1[0], b_cmp1[0], w_cmp2[0], b_cmp2[0], k_norm_g[0], batch, seq, True)
    vc = _compress(vc_raw, cmp_pos[1], w_cmp1[1], b_cmp1[1], w_cmp2[1], b_cmp2[1], k_norm_g[0], batch, seq, False)

    nq = N_REP * Q_BLOCK
    ncmp = seq // CMP_STRIDE

    nsel = seq // SEL_BLOCK
    oh_w = -(-nsel // LANE) * LANE

    def with_pos(k, pos, one_hot=False):
        feat = np.zeros((pos.shape[0], LANE - HEAD_DIM + (oh_w if one_hot else 0)), np.float32)
        feat[:, 0] = feat[:, 1] = pos // SEL_BLOCK
        feat[:, 2] = feat[:, 3] = pos % SEL_BLOCK
        if one_hot:
            feat[np.arange(pos.shape[0]), LANE - HEAD_DIM + pos // SEL_BLOCK] = 1.0
        feat = jnp.broadcast_to(jnp.asarray(feat, BF16)[None], (bgn,) + feat.shape)
        return jnp.concatenate([k.astype(BF16), feat], axis=-1)

    def front_pad(a, axis):
        pad = [(0, 0)] * a.ndim
        pad[axis] = (WINDOW, 0)
        return jnp.pad(a, pad)

    def tok_major(a):
        return a.reshape(batch, seq, N_KV, HEAD_DIM).transpose(0, 2, 1, 3).reshape(bgn, seq, HEAD_DIM)

    def feat_major(a):
        return a.reshape(batch, seq, N_KV, HEAD_DIM).transpose(0, 2, 3, 1).reshape(bgn, HEAD_DIM, seq).astype(BF16)

    head = np.arange(N_KV)[:, None] * N_REP + np.arange(nq)[None, :] // Q_BLOCK
    coef = np.exp2(-(head + 1.0)) * LOG2E
    c_hi = coef.astype(BF16).astype(np.float64)
    c_lo = (coef - c_hi).astype(BF16).astype(np.float64)
    qrows = np.zeros((N_KV, LANE - HEAD_DIM, nq), np.float32)
    qrows[:, 0], qrows[:, 1], qrows[:, 2], qrows[:, 3] = SEL_BLOCK * c_hi, SEL_BLOCK * c_lo, c_hi, c_lo
    qrows = jnp.broadcast_to(jnp.asarray(qrows, BF16)[None, :, None], (batch, N_KV, nqb, LANE - HEAD_DIM, nq))
    qt = (q.reshape(batch, nqb, Q_BLOCK, N_KV, N_REP, HEAD_DIM).transpose(0, 3, 1, 5, 4, 2)
          .reshape(batch, N_KV, nqb, HEAD_DIM, nq).astype(BF16))
    qt = jnp.concatenate([qt, qrows], axis=3).reshape(bgn, nqb, LANE, nq)
    gt = (gates[:, :N_GATE].reshape(batch, nqb, Q_BLOCK, N_KV, N_REP, 3).transpose(0, 3, 1, 5, 4, 2)
          .reshape(bgn, nqb, 3, nq))
    pos_t = np.arange(seq)
    pos_c = np.arange(ncmp) * CMP_STRIDE + (CMP_LEN - 1)
    kc_b = with_pos(kc.reshape(bgn, ncmp, HEAD_DIM), pos_c)
    vct = vc.reshape(bgn, ncmp, HEAD_DIM).transpose(0, 2, 1).astype(BF16)
    c0 = np.arange(ncmp)[None, :] * CMP_STRIDE
    n0 = np.arange(seq // SEL_BLOCK)[:, None] * SEL_BLOCK
    ovt = np.clip(np.minimum(c0 + CMP_LEN, n0 + SEL_BLOCK) - np.maximum(c0, n0), 0, None) / CMP_LEN
    pos_w = np.maximum(np.arange(seq + WINDOW) - WINDOW, 0)
    ot = _attention(qt, kc_b, vct, with_pos(tok_major(ks), pos_t, one_hot=True), feat_major(vs),
                    with_pos(front_pad(tok_major(kw), 1), pos_w), front_pad(feat_major(vw), 2), gt,
                    jnp.asarray(ovt, BF16))
    o_attn = (ot.reshape(batch, N_KV, nqb, HEAD_DIM, N_REP, Q_BLOCK).transpose(0, 2, 5, 1, 4, 3)
              .reshape(n, D_ATTN))

    bias_full = jnp.repeat(gm_b_s.T, GM_GROUP_DIM, axis=1)
    wr_pad = jnp.concatenate([w_router, jnp.zeros((D_MODEL, LANE - N_EXPERTS), F32)], axis=1)
    br_pad = jnp.concatenate([b_router, jnp.full((LANE - N_EXPERTS,), NEG, F32)]).reshape(1, LANE)
    x1, xn3, idx_pad, gate_pad = _mix(x2, o_attn, u_act, v_act, gm_w_s, bias_full,
                                     out_norm_attn_g.reshape(1, D_ATTN), out_norm_gm_g.reshape(1, D_GM),
                                     w_out.astype(BF16), norm2_g.reshape(1, D_MODEL), wr_pad, br_pad)

    s_tot = n * TOP_K
    nb = s_tot // BM_MOE + N_EXPERTS
    e_flat = idx_pad[:, :TOP_K].reshape(s_tot)
    onehot = (e_flat[:, None] == jnp.arange(N_EXPERTS, dtype=jnp.int32)[None, :]).astype(jnp.int32)
    csum = jnp.cumsum(onehot, axis=0)
    rank = jnp.sum(csum * onehot, axis=1) - 1
    counts = csum[-1]
    padded = ((counts + BM_MOE - 1) // BM_MOE) * BM_MOE
    pad_end = jnp.cumsum(padded)
    pad_start = pad_end - padded
    dest = pad_start[e_flat] + rank
    tok_flat = jnp.arange(s_tot, dtype=jnp.int32) // TOP_K
    tok_buf = jnp.zeros((nb * BM_MOE,), jnp.int32).at[dest].set(tok_flat)
    blk_start = jnp.arange(nb, dtype=jnp.int32) * BM_MOE
    blk_expert = jnp.minimum(jnp.sum((blk_start[:, None] >= pad_end[None, :]).astype(jnp.int32), axis=1),
                             N_EXPERTS - 1)
    blk_valid = (blk_start < pad_end[-1]).astype(jnp.int32)

    bg = b_gate_up[:, 0::2].reshape(N_EXPERTS, 1, D_EXPERT)
    bl = b_gate_up[:, 1::2].reshape(N_EXPERTS, 1, D_EXPERT)
    y_rows = _moe(blk_expert, blk_valid, tok_buf.reshape(nb, 1, BM_MOE), xn3, w_gate_up, bg, bl, w_down,
                  b_down.reshape(N_EXPERTS, 1, D_MODEL))

    dest_blocks = (dest.reshape(n // TM_CMB, TM_CMB, TOP_K).transpose(0, 2, 1)
                   .reshape(n // TM_CMB, 1, TOP_K * TM_CMB).astype(jnp.int32))
    out = _combine(dest_blocks, x1, gate_pad, y_rows)
    return out.reshape(batch, seq, D_MODEL)
```

```python
import functools

import jax
import jax.numpy as jnp
import numpy as np
from jax import lax
from jax.experimental import pallas as pl
from jax.experimental.pallas import tpu as pltpu

F32 = jnp.float32
BF16 = jnp.bfloat16
HIGHEST = lax.Precision.HIGHEST
_NT = (((1,), (1,)), ((), ()))

D_MODEL = 1024
N_HEADS = 8
HEAD_DIM = 64
N_KV = 2
N_REP = N_HEADS // N_KV
D_ATTN = N_HEADS * HEAD_DIM
D_KV = N_KV * HEAD_DIM
N_GM_GROUPS = 8
GM_GROUP_DIM = 64
D_GM = N_GM_GROUPS * GM_GROUP_DIM
N_GATE = 3 * N_HEADS
CMP_LEN = 32
CMP_STRIDE = 16
CMP_HIDDEN = 128
SEL_BLOCK = 64
N_SEL = 16
WINDOW = 512
Q_BLOCK = 128
FORCE_BONUS = 1.0e4
GM_CHUNK = 128
N_EXPERTS = 32
TOP_K = 4
D_EXPERT = 1024
SWIGLU_LIMIT = 7.0
SWIGLU_ALPHA = 1.702
EPS = 1e-6
NEG = -1.0e30
LOG2E = 1.4426950408889634

LANE = 128
ROW_SUB = D_MODEL // LANE
VMEM_LIMIT = 48 * 1024 * 1024
VMEM_LIMIT_MOE = 56 * 1024 * 1024

_C_Q = 0
_C_KC = _C_Q + D_ATTN
_C_VC = _C_KC + D_KV
_C_KS = _C_VC + D_KV
_C_VS = _C_KS + D_KV
_C_KW = _C_VS + D_KV
_C_VW = _C_KW + D_KV
_C_U = _C_VW + D_KV
_C_V = _C_U + D_GM
_C_G = _C_V + D_GM
D_IN_PAD = _C_G + LANE

TM_IN = 256
TM_MIX = 256
KC_SEL = 512
BM_MOE = 256
MOE_AHEAD = 2
TM_CMB = 128


def _rms(x, eps=EPS):
    return lax.rsqrt(jnp.mean(x * x, axis=-1, keepdims=True) + eps)


def _inproj_kernel(x_ref, g1_ref, w_ref, qg_ref, kg_ref, vg_ref,
                   q_ref, kc_ref, vc_ref, ks_ref, vs_ref, kw_ref, vw_ref, gate_ref, u_ref, v_ref):
    x = x_ref[...]
    h = (x * _rms(x)) * g1_ref[...]
    z = jnp.dot(h.astype(BF16), w_ref[...], preferred_element_type=F32)

    def head_norm(col0, n, gain, scale):
        outs = []
        for i in range(n):
            sl = z[:, col0 + i * HEAD_DIM: col0 + (i + 1) * HEAD_DIM]
            outs.append((sl * _rms(sl)) * gain * scale)
        return jnp.concatenate(outs, axis=-1)

    q_ref[...] = head_norm(_C_Q, N_HEADS, qg_ref[...], HEAD_DIM ** -0.5 * LOG2E)
    kc_ref[...] = z[:, _C_KC:_C_KC + D_KV]
    vc_ref[...] = z[:, _C_VC:_C_VC + D_KV]
    ks_ref[...] = head_norm(_C_KS, N_KV, kg_ref[1:2, :], 1.0)
    vs_ref[...] = z[:, _C_VS:_C_VS + D_KV]
    kw_ref[...] = head_norm(_C_KW, N_KV, kg_ref[2:3, :], 1.0)
    vw_ref[...] = z[:, _C_VW:_C_VW + D_KV]
    gate_ref[...] = jax.nn.sigmoid(z[:, _C_G:_C_G + LANE])
    u_ref[...] = jax.nn.gelu(z[:, _C_U:_C_U + D_GM])
    gv = jax.nn.gelu(z[:, _C_V:_C_V + D_GM])
    v_ref[...] = (gv * _rms(gv)) * vg_ref[...]


def _inproj(x2, norm1_g, w_r, q_norm_g, k_norm_g, gm_v_norm_g):
    n = x2.shape[0]
    row = lambda c: pl.BlockSpec((TM_IN, c), lambda i: (i, 0))
    full = lambda a: pl.BlockSpec(a.shape, lambda i: (0,) * a.ndim)
    g1 = norm1_g.reshape(1, D_MODEL)
    qg = q_norm_g.reshape(1, HEAD_DIM)
    vg = gm_v_norm_g.reshape(1, D_GM)
    widths = (D_ATTN, D_KV, D_KV, D_KV, D_KV, D_KV, D_KV, LANE, D_GM, D_GM)
    return pl.pallas_call(
        _inproj_kernel,
        grid=(n // TM_IN,),
        in_specs=[row(D_MODEL), full(g1), full(w_r), full(qg), full(k_norm_g), full(vg)],
        out_specs=[row(c) for c in widths],
        out_shape=[jax.ShapeDtypeStruct((n, c), F32) for c in widths],
        compiler_params=pltpu.CompilerParams(dimension_semantics=("arbitrary",), vmem_limit_bytes=VMEM_LIMIT),
        name="inproj",
    )(x2, g1, w_r, qg, k_norm_g, vg)


def _compress_kernel(a_ref, pos_ref, w1_ref, w1a_ref, w1b_ref, b1_ref, w2_ref, b2_ref, kg_ref, o_ref, *, norm):
    a = a_ref[0]
    nseg = a.shape[0]
    c = jnp.dot(pos_ref[...], w1_ref[...], precision=HIGHEST, preferred_element_type=F32)[0:1] + b1_ref[...]
    row = lax.broadcasted_iota(jnp.int32, (nseg, 1), 0)
    for g in range(N_KV):
        pa = jnp.dot(a, w1a_ref[g], precision=HIGHEST, preferred_element_type=F32)
        pb = jnp.dot(a, w1b_ref[g], precision=HIGHEST, preferred_element_type=F32)
        hid = jax.nn.gelu(pa + pltpu.roll(pb, nseg - 1, 0) + c)
        out = jnp.dot(hid, w2_ref[...], precision=HIGHEST, preferred_element_type=F32) + b2_ref[...]
        if norm:
            out = (out * _rms(out)) * kg_ref[...]
        o_ref[0, g] = jnp.where(row < nseg - 1, out, 0.0)


def _compress(raw, pos, w1, b1, w2, b2, gain, batch, seq, norm):
    nseg = seq // CMP_STRIDE
    half = CMP_STRIDE * HEAD_DIM
    a = raw.reshape(batch, nseg, CMP_STRIDE * D_KV)
    pos8 = jnp.broadcast_to(pos.reshape(1, CMP_LEN * HEAD_DIM), (8, CMP_LEN * HEAD_DIM))

    def expand(wh):
        wh = wh.reshape(CMP_STRIDE, HEAD_DIM, CMP_HIDDEN)
        z = jnp.zeros((N_KV, CMP_STRIDE, N_KV, HEAD_DIM, CMP_HIDDEN), F32)
        for g in range(N_KV):
            z = z.at[g, :, g].set(wh)
        return z.reshape(N_KV, CMP_STRIDE * D_KV, CMP_HIDDEN)

    w1a, w1b = expand(w1[:half]), expand(w1[half:])
    b1r, b2r, gr = b1.reshape(1, CMP_HIDDEN), b2.reshape(1, HEAD_DIM), gain.reshape(1, HEAD_DIM)
    full = lambda t: pl.BlockSpec(t.shape, lambda i: (0,) * t.ndim)
    return pl.pallas_call(
        functools.partial(_compress_kernel, norm=norm),
        grid=(batch,),
        in_specs=[pl.BlockSpec((1, nseg, CMP_STRIDE * D_KV), lambda i: (i, 0, 0)),
                  full(pos8), full(w1), full(w1a), full(w1b), full(b1r), full(w2), full(b2r), full(gr)],
        out_specs=pl.BlockSpec((1, N_KV, nseg, HEAD_DIM), lambda i: (i, 0, 0, 0)),
        out_shape=jax.ShapeDtypeStruct((batch, N_KV, nseg, HEAD_DIM), F32),
        compiler_params=pltpu.CompilerParams(dimension_semantics=("arbitrary",), vmem_limit_bytes=VMEM_LIMIT),
        name="compress_k" if norm else "compress_v",
    )(a, pos8, w1, w1a, w1b, b1r, w2, b2r, gr)


def _attn_kernel(qt_ref, kc_ref, vct_ref, ks_ref, vst_ref, kw_ref, vwt_ref, g_ref, ovt_ref, o_ref,
                 qs_ref, s0_ref, s1_ref, p0_ref, p1_ref, st_ref, acc_ref):
    qb = pl.program_id(1)
    nq = N_REP * Q_BLOCK
    q0 = qb * Q_BLOCK
    qt = qt_ref[0, 0]
    ql = lax.broadcasted_iota(jnp.int32, (1, nq), 1) % Q_BLOCK
    t_row = (q0 + ql).astype(F32)
    m_init = 0.5 * NEG

    def online(s, m, l):
        m_new = jnp.maximum(m, jnp.max(s, axis=0, keepdims=True))
        alpha = jnp.exp2(m - m_new)
        p = jnp.exp2(s - m_new)
        return p, m_new, alpha, alpha * l + jnp.sum(p, axis=0, keepdims=True)

    def inv(l):
        return jnp.where(l > 0.0, 1.0 / l, 0.0)

    m0 = jnp.full((1, nq), m_init, F32)
    l0 = jnp.zeros((1, nq), F32)
    a0 = jnp.zeros((HEAD_DIM, nq), F32)

    ncmp = kc_ref.shape[1]
    s = jnp.dot(kc_ref[0], qt, preferred_element_type=F32)
    c_end = (lax.broadcasted_iota(jnp.int32, (ncmp, 1), 0) * CMP_STRIDE + (CMP_LEN - 1)).astype(F32)
    p, _, _, l = online(jnp.where(c_end <= t_row, s, NEG), m0, l0)
    p = p * inv(l)
    o_cmp = jnp.dot(vct_ref[0], p.astype(BF16), preferred_element_type=F32)

    psum = p[:, 0:Q_BLOCK]
    for r in range(1, N_REP):
        psum = psum + p[:, r * Q_BLOCK:(r + 1) * Q_BLOCK]
    nsel = ovt_ref.shape[0]
    p_hi = psum.astype(BF16)
    p_lo = (psum - p_hi.astype(F32)).astype(BF16)
    imp = (jnp.dot(ovt_ref[...], p_hi, preferred_element_type=F32)
           + jnp.dot(ovt_ref[...], p_lo, preferred_element_type=F32))
    n_col = lax.broadcasted_iota(jnp.int32, (nsel, 1), 0).astype(F32)
    n_start = n_col * SEL_BLOCK
    tq = t_row[:, 0:Q_BLOCK]
    cur = jnp.floor(tq * (1.0 / SEL_BLOCK)) * SEL_BLOCK
    forced = (n_start == cur) | (n_start == 0.0)
    valid = n_start <= tq
    imp = jnp.where(forced, imp + FORCE_BONUS, imp)
    imp = jnp.where(valid, imp, NEG)
    sel = jnp.zeros((nsel, Q_BLOCK), F32)
    for _ in range(min(N_SEL, nsel)):
        mx = jnp.max(imp, axis=0, keepdims=True)
        first = jnp.min(jnp.where(imp == mx, n_col, float(nsel)), axis=0, keepdims=True)
        hit = n_col == first
        sel = jnp.where(hit, 1.0, sel)
        imp = jnp.where(hit, -jnp.inf, imp)
    selb = jnp.where(valid & (sel > 0.0), 0.0, NEG).astype(BF16)
    qs_ref[0:LANE, :] = qt
    qs_ref[LANE:LANE + nsel, :] = jnp.concatenate([selb] * N_REP, axis=1)
    if qs_ref.shape[0] > LANE + nsel:
        qs_ref[LANE + nsel:, :] = jnp.zeros((qs_ref.shape[0] - LANE - nsel, nq), BF16)

    def attend(k_blk, vt_blk, q_op, bias, carry):
        m, l, acc = carry
        s = jnp.dot(k_blk, q_op, preferred_element_type=F32)
        if bias is not None:
            s = s + bias
        p, m, alpha, l = online(s, m, l)
        pv = jnp.dot(vt_blk, p.astype(BF16), preferred_element_type=F32)
        return m, l, alpha * acc + pv

    seq = ks_ref.shape[1]

    def scores(j):
        k0 = pl.multiple_of(jnp.minimum(j * KC_SEL, seq - KC_SEL), KC_SEL)
        s = jnp.dot(ks_ref[0, pl.ds(k0, KC_SEL), :], qs_ref[...], preferred_element_type=F32)
        return s, jnp.max(s, axis=0, keepdims=True)

    def values(j, p):
        k0 = pl.multiple_of(jnp.maximum(j, 0) * KC_SEL, KC_SEL)
        return jnp.dot(vst_ref[0, :, pl.ds(k0, KC_SEL)], p, preferred_element_type=F32)

    def stage(j, s_cur, s_nxt, p_cur, p_prv):
        m, l, alpha_prev, mx = st_ref[0:1, :], st_ref[1:2, :], st_ref[2:3, :], st_ref[3:4, :]
        m_new = jnp.maximum(m, mx)
        alpha = jnp.exp2(m - m_new)
        p = jnp.exp2(s_cur[...] - m_new)
        acc_ref[...] = alpha_prev * acc_ref[...] + values(j - 1, p_prv[...])
        p_cur[...] = p.astype(BF16)
        s_next, mx_next = scores(j + 1)
        s_nxt[...] = s_next
        st_ref[0:1, :] = m_new
        st_ref[1:2, :] = alpha * l + jnp.sum(p, axis=0, keepdims=True)
        st_ref[2:3, :] = alpha
        st_ref[3:4, :] = mx_next

    n_full = q0 // KC_SEL
    s_first, mx_first = scores(0)

    @pl.when(n_full % 2 == 0)
    def _():
        s0_ref[...] = s_first

    @pl.when(n_full % 2 == 1)
    def _():
        s1_ref[...] = s_first

    p0_ref[...] = jnp.zeros(p0_ref.shape, BF16)
    p1_ref[...] = jnp.zeros(p1_ref.shape, BF16)
    st_ref[0:1, :] = m0
    st_ref[1:2, :] = l0
    st_ref[2:3, :] = jnp.ones((1, nq), F32)
    st_ref[3:4, :] = mx_first
    acc_ref[...] = a0

    def sel_body(j, carry):
        @pl.when((n_full - j) % 2 == 0)
        def _():
            stage(j, s0_ref, s1_ref, p0_ref, p1_ref)

        @pl.when((n_full - j) % 2 == 1)
        def _():
            stage(j, s1_ref, s0_ref, p1_ref, p0_ref)
        return carry

    lax.fori_loop(0, n_full, sel_body, 0)
    pos_last = (n_full * KC_SEL + lax.broadcasted_iota(jnp.int32, (KC_SEL, 1), 0)).astype(F32)
    p, _, alpha, l_sel = online(s0_ref[...] + jnp.where(pos_last <= t_row, 0.0, NEG), st_ref[0:1, :], st_ref[1:2, :])
    acc = st_ref[2:3, :] * acc_ref[...] + values(n_full - 1, p1_ref[...])
    o_sel = alpha * acc + values(n_full, p.astype(BF16))

    n_wk = WINDOW + Q_BLOCK
    kk = lax.broadcasted_iota(jnp.int32, (n_wk, 1), 0)
    in_win = (kk - WINDOW <= ql) & (kk > ql) & (kk >= WINDOW - q0)
    _, l_win, o_win = attend(kw_ref[0, pl.ds(pl.multiple_of(q0, Q_BLOCK), n_wk), :],
                             vwt_ref[0, :, pl.ds(pl.multiple_of(q0, Q_BLOCK), n_wk)], qt,
                             jnp.where(in_win, 0.0, NEG), (m0, l0, a0))

    gt = g_ref[0, 0]
    o_ref[0, 0] = (gt[0:1] * o_cmp + gt[1:2] * (o_sel * inv(l_sel)) + gt[2:3] * (o_win * inv(l_win)))


def _attention(qt, kc, vct, ks, vst, kw, vwt, gt, ovt):
    bgn, nqb = qt.shape[0], qt.shape[1]
    seq = ks.shape[1]
    nq = N_REP * Q_BLOCK
    per_bg = lambda a: pl.BlockSpec((1,) + a.shape[1:], lambda b, i: (b,) + (0,) * (a.ndim - 1))
    return pl.pallas_call(
        _attn_kernel,
        grid=(bgn, nqb),
        in_specs=[pl.BlockSpec((1, 1, LANE, nq), lambda b, i: (b, i, 0, 0)),
                  per_bg(kc), per_bg(vct), per_bg(ks), per_bg(vst), per_bg(kw), per_bg(vwt),
                  pl.BlockSpec((1, 1, 3, nq), lambda b, i: (b, i, 0, 0)),
                  pl.BlockSpec(ovt.shape, lambda b, i: (0, 0))],
        out_specs=pl.BlockSpec((1, 1, HEAD_DIM, nq), lambda b, i: (b, i, 0, 0)),
        out_shape=jax.ShapeDtypeStruct((bgn, nqb, HEAD_DIM, nq), F32),
        scratch_shapes=[pltpu.VMEM((ks.shape[2], nq), BF16),
                        pltpu.VMEM((KC_SEL, nq), F32), pltpu.VMEM((KC_SEL, nq), F32),
                        pltpu.VMEM((KC_SEL, nq), BF16), pltpu.VMEM((KC_SEL, nq), BF16),
                        pltpu.VMEM((8, nq), F32), pltpu.VMEM((HEAD_DIM, nq), F32)],
        compiler_params=pltpu.CompilerParams(dimension_semantics=("arbitrary", "arbitrary"),
                                             vmem_limit_bytes=VMEM_LIMIT),
        name="nsa_attention",
    )(qt, kc, vct, ks, vst, kw, vwt, gt, ovt)


def _mix_kernel(x_ref, oa_ref, u_ref, v_ref, ws_ref, bs_ref, ga_ref, gg_ref, wo_ref, g2_ref, wr_ref, br_ref,
                x1_ref, xn_ref, idx_ref, gate_ref):
    tm = x_ref.shape[0]
    rr = lax.broadcasted_iota(jnp.int32, (GM_CHUNK, GM_CHUNK), 0)
    cc = lax.broadcasted_iota(jnp.int32, (GM_CHUNK, GM_CHUNK), 1)
    grp = lax.broadcasted_iota(jnp.int32, (1, D_GM), 1) // GM_GROUP_DIM
    ws = [jnp.where(rr >= cc, ws_ref[g], 0.0).astype(BF16) for g in range(N_GM_GROUPS)]
    ys = []
    for c in range(tm // GM_CHUNK):
        vch = v_ref[c * GM_CHUNK:(c + 1) * GM_CHUNK, :].astype(BF16)
        y = bs_ref[...]
        for g in range(N_GM_GROUPS):
            y = y + jnp.where(grp == g, jnp.dot(ws[g], vch, preferred_element_type=F32), 0.0)
        ys.append(y)
    o_gm = u_ref[...] * jnp.concatenate(ys, axis=0)
    o_at = oa_ref[...]
    mixed = jnp.concatenate([(o_at * _rms(o_at)) * ga_ref[...], (o_gm * _rms(o_gm)) * gg_ref[...]], axis=-1)
    x1 = x_ref[...] + jnp.dot(mixed.astype(BF16), wo_ref[...], preferred_element_type=F32)
    x1_ref[...] = x1
    xn = (x1 * _rms(x1)) * g2_ref[...]
    for s in range(ROW_SUB):
        xn_ref[:, s, :] = xn[:, s * LANE:(s + 1) * LANE]
    logits = jnp.dot(xn, wr_ref[...], precision=HIGHEST, preferred_element_type=F32) + br_ref[...]
    lane = lax.broadcasted_iota(jnp.int32, (1, LANE), 1).astype(F32)
    idx_out = jnp.zeros((tm, LANE), F32)
    val_out = jnp.zeros((tm, LANE), F32)
    vals = []
    for k in range(TOP_K):
        mx = jnp.max(logits, axis=-1, keepdims=True)
        first = jnp.min(jnp.where(logits == mx, lane, float(LANE)), axis=-1, keepdims=True)
        logits = jnp.where(lane == first, -jnp.inf, logits)
        idx_out = jnp.where(lane == float(k), first, idx_out)
        vals.append(mx)
    es = [jnp.exp(v - vals[0]) for v in vals]
    den = es[0] + es[1] + es[2] + es[3]
    for k in range(TOP_K):
        val_out = jnp.where(lane == float(k), es[k] / den, val_out)
    idx_ref[...] = idx_out.astype(jnp.int32)
    gate_ref[...] = val_out


def _mix(x2, o_attn, u_act, v_act, gm_w_s, bias_full, ga, gg, w_out_b, g2, wr_pad, br_pad):
    n = x2.shape[0]
    row = lambda c: pl.BlockSpec((TM_MIX, c), lambda i: (i, 0))
    full = lambda a: pl.BlockSpec(a.shape, lambda i: (0,) * a.ndim)
    return pl.pallas_call(
        _mix_kernel,
        grid=(n // TM_MIX,),
        in_specs=[row(D_MODEL), row(D_ATTN), row(D_GM), row(D_GM), full(gm_w_s), full(bias_full), full(ga), full(gg),
                  full(w_out_b), full(g2), full(wr_pad), full(br_pad)],
        out_specs=[row(D_MODEL), pl.BlockSpec((TM_MIX, ROW_SUB, LANE), lambda i: (i, 0, 0)), row(LANE), row(LANE)],
        out_shape=[jax.ShapeDtypeStruct((n, D_MODEL), F32), jax.ShapeDtypeStruct((n, ROW_SUB, LANE), F32),
                   jax.ShapeDtypeStruct((n, LANE), jnp.int32), jax.ShapeDtypeStruct((n, LANE), F32)],
        compiler_params=pltpu.CompilerParams(dimension_semantics=("arbitrary",), vmem_limit_bytes=VMEM_LIMIT),
        name="mix_outproj_router",
    )(x2, o_attn, u_act, v_act, gm_w_s, bias_full, ga, gg, w_out_b, g2, wr_pad, br_pad)


def _row_gather(idx_ref, n_rows, src_hbm, dst_ref, sem):
    def start():
        for r in range(n_rows):
            pltpu.make_async_copy(src_hbm.at[pl.ds(idx_ref[0, 0, r], 1), :], dst_ref.at[pl.ds(r, 1), :], sem).start()

    def wait():
        pltpu.make_async_copy(src_hbm.at[pl.ds(0, n_rows), :], dst_ref, sem).wait()

    return start, wait


def _tile_row_gather(idx_ref, n_rows, src_hbm, dst_ref, sem):
    def start():
        for r in range(n_rows):
            t = idx_ref[0, 0, r]
            pltpu.make_async_copy(src_hbm.at[lax.shift_right_logical(t, 3), t & (ROW_SUB - 1)],
                                  dst_ref.at[r // ROW_SUB, :, r % ROW_SUB, :], sem).start()

    def wait():
        pltpu.make_async_copy(src_hbm.at[pl.ds(0, n_rows // ROW_SUB)], dst_ref, sem).wait()

    return start, wait


def _tiles_to_matrix(ref):
    rows = ref.shape[0] * ROW_SUB
    return jnp.concatenate([ref[:, c].reshape(rows, LANE) for c in range(ROW_SUB)], axis=1)


def _moe_kernel(be_ref, bv_ref, tok_ref, tok_n1_ref, tok_n2_ref, x_hbm, wgu_ref, bg_ref, bl_ref, wd_ref, bd_ref,
                o_ref, xbuf, sems, wt_s, wg_s, wl_s, wd_s):
    i = pl.program_id(0)
    slot = i % (MOE_AHEAD + 1)
    slot_n2 = (i + MOE_AHEAD) % (MOE_AHEAD + 1)
    start_cur, wait_cur = _tile_row_gather(tok_ref, BM_MOE, x_hbm, xbuf.at[slot], sems.at[slot])
    start_n1, _ = _tile_row_gather(tok_n1_ref, BM_MOE, x_hbm, xbuf.at[1], sems.at[1])
    start_n2, _ = _tile_row_gather(tok_n2_ref, BM_MOE, x_hbm, xbuf.at[slot_n2], sems.at[slot_n2])

    prev = jnp.maximum(i - 1, 0)

    @pl.when(i == 0)
    def _():
        start_cur()
        start_n1()

    @pl.when((bv_ref[i] == 1) & ((i == 0) | (be_ref[i] != be_ref[prev])))
    def _():
        tc = wt_s.shape[1]
        for c in range(2 * D_EXPERT // tc):
            wt = wgu_ref[0, :, c * tc:(c + 1) * tc].T
            for j in range(ROW_SUB):
                wt_s[j] = wt[:, j * LANE:(j + 1) * LANE]
            for first, dst in ((0, wg_s), (1, wl_s)):
                half = jnp.concatenate([wt_s[j, pl.ds(first, tc // 2, stride=2), :] for j in range(ROW_SUB)], axis=1)
                dst[c * tc // 2:(c + 1) * tc // 2, :] = half.astype(BF16)
        wd_s[...] = wd_ref[0].astype(BF16)

    @pl.when(bv_ref[i] == 1)
    def _():
        wait_cur()
        start_n2()
        xb = _tiles_to_matrix(xbuf.at[slot]).astype(BF16)
        hg = lax.dot_general(xb, wg_s[...], _NT, preferred_element_type=F32) + bg_ref[0]
        hl = lax.dot_general(xb, wl_s[...], _NT, preferred_element_type=F32) + bl_ref[0]
        hg = jnp.minimum(hg, SWIGLU_LIMIT)
        hl = jnp.clip(hl, -SWIGLU_LIMIT, SWIGLU_LIMIT)
        a = hg * jax.nn.sigmoid(SWIGLU_ALPHA * hg) * (hl + 1.0)
        o_ref[...] = jnp.dot(a.astype(BF16), wd_s[...], preferred_element_type=F32) + bd_ref[0]

    @pl.when((bv_ref[i] == 0) & ((i == 1) | ((i >= MOE_AHEAD) & (bv_ref[jnp.maximum(i - MOE_AHEAD, 0)] == 1))))
    def _():
        wait_cur()

    @pl.when(bv_ref[i] == 0)
    def _():
        o_ref[...] = jnp.zeros(o_ref.shape, F32)


def _moe(blk_expert, blk_valid, tok_blocks, xn3, w_gate_up, bg, bl, w_down, bd):
    nb = blk_expert.shape[0]
    per_e = lambda a: pl.BlockSpec((1,) + a.shape[1:], lambda i, be, bv: (be[i],) + (0,) * (a.ndim - 1))

    def tok_spec(ahead):
        return pl.BlockSpec((1, 1, BM_MOE), lambda i, be, bv: (jnp.minimum(i + ahead, nb - 1), 0, 0),
                            memory_space=pltpu.SMEM)

    grid_spec = pltpu.PrefetchScalarGridSpec(
        num_scalar_prefetch=2,
        grid=(nb,),
        in_specs=[tok_spec(0), tok_spec(1), tok_spec(MOE_AHEAD),
                  pl.BlockSpec(memory_space=pl.ANY),
                  per_e(w_gate_up), per_e(bg), per_e(bl), per_e(w_down), per_e(bd)],
        out_specs=pl.BlockSpec((BM_MOE, D_MODEL), lambda i, be, bv: (i, 0)),
        scratch_shapes=[pltpu.VMEM((MOE_AHEAD + 1, BM_MOE // ROW_SUB, ROW_SUB, ROW_SUB, LANE), F32),
                        pltpu.SemaphoreType.DMA((MOE_AHEAD + 1,)),
                        pltpu.VMEM((ROW_SUB, 256, LANE), F32), pltpu.VMEM((D_EXPERT, D_MODEL), BF16),
                        pltpu.VMEM((D_EXPERT, D_MODEL), BF16), pltpu.VMEM((D_EXPERT, D_MODEL), BF16)],
    )
    return pl.pallas_call(
        _moe_kernel,
        grid_spec=grid_spec,
        out_shape=jax.ShapeDtypeStruct((nb * BM_MOE, D_MODEL), F32),
        compiler_params=pltpu.CompilerParams(dimension_semantics=("arbitrary",), vmem_limit_bytes=VMEM_LIMIT_MOE),
        name="moe_experts",
    )(blk_expert, blk_valid, tok_blocks, tok_blocks, tok_blocks,
      xn3.reshape(xn3.shape[0] // ROW_SUB, ROW_SUB, ROW_SUB, LANE), w_gate_up, bg, bl, w_down, bd)


def _combine_kernel(dest_ref, dest_next_ref, x1_ref, gate_ref, y_hbm, o_ref, buf, sems):
    i = pl.program_id(0)
    slot = i % 2
    n_rows = TOP_K * TM_CMB
    start_cur, wait_cur = _row_gather(dest_ref, n_rows, y_hbm, buf.at[slot], sems.at[slot])
    start_next, _ = _row_gather(dest_next_ref, n_rows, y_hbm, buf.at[1 - slot], sems.at[1 - slot])

    @pl.when(i == 0)
    def _():
        start_cur()

    @pl.when(i + 1 < pl.num_programs(0))
    def _():
        start_next()

    wait_cur()
    gate = gate_ref[...]
    acc = x1_ref[...]
    for k in range(TOP_K):
        acc = acc + gate[:, k:k + 1] * buf[slot, k * TM_CMB:(k + 1) * TM_CMB, :]
    o_ref[...] = acc


def _combine(dest_blocks, x1, gate_pad, y_rows):
    n = x1.shape[0]
    nt = n // TM_CMB
    n_rows = TOP_K * TM_CMB
    return pl.pallas_call(
        _combine_kernel,
        grid=(nt,),
        in_specs=[pl.BlockSpec((1, 1, n_rows), lambda i: (i, 0, 0), memory_space=pltpu.SMEM),
                  pl.BlockSpec((1, 1, n_rows), lambda i: (jnp.minimum(i + 1, nt - 1), 0, 0),
                               memory_space=pltpu.SMEM),
                  pl.BlockSpec((TM_CMB, D_MODEL), lambda i: (i, 0)),
                  pl.BlockSpec((TM_CMB, LANE), lambda i: (i, 0)),
                  pl.BlockSpec(memory_space=pl.ANY)],
        out_specs=pl.BlockSpec((TM_CMB, D_MODEL), lambda i: (i, 0)),
        out_shape=jax.ShapeDtypeStruct((n, D_MODEL), F32),
        scratch_shapes=[pltpu.VMEM((2, n_rows, D_MODEL), F32), pltpu.SemaphoreType.DMA((2,))],
        compiler_params=pltpu.CompilerParams(dimension_semantics=("arbitrary",), vmem_limit_bytes=VMEM_LIMIT),
        name="moe_combine",
    )(dest_blocks, dest_blocks, x1, gate_pad, y_rows)


def kernel(x, norm1_g, w_in, q_norm_g, k_norm_g, cmp_pos, w_cmp1, b_cmp1, w_cmp2, b_cmp2, gm_v_norm_g, gm_w_s,
           gm_b_s, out_norm_attn_g, out_norm_gm_g, w_out, norm2_g, w_router, b_router, w_gate_up, b_gate_up,
           w_down, b_down):
    batch, seq, _ = x.shape
    n = batch * seq
    nqb = seq // Q_BLOCK
    bgn = batch * N_KV
    x2 = x.reshape(n, D_MODEL)

    c_gate = D_ATTN + 6 * D_KV
    w_r = jnp.concatenate([w_in[:, :c_gate], w_in[:, c_gate + N_GATE:], w_in[:, c_gate:c_gate + N_GATE],
                           jnp.zeros((D_MODEL, LANE - N_GATE), F32)], axis=1).astype(BF16)
    q, kc_raw, vc_raw, ks, vs, kw, vw, gates, u_act, v_act = _inproj(x2, norm1_g, w_r, q_norm_g, k_norm_g,
                                                                     gm_v_norm_g)

    kc = _compress(kc_raw, cmp_pos[0], w_cmp1[0], b_cmp1[0], w_cmp2[0], b_cmp2[0], k_norm_g[0], batch, seq, True)
    vc = _compress(vc_raw, cmp_pos[1], w_cmp1[1], b_cmp1[1], w_cmp2[1], b_cmp2[1], k_norm_g[0], batch, seq, False)

    nq = N_REP * Q_BLOCK
    ncmp = seq // CMP_STRIDE

    nsel = seq // SEL_BLOCK
    oh_w = -(-nsel // LANE) * LANE

    def with_pos(k, pos, one_hot=False):
        feat = np.zeros((pos.shape[0], LANE - HEAD_DIM + (oh_w if one_hot else 0)), np.float32)
        feat[:, 0] = feat[:, 1] = pos // SEL_BLOCK
        feat[:, 2] = feat[:, 3] = pos % SEL_BLOCK
        if one_hot:
            feat[np.arange(pos.shape[0]), LANE - HEAD_DIM + pos // SEL_BLOCK] = 1.0
        feat = jnp.broadcast_to(jnp.asarray(feat, BF16)[None], (bgn,) + feat.shape)
        return jnp.concatenate([k.astype(BF16), feat], axis=-1)

    def front_pad(a, axis):
        pad = [(0, 0)] * a.ndim
        pad[axis] = (WINDOW, 0)
        return jnp.pad(a, pad)

    def tok_major(a):
        return a.reshape(batch, seq, N_KV, HEAD_DIM).transpose(0, 2, 1, 3).reshape(bgn, seq, HEAD_DIM)

    def feat_major(a):
        return a.reshape(batch, seq, N_KV, HEAD_DIM).transpose(0, 2, 3, 1).reshape(bgn, HEAD_DIM, seq).astype(BF16)

    head = np.arange(N_KV)[:, None] * N_REP + np.arange(nq)[None, :] // Q_BLOCK
    coef = np.exp2(-(head + 1.0)) * LOG2E
    c_hi = coef.astype(BF16).astype(np.float64)
    c_lo = (coef - c_hi).astype(BF16).astype(np.float64)
    qrows = np.zeros((N_KV, LANE - HEAD_DIM, nq), np.float32)
    qrows[:, 0], qrows[:, 1], qrows[:, 2], qrows[:, 3] = SEL_BLOCK * c_hi, SEL_BLOCK * c_lo, c_hi, c_lo
    qrows = jnp.broadcast_to(jnp.asarray(qrows, BF16)[None, :, None], (batch, N_KV, nqb, LANE - HEAD_DIM, nq))
    qt = (q.reshape(batch, nqb, Q_BLOCK, N_KV, N_REP, HEAD_DIM).transpose(0, 3, 1, 5, 4, 2)
          .reshape(batch, N_KV, nqb, HEAD_DIM, nq).astype(BF16))
    qt = jnp.concatenate([qt, qrows], axis=3).reshape(bgn, nqb, LANE, nq)
    gt = (gates[:, :N_GATE].reshape(batch, nqb, Q_BLOCK, N_KV, N_REP, 3).transpose(0, 3, 1, 5, 4, 2)
          .reshape(bgn, nqb, 3, nq))
    pos_t = np.arange(seq)
    pos_c = np.arange(ncmp) * CMP_STRIDE + (CMP_LEN - 1)
    kc_b = with_pos(kc.reshape(bgn, ncmp, HEAD_DIM), pos_c)
    vct = vc.reshape(bgn, ncmp, HEAD_DIM).transpose(0, 2, 1).astype(BF16)
    c0 = np.arange(ncmp)[None, :] * CMP_STRIDE
    n0 = np.arange(seq // SEL_BLOCK)[:, None] * SEL_BLOCK
    ovt = np.clip(np.minimum(c0 + CMP_LEN, n0 + SEL_BLOCK) - np.maximum(c0, n0), 0, None) / CMP_LEN
    pos_w = np.maximum(np.arange(seq + WINDOW) - WINDOW, 0)
    ot = _attention(qt, kc_b, vct, with_pos(tok_major(ks), pos_t, one_hot=True), feat_major(vs),
                    with_pos(front_pad(tok_major(kw), 1), pos_w), front_pad(feat_major(vw), 2), gt,
                    jnp.asarray(ovt, BF16))
    o_attn = (ot.reshape(batch, N_KV, nqb, HEAD_DIM, N_REP, Q_BLOCK).transpose(0, 2, 5, 1, 4, 3)
              .reshape(n, D_ATTN))

    bias_full = jnp.repeat(gm_b_s.T, GM_GROUP_DIM, axis=1)
    wr_pad = jnp.concatenate([w_router, jnp.zeros((D_MODEL, LANE - N_EXPERTS), F32)], axis=1)
    br_pad = jnp.concatenate([b_router, jnp.full((LANE - N_EXPERTS,), NEG, F32)]).reshape(1, LANE)
    x1, xn3, idx_pad, gate_pad = _mix(x2, o_attn, u_act, v_act, gm_w_s, bias_full,
                                     out_norm_attn_g.reshape(1, D_ATTN), out_norm_gm_g.reshape(1, D_GM),
                                     w_out.astype(BF16), norm2_g.reshape(1, D_MODEL), wr_pad, br_pad)

    s_tot = n * TOP_K
    nb = s_tot // BM_MOE + N_EXPERTS - 1 + MOE_AHEAD
    e_flat = idx_pad[:, :TOP_K].reshape(s_tot)
    onehot = (e_flat[:, None] == jnp.arange(N_EXPERTS, dtype=jnp.int32)[None, :]).astype(jnp.int32)
    csum = jnp.cumsum(onehot, axis=0)
    rank = jnp.sum(csum * onehot, axis=1) - 1
    counts = csum[-1]
    padded = ((counts + BM_MOE - 1) // BM_MOE) * BM_MOE
    pad_end = jnp.cumsum(padded)
    pad_start = pad_end - padded
    dest = pad_start[e_flat] + rank
    tok_flat = jnp.arange(s_tot, dtype=jnp.int32) // TOP_K
    tok_buf = jnp.zeros((nb * BM_MOE,), jnp.int32).at[dest].set(tok_flat)
    blk_start = jnp.arange(nb, dtype=jnp.int32) * BM_MOE
    blk_expert = jnp.minimum(jnp.sum((blk_start[:, None] >= pad_end[None, :]).astype(jnp.int32), axis=1),
                             N_EXPERTS - 1)
    blk_valid = (blk_start < pad_end[-1]).astype(jnp.int32)

    bg = b_gate_up[:, 0::2].reshape(N_EXPERTS, 1, D_EXPERT)
    bl = b_gate_up[:, 1::2].reshape(N_EXPERTS, 1, D_EXPERT)
    y_rows = _moe(blk_expert, blk_valid, tok_buf.reshape(nb, 1, BM_MOE), xn3, w_gate_up, bg, bl, w_down,
                  b_down.reshape(N_EXPERTS, 1, D_MODEL))

    dest_blocks = (dest.reshape(n // TM_CMB, TM_CMB, TOP_K).transpose(0, 2, 1)
                   .reshape(n // TM_CMB, 1, TOP_K * TM_CMB).astype(jnp.int32))
    out = _combine(dest_blocks, x1, gate_pad, y_rows)
    return out.reshape(batch, seq, D_MODEL)
```

```python
import functools

import jax
import jax.numpy as jnp
import numpy as np
from jax import lax
from jax.experimental import pallas as pl
from jax.experimental.pallas import tpu as pltpu

F32 = jnp.float32
BF16 = jnp.bfloat16
HIGHEST = lax.Precision.HIGHEST
_NT = (((1,), (1,)), ((), ()))

D_MODEL = 1024
N_HEADS = 8
HEAD_DIM = 64
N_KV = 2
N_REP = N_HEADS // N_KV
D_ATTN = N_HEADS * HEAD_DIM
D_KV = N_KV * HEAD_DIM
N_GM_GROUPS = 8
GM_GROUP_DIM = 64
D_GM = N_GM_GROUPS * GM_GROUP_DIM
N_GATE = 3 * N_HEADS
CMP_LEN = 32
CMP_STRIDE = 16
CMP_HIDDEN = 128
SEL_BLOCK = 64
N_SEL = 16
WINDOW = 512
Q_BLOCK = 128
FORCE_BONUS = 1.0e4
GM_CHUNK = 128
N_EXPERTS = 32
TOP_K = 4
D_EXPERT = 1024
SWIGLU_LIMIT = 7.0
SWIGLU_ALPHA = 1.702
EPS = 1e-6
NEG = -1.0e30
LOG2E = 1.4426950408889634

LANE = 128
ROW_SUB = D_MODEL // LANE
VMEM_LIMIT = 48 * 1024 * 1024
VMEM_LIMIT_MOE = 56 * 1024 * 1024

_C_Q = 0
_C_KC = _C_Q + D_ATTN
_C_VC = _C_KC + D_KV
_C_KS = _C_VC + D_KV
_C_VS = _C_KS + D_KV
_C_KW = _C_VS + D_KV
_C_VW = _C_KW + D_KV
_C_U = _C_VW + D_KV
_C_V = _C_U + D_GM
_C_G = _C_V + D_GM
D_IN_PAD = _C_G + LANE

TM_IN = 256
TM_MIX = 256
KC_SEL = 512
BM_MOE = 256
MOE_AHEAD = 2
TM_CMB = 128


def _rms(x, eps=EPS):
    return lax.rsqrt(jnp.mean(x * x, axis=-1, keepdims=True) + eps)


def _inproj_kernel(x_ref, g1_ref, w_ref, qg_ref, kg_ref, vg_ref,
                   q_ref, kc_ref, vc_ref, ks_ref, vs_ref, kw_ref, vw_ref, gate_ref, u_ref, v_ref):
    x = x_ref[...]
    h = (x * _rms(x)) * g1_ref[...]
    z = jnp.dot(h.astype(BF16), w_ref[...], preferred_element_type=F32)

    def head_norm(col0, n, gain, scale):
        outs = []
        for i in range(n):
            sl = z[:, col0 + i * HEAD_DIM: col0 + (i + 1) * HEAD_DIM]
            outs.append((sl * _rms(sl)) * gain * scale)
        return jnp.concatenate(outs, axis=-1)

    q_ref[...] = head_norm(_C_Q, N_HEADS, qg_ref[...], HEAD_DIM ** -0.5 * LOG2E)
    kc_ref[...] = z[:, _C_KC:_C_KC + D_KV]
    vc_ref[...] = z[:, _C_VC:_C_VC + D_KV]
    ks_ref[...] = head_norm(_C_KS, N_KV, kg_ref[1:2, :], 1.0)
    vs_ref[...] = z[:, _C_VS:_C_VS + D_KV]
    kw_ref[...] = head_norm(_C_KW, N_KV, kg_ref[2:3, :], 1.0)
    vw_ref[...] = z[:, _C_VW:_C_VW + D_KV]
    gate_ref[...] = jax.nn.sigmoid(z[:, _C_G:_C_G + LANE])
    u_ref[...] = jax.nn.gelu(z[:, _C_U:_C_U + D_GM])
    gv = jax.nn.gelu(z[:, _C_V:_C_V + D_GM])
    v_ref[...] = (gv * _rms(gv)) * vg_ref[...]


def _inproj(x2, norm1_g, w_r, q_norm_g, k_norm_g, gm_v_norm_g):
    n = x2.shape[0]
    row = lambda c: pl.BlockSpec((TM_IN, c), lambda i: (i, 0))
    full = lambda a: pl.BlockSpec(a.shape, lambda i: (0,) * a.ndim)
    g1 = norm1_g.reshape(1, D_MODEL)
    qg = q_norm_g.reshape(1, HEAD_DIM)
    vg = gm_v_norm_g.reshape(1, D_GM)
    widths = (D_ATTN, D_KV, D_KV, D_KV, D_KV, D_KV, D_KV, LANE, D_GM, D_GM)
    return pl.pallas_call(
        _inproj_kernel,
        grid=(n // TM_IN,),
        in_specs=[row(D_MODEL), full(g1), full(w_r), full(qg), full(k_norm_g), full(vg)],
        out_specs=[row(c) for c in widths],
        out_shape=[jax.ShapeDtypeStruct((n, c), F32) for c in widths],
        compiler_params=pltpu.CompilerParams(dimension_semantics=("arbitrary",), vmem_limit_bytes=VMEM_LIMIT),
        name="inproj",
    )(x2, g1, w_r, qg, k_norm_g, vg)


def _compress_kernel(a_ref, pos_ref, w1_ref, w1a_ref, w1b_ref, b1_ref, w2_ref, b2_ref, kg_ref, o_ref, *, norm):
    a = a_ref[0]
    nseg = a.shape[0]
    c = jnp.dot(pos_ref[...], w1_ref[...], precision=HIGHEST, preferred_element_type=F32)[0:1] + b1_ref[...]
    row = lax.broadcasted_iota(jnp.int32, (nseg, 1), 0)
    for g in range(N_KV):
        pa = jnp.dot(a, w1a_ref[g], precision=HIGHEST, preferred_element_type=F32)
        pb = jnp.dot(a, w1b_ref[g], precision=HIGHEST, preferred_element_type=F32)
        hid = jax.nn.gelu(pa + pltpu.roll(pb, nseg - 1, 0) + c)
        out = jnp.dot(hid, w2_ref[...], precision=HIGHEST, preferred_element_type=F32) + b2_ref[...]
        if norm:
            out = (out * _rms(out)) * kg_ref[...]
        o_ref[0, g] = jnp.where(row < nseg - 1, out, 0.0)


def _compress(raw, pos, w1, b1, w2, b2, gain, batch, seq, norm):
    nseg = seq // CMP_STRIDE
    half = CMP_STRIDE * HEAD_DIM
    a = raw.reshape(batch, nseg, CMP_STRIDE * D_KV)
    pos8 = jnp.broadcast_to(pos.reshape(1, CMP_LEN * HEAD_DIM), (8, CMP_LEN * HEAD_DIM))

    def expand(wh):
        wh = wh.reshape(CMP_STRIDE, HEAD_DIM, CMP_HIDDEN)
        z = jnp.zeros((N_KV, CMP_STRIDE, N_KV, HEAD_DIM, CMP_HIDDEN), F32)
        for g in range(N_KV):
            z = z.at[g, :, g].set(wh)
        return z.reshape(N_KV, CMP_STRIDE * D_KV, CMP_HIDDEN)

    w1a, w1b = expand(w1[:half]), expand(w1[half:])
    b1r, b2r, gr = b1.reshape(1, CMP_HIDDEN), b2.reshape(1, HEAD_DIM), gain.reshape(1, HEAD_DIM)
    full = lambda t: pl.BlockSpec(t.shape, lambda i: (0,) * t.ndim)
    return pl.pallas_call(
        functools.partial(_compress_kernel, norm=norm),
        grid=(batch,),
        in_specs=[pl.BlockSpec((1, nseg, CMP_STRIDE * D_KV), lambda i: (i, 0, 0)),
                  full(pos8), full(w1), full(w1a), full(w1b), full(b1r), full(w2), full(b2r), full(gr)],
        out_specs=pl.BlockSpec((1, N_KV, nseg, HEAD_DIM), lambda i: (i, 0, 0, 0)),
        out_shape=jax.ShapeDtypeStruct((batch, N_KV, nseg, HEAD_DIM), F32),
        compiler_params=pltpu.CompilerParams(dimension_semantics=("arbitrary",), vmem_limit_bytes=VMEM_LIMIT),
        name="compress_k" if norm else "compress_v",
    )(a, pos8, w1, w1a, w1b, b1r, w2, b2r, gr)


def _attn_kernel(qt_ref, kc_ref, vct_ref, ks_ref, vst_ref, kw_ref, vwt_ref, g_ref, ovt_ref, o_ref,
                 qs_ref, s0_ref, s1_ref, p0_ref, p1_ref, st_ref, acc_ref):
    qb = pl.program_id(1)
    nq = N_REP * Q_BLOCK
    q0 = qb * Q_BLOCK
    qt = qt_ref[0, 0]
    ql = lax.broadcasted_iota(jnp.int32, (1, nq), 1) % Q_BLOCK
    t_row = (q0 + ql).astype(F32)
    m_init = 0.5 * NEG

    def online(s, m, l):
        m_new = jnp.maximum(m, jnp.max(s, axis=0, keepdims=True))
        alpha = jnp.exp2(m - m_new)
        p = jnp.exp2(s - m_new)
        return p, m_new, alpha, alpha * l + jnp.sum(p, axis=0, keepdims=True)

    def inv(l):
        return jnp.where(l > 0.0, 1.0 / l, 0.0)

    m0 = jnp.full((1, nq), m_init, F32)
    l0 = jnp.zeros((1, nq), F32)
    a0 = jnp.zeros((HEAD_DIM, nq), F32)

    ncmp = kc_ref.shape[1]
    s = jnp.dot(kc_ref[0], qt, preferred_element_type=F32)
    c_end = (lax.broadcasted_iota(jnp.int32, (ncmp, 1), 0) * CMP_STRIDE + (CMP_LEN - 1)).astype(F32)
    p, _, _, l = online(jnp.where(c_end <= t_row, s, NEG), m0, l0)
    p = p * inv(l)
    o_cmp = jnp.dot(vct_ref[0], p.astype(BF16), preferred_element_type=F32)

    psum = p[:, 0:Q_BLOCK]
    for r in range(1, N_REP):
        psum = psum + p[:, r * Q_BLOCK:(r + 1) * Q_BLOCK]
    nsel = ovt_ref.shape[0]
    p_hi = psum.astype(BF16)
    p_lo = (psum - p_hi.astype(F32)).astype(BF16)
    imp = (jnp.dot(ovt_ref[...], p_hi, preferred_element_type=F32)
           + jnp.dot(ovt_ref[...], p_lo, preferred_element_type=F32))
    n_col = lax.broadcasted_iota(jnp.int32, (nsel, 1), 0).astype(F32)
    n_start = n_col * SEL_BLOCK
    tq = t_row[:, 0:Q_BLOCK]
    cur = jnp.floor(tq * (1.0 / SEL_BLOCK)) * SEL_BLOCK
    forced = (n_start == cur) | (n_start == 0.0)
    valid = n_start <= tq
    imp = jnp.where(forced, imp + FORCE_BONUS, imp)
    imp = jnp.where(valid, imp, NEG)
    sel = jnp.zeros((nsel, Q_BLOCK), F32)
    for _ in range(min(N_SEL, nsel)):
        mx = jnp.max(imp, axis=0, keepdims=True)
        first = jnp.min(jnp.where(imp == mx, n_col, float(nsel)), axis=0, keepdims=True)
        hit = n_col == first
        sel = jnp.where(hit, 1.0, sel)
        imp = jnp.where(hit, -jnp.inf, imp)
    selb = jnp.where(valid & (sel > 0.0), 0.0, NEG).astype(BF16)
    qs_ref[0:LANE, :] = qt
    qs_ref[LANE:LANE + nsel, :] = jnp.concatenate([selb] * N_REP, axis=1)
    if qs_ref.shape[0] > LANE + nsel:
        qs_ref[LANE + nsel:, :] = jnp.zeros((qs_ref.shape[0] - LANE - nsel, nq), BF16)

    def attend(k_blk, vt_blk, q_op, bias, carry):
        m, l, acc = carry
        s = jnp.dot(k_blk, q_op, preferred_element_type=F32)
        if bias is not None:
            s = s + bias
        p, m, alpha, l = online(s, m, l)
        pv = jnp.dot(vt_blk, p.astype(BF16), preferred_element_type=F32)
        return m, l, alpha * acc + pv

    seq = ks_ref.shape[1]

    def scores(j):
        k0 = pl.multiple_of(jnp.minimum(j * KC_SEL, seq - KC_SEL), KC_SEL)
        s = jnp.dot(ks_ref[0, pl.ds(k0, KC_SEL), :], qs_ref[...], preferred_element_type=F32)
        return s, jnp.max(s, axis=0, keepdims=True)

    def values(j, p):
        k0 = pl.multiple_of(jnp.maximum(j, 0) * KC_SEL, KC_SEL)
        return jnp.dot(vst_ref[0, :, pl.ds(k0, KC_SEL)], p, preferred_element_type=F32)

    def stage(j, s_cur, s_nxt, p_cur, p_prv):
        m, l, alpha_prev, mx = st_ref[0:1, :], st_ref[1:2, :], st_ref[2:3, :], st_ref[3:4, :]
        m_new = jnp.maximum(m, mx)
        alpha = jnp.exp2(m - m_new)
        p = jnp.exp2(s_cur[...] - m_new)
        acc_ref[...] = alpha_prev * acc_ref[...] + values(j - 1, p_prv[...])
        p_cur[...] = p.astype(BF16)
        s_next, mx_next = scores(j + 1)
        s_nxt[...] = s_next
        st_ref[0:1, :] = m_new
        st_ref[1:2, :] = alpha * l + jnp.sum(p, axis=0, keepdims=True)
        st_ref[2:3, :] = alpha
        st_ref[3:4, :] = mx_next

    n_full = q0 // KC_SEL
    s_first, mx_first = scores(0)

    @pl.when(n_full % 2 == 0)
    def _():
        s0_ref[...] = s_first

    @pl.when(n_full % 2 == 1)
    def _():
        s1_ref[...] = s_first

    p0_ref[...] = jnp.zeros(p0_ref.shape, BF16)
    p1_ref[...] = jnp.zeros(p1_ref.shape, BF16)
    st_ref[0:1, :] = m0
    st_ref[1:2, :] = l0
    st_ref[2:3, :] = jnp.ones((1, nq), F32)
    st_ref[3:4, :] = mx_first
    acc_ref[...] = a0

    def sel_body(j, carry):
        @pl.when((n_full - j) % 2 == 0)
        def _():
            stage(j, s0_ref, s1_ref, p0_ref, p1_ref)

        @pl.when((n_full - j) % 2 == 1)
        def _():
            stage(j, s1_ref, s0_ref, p1_ref, p0_ref)
        return carry

    lax.fori_loop(0, n_full, sel_body, 0)
    pos_last = (n_full * KC_SEL + lax.broadcasted_iota(jnp.int32, (KC_SEL, 1), 0)).astype(F32)
    p, _, alpha, l_sel = online(s0_ref[...] + jnp.where(pos_last <= t_row, 0.0, NEG), st_ref[0:1, :], st_ref[1:2, :])
    acc = st_ref[2:3, :] * acc_ref[...] + values(n_full - 1, p1_ref[...])
    o_sel = alpha * acc + values(n_full, p.astype(BF16))

    n_wk = WINDOW + Q_BLOCK
    kk = lax.broadcasted_iota(jnp.int32, (n_wk, 1), 0)
    in_win = (kk - WINDOW <= ql) & (kk > ql) & (kk >= WINDOW - q0)
    _, l_win, o_win = attend(kw_ref[0, pl.ds(pl.multiple_of(q0, Q_BLOCK), n_wk), :],
                             vwt_ref[0, :, pl.ds(pl.multiple_of(q0, Q_BLOCK), n_wk)], qt,
                             jnp.where(in_win, 0.0, NEG), (m0, l0, a0))

    gt = g_ref[0, 0]
    o_ref[0, 0] = (gt[0:1] * o_cmp + gt[1:2] * (o_sel * inv(l_sel)) + gt[2:3] * (o_win * inv(l_win)))


def _attention(qt, kc, vct, ks, vst, kw, vwt, gt, ovt):
    bgn, nqb = qt.shape[0], qt.shape[1]
    seq = ks.shape[1]
    nq = N_REP * Q_BLOCK
    per_bg = lambda a: pl.BlockSpec((1,) + a.shape[1:], lambda b, i: (b,) + (0,) * (a.ndim - 1))
    return pl.pallas_call(
        _attn_kernel,
        grid=(bgn, nqb),
        in_specs=[pl.BlockSpec((1, 1, LANE, nq), lambda b, i: (b, i, 0, 0)),
                  per_bg(kc), per_bg(vct), per_bg(ks), per_bg(vst), per_bg(kw), per_bg(vwt),
                  pl.BlockSpec((1, 1, 3, nq), lambda b, i: (b, i, 0, 0)),
                  pl.BlockSpec(ovt.shape, lambda b, i: (0, 0))],
        out_specs=pl.BlockSpec((1, 1, HEAD_DIM, nq), lambda b, i: (b, i, 0, 0)),
        out_shape=jax.ShapeDtypeStruct((bgn, nqb, HEAD_DIM, nq), F32),
        scratch_shapes=[pltpu.VMEM((ks.shape[2], nq), BF16),
                        pltpu.VMEM((KC_SEL, nq), F32), pltpu.VMEM((KC_SEL, nq), F32),
                        pltpu.VMEM((KC_SEL, nq), BF16), pltpu.VMEM((KC_SEL, nq), BF16),
                        pltpu.VMEM((8, nq), F32), pltpu.VMEM((HEAD_DIM, nq), F32)],
        compiler_params=pltpu.CompilerParams(dimension_semantics=("arbitrary", "arbitrary"),
                                             vmem_limit_bytes=VMEM_LIMIT),
        name="nsa_attention",
    )(qt, kc, vct, ks, vst, kw, vwt, gt, ovt)


def _mix_kernel(x_ref, oa_ref, u_ref, v_ref, ws_ref, bs_ref, ga_ref, gg_ref, wo_ref, g2_ref, wr_ref, br_ref,
                x1_ref, xn_ref, idx_ref, gate_ref):
    tm = x_ref.shape[0]
    rr = lax.broadcasted_iota(jnp.int32, (GM_CHUNK, GM_CHUNK), 0)
    cc = lax.broadcasted_iota(jnp.int32, (GM_CHUNK, GM_CHUNK), 1)
    grp = lax.broadcasted_iota(jnp.int32, (1, D_GM), 1) // GM_GROUP_DIM
    ws = [jnp.where(rr >= cc, ws_ref[g], 0.0).astype(BF16) for g in range(N_GM_GROUPS)]
    ys = []
    for c in range(tm // GM_CHUNK):
        vch = v_ref[c * GM_CHUNK:(c + 1) * GM_CHUNK, :].astype(BF16)
        y = bs_ref[...]
        for g in range(N_GM_GROUPS):
            y = y + jnp.where(grp == g, jnp.dot(ws[g], vch, preferred_element_type=F32), 0.0)
        ys.append(y)
    o_gm = u_ref[...] * jnp.concatenate(ys, axis=0)
    o_at = oa_ref[...]
    mixed = jnp.concatenate([(o_at * _rms(o_at)) * ga_ref[...], (o_gm * _rms(o_gm)) * gg_ref[...]], axis=-1)
    x1 = x_ref[...] + jnp.dot(mixed.astype(BF16), wo_ref[...], preferred_element_type=F32)
    x1_ref[...] = x1
    xn = (x1 * _rms(x1)) * g2_ref[...]
    for s in range(ROW_SUB):
        xn_ref[:, s, :] = xn[:, s * LANE:(s + 1) * LANE]
    logits = jnp.dot(xn, wr_ref[...], precision=HIGHEST, preferred_element_type=F32) + br_ref[...]
    lane = lax.broadcasted_iota(jnp.int32, (1, LANE), 1).astype(F32)
    idx_out = jnp.zeros((tm, LANE), F32)
    val_out = jnp.zeros((tm, LANE), F32)
    vals = []
    for k in range(TOP_K):
        mx = jnp.max(logits, axis=-1, keepdims=True)
        first = jnp.min(jnp.where(logits == mx, lane, float(LANE)), axis=-1, keepdims=True)
        logits = jnp.where(lane == first, -jnp.inf, logits)
        idx_out = jnp.where(lane == float(k), first, idx_out)
        vals.append(mx)
    es = [jnp.exp(v - vals[0]) for v in vals]
    den = es[0] + es[1] + es[2] + es[3]
    for k in range(TOP_K):
        val_out = jnp.where(lane == float(k), es[k] / den, val_out)
    idx_ref[...] = idx_out.astype(jnp.int32)
    gate_ref[...] = val_out


def _mix(x2, o_attn, u_act, v_act, gm_w_s, bias_full, ga, gg, w_out_b, g2, wr_pad, br_pad):
    n = x2.shape[0]
    row = lambda c: pl.BlockSpec((TM_MIX, c), lambda i: (i, 0))
    full = lambda a: pl.BlockSpec(a.shape, lambda i: (0,) * a.ndim)
    return pl.pallas_call(
        _mix_kernel,
        grid=(n // TM_MIX,),
        in_specs=[row(D_MODEL), row(D_ATTN), row(D_GM), row(D_GM), full(gm_w_s), full(bias_full), full(ga), full(gg),
                  full(w_out_b), full(g2), full(wr_pad), full(br_pad)],
        out_specs=[row(D_MODEL), pl.BlockSpec((TM_MIX, ROW_SUB, LANE), lambda i: (i, 0, 0)), row(LANE), row(LANE)],
        out_shape=[jax.ShapeDtypeStruct((n, D_MODEL), F32), jax.ShapeDtypeStruct((n, ROW_SUB, LANE), F32),
                   jax.ShapeDtypeStruct((n, LANE), jnp.int32), jax.ShapeDtypeStruct((n, LANE), F32)],
        compiler_params=pltpu.CompilerParams(dimension_semantics=("arbitrary",), vmem_limit_bytes=VMEM_LIMIT),
        name="mix_outproj_router",
    )(x2, o_attn, u_act, v_act, gm_w_s, bias_full, ga, gg, w_out_b, g2, wr_pad, br_pad)


def _row_gather(idx_ref, n_rows, src_hbm, dst_ref, sem):
    def start():
        for r in range(n_rows):
            pltpu.make_async_copy(src_hbm.at[pl.ds(idx_ref[0, 0, r], 1), :], dst_ref.at[pl.ds(r, 1), :], sem).start()

    def wait():
        pltpu.make_async_copy(src_hbm.at[pl.ds(0, n_rows), :], dst_ref, sem).wait()

    return start, wait


def _tile_row_gather(idx_ref, n_rows, src_hbm, dst_ref, sem):
    def start():
        for r in range(n_rows):
            t = idx_ref[0, 0, r]
            pltpu.make_async_copy(src_hbm.at[lax.shift_right_logical(t, 3), t & (ROW_SUB - 1)],
                                  dst_ref.at[r // ROW_SUB, :, r % ROW_SUB, :], sem).start()

    def wait():
        pltpu.make_async_copy(src_hbm.at[pl.ds(0, n_rows // ROW_SUB)], dst_ref, sem).wait()

    return start, wait


def _tiles_to_matrix(ref):
    rows = ref.shape[0] * ROW_SUB
    return jnp.concatenate([ref[:, c].reshape(rows, LANE) for c in range(ROW_SUB)], axis=1)


def _moe_kernel(be_ref, bv_ref, tok_ref, tok_n1_ref, tok_n2_ref, x_hbm, wgu_ref, bg_ref, bl_ref, wd_ref, bd_ref,
                o_ref, xbuf, sems, wt_s, wg_s, wl_s, wd_s):
    i = pl.program_id(0)
    slot = i % (MOE_AHEAD + 1)
    slot_n2 = (i + MOE_AHEAD) % (MOE_AHEAD + 1)
    start_cur, wait_cur = _tile_row_gather(tok_ref, BM_MOE, x_hbm, xbuf.at[slot], sems.at[slot])
    start_n1, _ = _tile_row_gather(tok_n1_ref, BM_MOE, x_hbm, xbuf.at[1], sems.at[1])
    start_n2, _ = _tile_row_gather(tok_n2_ref, BM_MOE, x_hbm, xbuf.at[slot_n2], sems.at[slot_n2])

    prev = jnp.maximum(i - 1, 0)

    @pl.when(i == 0)
    def _():
        start_cur()
        start_n1()

    @pl.when((bv_ref[i] == 1) & ((i == 0) | (be_ref[i] != be_ref[prev])))
    def _():
        tc = wt_s.shape[1]
        for c in range(2 * D_EXPERT // tc):
            wt = wgu_ref[0, :, c * tc:(c + 1) * tc].T
            for j in range(ROW_SUB):
                wt_s[j] = wt[:, j * LANE:(j + 1) * LANE]
            for first, dst in ((0, wg_s), (1, wl_s)):
                half = jnp.concatenate([wt_s[j, pl.ds(first, tc // 2, stride=2), :] for j in range(ROW_SUB)], axis=1)
                dst[:, c * tc // 2:(c + 1) * tc // 2] = half.T.astype(BF16)
        wd_s[...] = wd_ref[0].astype(BF16)

    @pl.when(bv_ref[i] == 1)
    def _():
        wait_cur()
        start_n2()
        xb = _tiles_to_matrix(xbuf.at[slot]).astype(BF16)
        hg = jnp.dot(xb, wg_s[...], preferred_element_type=F32) + bg_ref[0]
        hl = jnp.dot(xb, wl_s[...], preferred_element_type=F32) + bl_ref[0]
        hg = jnp.minimum(hg, SWIGLU_LIMIT)
        hl = jnp.clip(hl, -SWIGLU_LIMIT, SWIGLU_LIMIT)
        a = hg * jax.nn.sigmoid(SWIGLU_ALPHA * hg) * (hl + 1.0)
        o_ref[...] = jnp.dot(a.astype(BF16), wd_s[...], preferred_element_type=F32) + bd_ref[0]

    @pl.when((bv_ref[i] == 0) & ((i == 1) | ((i >= MOE_AHEAD) & (bv_ref[jnp.maximum(i - MOE_AHEAD, 0)] == 1))))
    def _():
        wait_cur()

    @pl.when(bv_ref[i] == 0)
    def _():
        o_ref[...] = jnp.zeros(o_ref.shape, F32)


def _moe(blk_expert, blk_valid, tok_blocks, xn3, w_gate_up, bg, bl, w_down, bd):
    nb = blk_expert.shape[0]
    per_e = lambda a: pl.BlockSpec((1,) + a.shape[1:], lambda i, be, bv: (be[i],) + (0,) * (a.ndim - 1))

    def tok_spec(ahead):
        return pl.BlockSpec((1, 1, BM_MOE), lambda i, be, bv: (jnp.minimum(i + ahead, nb - 1), 0, 0),
                            memory_space=pltpu.SMEM)

    grid_spec = pltpu.PrefetchScalarGridSpec(
        num_scalar_prefetch=2,
        grid=(nb,),
        in_specs=[tok_spec(0), tok_spec(1), tok_spec(MOE_AHEAD),
                  pl.BlockSpec(memory_space=pl.ANY),
                  per_e(w_gate_up), per_e(bg), per_e(bl), per_e(w_down), per_e(bd)],
        out_specs=pl.BlockSpec((BM_MOE, D_MODEL), lambda i, be, bv: (i, 0)),
        scratch_shapes=[pltpu.VMEM((MOE_AHEAD + 1, BM_MOE // ROW_SUB, ROW_SUB, ROW_SUB, LANE), F32),
                        pltpu.SemaphoreType.DMA((MOE_AHEAD + 1,)),
                        pltpu.VMEM((ROW_SUB, 256, LANE), F32), pltpu.VMEM((D_EXPERT, D_MODEL), BF16),
                        pltpu.VMEM((D_EXPERT, D_MODEL), BF16), pltpu.VMEM((D_EXPERT, D_MODEL), BF16)],
    )
    return pl.pallas_call(
        _moe_kernel,
        grid_spec=grid_spec,
        out_shape=jax.ShapeDtypeStruct((nb * BM_MOE, D_MODEL), F32),
        compiler_params=pltpu.CompilerParams(dimension_semantics=("arbitrary",), vmem_limit_bytes=VMEM_LIMIT_MOE),
        name="moe_experts",
    )(blk_expert, blk_valid, tok_blocks, tok_blocks, tok_blocks,
      xn3.reshape(xn3.shape[0] // ROW_SUB, ROW_SUB, ROW_SUB, LANE), w_gate_up, bg, bl, w_down, bd)


def _combine_kernel(dest_ref, dest_next_ref, x1_ref, gate_ref, y_hbm, o_ref, buf, sems):
    i = pl.program_id(0)
    slot = i % 2
    n_rows = TOP_K * TM_CMB
    start_cur, wait_cur = _row_gather(dest_ref, n_rows, y_hbm, buf.at[slot], sems.at[slot])
    start_next, _ = _row_gather(dest_next_ref, n_rows, y_hbm, buf.at[1 - slot], sems.at[1 - slot])

    @pl.when(i == 0)
    def _():
        start_cur()

    @pl.when(i + 1 < pl.num_programs(0))
    def _():
        start_next()

    wait_cur()
    gate = gate_ref[...]
    acc = x1_ref[...]
    for k in range(TOP_K):
        acc = acc + gate[:, k:k + 1] * buf[slot, k * TM_CMB:(k + 1) * TM_CMB, :]
    o_ref[...] = acc


def _combine(dest_blocks, x1, gate_pad, y_rows):
    n = x1.shape[0]
    nt = n // TM_CMB
    n_rows = TOP_K * TM_CMB
    return pl.pallas_call(
        _combine_kernel,
        grid=(nt,),
        in_specs=[pl.BlockSpec((1, 1, n_rows), lambda i: (i, 0, 0), memory_space=pltpu.SMEM),
                  pl.BlockSpec((1, 1, n_rows), lambda i: (jnp.minimum(i + 1, nt - 1), 0, 0),
                               memory_space=pltpu.SMEM),
                  pl.BlockSpec((TM_CMB, D_MODEL), lambda i: (i, 0)),
                  pl.BlockSpec((TM_CMB, LANE), lambda i: (i, 0)),
                  pl.BlockSpec(memory_space=pl.ANY)],
        out_specs=pl.BlockSpec((TM_CMB, D_MODEL), lambda i: (i, 0)),
        out_shape=jax.ShapeDtypeStruct((n, D_MODEL), F32),
        scratch_shapes=[pltpu.VMEM((2, n_rows, D_MODEL), F32), pltpu.SemaphoreType.DMA((2,))],
        compiler_params=pltpu.CompilerParams(dimension_semantics=("arbitrary",), vmem_limit_bytes=VMEM_LIMIT),
        name="moe_combine",
    )(dest_blocks, dest_blocks, x1, gate_pad, y_rows)


def kernel(x, norm1_g, w_in, q_norm_g, k_norm_g, cmp_pos, w_cmp1, b_cmp1, w_cmp2, b_cmp2, gm_v_norm_g, gm_w_s,
           gm_b_s, out_norm_attn_g, out_norm_gm_g, w_out, norm2_g, w_router, b_router, w_gate_up, b_gate_up,
           w_down, b_down):
    batch, seq, _ = x.shape
    n = batch * seq
    nqb = seq // Q_BLOCK
    bgn = batch * N_KV
    x2 = x.reshape(n, D_MODEL)

    c_gate = D_ATTN + 6 * D_KV
    w_r = jnp.concatenate([w_in[:, :c_gate], w_in[:, c_gate + N_GATE:], w_in[:, c_gate:c_gate + N_GATE],
                           jnp.zeros((D_MODEL, LANE - N_GATE), F32)], axis=1).astype(BF16)
    q, kc_raw, vc_raw, ks, vs, kw, vw, gates, u_act, v_act = _inproj(x2, norm1_g, w_r, q_norm_g, k_norm_g,
                                                                     gm_v_norm_g)

    kc = _compress(kc_raw, cmp_pos[0], w_cmp1[0], b_cmp1[0], w_cmp2[0], b_cmp2[0], k_norm_g[0], batch, seq, True)
    vc = _compress(vc_raw, cmp_pos[1], w_cmp1[1], b_cmp1[1], w_cmp2[1], b_cmp2[1], k_norm_g[0], batch, seq, False)

    nq = N_REP * Q_BLOCK
    ncmp = seq // CMP_STRIDE

    nsel = seq // SEL_BLOCK
    oh_w = -(-nsel // LANE) * LANE

    def with_pos(k, pos, one_hot=False):
        feat = np.zeros((pos.shape[0], LANE - HEAD_DIM + (oh_w if one_hot else 0)), np.float32)
        feat[:, 0] = feat[:, 1] = pos // SEL_BLOCK
        feat[:, 2] = feat[:, 3] = pos % SEL_BLOCK
        if one_hot:
            feat[np.arange(pos.shape[0]), LANE - HEAD_DIM + pos // SEL_BLOCK] = 1.0
        feat = jnp.broadcast_to(jnp.asarray(feat, BF16)[None], (bgn,) + feat.shape)
        return jnp.concatenate([k.astype(BF16), feat], axis=-1)

    def front_pad(a, axis):
        pad = [(0, 0)] * a.ndim
        pad[axis] = (WINDOW, 0)
        return jnp.pad(a, pad)

    def tok_major(a):
        return a.reshape(batch, seq, N_KV, HEAD_DIM).transpose(0, 2, 1, 3).reshape(bgn, seq, HEAD_DIM)

    def feat_major(a):
        return a.reshape(batch, seq, N_KV, HEAD_DIM).transpose(0, 2, 3, 1).reshape(bgn, HEAD_DIM, seq).astype(BF16)

    head = np.arange(N_KV)[:, None] * N_REP + np.arange(nq)[None, :] // Q_BLOCK
    coef = np.exp2(-(head + 1.0)) * LOG2E
    c_hi = coef.astype(BF16).astype(np.float64)
    c_lo = (coef - c_hi).astype(BF16).astype(np.float64)
    qrows = np.zeros((N_KV, LANE - HEAD_DIM, nq), np.float32)
    qrows[:, 0], qrows[:, 1], qrows[:, 2], qrows[:, 3] = SEL_BLOCK * c_hi, SEL_BLOCK * c_lo, c_hi, c_lo
    qrows = jnp.broadcast_to(jnp.asarray(qrows, BF16)[None, :, None], (batch, N_KV, nqb, LANE - HEAD_DIM, nq))
    qt = (q.reshape(batch, nqb, Q_BLOCK, N_KV, N_REP, HEAD_DIM).transpose(0, 3, 1, 5, 4, 2)
          .reshape(batch, N_KV, nqb, HEAD_DIM, nq).astype(BF16))
    qt = jnp.concatenate([qt, qrows], axis=3).reshape(bgn, nqb, LANE, nq)
    gt = (gates[:, :N_GATE].reshape(batch, nqb, Q_BLOCK, N_KV, N_REP, 3).transpose(0, 3, 1, 5, 4, 2)
          .reshape(bgn, nqb, 3, nq))
    pos_t = np.arange(seq)
    pos_c = np.arange(ncmp) * CMP_STRIDE + (CMP_LEN - 1)
    kc_b = with_pos(kc.reshape(bgn, ncmp, HEAD_DIM), pos_c)
    vct = vc.reshape(bgn, ncmp, HEAD_DIM).transpose(0, 2, 1).astype(BF16)
    c0 = np.arange(ncmp)[None, :] * CMP_STRIDE
    n0 = np.arange(seq // SEL_BLOCK)[:, None] * SEL_BLOCK
    ovt = np.clip(np.minimum(c0 + CMP_LEN, n0 + SEL_BLOCK) - np.maximum(c0, n0), 0, None) / CMP_LEN
    pos_w = np.maximum(np.arange(seq + WINDOW) - WINDOW, 0)
    ot = _attention(qt, kc_b, vct, with_pos(tok_major(ks), pos_t, one_hot=True), feat_major(vs),
                    with_pos(front_pad(tok_major(kw), 1), pos_w), front_pad(feat_major(vw), 2), gt,
                    jnp.asarray(ovt, BF16))
    o_attn = (ot.reshape(batch, N_KV, nqb, HEAD_DIM, N_REP, Q_BLOCK).transpose(0, 2, 5, 1, 4, 3)
              .reshape(n, D_ATTN))

    bias_full = jnp.repeat(gm_b_s.T, GM_GROUP_DIM, axis=1)
    wr_pad = jnp.concatenate([w_router, jnp.zeros((D_MODEL, LANE - N_EXPERTS), F32)], axis=1)
    br_pad = jnp.concatenate([b_router, jnp.full((LANE - N_EXPERTS,), NEG, F32)]).reshape(1, LANE)
    x1, xn3, idx_pad, gate_pad = _mix(x2, o_attn, u_act, v_act, gm_w_s, bias_full,
                                     out_norm_attn_g.reshape(1, D_ATTN), out_norm_gm_g.reshape(1, D_GM),
                                     w_out.astype(BF16), norm2_g.reshape(1, D_MODEL), wr_pad, br_pad)

    s_tot = n * TOP_K
    nb = s_tot // BM_MOE + N_EXPERTS - 1 + MOE_AHEAD
    e_flat = idx_pad[:, :TOP_K].reshape(s_tot)
    onehot = (e_flat[:, None] == jnp.arange(N_EXPERTS, dtype=jnp.int32)[None, :]).astype(jnp.int32)
    csum = jnp.cumsum(onehot, axis=0)
    rank = jnp.sum(csum * onehot, axis=1) - 1
    counts = csum[-1]
    padded = ((counts + BM_MOE - 1) // BM_MOE) * BM_MOE
    pad_end = jnp.cumsum(padded)
    pad_start = pad_end - padded
    dest = pad_start[e_flat] + rank
    tok_flat = jnp.arange(s_tot, dtype=jnp.int32) // TOP_K
    tok_buf = jnp.zeros((nb * BM_MOE,), jnp.int32).at[dest].set(tok_flat, unique_indices=True)
    blk_start = jnp.arange(nb, dtype=jnp.int32) * BM_MOE
    blk_expert = jnp.minimum(jnp.sum((blk_start[:, None] >= pad_end[None, :]).astype(jnp.int32), axis=1),
                             N_EXPERTS - 1)
    blk_valid = (blk_start < pad_end[-1]).astype(jnp.int32)

    bg = b_gate_up[:, 0::2].reshape(N_EXPERTS, 1, D_EXPERT)
    bl = b_gate_up[:, 1::2].reshape(N_EXPERTS, 1, D_EXPERT)
    y_rows = _moe(blk_expert, blk_valid, tok_buf.reshape(nb, 1, BM_MOE), xn3, w_gate_up, bg, bl, w_down,
                  b_down.reshape(N_EXPERTS, 1, D_MODEL))

    dest_blocks = (dest.reshape(n // TM_CMB, TM_CMB, TOP_K).transpose(0, 2, 1)
                   .reshape(n // TM_CMB, 1, TOP_K * TM_CMB).astype(jnp.int32))
    out = _combine(dest_blocks, x1, gate_pad, y_rows)
    return out.reshape(batch, seq, D_MODEL)
```

```python
import functools

import jax
import jax.numpy as jnp
import numpy as np
from jax import lax
from jax.experimental import pallas as pl
from jax.experimental.pallas import tpu as pltpu

F32 = jnp.float32
BF16 = jnp.bfloat16
HIGHEST = lax.Precision.HIGHEST
_NT = (((1,), (1,)), ((), ()))

D_MODEL = 1024
N_HEADS = 8
HEAD_DIM = 64
N_KV = 2
N_REP = N_HEADS // N_KV
D_ATTN = N_HEADS * HEAD_DIM
D_KV = N_KV * HEAD_DIM
N_GM_GROUPS = 8
GM_GROUP_DIM = 64
D_GM = N_GM_GROUPS * GM_GROUP_DIM
N_GATE = 3 * N_HEADS
CMP_LEN = 32
CMP_STRIDE = 16
CMP_HIDDEN = 128
SEL_BLOCK = 64
N_SEL = 16
WINDOW = 512
Q_BLOCK = 128
FORCE_BONUS = 1.0e4
GM_CHUNK = 128
N_EXPERTS = 32
TOP_K = 4
D_EXPERT = 1024
SWIGLU_LIMIT = 7.0
SWIGLU_ALPHA = 1.702
EPS = 1e-6
NEG = -1.0e30
LOG2E = 1.4426950408889634

LANE = 128
ROW_SUB = D_MODEL // LANE
VMEM_LIMIT = 48 * 1024 * 1024
VMEM_LIMIT_MOE = 56 * 1024 * 1024

_C_Q = 0
_C_KC = _C_Q + D_ATTN
_C_VC = _C_KC + D_KV
_C_KS = _C_VC + D_KV
_C_VS = _C_KS + D_KV
_C_KW = _C_VS + D_KV
_C_VW = _C_KW + D_KV
_C_U = _C_VW + D_KV
_C_V = _C_U + D_GM
_C_G = _C_V + D_GM
D_IN_PAD = _C_G + LANE

TM_IN = 256
TM_MIX = 256
KC_SEL = 512
BM_MOE = 256
MOE_AHEAD = 2
TM_CMB = 128


def _rms(x, eps=EPS):
    return lax.rsqrt(jnp.mean(x * x, axis=-1, keepdims=True) + eps)


def _inproj_kernel(x_ref, g1_ref, w_ref, qg_ref, kg_ref, vg_ref,
                   q_ref, kc_ref, vc_ref, ks_ref, vs_ref, kw_ref, vw_ref, gate_ref, u_ref, v_ref):
    x = x_ref[...]
    h = (x * _rms(x)) * g1_ref[...]
    z = jnp.dot(h.astype(BF16), w_ref[...], preferred_element_type=F32)

    def head_norm(col0, n, gain, scale):
        outs = []
        for i in range(n):
            sl = z[:, col0 + i * HEAD_DIM: col0 + (i + 1) * HEAD_DIM]
            outs.append((sl * _rms(sl)) * gain * scale)
        return jnp.concatenate(outs, axis=-1)

    q_ref[...] = head_norm(_C_Q, N_HEADS, qg_ref[...], HEAD_DIM ** -0.5 * LOG2E)
    kc_ref[...] = z[:, _C_KC:_C_KC + D_KV]
    vc_ref[...] = z[:, _C_VC:_C_VC + D_KV]
    ks_ref[...] = head_norm(_C_KS, N_KV, kg_ref[1:2, :], 1.0)
    vs_ref[...] = z[:, _C_VS:_C_VS + D_KV]
    kw_ref[...] = head_norm(_C_KW, N_KV, kg_ref[2:3, :], 1.0)
    vw_ref[...] = z[:, _C_VW:_C_VW + D_KV]
    gate_ref[...] = jax.nn.sigmoid(z[:, _C_G:_C_G + LANE])
    u_ref[...] = jax.nn.gelu(z[:, _C_U:_C_U + D_GM])
    gv = jax.nn.gelu(z[:, _C_V:_C_V + D_GM])
    v_ref[...] = (gv * _rms(gv)) * vg_ref[...]


def _inproj(x2, norm1_g, w_r, q_norm_g, k_norm_g, gm_v_norm_g):
    n = x2.shape[0]
    row = lambda c: pl.BlockSpec((TM_IN, c), lambda i: (i, 0))
    full = lambda a: pl.BlockSpec(a.shape, lambda i: (0,) * a.ndim)
    g1 = norm1_g.reshape(1, D_MODEL)
    qg = q_norm_g.reshape(1, HEAD_DIM)
    vg = gm_v_norm_g.reshape(1, D_GM)
    widths = (D_ATTN, D_KV, D_KV, D_KV, D_KV, D_KV, D_KV, LANE, D_GM, D_GM)
    return pl.pallas_call(
        _inproj_kernel,
        grid=(n // TM_IN,),
        in_specs=[row(D_MODEL), full(g1), full(w_r), full(qg), full(k_norm_g), full(vg)],
        out_specs=[row(c) for c in widths],
        out_shape=[jax.ShapeDtypeStruct((n, c), F32) for c in widths],
        compiler_params=pltpu.CompilerParams(dimension_semantics=("arbitrary",), vmem_limit_bytes=VMEM_LIMIT),
        name="inproj",
    )(x2, g1, w_r, qg, k_norm_g, vg)


def _compress_kernel(a_ref, pos_ref, w1_ref, w1a_ref, w1b_ref, b1_ref, w2_ref, b2_ref, kg_ref, o_ref, *, norm):
    a = a_ref[0]
    nseg = a.shape[0]
    c = jnp.dot(pos_ref[...], w1_ref[...], precision=HIGHEST, preferred_element_type=F32)[0:1] + b1_ref[...]
    row = lax.broadcasted_iota(jnp.int32, (nseg, 1), 0)
    for g in range(N_KV):
        pa = jnp.dot(a, w1a_ref[g], precision=HIGHEST, preferred_element_type=F32)
        pb = jnp.dot(a, w1b_ref[g], precision=HIGHEST, preferred_element_type=F32)
        hid = jax.nn.gelu(pa + pltpu.roll(pb, nseg - 1, 0) + c)
        out = jnp.dot(hid, w2_ref[...], precision=HIGHEST, preferred_element_type=F32) + b2_ref[...]
        if norm:
            out = (out * _rms(out)) * kg_ref[...]
        o_ref[0, g] = jnp.where(row < nseg - 1, out, 0.0)


def _compress(raw, pos, w1, b1, w2, b2, gain, batch, seq, norm):
    nseg = seq // CMP_STRIDE
    half = CMP_STRIDE * HEAD_DIM
    a = raw.reshape(batch, nseg, CMP_STRIDE * D_KV)
    pos8 = jnp.broadcast_to(pos.reshape(1, CMP_LEN * HEAD_DIM), (8, CMP_LEN * HEAD_DIM))

    def expand(wh):
        wh = wh.reshape(CMP_STRIDE, HEAD_DIM, CMP_HIDDEN)
        z = jnp.zeros((N_KV, CMP_STRIDE, N_KV, HEAD_DIM, CMP_HIDDEN), F32)
        for g in range(N_KV):
            z = z.at[g, :, g].set(wh)
        return z.reshape(N_KV, CMP_STRIDE * D_KV, CMP_HIDDEN)

    w1a, w1b = expand(w1[:half]), expand(w1[half:])
    b1r, b2r, gr = b1.reshape(1, CMP_HIDDEN), b2.reshape(1, HEAD_DIM), gain.reshape(1, HEAD_DIM)
    full = lambda t: pl.BlockSpec(t.shape, lambda i: (0,) * t.ndim)
    return pl.pallas_call(
        functools.partial(_compress_kernel, norm=norm),
        grid=(batch,),
        in_specs=[pl.BlockSpec((1, nseg, CMP_STRIDE * D_KV), lambda i: (i, 0, 0)),
                  full(pos8), full(w1), full(w1a), full(w1b), full(b1r), full(w2), full(b2r), full(gr)],
        out_specs=pl.BlockSpec((1, N_KV, nseg, HEAD_DIM), lambda i: (i, 0, 0, 0)),
        out_shape=jax.ShapeDtypeStruct((batch, N_KV, nseg, HEAD_DIM), F32),
        compiler_params=pltpu.CompilerParams(dimension_semantics=("arbitrary",), vmem_limit_bytes=VMEM_LIMIT),
        name="compress_k" if norm else "compress_v",
    )(a, pos8, w1, w1a, w1b, b1r, w2, b2r, gr)


def _attn_kernel(qt_ref, kc_ref, vct_ref, ks_ref, vst_ref, kw_ref, vwt_ref, g_ref, ovt_ref, o_ref,
                 qs_ref, s0_ref, s1_ref, p0_ref, p1_ref, st_ref, acc_ref):
    qb = pl.program_id(1)
    nq = N_REP * Q_BLOCK
    q0 = qb * Q_BLOCK
    qt = qt_ref[0, 0]
    ql = lax.broadcasted_iota(jnp.int32, (1, nq), 1) % Q_BLOCK
    t_row = (q0 + ql).astype(F32)
    m_init = 0.5 * NEG

    def online(s, m, l):
        m_new = jnp.maximum(m, jnp.max(s, axis=0, keepdims=True))
        alpha = jnp.exp2(m - m_new)
        p = jnp.exp2(s - m_new)
        return p, m_new, alpha, alpha * l + jnp.sum(p, axis=0, keepdims=True)

    def inv(l):
        return jnp.where(l > 0.0, 1.0 / l, 0.0)

    m0 = jnp.full((1, nq), m_init, F32)
    l0 = jnp.zeros((1, nq), F32)
    a0 = jnp.zeros((HEAD_DIM, nq), F32)

    ncmp = kc_ref.shape[1]
    s = jnp.dot(kc_ref[0], qt, preferred_element_type=F32)
    c_end = (lax.broadcasted_iota(jnp.int32, (ncmp, 1), 0) * CMP_STRIDE + (CMP_LEN - 1)).astype(F32)
    p, _, _, l = online(jnp.where(c_end <= t_row, s, NEG), m0, l0)
    p = p * inv(l)
    o_cmp = jnp.dot(vct_ref[0], p.astype(BF16), preferred_element_type=F32)

    psum = p[:, 0:Q_BLOCK]
    for r in range(1, N_REP):
        psum = psum + p[:, r * Q_BLOCK:(r + 1) * Q_BLOCK]
    nsel = ovt_ref.shape[0]
    p_hi = psum.astype(BF16)
    p_lo = (psum - p_hi.astype(F32)).astype(BF16)
    imp = (jnp.dot(ovt_ref[...], p_hi, preferred_element_type=F32)
           + jnp.dot(ovt_ref[...], p_lo, preferred_element_type=F32))
    n_col = lax.broadcasted_iota(jnp.int32, (nsel, 1), 0).astype(F32)
    n_start = n_col * SEL_BLOCK
    tq = t_row[:, 0:Q_BLOCK]
    cur = jnp.floor(tq * (1.0 / SEL_BLOCK)) * SEL_BLOCK
    forced = (n_start == cur) | (n_start == 0.0)
    valid = n_start <= tq
    imp = jnp.where(forced, imp + FORCE_BONUS, imp)
    imp = jnp.where(valid, imp, NEG)
    sel = jnp.zeros((nsel, Q_BLOCK), F32)
    for _ in range(min(N_SEL, nsel)):
        mx = jnp.max(imp, axis=0, keepdims=True)
        first = jnp.min(jnp.where(imp == mx, n_col, float(nsel)), axis=0, keepdims=True)
        hit = n_col == first
        sel = jnp.where(hit, 1.0, sel)
        imp = jnp.where(hit, -jnp.inf, imp)
    selb = jnp.where(valid & (sel > 0.0), 0.0, NEG).astype(BF16)
    qs_ref[0:LANE, :] = qt
    qs_ref[LANE:LANE + nsel, :] = jnp.concatenate([selb] * N_REP, axis=1)
    if qs_ref.shape[0] > LANE + nsel:
        qs_ref[LANE + nsel:, :] = jnp.zeros((qs_ref.shape[0] - LANE - nsel, nq), BF16)

    def attend(k_blk, vt_blk, q_op, bias, carry):
        m, l, acc = carry
        s = jnp.dot(k_blk, q_op, preferred_element_type=F32)
        if bias is not None:
            s = s + bias
        p, m, alpha, l = online(s, m, l)
        pv = jnp.dot(vt_blk, p.astype(BF16), preferred_element_type=F32)
        return m, l, alpha * acc + pv

    seq = ks_ref.shape[1]

    def scores(j):
        k0 = pl.multiple_of(jnp.minimum(j * KC_SEL, seq - KC_SEL), KC_SEL)
        s = jnp.dot(ks_ref[0, pl.ds(k0, KC_SEL), :], qs_ref[...], preferred_element_type=F32)
        return s, jnp.max(s, axis=0, keepdims=True)

    def values(j, p):
        k0 = pl.multiple_of(jnp.maximum(j, 0) * KC_SEL, KC_SEL)
        return jnp.dot(vst_ref[0, :, pl.ds(k0, KC_SEL)], p, preferred_element_type=F32)

    def stage(j, s_cur, s_nxt, p_cur, p_prv):
        m, l, alpha_prev, mx = st_ref[0:1, :], st_ref[1:2, :], st_ref[2:3, :], st_ref[3:4, :]
        m_new = jnp.maximum(m, mx)
        alpha = jnp.exp2(m - m_new)
        k0 = pl.multiple_of(jnp.minimum((j + 1) * KC_SEL, seq - KC_SEL), KC_SEL)
        k0p = pl.multiple_of(jnp.maximum(j - 1, 0) * KC_SEL, KC_SEL)
        sub = KC_SEL // 4
        psum, mx_next, zeros = None, None, []
        for q in range(4):
            rows = slice(q * sub, (q + 1) * sub)
            k_q = ks_ref[0, pl.ds(k0 + q * sub, sub), :]
            if q >= 1:
                k_q = k_q + jnp.concatenate([zeros[q - 1]] * (ks_ref.shape[2] // LANE), axis=1)
            s_q = jnp.dot(k_q, qs_ref[...], preferred_element_type=F32)
            s_nxt[rows, :] = s_q
            mx_q = jnp.max(s_q, axis=0, keepdims=True)
            mx_next = mx_q if mx_next is None else jnp.maximum(mx_next, mx_q)
            p_q = jnp.exp2(s_cur[rows, :] - m_new)
            ps_q = jnp.sum(p_q, axis=0, keepdims=True)
            psum = ps_q if psum is None else psum + ps_q
            p_q = p_q.astype(BF16)
            p_cur[rows, :] = p_q
            dep = ps_q[:, 0:LANE]
            for r in range(1, N_REP):
                dep = dep + ps_q[:, r * LANE:(r + 1) * LANE]
            bits = pltpu.bitcast(dep, jnp.int32)
            zeros.append(lax.shift_right_logical(lax.shift_right_logical(bits, 16), 16).astype(F32).astype(BF16))
            if q == 1:
                vt_prev = vst_ref[0, :, pl.ds(k0p, KC_SEL)] + jnp.concatenate([zeros[1]] * (KC_SEL // LANE), axis=1)
                acc_ref[...] = alpha_prev * acc_ref[...] + jnp.dot(vt_prev, p_prv[...], preferred_element_type=F32)
        st_ref[0:1, :] = m_new
        st_ref[1:2, :] = alpha * l + psum
        st_ref[2:3, :] = alpha
        st_ref[3:4, :] = mx_next

    n_full = q0 // KC_SEL
    s_first, mx_first = scores(0)

    @pl.when(n_full % 2 == 0)
    def _():
        s0_ref[...] = s_first

    @pl.when(n_full % 2 == 1)
    def _():
        s1_ref[...] = s_first

    p0_ref[...] = jnp.zeros(p0_ref.shape, BF16)
    p1_ref[...] = jnp.zeros(p1_ref.shape, BF16)
    st_ref[0:1, :] = m0
    st_ref[1:2, :] = l0
    st_ref[2:3, :] = jnp.ones((1, nq), F32)
    st_ref[3:4, :] = mx_first
    acc_ref[...] = a0

    def sel_body(j, carry):
        @pl.when((n_full - j) % 2 == 0)
        def _():
            stage(j, s0_ref, s1_ref, p0_ref, p1_ref)

        @pl.when((n_full - j) % 2 == 1)
        def _():
            stage(j, s1_ref, s0_ref, p1_ref, p0_ref)
        return carry

    lax.fori_loop(0, n_full, sel_body, 0)
    pos_last = (n_full * KC_SEL + lax.broadcasted_iota(jnp.int32, (KC_SEL, 1), 0)).astype(F32)
    p, _, alpha, l_sel = online(s0_ref[...] + jnp.where(pos_last <= t_row, 0.0, NEG), st_ref[0:1, :], st_ref[1:2, :])
    acc = st_ref[2:3, :] * acc_ref[...] + values(n_full - 1, p1_ref[...])
    o_sel = alpha * acc + values(n_full, p.astype(BF16))

    n_wk = WINDOW + Q_BLOCK
    kk = lax.broadcasted_iota(jnp.int32, (n_wk, 1), 0)
    in_win = (kk - WINDOW <= ql) & (kk > ql) & (kk >= WINDOW - q0)
    _, l_win, o_win = attend(kw_ref[0, pl.ds(pl.multiple_of(q0, Q_BLOCK), n_wk), :],
                             vwt_ref[0, :, pl.ds(pl.multiple_of(q0, Q_BLOCK), n_wk)], qt,
                             jnp.where(in_win, 0.0, NEG), (m0, l0, a0))

    gt = g_ref[0, 0]
    o_ref[0, 0] = (gt[0:1] * o_cmp + gt[1:2] * (o_sel * inv(l_sel)) + gt[2:3] * (o_win * inv(l_win)))


def _attention(qt, kc, vct, ks, vst, kw, vwt, gt, ovt):
    bgn, nqb = qt.shape[0], qt.shape[1]
    seq = ks.shape[1]
    nq = N_REP * Q_BLOCK
    per_bg = lambda a: pl.BlockSpec((1,) + a.shape[1:], lambda b, i: (b,) + (0,) * (a.ndim - 1))
    return pl.pallas_call(
        _attn_kernel,
        grid=(bgn, nqb),
        in_specs=[pl.BlockSpec((1, 1, LANE, nq), lambda b, i: (b, i, 0, 0)),
                  per_bg(kc), per_bg(vct), per_bg(ks), per_bg(vst), per_bg(kw), per_bg(vwt),
                  pl.BlockSpec((1, 1, 3, nq), lambda b, i: (b, i, 0, 0)),
                  pl.BlockSpec(ovt.shape, lambda b, i: (0, 0))],
        out_specs=pl.BlockSpec((1, 1, HEAD_DIM, nq), lambda b, i: (b, i, 0, 0)),
        out_shape=jax.ShapeDtypeStruct((bgn, nqb, HEAD_DIM, nq), F32),
        scratch_shapes=[pltpu.VMEM((ks.shape[2], nq), BF16),
                        pltpu.VMEM((KC_SEL, nq), F32), pltpu.VMEM((KC_SEL, nq), F32),
                        pltpu.VMEM((KC_SEL, nq), BF16), pltpu.VMEM((KC_SEL, nq), BF16),
                        pltpu.VMEM((8, nq), F32), pltpu.VMEM((HEAD_DIM, nq), F32)],
        compiler_params=pltpu.CompilerParams(dimension_semantics=("arbitrary", "arbitrary"),
                                             vmem_limit_bytes=VMEM_LIMIT),
        name="nsa_attention",
    )(qt, kc, vct, ks, vst, kw, vwt, gt, ovt)


def _mix_kernel(x_ref, oa_ref, u_ref, v_ref, ws_ref, bs_ref, ga_ref, gg_ref, wo_ref, g2_ref, wr_ref, br_ref,
                x1_ref, xn_ref, idx_ref, gate_ref):
    tm = x_ref.shape[0]
    rr = lax.broadcasted_iota(jnp.int32, (GM_CHUNK, GM_CHUNK), 0)
    cc = lax.broadcasted_iota(jnp.int32, (GM_CHUNK, GM_CHUNK), 1)
    grp = lax.broadcasted_iota(jnp.int32, (1, D_GM), 1) // GM_GROUP_DIM
    ws = [jnp.where(rr >= cc, ws_ref[g], 0.0).astype(BF16) for g in range(N_GM_GROUPS)]
    ys = []
    for c in range(tm // GM_CHUNK):
        vch = v_ref[c * GM_CHUNK:(c + 1) * GM_CHUNK, :].astype(BF16)
        y = bs_ref[...]
        for g in range(N_GM_GROUPS):
            y = y + jnp.where(grp == g, jnp.dot(ws[g], vch, preferred_element_type=F32), 0.0)
        ys.append(y)
    o_gm = u_ref[...] * jnp.concatenate(ys, axis=0)
    o_at = oa_ref[...]
    mixed = jnp.concatenate([(o_at * _rms(o_at)) * ga_ref[...], (o_gm * _rms(o_gm)) * gg_ref[...]], axis=-1)
    x1 = x_ref[...] + jnp.dot(mixed.astype(BF16), wo_ref[...], preferred_element_type=F32)
    x1_ref[...] = x1
    xn = (x1 * _rms(x1)) * g2_ref[...]
    for s in range(ROW_SUB):
        xn_ref[:, s, :] = xn[:, s * LANE:(s + 1) * LANE]
    logits = jnp.dot(xn, wr_ref[...], precision=HIGHEST, preferred_element_type=F32) + br_ref[...]
    lane = lax.broadcasted_iota(jnp.int32, (1, LANE), 1).astype(F32)
    idx_out = jnp.zeros((tm, LANE), F32)
    val_out = jnp.zeros((tm, LANE), F32)
    vals = []
    for k in range(TOP_K):
        mx = jnp.max(logits, axis=-1, keepdims=True)
        first = jnp.min(jnp.where(logits == mx, lane, float(LANE)), axis=-1, keepdims=True)
        logits = jnp.where(lane == first, -jnp.inf, logits)
        idx_out = jnp.where(lane == float(k), first, idx_out)
        vals.append(mx)
    es = [jnp.exp(v - vals[0]) for v in vals]
    den = es[0] + es[1] + es[2] + es[3]
    for k in range(TOP_K):
        val_out = jnp.where(lane == float(k), es[k] / den, val_out)
    idx_ref[...] = idx_out.astype(jnp.int32)
    gate_ref[...] = val_out


def _mix(x2, o_attn, u_act, v_act, gm_w_s, bias_full, ga, gg, w_out_b, g2, wr_pad, br_pad):
    n = x2.shape[0]
    row = lambda c: pl.BlockSpec((TM_MIX, c), lambda i: (i, 0))
    full = lambda a: pl.BlockSpec(a.shape, lambda i: (0,) * a.ndim)
    return pl.pallas_call(
        _mix_kernel,
        grid=(n // TM_MIX,),
        in_specs=[row(D_MODEL), row(D_ATTN), row(D_GM), row(D_GM), full(gm_w_s), full(bias_full), full(ga), full(gg),
                  full(w_out_b), full(g2), full(wr_pad), full(br_pad)],
        out_specs=[row(D_MODEL), pl.BlockSpec((TM_MIX, ROW_SUB, LANE), lambda i: (i, 0, 0)), row(LANE), row(LANE)],
        out_shape=[jax.ShapeDtypeStruct((n, D_MODEL), F32), jax.ShapeDtypeStruct((n, ROW_SUB, LANE), F32),
                   jax.ShapeDtypeStruct((n, LANE), jnp.int32), jax.ShapeDtypeStruct((n, LANE), F32)],
        compiler_params=pltpu.CompilerParams(dimension_semantics=("arbitrary",), vmem_limit_bytes=VMEM_LIMIT),
        name="mix_outproj_router",
    )(x2, o_attn, u_act, v_act, gm_w_s, bias_full, ga, gg, w_out_b, g2, wr_pad, br_pad)


def _row_gather(idx_ref, n_rows, src_hbm, dst_ref, sem):
    def start():
        for r in range(n_rows):
            pltpu.make_async_copy(src_hbm.at[pl.ds(idx_ref[0, 0, r], 1), :], dst_ref.at[pl.ds(r, 1), :], sem).start()

    def wait():
        pltpu.make_async_copy(src_hbm.at[pl.ds(0, n_rows), :], dst_ref, sem).wait()

    return start, wait


def _tile_row_gather(idx_ref, n_rows, src_hbm, dst_ref, sem):
    def start():
        for r in range(n_rows):
            t = idx_ref[0, 0, r]
            pltpu.make_async_copy(src_hbm.at[lax.shift_right_logical(t, 3), t & (ROW_SUB - 1)],
                                  dst_ref.at[r // ROW_SUB, :, r % ROW_SUB, :], sem).start(priority=r % 2)

    def wait():
        pltpu.make_async_copy(src_hbm.at[pl.ds(0, n_rows // ROW_SUB)], dst_ref, sem).wait()

    return start, wait


def _tiles_to_matrix(ref):
    rows = ref.shape[0] * ROW_SUB
    return jnp.concatenate([ref[:, c].reshape(rows, LANE) for c in range(ROW_SUB)], axis=1)


def _moe_kernel(be_ref, bv_ref, tok_ref, tok_n1_ref, tok_n2_ref, x_hbm, wgu_ref, bg_ref, bl_ref, wd_ref, bd_ref,
                o_ref, xbuf, sems, wt_s, wg_s, wl_s, wd_s):
    i = pl.program_id(0)
    slot = i % (MOE_AHEAD + 1)
    slot_n2 = (i + MOE_AHEAD) % (MOE_AHEAD + 1)
    start_cur, wait_cur = _tile_row_gather(tok_ref, BM_MOE, x_hbm, xbuf.at[slot], sems.at[slot])
    start_n1, _ = _tile_row_gather(tok_n1_ref, BM_MOE, x_hbm, xbuf.at[1], sems.at[1])
    start_n2, _ = _tile_row_gather(tok_n2_ref, BM_MOE, x_hbm, xbuf.at[slot_n2], sems.at[slot_n2])

    prev = jnp.maximum(i - 1, 0)

    @pl.when(i == 0)
    def _():
        start_cur()
        start_n1()

    @pl.when((bv_ref[i] == 1) & ((i == 0) | (be_ref[i] != be_ref[prev])))
    def _():
        tc = wt_s.shape[1]
        for c in range(2 * D_EXPERT // tc):
            wt = wgu_ref[0, :, c * tc:(c + 1) * tc].T
            for j in range(ROW_SUB):
                wt_s[j] = wt[:, j * LANE:(j + 1) * LANE]
            for first, dst in ((0, wg_s), (1, wl_s)):
                half = jnp.concatenate([wt_s[j, pl.ds(first, tc // 2, stride=2), :] for j in range(ROW_SUB)], axis=1)
                dst[c * tc // 2:(c + 1) * tc // 2, :] = half.astype(BF16)
        wd_s[...] = wd_ref[0].astype(BF16)

    @pl.when(bv_ref[i] == 1)
    def _():
        wait_cur()
        start_n2()
        xb = _tiles_to_matrix(xbuf.at[slot]).astype(BF16)
        hg = lax.dot_general(xb, wg_s[...], _NT, preferred_element_type=F32) + bg_ref[0]
        hl = lax.dot_general(xb, wl_s[...], _NT, preferred_element_type=F32) + bl_ref[0]
        hg = jnp.minimum(hg, SWIGLU_LIMIT)
        hl = jnp.clip(hl, -SWIGLU_LIMIT, SWIGLU_LIMIT)
        a = hg * jax.nn.sigmoid(SWIGLU_ALPHA * hg) * (hl + 1.0)
        o_ref[...] = jnp.dot(a.astype(BF16), wd_s[...], preferred_element_type=F32) + bd_ref[0]

    @pl.when((bv_ref[i] == 0) & ((i == 1) | ((i >= MOE_AHEAD) & (bv_ref[jnp.maximum(i - MOE_AHEAD, 0)] == 1))))
    def _():
        wait_cur()

    @pl.when(bv_ref[i] == 0)
    def _():
        o_ref[...] = jnp.zeros(o_ref.shape, F32)


def _moe(blk_expert, blk_valid, tok_blocks, xn3, w_gate_up, bg, bl, w_down, bd):
    nb = blk_expert.shape[0]
    per_e = lambda a: pl.BlockSpec((1,) + a.shape[1:], lambda i, be, bv: (be[i],) + (0,) * (a.ndim - 1))

    def tok_spec(ahead):
        return pl.BlockSpec((1, 1, BM_MOE), lambda i, be, bv: (jnp.minimum(i + ahead, nb - 1), 0, 0),
                            memory_space=pltpu.SMEM)

    grid_spec = pltpu.PrefetchScalarGridSpec(
        num_scalar_prefetch=2,
        grid=(nb,),
        in_specs=[tok_spec(0), tok_spec(1), tok_spec(MOE_AHEAD),
                  pl.BlockSpec(memory_space=pl.ANY),
                  per_e(w_gate_up), per_e(bg), per_e(bl), per_e(w_down), per_e(bd)],
        out_specs=pl.BlockSpec((BM_MOE, D_MODEL), lambda i, be, bv: (i, 0)),
        scratch_shapes=[pltpu.VMEM((MOE_AHEAD + 1, BM_MOE // ROW_SUB, ROW_SUB, ROW_SUB, LANE), F32),
                        pltpu.SemaphoreType.DMA((MOE_AHEAD + 1,)),
                        pltpu.VMEM((ROW_SUB, 256, LANE), F32), pltpu.VMEM((D_EXPERT, D_MODEL), BF16),
                        pltpu.VMEM((D_EXPERT, D_MODEL), BF16), pltpu.VMEM((D_EXPERT, D_MODEL), BF16)],
    )
    return pl.pallas_call(
        _moe_kernel,
        grid_spec=grid_spec,
        out_shape=jax.ShapeDtypeStruct((nb * BM_MOE, D_MODEL), F32),
        compiler_params=pltpu.CompilerParams(dimension_semantics=("arbitrary",), vmem_limit_bytes=VMEM_LIMIT_MOE),
        name="moe_experts",
    )(blk_expert, blk_valid, tok_blocks, tok_blocks, tok_blocks,
      xn3.reshape(xn3.shape[0] // ROW_SUB, ROW_SUB, ROW_SUB, LANE), w_gate_up, bg, bl, w_down, bd)


def _combine_kernel(dest_ref, dest_next_ref, x1_ref, gate_ref, y_hbm, o_ref, buf, sems):
    i = pl.program_id(0)
    slot = i % 2
    n_rows = TOP_K * TM_CMB
    start_cur, wait_cur = _row_gather(dest_ref, n_rows, y_hbm, buf.at[slot], sems.at[slot])
    start_next, _ = _row_gather(dest_next_ref, n_rows, y_hbm, buf.at[1 - slot], sems.at[1 - slot])

    @pl.when(i == 0)
    def _():
        start_cur()

    @pl.when(i + 1 < pl.num_programs(0))
    def _():
        start_next()

    wait_cur()
    gate = gate_ref[...]
    acc = x1_ref[...]
    for k in range(TOP_K):
        acc = acc + gate[:, k:k + 1] * buf[slot, k * TM_CMB:(k + 1) * TM_CMB, :]
    o_ref[...] = acc


def _combine(dest_blocks, x1, gate_pad, y_rows):
    n = x1.shape[0]
    nt = n // TM_CMB
    n_rows = TOP_K * TM_CMB
    return pl.pallas_call(
        _combine_kernel,
        grid=(nt,),
        in_specs=[pl.BlockSpec((1, 1, n_rows), lambda i: (i, 0, 0), memory_space=pltpu.SMEM),
                  pl.BlockSpec((1, 1, n_rows), lambda i: (jnp.minimum(i + 1, nt - 1), 0, 0),
                               memory_space=pltpu.SMEM),
                  pl.BlockSpec((TM_CMB, D_MODEL), lambda i: (i, 0)),
                  pl.BlockSpec((TM_CMB, LANE), lambda i: (i, 0)),
                  pl.BlockSpec(memory_space=pl.ANY)],
        out_specs=pl.BlockSpec((TM_CMB, D_MODEL), lambda i: (i, 0)),
        out_shape=jax.ShapeDtypeStruct((n, D_MODEL), F32),
        scratch_shapes=[pltpu.VMEM((2, n_rows, D_MODEL), F32), pltpu.SemaphoreType.DMA((2,))],
        compiler_params=pltpu.CompilerParams(dimension_semantics=("arbitrary",), vmem_limit_bytes=VMEM_LIMIT),
        name="moe_combine",
    )(dest_blocks, dest_blocks, x1, gate_pad, y_rows)


def kernel(x, norm1_g, w_in, q_norm_g, k_norm_g, cmp_pos, w_cmp1, b_cmp1, w_cmp2, b_cmp2, gm_v_norm_g, gm_w_s,
           gm_b_s, out_norm_attn_g, out_norm_gm_g, w_out, norm2_g, w_router, b_router, w_gate_up, b_gate_up,
           w_down, b_down):
    batch, seq, _ = x.shape
    n = batch * seq
    nqb = seq // Q_BLOCK
    bgn = batch * N_KV
    x2 = x.reshape(n, D_MODEL)

    c_gate = D_ATTN + 6 * D_KV
    w_r = jnp.concatenate([w_in[:, :c_gate], w_in[:, c_gate + N_GATE:], w_in[:, c_gate:c_gate + N_GATE],
                           jnp.zeros((D_MODEL, LANE - N_GATE), F32)], axis=1).astype(BF16)
    q, kc_raw, vc_raw, ks, vs, kw, vw, gates, u_act, v_act = _inproj(x2, norm1_g, w_r, q_norm_g, k_norm_g,
                                                                     gm_v_norm_g)

    kc = _compress(kc_raw, cmp_pos[0], w_cmp1[0], b_cmp1[0], w_cmp2[0], b_cmp2[0], k_norm_g[0], batch, seq, True)
    vc = _compress(vc_raw, cmp_pos[1], w_cmp1[1], b_cmp1[1], w_cmp2[1], b_cmp2[1], k_norm_g[0], batch, seq, False)

    nq = N_REP * Q_BLOCK
    ncmp = seq // CMP_STRIDE

    nsel = seq // SEL_BLOCK
    oh_w = -(-nsel // LANE) * LANE

    def with_pos(k, pos, one_hot=False):
        feat = np.zeros((pos.shape[0], LANE - HEAD_DIM + (oh_w if one_hot else 0)), np.float32)
        feat[:, 0] = feat[:, 1] = pos // SEL_BLOCK
        feat[:, 2] = feat[:, 3] = pos % SEL_BLOCK
        if one_hot:
            feat[np.arange(pos.shape[0]), LANE - HEAD_DIM + pos // SEL_BLOCK] = 1.0
        feat = jnp.broadcast_to(jnp.asarray(feat, BF16)[None], (bgn,) + feat.shape)
        return jnp.concatenate([k.astype(BF16), feat], axis=-1)

    def front_pad(a, axis):
        pad = [(0, 0)] * a.ndim
        pad[axis] = (WINDOW, 0)
        return jnp.pad(a, pad)

    def tok_major(a):
        return a.reshape(batch, seq, N_KV, HEAD_DIM).transpose(0, 2, 1, 3).reshape(bgn, seq, HEAD_DIM)

    def feat_major(a):
        return a.reshape(batch, seq, N_KV, HEAD_DIM).transpose(0, 2, 3, 1).reshape(bgn, HEAD_DIM, seq).astype(BF16)

    head = np.arange(N_KV)[:, None] * N_REP + np.arange(nq)[None, :] // Q_BLOCK
    coef = np.exp2(-(head + 1.0)) * LOG2E
    c_hi = coef.astype(BF16).astype(np.float64)
    c_lo = (coef - c_hi).astype(BF16).astype(np.float64)
    qrows = np.zeros((N_KV, LANE - HEAD_DIM, nq), np.float32)
    qrows[:, 0], qrows[:, 1], qrows[:, 2], qrows[:, 3] = SEL_BLOCK * c_hi, SEL_BLOCK * c_lo, c_hi, c_lo
    qrows = jnp.broadcast_to(jnp.asarray(qrows, BF16)[None, :, None], (batch, N_KV, nqb, LANE - HEAD_DIM, nq))
    qt = (q.reshape(batch, nqb, Q_BLOCK, N_KV, N_REP, HEAD_DIM).transpose(0, 3, 1, 5, 4, 2)
          .reshape(batch, N_KV, nqb, HEAD_DIM, nq).astype(BF16))
    qt = jnp.concatenate([qt, qrows], axis=3).reshape(bgn, nqb, LANE, nq)
    gt = (gates[:, :N_GATE].reshape(batch, nqb, Q_BLOCK, N_KV, N_REP, 3).transpose(0, 3, 1, 5, 4, 2)
          .reshape(bgn, nqb, 3, nq))
    pos_t = np.arange(seq)
    pos_c = np.arange(ncmp) * CMP_STRIDE + (CMP_LEN - 1)
    kc_b = with_pos(kc.reshape(bgn, ncmp, HEAD_DIM), pos_c)
    vct = vc.reshape(bgn, ncmp, HEAD_DIM).transpose(0, 2, 1).astype(BF16)
    c0 = np.arange(ncmp)[None, :] * CMP_STRIDE
    n0 = np.arange(seq // SEL_BLOCK)[:, None] * SEL_BLOCK
    ovt = np.clip(np.minimum(c0 + CMP_LEN, n0 + SEL_BLOCK) - np.maximum(c0, n0), 0, None) / CMP_LEN
    pos_w = np.maximum(np.arange(seq + WINDOW) - WINDOW, 0)
    ot = _attention(qt, kc_b, vct, with_pos(tok_major(ks), pos_t, one_hot=True), feat_major(vs),
                    with_pos(front_pad(tok_major(kw), 1), pos_w), front_pad(feat_major(vw), 2), gt,
                    jnp.asarray(ovt, BF16))
    o_attn = (ot.reshape(batch, N_KV, nqb, HEAD_DIM, N_REP, Q_BLOCK).transpose(0, 2, 5, 1, 4, 3)
              .reshape(n, D_ATTN))

    bias_full = jnp.repeat(gm_b_s.T, GM_GROUP_DIM, axis=1)
    wr_pad = jnp.concatenate([w_router, jnp.zeros((D_MODEL, LANE - N_EXPERTS), F32)], axis=1)
    br_pad = jnp.concatenate([b_router, jnp.full((LANE - N_EXPERTS,), NEG, F32)]).reshape(1, LANE)
    x1, xn3, idx_pad, gate_pad = _mix(x2, o_attn, u_act, v_act, gm_w_s, bias_full,
                                     out_norm_attn_g.reshape(1, D_ATTN), out_norm_gm_g.reshape(1, D_GM),
                                     w_out.astype(BF16), norm2_g.reshape(1, D_MODEL), wr_pad, br_pad)

    s_tot = n * TOP_K
    nb = s_tot // BM_MOE + N_EXPERTS - 1 + MOE_AHEAD
    e_flat = idx_pad[:, :TOP_K].reshape(s_tot)
    onehot = (e_flat[:, None] == jnp.arange(N_EXPERTS, dtype=jnp.int32)[None, :]).astype(jnp.int32)
    csum = jnp.cumsum(onehot, axis=0)
    rank = jnp.sum(csum * onehot, axis=1) - 1
    counts = csum[-1]
    padded = ((counts + BM_MOE - 1) // BM_MOE) * BM_MOE
    pad_end = jnp.cumsum(padded)
    pad_start = pad_end - padded
    dest = pad_start[e_flat] + rank
    tok_flat = jnp.arange(s_tot, dtype=jnp.int32) // TOP_K
    tok_buf = jnp.zeros((nb * BM_MOE,), jnp.int32).at[dest].set(tok_flat, unique_indices=True)
    blk_start = jnp.arange(nb, dtype=jnp.int32) * BM_MOE
    blk_expert = jnp.minimum(jnp.sum((blk_start[:, None] >= pad_end[None, :]).astype(jnp.int32), axis=1),
                             N_EXPERTS - 1)
    blk_valid = (blk_start < pad_end[-1]).astype(jnp.int32)

    bg = b_gate_up[:, 0::2].reshape(N_EXPERTS, 1, D_EXPERT)
    bl = b_gate_up[:, 1::2].reshape(N_EXPERTS, 1, D_EXPERT)
    y_rows = _moe(blk_expert, blk_valid, tok_buf.reshape(nb, 1, BM_MOE), xn3, w_gate_up, bg, bl, w_down,
                  b_down.reshape(N_EXPERTS, 1, D_MODEL))

    dest_blocks = (dest.reshape(n // TM_CMB, TM_CMB, TOP_K).transpose(0, 2, 1)
                   .reshape(n // TM_CMB, 1, TOP_K * TM_CMB).astype(jnp.int32))
    out = _combine(dest_blocks, x1, gate_pad, y_rows)
    return out.reshape(batch, seq, D_MODEL)
```

```python
import functools

import jax
import jax.numpy as jnp
import numpy as np
from jax import lax
from jax.experimental import pallas as pl
from jax.experimental.pallas import tpu as pltpu

F32 = jnp.float32
BF16 = jnp.bfloat16
HIGHEST = lax.Precision.HIGHEST
_NT = (((1,), (1,)), ((), ()))

D_MODEL = 1024
N_HEADS = 8
HEAD_DIM = 64
N_KV = 2
N_REP = N_HEADS // N_KV
D_ATTN = N_HEADS * HEAD_DIM
D_KV = N_KV * HEAD_DIM
N_GM_GROUPS = 8
GM_GROUP_DIM = 64
D_GM = N_GM_GROUPS * GM_GROUP_DIM
N_GATE = 3 * N_HEADS
CMP_LEN = 32
CMP_STRIDE = 16
CMP_HIDDEN = 128
SEL_BLOCK = 64
N_SEL = 16
WINDOW = 512
Q_BLOCK = 128
FORCE_BONUS = 1.0e4
GM_CHUNK = 128
N_EXPERTS = 32
TOP_K = 4
D_EXPERT = 1024
SWIGLU_LIMIT = 7.0
SWIGLU_ALPHA = 1.702
EPS = 1e-6
NEG = -1.0e30
LOG2E = 1.4426950408889634

LANE = 128
ROW_SUB = D_MODEL // LANE
VMEM_LIMIT = 48 * 1024 * 1024
VMEM_LIMIT_MOE = 56 * 1024 * 1024

_C_Q = 0
_C_KC = _C_Q + D_ATTN
_C_VC = _C_KC + D_KV
_C_KS = _C_VC + D_KV
_C_VS = _C_KS + D_KV
_C_KW = _C_VS + D_KV
_C_VW = _C_KW + D_KV
_C_U = _C_VW + D_KV
_C_V = _C_U + D_GM
_C_G = _C_V + D_GM
D_IN_PAD = _C_G + LANE

TM_IN = 256
TM_MIX = 256
KC_SEL = 512
BM_MOE = 256
MOE_AHEAD = 2
TM_CMB = 128


def _rms(x, eps=EPS):
    return lax.rsqrt(jnp.mean(x * x, axis=-1, keepdims=True) + eps)


def _inproj_kernel(x_ref, g1_ref, w_ref, qg_ref, kg_ref, vg_ref,
                   q_ref, kc_ref, vc_ref, ks_ref, vs_ref, kw_ref, vw_ref, gate_ref, u_ref, v_ref):
    x = x_ref[...]
    h = (x * _rms(x)) * g1_ref[...]
    z = jnp.dot(h.astype(BF16), w_ref[...], preferred_element_type=F32)

    def head_norm(col0, n, gain, scale):
        outs = []
        for i in range(n):
            sl = z[:, col0 + i * HEAD_DIM: col0 + (i + 1) * HEAD_DIM]
            outs.append((sl * _rms(sl)) * gain * scale)
        return jnp.concatenate(outs, axis=-1)

    q_ref[...] = head_norm(_C_Q, N_HEADS, qg_ref[...], HEAD_DIM ** -0.5 * LOG2E)
    kc_ref[...] = z[:, _C_KC:_C_KC + D_KV]
    vc_ref[...] = z[:, _C_VC:_C_VC + D_KV]
    ks_ref[...] = head_norm(_C_KS, N_KV, kg_ref[1:2, :], 1.0)
    vs_ref[...] = z[:, _C_VS:_C_VS + D_KV]
    kw_ref[...] = head_norm(_C_KW, N_KV, kg_ref[2:3, :], 1.0)
    vw_ref[...] = z[:, _C_VW:_C_VW + D_KV]
    gate_ref[...] = jax.nn.sigmoid(z[:, _C_G:_C_G + LANE])
    u_ref[...] = jax.nn.gelu(z[:, _C_U:_C_U + D_GM])
    gv = jax.nn.gelu(z[:, _C_V:_C_V + D_GM])
    v_ref[...] = (gv * _rms(gv)) * vg_ref[...]


def _inproj(x2, norm1_g, w_r, q_norm_g, k_norm_g, gm_v_norm_g):
    n = x2.shape[0]
    row = lambda c: pl.BlockSpec((TM_IN, c), lambda i: (i, 0))
    full = lambda a: pl.BlockSpec(a.shape, lambda i: (0,) * a.ndim)
    g1 = norm1_g.reshape(1, D_MODEL)
    qg = q_norm_g.reshape(1, HEAD_DIM)
    vg = gm_v_norm_g.reshape(1, D_GM)
    widths = (D_ATTN, D_KV, D_KV, D_KV, D_KV, D_KV, D_KV, LANE, D_GM, D_GM)
    return pl.pallas_call(
        _inproj_kernel,
        grid=(n // TM_IN,),
        in_specs=[row(D_MODEL), full(g1), full(w_r), full(qg), full(k_norm_g), full(vg)],
        out_specs=[row(c) for c in widths],
        out_shape=[jax.ShapeDtypeStruct((n, c), F32) for c in widths],
        compiler_params=pltpu.CompilerParams(dimension_semantics=("arbitrary",), vmem_limit_bytes=VMEM_LIMIT),
        name="inproj",
    )(x2, g1, w_r, qg, k_norm_g, vg)


def _compress_kernel(a_ref, pos_ref, w1_ref, w1a_ref, w1b_ref, b1_ref, w2_ref, b2_ref, kg_ref, o_ref, *, norm):
    a = a_ref[0]
    nseg = a.shape[0]
    c = jnp.dot(pos_ref[...], w1_ref[...], precision=HIGHEST, preferred_element_type=F32)[0:1] + b1_ref[...]
    row = lax.broadcasted_iota(jnp.int32, (nseg, 1), 0)
    for g in range(N_KV):
        pa = jnp.dot(a, w1a_ref[g], precision=HIGHEST, preferred_element_type=F32)
        pb = jnp.dot(a, w1b_ref[g], precision=HIGHEST, preferred_element_type=F32)
        hid = jax.nn.gelu(pa + pltpu.roll(pb, nseg - 1, 0) + c)
        out = jnp.dot(hid, w2_ref[...], precision=HIGHEST, preferred_element_type=F32) + b2_ref[...]
        if norm:
            out = (out * _rms(out)) * kg_ref[...]
        o_ref[0, g] = jnp.where(row < nseg - 1, out, 0.0)


def _compress(raw, pos, w1, b1, w2, b2, gain, batch, seq, norm):
    nseg = seq // CMP_STRIDE
    half = CMP_STRIDE * HEAD_DIM
    a = raw.reshape(batch, nseg, CMP_STRIDE * D_KV)
    pos8 = jnp.broadcast_to(pos.reshape(1, CMP_LEN * HEAD_DIM), (8, CMP_LEN * HEAD_DIM))

    def expand(wh):
        wh = wh.reshape(CMP_STRIDE, HEAD_DIM, CMP_HIDDEN)
        z = jnp.zeros((N_KV, CMP_STRIDE, N_KV, HEAD_DIM, CMP_HIDDEN), F32)
        for g in range(N_KV):
            z = z.at[g, :, g].set(wh)
        return z.reshape(N_KV, CMP_STRIDE * D_KV, CMP_HIDDEN)

    w1a, w1b = expand(w1[:half]), expand(w1[half:])
    b1r, b2r, gr = b1.reshape(1, CMP_HIDDEN), b2.reshape(1, HEAD_DIM), gain.reshape(1, HEAD_DIM)
    full = lambda t: pl.BlockSpec(t.shape, lambda i: (0,) * t.ndim)
    return pl.pallas_call(
        functools.partial(_compress_kernel, norm=norm),
        grid=(batch,),
        in_specs=[pl.BlockSpec((1, nseg, CMP_STRIDE * D_KV), lambda i: (i, 0, 0)),
                  full(pos8), full(w1), full(w1a), full(w1b), full(b1r), full(w2), full(b2r), full(gr)],
        out_specs=pl.BlockSpec((1, N_KV, nseg, HEAD_DIM), lambda i: (i, 0, 0, 0)),
        out_shape=jax.ShapeDtypeStruct((batch, N_KV, nseg, HEAD_DIM), F32),
        compiler_params=pltpu.CompilerParams(dimension_semantics=("arbitrary",), vmem_limit_bytes=VMEM_LIMIT),
        name="compress_k" if norm else "compress_v",
    )(a, pos8, w1, w1a, w1b, b1r, w2, b2r, gr)


def _attn_kernel(qt_ref, kc_ref, vct_ref, ks_ref, vst_ref, kw_ref, vwt_ref, g_ref, ovt_ref, o_ref,
                 qs_ref, s0_ref, s1_ref, p0_ref, p1_ref, st_ref, acc_ref):
    qb = pl.program_id(1)
    nq = N_REP * Q_BLOCK
    q0 = qb * Q_BLOCK
    qt = qt_ref[0, 0]
    ql = lax.broadcasted_iota(jnp.int32, (1, nq), 1) % Q_BLOCK
    t_row = (q0 + ql).astype(F32)
    m_init = 0.5 * NEG

    def online(s, m, l):
        m_new = jnp.maximum(m, jnp.max(s, axis=0, keepdims=True))
        alpha = jnp.exp2(m - m_new)
        p = jnp.exp2(s - m_new)
        return p, m_new, alpha, alpha * l + jnp.sum(p, axis=0, keepdims=True)

    def inv(l):
        return jnp.where(l > 0.0, 1.0 / l, 0.0)

    m0 = jnp.full((1, nq), m_init, F32)
    l0 = jnp.zeros((1, nq), F32)
    a0 = jnp.zeros((HEAD_DIM, nq), F32)

    ncmp = kc_ref.shape[1]
    s = jnp.dot(kc_ref[0], qt, preferred_element_type=F32)
    c_end = (lax.broadcasted_iota(jnp.int32, (ncmp, 1), 0) * CMP_STRIDE + (CMP_LEN - 1)).astype(F32)
    p, _, _, l = online(jnp.where(c_end <= t_row, s, NEG), m0, l0)
    p = p * inv(l)
    o_cmp = jnp.dot(vct_ref[0], p.astype(BF16), preferred_element_type=F32)

    psum = p[:, 0:Q_BLOCK]
    for r in range(1, N_REP):
        psum = psum + p[:, r * Q_BLOCK:(r + 1) * Q_BLOCK]
    nsel = ovt_ref.shape[0]
    p_hi = psum.astype(BF16)
    p_lo = (psum - p_hi.astype(F32)).astype(BF16)
    imp = (jnp.dot(ovt_ref[...], p_hi, preferred_element_type=F32)
           + jnp.dot(ovt_ref[...], p_lo, preferred_element_type=F32))
    n_col = lax.broadcasted_iota(jnp.int32, (nsel, 1), 0).astype(F32)
    n_start = n_col * SEL_BLOCK
    tq = t_row[:, 0:Q_BLOCK]
    cur = jnp.floor(tq * (1.0 / SEL_BLOCK)) * SEL_BLOCK
    forced = (n_start == cur) | (n_start == 0.0)
    valid = n_start <= tq
    imp = jnp.where(forced, imp + FORCE_BONUS, imp)
    imp = jnp.where(valid, imp, NEG)
    sel = jnp.zeros((nsel, Q_BLOCK), F32)
    for _ in range(min(N_SEL, nsel)):
        mx = jnp.max(imp, axis=0, keepdims=True)
        first = jnp.min(jnp.where(imp == mx, n_col, float(nsel)), axis=0, keepdims=True)
        hit = n_col == first
        sel = jnp.where(hit, 1.0, sel)
        imp = jnp.where(hit, -jnp.inf, imp)
    selb = jnp.where(valid & (sel > 0.0), 0.0, NEG).astype(BF16)
    qs_ref[0:LANE, :] = qt
    qs_ref[LANE:LANE + nsel, :] = jnp.concatenate([selb] * N_REP, axis=1)
    if qs_ref.shape[0] > LANE + nsel:
        qs_ref[LANE + nsel:, :] = jnp.zeros((qs_ref.shape[0] - LANE - nsel, nq), BF16)

    def attend(k_blk, vt_blk, q_op, bias, carry):
        m, l, acc = carry
        s = jnp.dot(k_blk, q_op, preferred_element_type=F32)
        if bias is not None:
            s = s + bias
        p, m, alpha, l = online(s, m, l)
        pv = jnp.dot(vt_blk, p.astype(BF16), preferred_element_type=F32)
        return m, l, alpha * acc + pv

    seq = ks_ref.shape[1]

    def scores(j):
        k0 = pl.multiple_of(jnp.minimum(j * KC_SEL, seq - KC_SEL), KC_SEL)
        s = jnp.dot(ks_ref[0, pl.ds(k0, KC_SEL), :], qs_ref[...], preferred_element_type=F32)
        return s, jnp.max(s, axis=0, keepdims=True)

    def values(j, p):
        k0 = pl.multiple_of(jnp.maximum(j, 0) * KC_SEL, KC_SEL)
        return jnp.dot(vst_ref[0, :, pl.ds(k0, KC_SEL)], p, preferred_element_type=F32)

    def stage(j, s_cur, s_nxt, p_cur, p_prv):
        m, l, alpha_prev, mx = st_ref[0:1, :], st_ref[1:2, :], st_ref[2:3, :], st_ref[3:4, :]
        m_new = jnp.maximum(m, mx)
        alpha = jnp.exp2(m - m_new)
        k0 = pl.multiple_of(jnp.minimum((j + 1) * KC_SEL, seq - KC_SEL), KC_SEL)
        k0p = pl.multiple_of(jnp.maximum(j - 1, 0) * KC_SEL, KC_SEL)
        sub = KC_SEL // 4
        psum, mx_next, zeros = None, None, []
        for q in range(4):
            rows = slice(q * sub, (q + 1) * sub)
            k_q = ks_ref[0, pl.ds(k0 + q * sub, sub), :]
            if q >= 1:
                k_q = k_q + jnp.concatenate([zeros[q - 1]] * (ks_ref.shape[2] // LANE), axis=1)
            s_q = jnp.dot(k_q, qs_ref[...], preferred_element_type=F32)
            s_nxt[rows, :] = s_q
            mx_q = jnp.max(s_q, axis=0, keepdims=True)
            mx_next = mx_q if mx_next is None else jnp.maximum(mx_next, mx_q)
            p_q = jnp.exp2(s_cur[rows, :] - m_new)
            ps_q = jnp.sum(p_q, axis=0, keepdims=True)
            psum = ps_q if psum is None else psum + ps_q
            p_q = p_q.astype(BF16)
            p_cur[rows, :] = p_q
            dep = ps_q[:, 0:LANE]
            for r in range(1, N_REP):
                dep = dep + ps_q[:, r * LANE:(r + 1) * LANE]
            bits = pltpu.bitcast(dep, jnp.int32)
            zeros.append(lax.shift_right_logical(lax.shift_right_logical(bits, 16), 16).astype(F32).astype(BF16))
            if q == 1:
                vt_prev = vst_ref[0, :, pl.ds(k0p, KC_SEL)] + jnp.concatenate([zeros[1]] * (KC_SEL // LANE), axis=1)
                acc_ref[...] = alpha_prev * acc_ref[...] + jnp.dot(vt_prev, p_prv[...], preferred_element_type=F32)
        st_ref[0:1, :] = m_new
        st_ref[1:2, :] = alpha * l + psum
        st_ref[2:3, :] = alpha
        st_ref[3:4, :] = mx_next

    n_full = q0 // KC_SEL
    s_first, mx_first = scores(0)

    @pl.when(n_full % 2 == 0)
    def _():
        s0_ref[...] = s_first

    @pl.when(n_full % 2 == 1)
    def _():
        s1_ref[...] = s_first

    p0_ref[...] = jnp.zeros(p0_ref.shape, BF16)
    p1_ref[...] = jnp.zeros(p1_ref.shape, BF16)
    st_ref[0:1, :] = m0
    st_ref[1:2, :] = l0
    st_ref[2:3, :] = jnp.ones((1, nq), F32)
    st_ref[3:4, :] = mx_first
    acc_ref[...] = a0

    def sel_body(j, carry):
        @pl.when((n_full - j) % 2 == 0)
        def _():
            stage(j, s0_ref, s1_ref, p0_ref, p1_ref)

        @pl.when((n_full - j) % 2 == 1)
        def _():
            stage(j, s1_ref, s0_ref, p1_ref, p0_ref)
        return carry

    lax.fori_loop(0, n_full, sel_body, 0)
    pos_last = (n_full * KC_SEL + lax.broadcasted_iota(jnp.int32, (KC_SEL, 1), 0)).astype(F32)
    p, _, alpha, l_sel = online(s0_ref[...] + jnp.where(pos_last <= t_row, 0.0, NEG), st_ref[0:1, :], st_ref[1:2, :])
    acc = st_ref[2:3, :] * acc_ref[...] + values(n_full - 1, p1_ref[...])
    o_sel = alpha * acc + values(n_full, p.astype(BF16))

    n_wk = WINDOW + Q_BLOCK
    kk = lax.broadcasted_iota(jnp.int32, (n_wk, 1), 0)
    in_win = (kk - WINDOW <= ql) & (kk > ql) & (kk >= WINDOW - q0)
    _, l_win, o_win = attend(kw_ref[0, pl.ds(pl.multiple_of(q0, Q_BLOCK), n_wk), :],
                             vwt_ref[0, :, pl.ds(pl.multiple_of(q0, Q_BLOCK), n_wk)], qt,
                             jnp.where(in_win, 0.0, NEG), (m0, l0, a0))

    gt = g_ref[0, 0]
    o_ref[0, 0] = (gt[0:1] * o_cmp + gt[1:2] * (o_sel * inv(l_sel)) + gt[2:3] * (o_win * inv(l_win)))


def _attention(qt, kc, vct, ks, vst, kw, vwt, gt, ovt):
    bgn, nqb = qt.shape[0], qt.shape[1]
    seq = ks.shape[1]
    nq = N_REP * Q_BLOCK
    per_bg = lambda a: pl.BlockSpec((1,) + a.shape[1:], lambda b, i: (b,) + (0,) * (a.ndim - 1))
    return pl.pallas_call(
        _attn_kernel,
        grid=(bgn, nqb),
        in_specs=[pl.BlockSpec((1, 1, LANE, nq), lambda b, i: (b, i, 0, 0)),
                  per_bg(kc), per_bg(vct), per_bg(ks), per_bg(vst), per_bg(kw), per_bg(vwt),
                  pl.BlockSpec((1, 1, 3, nq), lambda b, i: (b, i, 0, 0)),
                  pl.BlockSpec(ovt.shape, lambda b, i: (0, 0))],
        out_specs=pl.BlockSpec((1, 1, HEAD_DIM, nq), lambda b, i: (b, i, 0, 0)),
        out_shape=jax.ShapeDtypeStruct((bgn, nqb, HEAD_DIM, nq), F32),
        scratch_shapes=[pltpu.VMEM((ks.shape[2], nq), BF16),
                        pltpu.VMEM((KC_SEL, nq), F32), pltpu.VMEM((KC_SEL, nq), F32),
                        pltpu.VMEM((KC_SEL, nq), BF16), pltpu.VMEM((KC_SEL, nq), BF16),
                        pltpu.VMEM((8, nq), F32), pltpu.VMEM((HEAD_DIM, nq), F32)],
        compiler_params=pltpu.CompilerParams(dimension_semantics=("arbitrary", "arbitrary"),
                                             vmem_limit_bytes=VMEM_LIMIT),
        name="nsa_attention",
    )(qt, kc, vct, ks, vst, kw, vwt, gt, ovt)


def _mix_kernel(x_ref, oa_ref, u_ref, v_ref, ws_ref, bs_ref, ga_ref, gg_ref, wo_ref, g2_ref, wr_ref, br_ref,
                x1_ref, xn_ref, idx_ref, gate_ref):
    tm = x_ref.shape[0]
    rr = lax.broadcasted_iota(jnp.int32, (GM_CHUNK, GM_CHUNK), 0)
    cc = lax.broadcasted_iota(jnp.int32, (GM_CHUNK, GM_CHUNK), 1)
    grp = lax.broadcasted_iota(jnp.int32, (1, D_GM), 1) // GM_GROUP_DIM
    ws = [jnp.where(rr >= cc, ws_ref[g], 0.0).astype(BF16) for g in range(N_GM_GROUPS)]
    ys = []
    for c in range(tm // GM_CHUNK):
        vch = v_ref[c * GM_CHUNK:(c + 1) * GM_CHUNK, :].astype(BF16)
        y = bs_ref[...]
        for g in range(N_GM_GROUPS):
            y = y + jnp.where(grp == g, jnp.dot(ws[g], vch, preferred_element_type=F32), 0.0)
        ys.append(y)
    o_gm = u_ref[...] * jnp.concatenate(ys, axis=0)
    o_at = oa_ref[...]
    mixed = jnp.concatenate([(o_at * _rms(o_at)) * ga_ref[...], (o_gm * _rms(o_gm)) * gg_ref[...]], axis=-1)
    x1 = x_ref[...] + jnp.dot(mixed.astype(BF16), wo_ref[...], preferred_element_type=F32)
    x1_ref[...] = x1
    xn = (x1 * _rms(x1)) * g2_ref[...]
    for s in range(ROW_SUB):
        xn_ref[:, s, :] = xn[:, s * LANE:(s + 1) * LANE]
    logits = jnp.dot(xn, wr_ref[...], precision=HIGHEST, preferred_element_type=F32) + br_ref[...]
    lane = lax.broadcasted_iota(jnp.int32, (1, LANE), 1).astype(F32)
    idx_out = jnp.zeros((tm, LANE), F32)
    val_out = jnp.zeros((tm, LANE), F32)
    vals = []
    for k in range(TOP_K):
        mx = jnp.max(logits, axis=-1, keepdims=True)
        first = jnp.min(jnp.where(logits == mx, lane, float(LANE)), axis=-1, keepdims=True)
        logits = jnp.where(lane == first, -jnp.inf, logits)
        idx_out = jnp.where(lane == float(k), first, idx_out)
        vals.append(mx)
    es = [jnp.exp(v - vals[0]) for v in vals]
    den = es[0] + es[1] + es[2] + es[3]
    for k in range(TOP_K):
        val_out = jnp.where(lane == float(k), es[k] / den, val_out)
    idx_ref[...] = idx_out.astype(jnp.int32)
    gate_ref[...] = val_out


def _mix(x2, o_attn, u_act, v_act, gm_w_s, bias_full, ga, gg, w_out_b, g2, wr_pad, br_pad):
    n = x2.shape[0]
    row = lambda c: pl.BlockSpec((TM_MIX, c), lambda i: (i, 0))
    full = lambda a: pl.BlockSpec(a.shape, lambda i: (0,) * a.ndim)
    return pl.pallas_call(
        _mix_kernel,
        grid=(n // TM_MIX,),
        in_specs=[row(D_MODEL), row(D_ATTN), row(D_GM), row(D_GM), full(gm_w_s), full(bias_full), full(ga), full(gg),
                  full(w_out_b), full(g2), full(wr_pad), full(br_pad)],
        out_specs=[row(D_MODEL), pl.BlockSpec((TM_MIX, ROW_SUB, LANE), lambda i: (i, 0, 0)), row(LANE), row(LANE)],
        out_shape=[jax.ShapeDtypeStruct((n, D_MODEL), F32), jax.ShapeDtypeStruct((n, ROW_SUB, LANE), F32),
                   jax.ShapeDtypeStruct((n, LANE), jnp.int32), jax.ShapeDtypeStruct((n, LANE), F32)],
        compiler_params=pltpu.CompilerParams(dimension_semantics=("arbitrary",), vmem_limit_bytes=VMEM_LIMIT),
        name="mix_outproj_router",
    )(x2, o_attn, u_act, v_act, gm_w_s, bias_full, ga, gg, w_out_b, g2, wr_pad, br_pad)


def _row_gather(idx_ref, n_rows, src_hbm, dst_ref, sem):
    def start():
        for r in range(n_rows):
            pltpu.make_async_copy(src_hbm.at[pl.ds(idx_ref[0, 0, r], 1), :], dst_ref.at[pl.ds(r, 1), :], sem).start()

    def wait():
        pltpu.make_async_copy(src_hbm.at[pl.ds(0, n_rows), :], dst_ref, sem).wait()

    return start, wait


def _tile_row_gather(idx_ref, n_rows, src_hbm, dst_ref, sem):
    def start():
        for r in range(n_rows):
            t = idx_ref[0, 0, r]
            pltpu.make_async_copy(src_hbm.at[lax.shift_right_logical(t, 3), t & (ROW_SUB - 1)],
                                  dst_ref.at[r // ROW_SUB, :, r % ROW_SUB, :], sem).start(priority=r % 2)

    def wait():
        pltpu.make_async_copy(src_hbm.at[pl.ds(0, n_rows // ROW_SUB)], dst_ref, sem).wait()

    return start, wait


def _tiles_to_matrix(ref):
    rows = ref.shape[0] * ROW_SUB
    return jnp.concatenate([ref[:, c].reshape(rows, LANE) for c in range(ROW_SUB)], axis=1)


def _moe_kernel(be_ref, bv_ref, bn_ref, bs_ref, tok_ref, tok_n1_ref, tok_n2_ref, x_hbm, wgu_hbm, bg_ref, bl_ref,
                wd_hbm, bd_ref, o_ref, xbuf, sems, wt_s, wg_s, wl_s, wd_s, wgu_buf, wd_buf, wsems):
    i = pl.program_id(0)
    slot = i % (MOE_AHEAD + 1)
    slot_n2 = (i + MOE_AHEAD) % (MOE_AHEAD + 1)
    start_cur, wait_cur = _tile_row_gather(tok_ref, BM_MOE, x_hbm, xbuf.at[slot], sems.at[slot])
    start_n1, _ = _tile_row_gather(tok_n1_ref, BM_MOE, x_hbm, xbuf.at[1], sems.at[1])
    start_n2, _ = _tile_row_gather(tok_n2_ref, BM_MOE, x_hbm, xbuf.at[slot_n2], sems.at[slot_n2])

    prev = jnp.maximum(i - 1, 0)

    def fetch_weights(e, ws):
        n_piece = 4
        rows = D_MODEL // n_piece
        copies = [pltpu.make_async_copy(wgu_hbm.at[e, pl.ds(c * rows, rows)], wgu_buf.at[ws, pl.ds(c * rows, rows)],
                                        wsems.at[ws]) for c in range(n_piece)]
        copies += [pltpu.make_async_copy(wd_hbm.at[e, pl.ds(c * rows, rows)], wd_buf.at[ws, pl.ds(c * rows, rows)],
                                         wsems.at[ws]) for c in range(n_piece)]
        return copies

    @pl.when(i == 0)
    def _():
        start_cur()
        start_n1()
        for cp in fetch_weights(be_ref[0], 0):
            cp.start()

    @pl.when((bv_ref[i] == 1) & ((i == 0) | (be_ref[i] != be_ref[prev])))
    def _():
        ws = bs_ref[i]
        for cp in fetch_weights(be_ref[i], ws):
            cp.wait()

        @pl.when(bn_ref[i] >= 0)
        def _():
            for cp in fetch_weights(bn_ref[i], 1 - ws):
                cp.start()

        tc = wt_s.shape[1]
        for c in range(2 * D_EXPERT // tc):
            wt = wgu_buf[ws, :, c * tc:(c + 1) * tc].T
            for j in range(ROW_SUB):
                wt_s[j] = wt[:, j * LANE:(j + 1) * LANE]
            for first, dst in ((0, wg_s), (1, wl_s)):
                half = jnp.concatenate([wt_s[j, pl.ds(first, tc // 2, stride=2), :] for j in range(ROW_SUB)], axis=1)
                dst[c * tc // 2:(c + 1) * tc // 2, :] = half.astype(BF16)
        wd_s[...] = wd_buf[ws].astype(BF16)

    @pl.when(bv_ref[i] == 1)
    def _():
        wait_cur()
        start_n2()
        xb = _tiles_to_matrix(xbuf.at[slot]).astype(BF16)
        hg = lax.dot_general(xb, wg_s[...], _NT, preferred_element_type=F32) + bg_ref[0]
        hl = lax.dot_general(xb, wl_s[...], _NT, preferred_element_type=F32) + bl_ref[0]
        hg = jnp.minimum(hg, SWIGLU_LIMIT)
        hl = jnp.clip(hl, -SWIGLU_LIMIT, SWIGLU_LIMIT)
        a = hg * jax.nn.sigmoid(SWIGLU_ALPHA * hg) * (hl + 1.0)
        o_ref[...] = jnp.dot(a.astype(BF16), wd_s[...], preferred_element_type=F32) + bd_ref[0]

    @pl.when((bv_ref[i] == 0) & ((i == 1) | ((i >= MOE_AHEAD) & (bv_ref[jnp.maximum(i - MOE_AHEAD, 0)] == 1))))
    def _():
        wait_cur()

    @pl.when(bv_ref[i] == 0)
    def _():
        o_ref[...] = jnp.zeros(o_ref.shape, F32)


def _moe(blk_expert, blk_valid, blk_next, blk_wslot, tok_blocks, xn3, w_gate_up, bg, bl, w_down, bd):
    nb = blk_expert.shape[0]
    per_e = lambda a: pl.BlockSpec((1,) + a.shape[1:], lambda i, be, *_: (be[i],) + (0,) * (a.ndim - 1))

    def tok_spec(ahead):
        return pl.BlockSpec((1, 1, BM_MOE), lambda i, *_: (jnp.minimum(i + ahead, nb - 1), 0, 0),
                            memory_space=pltpu.SMEM)

    grid_spec = pltpu.PrefetchScalarGridSpec(
        num_scalar_prefetch=4,
        grid=(nb,),
        in_specs=[tok_spec(0), tok_spec(1), tok_spec(MOE_AHEAD),
                  pl.BlockSpec(memory_space=pl.ANY),
                  pl.BlockSpec(memory_space=pl.ANY), per_e(bg), per_e(bl),
                  pl.BlockSpec(memory_space=pl.ANY), per_e(bd)],
        out_specs=pl.BlockSpec((BM_MOE, D_MODEL), lambda i, *_: (i, 0)),
        scratch_shapes=[pltpu.VMEM((MOE_AHEAD + 1, BM_MOE // ROW_SUB, ROW_SUB, ROW_SUB, LANE), F32),
                        pltpu.SemaphoreType.DMA((MOE_AHEAD + 1,)),
                        pltpu.VMEM((ROW_SUB, 256, LANE), F32), pltpu.VMEM((D_EXPERT, D_MODEL), BF16),
                        pltpu.VMEM((D_EXPERT, D_MODEL), BF16), pltpu.VMEM((D_EXPERT, D_MODEL), BF16),
                        pltpu.VMEM((2, D_MODEL, 2 * D_EXPERT), F32), pltpu.VMEM((2, D_EXPERT, D_MODEL), F32),
                        pltpu.SemaphoreType.DMA((2,))],
    )
    return pl.pallas_call(
        _moe_kernel,
        grid_spec=grid_spec,
        out_shape=jax.ShapeDtypeStruct((nb * BM_MOE, D_MODEL), F32),
        compiler_params=pltpu.CompilerParams(dimension_semantics=("arbitrary",), vmem_limit_bytes=VMEM_LIMIT_MOE),
        name="moe_experts",
    )(blk_expert, blk_valid, blk_next, blk_wslot, tok_blocks, tok_blocks, tok_blocks,
      xn3.reshape(xn3.shape[0] // ROW_SUB, ROW_SUB, ROW_SUB, LANE), w_gate_up, bg, bl, w_down, bd)


def _combine_kernel(dest_ref, dest_next_ref, x1_ref, gate_ref, y_hbm, o_ref, buf, sems):
    i = pl.program_id(0)
    slot = i % 2
    n_rows = TOP_K * TM_CMB
    start_cur, wait_cur = _row_gather(dest_ref, n_rows, y_hbm, buf.at[slot], sems.at[slot])
    start_next, _ = _row_gather(dest_next_ref, n_rows, y_hbm, buf.at[1 - slot], sems.at[1 - slot])

    @pl.when(i == 0)
    def _():
        start_cur()

    @pl.when(i + 1 < pl.num_programs(0))
    def _():
        start_next()

    wait_cur()
    gate = gate_ref[...]
    acc = x1_ref[...]
    for k in range(TOP_K):
        acc = acc + gate[:, k:k + 1] * buf[slot, k * TM_CMB:(k + 1) * TM_CMB, :]
    o_ref[...] = acc


def _combine(dest_blocks, x1, gate_pad, y_rows):
    n = x1.shape[0]
    nt = n // TM_CMB
    n_rows = TOP_K * TM_CMB
    return pl.pallas_call(
        _combine_kernel,
        grid=(nt,),
        in_specs=[pl.BlockSpec((1, 1, n_rows), lambda i: (i, 0, 0), memory_space=pltpu.SMEM),
                  pl.BlockSpec((1, 1, n_rows), lambda i: (jnp.minimum(i + 1, nt - 1), 0, 0),
                               memory_space=pltpu.SMEM),
                  pl.BlockSpec((TM_CMB, D_MODEL), lambda i: (i, 0)),
                  pl.BlockSpec((TM_CMB, LANE), lambda i: (i, 0)),
                  pl.BlockSpec(memory_space=pl.ANY)],
        out_specs=pl.BlockSpec((TM_CMB, D_MODEL), lambda i: (i, 0)),
        out_shape=jax.ShapeDtypeStruct((n, D_MODEL), F32),
        scratch_shapes=[pltpu.VMEM((2, n_rows, D_MODEL), F32), pltpu.SemaphoreType.DMA((2,))],
        compiler_params=pltpu.CompilerParams(dimension_semantics=("arbitrary",), vmem_limit_bytes=VMEM_LIMIT),
        name="moe_combine",
    )(dest_blocks, dest_blocks, x1, gate_pad, y_rows)


def kernel(x, norm1_g, w_in, q_norm_g, k_norm_g, cmp_pos, w_cmp1, b_cmp1, w_cmp2, b_cmp2, gm_v_norm_g, gm_w_s,
           gm_b_s, out_norm_attn_g, out_norm_gm_g, w_out, norm2_g, w_router, b_router, w_gate_up, b_gate_up,
           w_down, b_down):
    batch, seq, _ = x.shape
    n = batch * seq
    nqb = seq // Q_BLOCK
    bgn = batch * N_KV
    x2 = x.reshape(n, D_MODEL)

    c_gate = D_ATTN + 6 * D_KV
    w_r = jnp.concatenate([w_in[:, :c_gate], w_in[:, c_gate + N_GATE:], w_in[:, c_gate:c_gate + N_GATE],
                           jnp.zeros((D_MODEL, LANE - N_GATE), F32)], axis=1).astype(BF16)
    q, kc_raw, vc_raw, ks, vs, kw, vw, gates, u_act, v_act = _inproj(x2, norm1_g, w_r, q_norm_g, k_norm_g,
                                                                     gm_v_norm_g)

    kc = _compress(kc_raw, cmp_pos[0], w_cmp1[0], b_cmp1[0], w_cmp2[0], b_cmp2[0], k_norm_g[0], batch, seq, True)
    vc = _compress(vc_raw, cmp_pos[1], w_cmp1[1], b_cmp1[1], w_cmp2[1], b_cmp2[1], k_norm_g[0], batch, seq, False)

    nq = N_REP * Q_BLOCK
    ncmp = seq // CMP_STRIDE

    nsel = seq // SEL_BLOCK
    oh_w = -(-nsel // LANE) * LANE

    def with_pos(k, pos, one_hot=False):
        feat = np.zeros((pos.shape[0], LANE - HEAD_DIM + (oh_w if one_hot else 0)), np.float32)
        feat[:, 0] = feat[:, 1] = pos // SEL_BLOCK
        feat[:, 2] = feat[:, 3] = pos % SEL_BLOCK
        if one_hot:
            feat[np.arange(pos.shape[0]), LANE - HEAD_DIM + pos // SEL_BLOCK] = 1.0
        feat = jnp.broadcast_to(jnp.asarray(feat, BF16)[None], (bgn,) + feat.shape)
        return jnp.concatenate([k.astype(BF16), feat], axis=-1)

    def front_pad(a, axis):
        pad = [(0, 0)] * a.ndim
        pad[axis] = (WINDOW, 0)
        return jnp.pad(a, pad)

    def tok_major(a):
        return a.reshape(batch, seq, N_KV, HEAD_DIM).transpose(0, 2, 1, 3).reshape(bgn, seq, HEAD_DIM)

    def feat_major(a):
        return a.reshape(batch, seq, N_KV, HEAD_DIM).transpose(0, 2, 3, 1).reshape(bgn, HEAD_DIM, seq).astype(BF16)

    head = np.arange(N_KV)[:, None] * N_REP + np.arange(nq)[None, :] // Q_BLOCK
    coef = np.exp2(-(head + 1.0)) * LOG2E
    c_hi = coef.astype(BF16).astype(np.float64)
    c_lo = (coef - c_hi).astype(BF16).astype(np.float64)
    qrows = np.zeros((N_KV, LANE - HEAD_DIM, nq), np.float32)
    qrows[:, 0], qrows[:, 1], qrows[:, 2], qrows[:, 3] = SEL_BLOCK * c_hi, SEL_BLOCK * c_lo, c_hi, c_lo
    qrows = jnp.broadcast_to(jnp.asarray(qrows, BF16)[None, :, None], (batch, N_KV, nqb, LANE - HEAD_DIM, nq))
    qt = (q.reshape(batch, nqb, Q_BLOCK, N_KV, N_REP, HEAD_DIM).transpose(0, 3, 1, 5, 4, 2)
          .reshape(batch, N_KV, nqb, HEAD_DIM, nq).astype(BF16))
    qt = jnp.concatenate([qt, qrows], axis=3).reshape(bgn, nqb, LANE, nq)
    gt = (gates[:, :N_GATE].reshape(batch, nqb, Q_BLOCK, N_KV, N_REP, 3).transpose(0, 3, 1, 5, 4, 2)
          .reshape(bgn, nqb, 3, nq))
    pos_t = np.arange(seq)
    pos_c = np.arange(ncmp) * CMP_STRIDE + (CMP_LEN - 1)
    kc_b = with_pos(kc.reshape(bgn, ncmp, HEAD_DIM), pos_c)
    vct = vc.reshape(bgn, ncmp, HEAD_DIM).transpose(0, 2, 1).astype(BF16)
    c0 = np.arange(ncmp)[None, :] * CMP_STRIDE
    n0 = np.arange(seq // SEL_BLOCK)[:, None] * SEL_BLOCK
    ovt = np.clip(np.minimum(c0 + CMP_LEN, n0 + SEL_BLOCK) - np.maximum(c0, n0), 0, None) / CMP_LEN
    pos_w = np.maximum(np.arange(seq + WINDOW) - WINDOW, 0)
    ot = _attention(qt, kc_b, vct, with_pos(tok_major(ks), pos_t, one_hot=True), feat_major(vs),
                    with_pos(front_pad(tok_major(kw), 1), pos_w), front_pad(feat_major(vw), 2), gt,
                    jnp.asarray(ovt, BF16))
    o_attn = (ot.reshape(batch, N_KV, nqb, HEAD_DIM, N_REP, Q_BLOCK).transpose(0, 2, 5, 1, 4, 3)
              .reshape(n, D_ATTN))

    bias_full = jnp.repeat(gm_b_s.T, GM_GROUP_DIM, axis=1)
    wr_pad = jnp.concatenate([w_router, jnp.zeros((D_MODEL, LANE - N_EXPERTS), F32)], axis=1)
    br_pad = jnp.concatenate([b_router, jnp.full((LANE - N_EXPERTS,), NEG, F32)]).reshape(1, LANE)
    x1, xn3, idx_pad, gate_pad = _mix(x2, o_attn, u_act, v_act, gm_w_s, bias_full,
                                     out_norm_attn_g.reshape(1, D_ATTN), out_norm_gm_g.reshape(1, D_GM),
                                     w_out.astype(BF16), norm2_g.reshape(1, D_MODEL), wr_pad, br_pad)

    s_tot = n * TOP_K
    nb = s_tot // BM_MOE + N_EXPERTS - 1 + MOE_AHEAD
    e_flat = idx_pad[:, :TOP_K].reshape(s_tot)
    onehot = (e_flat[:, None] == jnp.arange(N_EXPERTS, dtype=jnp.int32)[None, :]).astype(jnp.int32)
    csum = jnp.cumsum(onehot, axis=0)
    rank = jnp.sum(csum * onehot, axis=1) - 1
    counts = csum[-1]
    padded = ((counts + BM_MOE - 1) // BM_MOE) * BM_MOE
    pad_end = jnp.cumsum(padded)
    pad_start = pad_end - padded
    dest = pad_start[e_flat] + rank
    tok_flat = jnp.arange(s_tot, dtype=jnp.int32) // TOP_K
    tok_buf = jnp.zeros((nb * BM_MOE,), jnp.int32).at[dest].set(tok_flat, unique_indices=True)
    blk_start = jnp.arange(nb, dtype=jnp.int32) * BM_MOE
    blk_expert = jnp.minimum(jnp.sum((blk_start[:, None] >= pad_end[None, :]).astype(jnp.int32), axis=1),
                             N_EXPERTS - 1)
    blk_valid = (blk_start < pad_end[-1]).astype(jnp.int32)
    e_ids = jnp.arange(N_EXPERTS, dtype=jnp.int32)
    present = counts > 0
    ordinal = jnp.cumsum(present.astype(jnp.int32)) - 1
    later = jnp.where(present[None, :] & (e_ids[None, :] > e_ids[:, None]), e_ids[None, :], N_EXPERTS)
    nxt = jnp.min(later, axis=1)
    blk_next = jnp.where(nxt < N_EXPERTS, nxt, -1)[blk_expert].astype(jnp.int32)
    blk_wslot = (ordinal[blk_expert] % 2).astype(jnp.int32)

    bg = b_gate_up[:, 0::2].reshape(N_EXPERTS, 1, D_EXPERT)
    bl = b_gate_up[:, 1::2].reshape(N_EXPERTS, 1, D_EXPERT)
    y_rows = _moe(blk_expert, blk_valid, blk_next, blk_wslot, tok_buf.reshape(nb, 1, BM_MOE), xn3, w_gate_up, bg, bl,
                  w_down,
                  b_down.reshape(N_EXPERTS, 1, D_MODEL))

    dest_blocks = (dest.reshape(n // TM_CMB, TM_CMB, TOP_K).transpose(0, 2, 1)
                   .reshape(n // TM_CMB, 1, TOP_K * TM_CMB).astype(jnp.int32))
    out = _combine(dest_blocks, x1, gate_pad, y_rows)
    return out.reshape(batch, seq, D_MODEL)
```

```python
import functools

import jax
import jax.numpy as jnp
import numpy as np
from jax import lax
from jax.experimental import pallas as pl
from jax.experimental.pallas import tpu as pltpu

F32 = jnp.float32
BF16 = jnp.bfloat16
HIGHEST = lax.Precision.HIGHEST
_NT = (((1,), (1,)), ((), ()))

D_MODEL = 1024
N_HEADS = 8
HEAD_DIM = 64
N_KV = 2
N_REP = N_HEADS // N_KV
D_ATTN = N_HEADS * HEAD_DIM
D_KV = N_KV * HEAD_DIM
N_GM_GROUPS = 8
GM_GROUP_DIM = 64
D_GM = N_GM_GROUPS * GM_GROUP_DIM
N_GATE = 3 * N_HEADS
CMP_LEN = 32
CMP_STRIDE = 16
CMP_HIDDEN = 128
SEL_BLOCK = 64
N_SEL = 16
WINDOW = 512
Q_BLOCK = 128
FORCE_BONUS = 1.0e4
GM_CHUNK = 128
N_EXPERTS = 32
TOP_K = 4
D_EXPERT = 1024
SWIGLU_LIMIT = 7.0
SWIGLU_ALPHA = 1.702
EPS = 1e-6
NEG = -1.0e30
LOG2E = 1.4426950408889634

LANE = 128
ROW_SUB = D_MODEL // LANE
VMEM_LIMIT = 48 * 1024 * 1024
VMEM_LIMIT_MOE = 56 * 1024 * 1024

_C_Q = 0
_C_KC = _C_Q + D_ATTN
_C_VC = _C_KC + D_KV
_C_KS = _C_VC + D_KV
_C_VS = _C_KS + D_KV
_C_KW = _C_VS + D_KV
_C_VW = _C_KW + D_KV
_C_U = _C_VW + D_KV
_C_V = _C_U + D_GM
_C_G = _C_V + D_GM
D_IN_PAD = _C_G + LANE

TM_IN = 256
TM_MIX = 256
KC_SEL = 512
BM_MOE = 256
MOE_AHEAD = 2
TM_CMB = 128


def _rms(x, eps=EPS):
    return lax.rsqrt(jnp.mean(x * x, axis=-1, keepdims=True) + eps)


def _inproj_kernel(x_ref, g1_ref, w_ref, qg_ref, kg_ref, vg_ref,
                   q_ref, kc_ref, vc_ref, ks_ref, vs_ref, kw_ref, vw_ref, gate_ref, u_ref, v_ref):
    x = x_ref[...]
    h = (x * _rms(x)) * g1_ref[...]
    z = jnp.dot(h.astype(BF16), w_ref[...], preferred_element_type=F32)

    def head_norm(col0, n, gain, scale):
        outs = []
        for i in range(n):
            sl = z[:, col0 + i * HEAD_DIM: col0 + (i + 1) * HEAD_DIM]
            outs.append((sl * _rms(sl)) * gain * scale)
        return jnp.concatenate(outs, axis=-1)

    q_ref[...] = head_norm(_C_Q, N_HEADS, qg_ref[...], HEAD_DIM ** -0.5 * LOG2E)
    kc_ref[...] = z[:, _C_KC:_C_KC + D_KV]
    vc_ref[...] = z[:, _C_VC:_C_VC + D_KV]
    ks_ref[...] = head_norm(_C_KS, N_KV, kg_ref[1:2, :], 1.0)
    vs_ref[...] = z[:, _C_VS:_C_VS + D_KV]
    kw_ref[...] = head_norm(_C_KW, N_KV, kg_ref[2:3, :], 1.0)
    vw_ref[...] = z[:, _C_VW:_C_VW + D_KV]
    gate_ref[...] = jax.nn.sigmoid(z[:, _C_G:_C_G + LANE])
    u_ref[...] = jax.nn.gelu(z[:, _C_U:_C_U + D_GM])
    gv = jax.nn.gelu(z[:, _C_V:_C_V + D_GM])
    v_ref[...] = (gv * _rms(gv)) * vg_ref[...]


def _inproj(x2, norm1_g, w_r, q_norm_g, k_norm_g, gm_v_norm_g):
    n = x2.shape[0]
    row = lambda c: pl.BlockSpec((TM_IN, c), lambda i: (i, 0))
    full = lambda a: pl.BlockSpec(a.shape, lambda i: (0,) * a.ndim)
    g1 = norm1_g.reshape(1, D_MODEL)
    qg = q_norm_g.reshape(1, HEAD_DIM)
    vg = gm_v_norm_g.reshape(1, D_GM)
    widths = (D_ATTN, D_KV, D_KV, D_KV, D_KV, D_KV, D_KV, LANE, D_GM, D_GM)
    return pl.pallas_call(
        _inproj_kernel,
        grid=(n // TM_IN,),
        in_specs=[row(D_MODEL), full(g1), full(w_r), full(qg), full(k_norm_g), full(vg)],
        out_specs=[row(c) for c in widths],
        out_shape=[jax.ShapeDtypeStruct((n, c), F32) for c in widths],
        compiler_params=pltpu.CompilerParams(dimension_semantics=("arbitrary",), vmem_limit_bytes=VMEM_LIMIT),
        name="inproj",
    )(x2, g1, w_r, qg, k_norm_g, vg)


def _compress_kernel(a_ref, pos_ref, w1_ref, w1a_ref, w1b_ref, b1_ref, w2_ref, b2_ref, kg_ref, o_ref, *, norm):
    a = a_ref[0]
    nseg = a.shape[0]
    c = jnp.dot(pos_ref[...], w1_ref[...], precision=HIGHEST, preferred_element_type=F32)[0:1] + b1_ref[...]
    row = lax.broadcasted_iota(jnp.int32, (nseg, 1), 0)
    for g in range(N_KV):
        pa = jnp.dot(a, w1a_ref[g], precision=HIGHEST, preferred_element_type=F32)
        pb = jnp.dot(a, w1b_ref[g], precision=HIGHEST, preferred_element_type=F32)
        hid = jax.nn.gelu(pa + pltpu.roll(pb, nseg - 1, 0) + c)
        out = jnp.dot(hid, w2_ref[...], precision=HIGHEST, preferred_element_type=F32) + b2_ref[...]
        if norm:
            out = (out * _rms(out)) * kg_ref[...]
        o_ref[0, g] = jnp.where(row < nseg - 1, out, 0.0)


def _compress(raw, pos, w1, b1, w2, b2, gain, batch, seq, norm):
    nseg = seq // CMP_STRIDE
    half = CMP_STRIDE * HEAD_DIM
    a = raw.reshape(batch, nseg, CMP_STRIDE * D_KV)
    pos8 = jnp.broadcast_to(pos.reshape(1, CMP_LEN * HEAD_DIM), (8, CMP_LEN * HEAD_DIM))

    def expand(wh):
        wh = wh.reshape(CMP_STRIDE, HEAD_DIM, CMP_HIDDEN)
        z = jnp.zeros((N_KV, CMP_STRIDE, N_KV, HEAD_DIM, CMP_HIDDEN), F32)
        for g in range(N_KV):
            z = z.at[g, :, g].set(wh)
        return z.reshape(N_KV, CMP_STRIDE * D_KV, CMP_HIDDEN)

    w1a, w1b = expand(w1[:half]), expand(w1[half:])
    b1r, b2r, gr = b1.reshape(1, CMP_HIDDEN), b2.reshape(1, HEAD_DIM), gain.reshape(1, HEAD_DIM)
    full = lambda t: pl.BlockSpec(t.shape, lambda i: (0,) * t.ndim)
    return pl.pallas_call(
        functools.partial(_compress_kernel, norm=norm),
        grid=(batch,),
        in_specs=[pl.BlockSpec((1, nseg, CMP_STRIDE * D_KV), lambda i: (i, 0, 0)),
                  full(pos8), full(w1), full(w1a), full(w1b), full(b1r), full(w2), full(b2r), full(gr)],
        out_specs=pl.BlockSpec((1, N_KV, nseg, HEAD_DIM), lambda i: (i, 0, 0, 0)),
        out_shape=jax.ShapeDtypeStruct((batch, N_KV, nseg, HEAD_DIM), F32),
        compiler_params=pltpu.CompilerParams(dimension_semantics=("arbitrary",), vmem_limit_bytes=VMEM_LIMIT),
        name="compress_k" if norm else "compress_v",
    )(a, pos8, w1, w1a, w1b, b1r, w2, b2r, gr)


def _attn_kernel(qt_ref, kc_ref, vct_ref, ks_ref, vst_ref, kw_ref, vwt_ref, g_ref, ovt_ref, o_ref,
                 qs_ref, s0_ref, s1_ref, p0_ref, p1_ref, st_ref, acc_ref):
    qb = pl.program_id(1)
    nq = N_REP * Q_BLOCK
    q0 = qb * Q_BLOCK
    qt = qt_ref[0, 0]
    ql = lax.broadcasted_iota(jnp.int32, (1, nq), 1) % Q_BLOCK
    t_row = (q0 + ql).astype(F32)
    m_init = 0.5 * NEG

    def online(s, m, l):
        m_new = jnp.maximum(m, jnp.max(s, axis=0, keepdims=True))
        alpha = jnp.exp2(m - m_new)
        p = jnp.exp2(s - m_new)
        return p, m_new, alpha, alpha * l + jnp.sum(p, axis=0, keepdims=True)

    def inv(l):
        return jnp.where(l > 0.0, 1.0 / l, 0.0)

    m0 = jnp.full((1, nq), m_init, F32)
    l0 = jnp.zeros((1, nq), F32)
    a0 = jnp.zeros((HEAD_DIM, nq), F32)

    ncmp = kc_ref.shape[1]
    s = jnp.dot(kc_ref[0], qt, preferred_element_type=F32)
    c_end = (lax.broadcasted_iota(jnp.int32, (ncmp, 1), 0) * CMP_STRIDE + (CMP_LEN - 1)).astype(F32)
    p, _, _, l = online(jnp.where(c_end <= t_row, s, NEG), m0, l0)
    p = p * inv(l)
    o_cmp = jnp.dot(vct_ref[0], p.astype(BF16), preferred_element_type=F32)

    psum = p[:, 0:Q_BLOCK]
    for r in range(1, N_REP):
        psum = psum + p[:, r * Q_BLOCK:(r + 1) * Q_BLOCK]
    nsel = ovt_ref.shape[0]
    p_hi = psum.astype(BF16)
    p_lo = (psum - p_hi.astype(F32)).astype(BF16)
    imp = (jnp.dot(ovt_ref[...], p_hi, preferred_element_type=F32)
           + jnp.dot(ovt_ref[...], p_lo, preferred_element_type=F32))
    n_col = lax.broadcasted_iota(jnp.int32, (nsel, 1), 0).astype(F32)
    n_start = n_col * SEL_BLOCK
    tq = t_row[:, 0:Q_BLOCK]
    cur = jnp.floor(tq * (1.0 / SEL_BLOCK)) * SEL_BLOCK
    forced = (n_start == cur) | (n_start == 0.0)
    valid = n_start <= tq
    imp = jnp.where(forced, imp + FORCE_BONUS, imp)
    imp = jnp.where(valid, imp, NEG)
    sel = jnp.zeros((nsel, Q_BLOCK), F32)
    for _ in range(min(N_SEL, nsel)):
        mx = jnp.max(imp, axis=0, keepdims=True)
        first = jnp.min(jnp.where(imp == mx, n_col, float(nsel)), axis=0, keepdims=True)
        hit = n_col == first
        sel = jnp.where(hit, 1.0, sel)
        imp = jnp.where(hit, -jnp.inf, imp)
    selb = jnp.where(valid & (sel > 0.0), 0.0, NEG).astype(BF16)
    qs_ref[0:LANE, :] = qt
    qs_ref[LANE:LANE + nsel, :] = jnp.concatenate([selb] * N_REP, axis=1)
    if qs_ref.shape[0] > LANE + nsel:
        qs_ref[LANE + nsel:, :] = jnp.zeros((qs_ref.shape[0] - LANE - nsel, nq), BF16)

    def attend(k_blk, vt_blk, q_op, bias, carry):
        m, l, acc = carry
        s = jnp.dot(k_blk, q_op, preferred_element_type=F32)
        if bias is not None:
            s = s + bias
        p, m, alpha, l = online(s, m, l)
        pv = jnp.dot(vt_blk, p.astype(BF16), preferred_element_type=F32)
        return m, l, alpha * acc + pv

    seq = ks_ref.shape[1]

    def scores(j):
        k0 = pl.multiple_of(jnp.minimum(j * KC_SEL, seq - KC_SEL), KC_SEL)
        s = jnp.dot(ks_ref[0, pl.ds(k0, KC_SEL), :], qs_ref[...], preferred_element_type=F32)
        return s, jnp.max(s, axis=0, keepdims=True)

    def values(j, p):
        k0 = pl.multiple_of(jnp.maximum(j, 0) * KC_SEL, KC_SEL)
        return jnp.dot(vst_ref[0, :, pl.ds(k0, KC_SEL)], p, preferred_element_type=F32)

    def stage(j, s_cur, s_nxt, p_cur, p_prv):
        m, l, alpha_prev, mx = st_ref[0:1, :], st_ref[1:2, :], st_ref[2:3, :], st_ref[3:4, :]
        m_new = jnp.maximum(m, mx)
        alpha = jnp.exp2(m - m_new)
        k0 = pl.multiple_of(jnp.minimum((j + 1) * KC_SEL, seq - KC_SEL), KC_SEL)
        k0p = pl.multiple_of(jnp.maximum(j - 1, 0) * KC_SEL, KC_SEL)
        sub = KC_SEL // 4
        psum, mx_next, zeros = None, None, []
        for q in range(4):
            rows = slice(q * sub, (q + 1) * sub)
            k_q = ks_ref[0, pl.ds(k0 + q * sub, sub), :]
            if q >= 1:
                k_q = k_q + jnp.concatenate([zeros[q - 1]] * (ks_ref.shape[2] // LANE), axis=1)
            s_q = jnp.dot(k_q, qs_ref[...], preferred_element_type=F32)
            s_nxt[rows, :] = s_q
            mx_q = jnp.max(s_q, axis=0, keepdims=True)
            mx_next = mx_q if mx_next is None else jnp.maximum(mx_next, mx_q)
            p_q = jnp.exp2(s_cur[rows, :] - m_new)
            ps_q = jnp.sum(p_q, axis=0, keepdims=True)
            psum = ps_q if psum is None else psum + ps_q
            p_q = p_q.astype(BF16)
            p_cur[rows, :] = p_q
            dep = ps_q[:, 0:LANE]
            for r in range(1, N_REP):
                dep = dep + ps_q[:, r * LANE:(r + 1) * LANE]
            bits = pltpu.bitcast(dep, jnp.int32)
            zeros.append(lax.shift_right_logical(lax.shift_right_logical(bits, 16), 16).astype(F32).astype(BF16))
            if q == 1:
                vt_prev = vst_ref[0, :, pl.ds(k0p, KC_SEL)] + jnp.concatenate([zeros[1]] * (KC_SEL // LANE), axis=1)
                acc_ref[...] = alpha_prev * acc_ref[...] + jnp.dot(vt_prev, p_prv[...], preferred_element_type=F32)
        st_ref[0:1, :] = m_new
        st_ref[1:2, :] = alpha * l + psum
        st_ref[2:3, :] = alpha
        st_ref[3:4, :] = mx_next

    n_full = q0 // KC_SEL
    s_first, mx_first = scores(0)

    @pl.when(n_full % 2 == 0)
    def _():
        s0_ref[...] = s_first

    @pl.when(n_full % 2 == 1)
    def _():
        s1_ref[...] = s_first

    p0_ref[...] = jnp.zeros(p0_ref.shape, BF16)
    p1_ref[...] = jnp.zeros(p1_ref.shape, BF16)
    st_ref[0:1, :] = m0
    st_ref[1:2, :] = l0
    st_ref[2:3, :] = jnp.ones((1, nq), F32)
    st_ref[3:4, :] = mx_first
    acc_ref[...] = a0

    def sel_body(j, carry):
        @pl.when((n_full - j) % 2 == 0)
        def _():
            stage(j, s0_ref, s1_ref, p0_ref, p1_ref)

        @pl.when((n_full - j) % 2 == 1)
        def _():
            stage(j, s1_ref, s0_ref, p1_ref, p0_ref)
        return carry

    lax.fori_loop(0, n_full, sel_body, 0)
    pos_last = (n_full * KC_SEL + lax.broadcasted_iota(jnp.int32, (KC_SEL, 1), 0)).astype(F32)
    p, _, alpha, l_sel = online(s0_ref[...] + jnp.where(pos_last <= t_row, 0.0, NEG), st_ref[0:1, :], st_ref[1:2, :])
    acc = st_ref[2:3, :] * acc_ref[...] + values(n_full - 1, p1_ref[...])
    o_sel = alpha * acc + values(n_full, p.astype(BF16))

    n_wk = WINDOW + Q_BLOCK
    kk = lax.broadcasted_iota(jnp.int32, (n_wk, 1), 0)
    in_win = (kk - WINDOW <= ql) & (kk > ql) & (kk >= WINDOW - q0)
    _, l_win, o_win = attend(kw_ref[0, pl.ds(pl.multiple_of(q0, Q_BLOCK), n_wk), :],
                             vwt_ref[0, :, pl.ds(pl.multiple_of(q0, Q_BLOCK), n_wk)], qt,
                             jnp.where(in_win, 0.0, NEG), (m0, l0, a0))

    gt = g_ref[0, 0]
    o_ref[0, 0] = (gt[0:1] * o_cmp + gt[1:2] * (o_sel * inv(l_sel)) + gt[2:3] * (o_win * inv(l_win)))


def _attention(qt, kc, vct, ks, vst, kw, vwt, gt, ovt):
    bgn, nqb = qt.shape[0], qt.shape[1]
    seq = ks.shape[1]
    nq = N_REP * Q_BLOCK
    per_bg = lambda a: pl.BlockSpec((1,) + a.shape[1:], lambda b, i: (b,) + (0,) * (a.ndim - 1))
    return pl.pallas_call(
        _attn_kernel,
        grid=(bgn, nqb),
        in_specs=[pl.BlockSpec((1, 1, LANE, nq), lambda b, i: (b, i, 0, 0)),
                  per_bg(kc), per_bg(vct), per_bg(ks), per_bg(vst), per_bg(kw), per_bg(vwt),
                  pl.BlockSpec((1, 1, 3, nq), lambda b, i: (b, i, 0, 0)),
                  pl.BlockSpec(ovt.shape, lambda b, i: (0, 0))],
        out_specs=pl.BlockSpec((1, 1, HEAD_DIM, nq), lambda b, i: (b, i, 0, 0)),
        out_shape=jax.ShapeDtypeStruct((bgn, nqb, HEAD_DIM, nq), F32),
        scratch_shapes=[pltpu.VMEM((ks.shape[2], nq), BF16),
                        pltpu.VMEM((KC_SEL, nq), F32), pltpu.VMEM((KC_SEL, nq), F32),
                        pltpu.VMEM((KC_SEL, nq), BF16), pltpu.VMEM((KC_SEL, nq), BF16),
                        pltpu.VMEM((8, nq), F32), pltpu.VMEM((HEAD_DIM, nq), F32)],
        compiler_params=pltpu.CompilerParams(dimension_semantics=("arbitrary", "arbitrary"),
                                             vmem_limit_bytes=VMEM_LIMIT),
        name="nsa_attention",
    )(qt, kc, vct, ks, vst, kw, vwt, gt, ovt)


def _mix_kernel(x_ref, oa_ref, u_ref, v_ref, ws_ref, bs_ref, ga_ref, gg_ref, wo_ref, g2_ref, wr_ref, br_ref,
                x1_ref, xn_ref, idx_ref, gate_ref):
    tm = x_ref.shape[0]
    rr = lax.broadcasted_iota(jnp.int32, (GM_CHUNK, GM_CHUNK), 0)
    cc = lax.broadcasted_iota(jnp.int32, (GM_CHUNK, GM_CHUNK), 1)
    grp = lax.broadcasted_iota(jnp.int32, (1, D_GM), 1) // GM_GROUP_DIM
    ws = [jnp.where(rr >= cc, ws_ref[g], 0.0).astype(BF16) for g in range(N_GM_GROUPS)]
    ys = []
    for c in range(tm // GM_CHUNK):
        vch = v_ref[c * GM_CHUNK:(c + 1) * GM_CHUNK, :].astype(BF16)
        y = bs_ref[...]
        for g in range(N_GM_GROUPS):
            y = y + jnp.where(grp == g, jnp.dot(ws[g], vch, preferred_element_type=F32), 0.0)
        ys.append(y)
    o_gm = u_ref[...] * jnp.concatenate(ys, axis=0)
    o_at = oa_ref[...]
    mixed = jnp.concatenate([(o_at * _rms(o_at)) * ga_ref[...], (o_gm * _rms(o_gm)) * gg_ref[...]], axis=-1)
    x1 = x_ref[...] + jnp.dot(mixed.astype(BF16), wo_ref[...], preferred_element_type=F32)
    x1_ref[...] = x1
    xn = (x1 * _rms(x1)) * g2_ref[...]
    for s in range(ROW_SUB):
        xn_ref[:, s, :] = xn[:, s * LANE:(s + 1) * LANE]
    logits = jnp.dot(xn, wr_ref[...], precision=HIGHEST, preferred_element_type=F32) + br_ref[...]
    lane = lax.broadcasted_iota(jnp.int32, (1, LANE), 1).astype(F32)
    idx_out = jnp.zeros((tm, LANE), F32)
    val_out = jnp.zeros((tm, LANE), F32)
    vals = []
    for k in range(TOP_K):
        mx = jnp.max(logits, axis=-1, keepdims=True)
        first = jnp.min(jnp.where(logits == mx, lane, float(LANE)), axis=-1, keepdims=True)
        logits = jnp.where(lane == first, -jnp.inf, logits)
        idx_out = jnp.where(lane == float(k), first, idx_out)
        vals.append(mx)
    es = [jnp.exp(v - vals[0]) for v in vals]
    den = es[0] + es[1] + es[2] + es[3]
    for k in range(TOP_K):
        val_out = jnp.where(lane == float(k), es[k] / den, val_out)
    idx_ref[...] = idx_out.astype(jnp.int32)
    gate_ref[...] = val_out


def _mix(x2, o_attn, u_act, v_act, gm_w_s, bias_full, ga, gg, w_out_b, g2, wr_pad, br_pad):
    n = x2.shape[0]
    row = lambda c: pl.BlockSpec((TM_MIX, c), lambda i: (i, 0))
    full = lambda a: pl.BlockSpec(a.shape, lambda i: (0,) * a.ndim)
    return pl.pallas_call(
        _mix_kernel,
        grid=(n // TM_MIX,),
        in_specs=[row(D_MODEL), row(D_ATTN), row(D_GM), row(D_GM), full(gm_w_s), full(bias_full), full(ga), full(gg),
                  full(w_out_b), full(g2), full(wr_pad), full(br_pad)],
        out_specs=[row(D_MODEL), pl.BlockSpec((TM_MIX, ROW_SUB, LANE), lambda i: (i, 0, 0)), row(LANE), row(LANE)],
        out_shape=[jax.ShapeDtypeStruct((n, D_MODEL), F32), jax.ShapeDtypeStruct((n, ROW_SUB, LANE), F32),
                   jax.ShapeDtypeStruct((n, LANE), jnp.int32), jax.ShapeDtypeStruct((n, LANE), F32)],
        compiler_params=pltpu.CompilerParams(dimension_semantics=("arbitrary",), vmem_limit_bytes=VMEM_LIMIT),
        name="mix_outproj_router",
    )(x2, o_attn, u_act, v_act, gm_w_s, bias_full, ga, gg, w_out_b, g2, wr_pad, br_pad)


def _row_gather(idx_ref, n_rows, src_hbm, dst_ref, sem):
    def start():
        for r in range(n_rows):
            pltpu.make_async_copy(src_hbm.at[pl.ds(idx_ref[0, 0, r], 1), :], dst_ref.at[pl.ds(r, 1), :], sem).start()

    def wait():
        pltpu.make_async_copy(src_hbm.at[pl.ds(0, n_rows), :], dst_ref, sem).wait()

    return start, wait


def _tile_row_gather(idx_ref, n_rows, src_hbm, dst_ref, sem):
    def start():
        for r in range(n_rows):
            t = idx_ref[0, 0, r]
            pltpu.make_async_copy(src_hbm.at[lax.shift_right_logical(t, 3), t & (ROW_SUB - 1)],
                                  dst_ref.at[r // ROW_SUB, :, r % ROW_SUB, :], sem).start(priority=r % 2)

    def wait():
        pltpu.make_async_copy(src_hbm.at[pl.ds(0, n_rows // ROW_SUB)], dst_ref, sem).wait()

    return start, wait


def _tiles_to_matrix(ref):
    rows = ref.shape[0] * ROW_SUB
    return jnp.concatenate([ref[:, c].reshape(rows, LANE) for c in range(ROW_SUB)], axis=1)


def _moe_kernel(be_ref, bv_ref, bn_ref, bs_ref, tok_ref, tok_n1_ref, tok_n2_ref, x_hbm, wgu_hbm, bg_ref, bl_ref,
                wd_hbm, bd_ref, o_ref, xbuf, sems, wt_s, wg_s, wl_s, wd_s, wgu_buf, wd_buf, wsems):
    i = pl.program_id(0)
    slot = i % (MOE_AHEAD + 1)
    slot_n2 = (i + MOE_AHEAD) % (MOE_AHEAD + 1)
    start_cur, wait_cur = _tile_row_gather(tok_ref, BM_MOE, x_hbm, xbuf.at[slot], sems.at[slot])
    start_n1, _ = _tile_row_gather(tok_n1_ref, BM_MOE, x_hbm, xbuf.at[1], sems.at[1])
    start_n2, _ = _tile_row_gather(tok_n2_ref, BM_MOE, x_hbm, xbuf.at[slot_n2], sems.at[slot_n2])

    prev = jnp.maximum(i - 1, 0)

    def fetch_weights(e, ws):
        n_piece = 4
        rows = D_MODEL // n_piece
        copies = [pltpu.make_async_copy(wgu_hbm.at[e, pl.ds(c * rows, rows)], wgu_buf.at[ws, pl.ds(c * rows, rows)],
                                        wsems.at[ws]) for c in range(n_piece)]
        copies += [pltpu.make_async_copy(wd_hbm.at[e, pl.ds(c * rows, rows)], wd_buf.at[ws, pl.ds(c * rows, rows)],
                                         wsems.at[ws]) for c in range(n_piece)]
        return copies

    @pl.when(i == 0)
    def _():
        start_cur()
        start_n1()
        for cp in fetch_weights(be_ref[0], 0):
            cp.start()

    @pl.when((bv_ref[i] == 1) & ((i == 0) | (be_ref[i] != be_ref[prev])))
    def _():
        ws = bs_ref[i]
        for cp in fetch_weights(be_ref[i], ws):
            cp.wait()

        @pl.when(bn_ref[i] >= 0)
        def _():
            for cp in fetch_weights(bn_ref[i], 1 - ws):
                cp.start()

        tc = wt_s.shape[1]
        for c in range(2 * D_EXPERT // tc):
            wt = wgu_buf[ws, :, c * tc:(c + 1) * tc].T
            for j in range(ROW_SUB):
                wt_s[j] = wt[:, j * LANE:(j + 1) * LANE]
            for first, dst in ((0, wg_s), (1, wl_s)):
                half = jnp.concatenate([wt_s[j, pl.ds(first, tc // 2, stride=2), :] for j in range(ROW_SUB)], axis=1)
                dst[c * tc // 2:(c + 1) * tc // 2, :] = half.astype(BF16)
        wd_s[...] = wd_buf[ws].astype(BF16)

    @pl.when(bv_ref[i] == 1)
    def _():
        wait_cur()
        start_n2()
        xb = _tiles_to_matrix(xbuf.at[slot]).astype(BF16)
        hg = lax.dot_general(xb, wg_s[...], _NT, preferred_element_type=F32) + bg_ref[0]
        hl = lax.dot_general(xb, wl_s[...], _NT, preferred_element_type=F32) + bl_ref[0]
        hg = jnp.minimum(hg, SWIGLU_LIMIT)
        hl = jnp.clip(hl, -SWIGLU_LIMIT, SWIGLU_LIMIT)
        a = hg * jax.nn.sigmoid(SWIGLU_ALPHA * hg) * (hl + 1.0)
        o_ref[...] = jnp.dot(a.astype(BF16), wd_s[...], preferred_element_type=F32) + bd_ref[0]

    @pl.when((bv_ref[i] == 0) & ((i == 1) | ((i >= MOE_AHEAD) & (bv_ref[jnp.maximum(i - MOE_AHEAD, 0)] == 1))))
    def _():
        wait_cur()

    @pl.when(bv_ref[i] == 0)
    def _():
        o_ref[...] = jnp.zeros(o_ref.shape, F32)


def _moe(blk_expert, blk_valid, blk_next, blk_wslot, tok_blocks, xn3, w_gate_up, bg, bl, w_down, bd):
    nb = blk_expert.shape[0]
    per_e = lambda a: pl.BlockSpec((1,) + a.shape[1:], lambda i, be, *_: (be[i],) + (0,) * (a.ndim - 1))

    def tok_spec(ahead):
        return pl.BlockSpec((1, 1, BM_MOE), lambda i, *_: (jnp.minimum(i + ahead, nb - 1), 0, 0),
                            memory_space=pltpu.SMEM)

    grid_spec = pltpu.PrefetchScalarGridSpec(
        num_scalar_prefetch=4,
        grid=(nb,),
        in_specs=[tok_spec(0), tok_spec(1), tok_spec(MOE_AHEAD),
                  pl.BlockSpec(memory_space=pl.ANY),
                  pl.BlockSpec(memory_space=pl.ANY), per_e(bg), per_e(bl),
                  pl.BlockSpec(memory_space=pl.ANY), per_e(bd)],
        out_specs=pl.BlockSpec((BM_MOE, D_MODEL), lambda i, *_: (i, 0)),
        scratch_shapes=[pltpu.VMEM((MOE_AHEAD + 1, BM_MOE // ROW_SUB, ROW_SUB, ROW_SUB, LANE), F32),
                        pltpu.SemaphoreType.DMA((MOE_AHEAD + 1,)),
                        pltpu.VMEM((ROW_SUB, 256, LANE), F32), pltpu.VMEM((D_EXPERT, D_MODEL), BF16),
                        pltpu.VMEM((D_EXPERT, D_MODEL), BF16), pltpu.VMEM((D_EXPERT, D_MODEL), BF16),
                        pltpu.VMEM((2, D_MODEL, 2 * D_EXPERT), F32), pltpu.VMEM((2, D_EXPERT, D_MODEL), F32),
                        pltpu.SemaphoreType.DMA((2,))],
    )
    return pl.pallas_call(
        _moe_kernel,
        grid_spec=grid_spec,
        out_shape=jax.ShapeDtypeStruct((nb * BM_MOE, D_MODEL), F32),
        compiler_params=pltpu.CompilerParams(dimension_semantics=("arbitrary",), vmem_limit_bytes=VMEM_LIMIT_MOE),
        name="moe_experts",
    )(blk_expert, blk_valid, blk_next, blk_wslot, tok_blocks, tok_blocks, tok_blocks,
      xn3.reshape(xn3.shape[0] // ROW_SUB, ROW_SUB, ROW_SUB, LANE), w_gate_up, bg, bl, w_down, bd)


def _slot_table_kernel(lo_ref, hi_ref, dest_ref, tok_ref):
    i = pl.program_id(0)
    n_chunk = dest_ref.shape[2]

    @pl.when(i == 0)
    def _():
        def clear(s, carry):
            tok_ref[s] = 0
            return carry
        for e in range(lo_ref.shape[0]):
            lax.fori_loop(lo_ref[e], hi_ref[e], clear, 0)

    def put(s, carry):
        tok_ref[dest_ref[0, 0, s]] = lax.shift_right_logical(i * n_chunk + s, TOP_K.bit_length() - 1)
        return carry
    lax.fori_loop(0, n_chunk, put, 0, unroll=8)


def _slot_table(pad_lo, pad_hi, dest, n_slots):
    n_chunk = 8192
    s_tot = dest.shape[0]
    grid_spec = pltpu.PrefetchScalarGridSpec(
        num_scalar_prefetch=2,
        grid=(s_tot // n_chunk,),
        in_specs=[pl.BlockSpec((1, 1, n_chunk), lambda i, *_: (i, 0, 0), memory_space=pltpu.SMEM)],
        out_specs=pl.BlockSpec(memory_space=pltpu.SMEM),
    )
    return pl.pallas_call(
        _slot_table_kernel,
        grid_spec=grid_spec,
        out_shape=jax.ShapeDtypeStruct((n_slots,), jnp.int32),
        compiler_params=pltpu.CompilerParams(dimension_semantics=("arbitrary",)),
        name="moe_slot_table",
    )(pad_lo, pad_hi, dest.reshape(s_tot // n_chunk, 1, n_chunk))


def _combine_kernel(dest_ref, dest_next_ref, x1_ref, gate_ref, y_hbm, o_ref, buf, sems):
    i = pl.program_id(0)
    slot = i % 2
    n_rows = TOP_K * TM_CMB
    start_cur, wait_cur = _row_gather(dest_ref, n_rows, y_hbm, buf.at[slot], sems.at[slot])
    start_next, _ = _row_gather(dest_next_ref, n_rows, y_hbm, buf.at[1 - slot], sems.at[1 - slot])

    @pl.when(i == 0)
    def _():
        start_cur()

    @pl.when(i + 1 < pl.num_programs(0))
    def _():
        start_next()

    wait_cur()
    gate = gate_ref[...]
    acc = x1_ref[...]
    for k in range(TOP_K):
        acc = acc + gate[:, k:k + 1] * buf[slot, k * TM_CMB:(k + 1) * TM_CMB, :]
    o_ref[...] = acc


def _combine(dest_blocks, x1, gate_pad, y_rows):
    n = x1.shape[0]
    nt = n // TM_CMB
    n_rows = TOP_K * TM_CMB
    return pl.pallas_call(
        _combine_kernel,
        grid=(nt,),
        in_specs=[pl.BlockSpec((1, 1, n_rows), lambda i: (i, 0, 0), memory_space=pltpu.SMEM),
                  pl.BlockSpec((1, 1, n_rows), lambda i: (jnp.minimum(i + 1, nt - 1), 0, 0),
                               memory_space=pltpu.SMEM),
                  pl.BlockSpec((TM_CMB, D_MODEL), lambda i: (i, 0)),
                  pl.BlockSpec((TM_CMB, LANE), lambda i: (i, 0)),
                  pl.BlockSpec(memory_space=pl.ANY)],
        out_specs=pl.BlockSpec((TM_CMB, D_MODEL), lambda i: (i, 0)),
        out_shape=jax.ShapeDtypeStruct((n, D_MODEL), F32),
        scratch_shapes=[pltpu.VMEM((2, n_rows, D_MODEL), F32), pltpu.SemaphoreType.DMA((2,))],
        compiler_params=pltpu.CompilerParams(dimension_semantics=("arbitrary",), vmem_limit_bytes=VMEM_LIMIT),
        name="moe_combine",
    )(dest_blocks, dest_blocks, x1, gate_pad, y_rows)


def kernel(x, norm1_g, w_in, q_norm_g, k_norm_g, cmp_pos, w_cmp1, b_cmp1, w_cmp2, b_cmp2, gm_v_norm_g, gm_w_s,
           gm_b_s, out_norm_attn_g, out_norm_gm_g, w_out, norm2_g, w_router, b_router, w_gate_up, b_gate_up,
           w_down, b_down):
    batch, seq, _ = x.shape
    n = batch * seq
    nqb = seq // Q_BLOCK
    bgn = batch * N_KV
    x2 = x.reshape(n, D_MODEL)

    c_gate = D_ATTN + 6 * D_KV
    w_r = jnp.concatenate([w_in[:, :c_gate], w_in[:, c_gate + N_GATE:], w_in[:, c_gate:c_gate + N_GATE],
                           jnp.zeros((D_MODEL, LANE - N_GATE), F32)], axis=1).astype(BF16)
    q, kc_raw, vc_raw, ks, vs, kw, vw, gates, u_act, v_act = _inproj(x2, norm1_g, w_r, q_norm_g, k_norm_g,
                                                                     gm_v_norm_g)

    kc = _compress(kc_raw, cmp_pos[0], w_cmp1[0], b_cmp1[0], w_cmp2[0], b_cmp2[0], k_norm_g[0], batch, seq, True)
    vc = _compress(vc_raw, cmp_pos[1], w_cmp1[1], b_cmp1[1], w_cmp2[1], b_cmp2[1], k_norm_g[0], batch, seq, False)

    nq = N_REP * Q_BLOCK
    ncmp = seq // CMP_STRIDE

    nsel = seq // SEL_BLOCK
    oh_w = -(-nsel // LANE) * LANE

    def with_pos(k, pos, one_hot=False):
        feat = np.zeros((pos.shape[0], LANE - HEAD_DIM + (oh_w if one_hot else 0)), np.float32)
        feat[:, 0] = feat[:, 1] = pos // SEL_BLOCK
        feat[:, 2] = feat[:, 3] = pos % SEL_BLOCK
        if one_hot:
            feat[np.arange(pos.shape[0]), LANE - HEAD_DIM + pos // SEL_BLOCK] = 1.0
        feat = jnp.broadcast_to(jnp.asarray(feat, BF16)[None], (bgn,) + feat.shape)
        return jnp.concatenate([k.astype(BF16), feat], axis=-1)

    def front_pad(a, axis):
        pad = [(0, 0)] * a.ndim
        pad[axis] = (WINDOW, 0)
        return jnp.pad(a, pad)

    def tok_major(a):
        return a.reshape(batch, seq, N_KV, HEAD_DIM).transpose(0, 2, 1, 3).reshape(bgn, seq, HEAD_DIM)

    def feat_major(a):
        return a.reshape(batch, seq, N_KV, HEAD_DIM).transpose(0, 2, 3, 1).reshape(bgn, HEAD_DIM, seq).astype(BF16)

    head = np.arange(N_KV)[:, None] * N_REP + np.arange(nq)[None, :] // Q_BLOCK
    coef = np.exp2(-(head + 1.0)) * LOG2E
    c_hi = coef.astype(BF16).astype(np.float64)
    c_lo = (coef - c_hi).astype(BF16).astype(np.float64)
    qrows = np.zeros((N_KV, LANE - HEAD_DIM, nq), np.float32)
    qrows[:, 0], qrows[:, 1], qrows[:, 2], qrows[:, 3] = SEL_BLOCK * c_hi, SEL_BLOCK * c_lo, c_hi, c_lo
    qrows = jnp.broadcast_to(jnp.asarray(qrows, BF16)[None, :, None], (batch, N_KV, nqb, LANE - HEAD_DIM, nq))
    qt = (q.reshape(batch, nqb, Q_BLOCK, N_KV, N_REP, HEAD_DIM).transpose(0, 3, 1, 5, 4, 2)
          .reshape(batch, N_KV, nqb, HEAD_DIM, nq).astype(BF16))
    qt = jnp.concatenate([qt, qrows], axis=3).reshape(bgn, nqb, LANE, nq)
    gt = (gates[:, :N_GATE].reshape(batch, nqb, Q_BLOCK, N_KV, N_REP, 3).transpose(0, 3, 1, 5, 4, 2)
          .reshape(bgn, nqb, 3, nq))
    pos_t = np.arange(seq)
    pos_c = np.arange(ncmp) * CMP_STRIDE + (CMP_LEN - 1)
    kc_b = with_pos(kc.reshape(bgn, ncmp, HEAD_DIM), pos_c)
    vct = vc.reshape(bgn, ncmp, HEAD_DIM).transpose(0, 2, 1).astype(BF16)
    c0 = np.arange(ncmp)[None, :] * CMP_STRIDE
    n0 = np.arange(seq // SEL_BLOCK)[:, None] * SEL_BLOCK
    ovt = np.clip(np.minimum(c0 + CMP_LEN, n0 + SEL_BLOCK) - np.maximum(c0, n0), 0, None) / CMP_LEN
    pos_w = np.maximum(np.arange(seq + WINDOW) - WINDOW, 0)
    ot = _attention(qt, kc_b, vct, with_pos(tok_major(ks), pos_t, one_hot=True), feat_major(vs),
                    with_pos(front_pad(tok_major(kw), 1), pos_w), front_pad(feat_major(vw), 2), gt,
                    jnp.asarray(ovt, BF16))
    o_attn = (ot.reshape(batch, N_KV, nqb, HEAD_DIM, N_REP, Q_BLOCK).transpose(0, 2, 5, 1, 4, 3)
              .reshape(n, D_ATTN))

    bias_full = jnp.repeat(gm_b_s.T, GM_GROUP_DIM, axis=1)
    wr_pad = jnp.concatenate([w_router, jnp.zeros((D_MODEL, LANE - N_EXPERTS), F32)], axis=1)
    br_pad = jnp.concatenate([b_router, jnp.full((LANE - N_EXPERTS,), NEG, F32)]).reshape(1, LANE)
    x1, xn3, idx_pad, gate_pad = _mix(x2, o_attn, u_act, v_act, gm_w_s, bias_full,
                                     out_norm_attn_g.reshape(1, D_ATTN), out_norm_gm_g.reshape(1, D_GM),
                                     w_out.astype(BF16), norm2_g.reshape(1, D_MODEL), wr_pad, br_pad)

    s_tot = n * TOP_K
    nb = s_tot // BM_MOE + N_EXPERTS - 1 + MOE_AHEAD
    e_flat = idx_pad[:, :TOP_K].reshape(s_tot)
    onehot = (e_flat[:, None] == jnp.arange(N_EXPERTS, dtype=jnp.int32)[None, :]).astype(jnp.int32)
    csum = jnp.cumsum(onehot, axis=0)
    rank = jnp.sum(csum * onehot, axis=1) - 1
    counts = csum[-1]
    padded = ((counts + BM_MOE - 1) // BM_MOE) * BM_MOE
    pad_end = jnp.cumsum(padded)
    pad_start = pad_end - padded
    dest = pad_start[e_flat] + rank
    pad_lo = jnp.concatenate([pad_start + counts, pad_end[-1:]]).astype(jnp.int32)
    pad_hi = jnp.concatenate([pad_end, jnp.full((1,), nb * BM_MOE)]).astype(jnp.int32)
    tok_buf = _slot_table(pad_lo, pad_hi, dest.astype(jnp.int32), nb * BM_MOE)
    blk_start = jnp.arange(nb, dtype=jnp.int32) * BM_MOE
    blk_expert = jnp.minimum(jnp.sum((blk_start[:, None] >= pad_end[None, :]).astype(jnp.int32), axis=1),
                             N_EXPERTS - 1)
    blk_valid = (blk_start < pad_end[-1]).astype(jnp.int32)
    e_ids = jnp.arange(N_EXPERTS, dtype=jnp.int32)
    present = counts > 0
    ordinal = jnp.cumsum(present.astype(jnp.int32)) - 1
    later = jnp.where(present[None, :] & (e_ids[None, :] > e_ids[:, None]), e_ids[None, :], N_EXPERTS)
    nxt = jnp.min(later, axis=1)
    blk_next = jnp.where(nxt < N_EXPERTS, nxt, -1)[blk_expert].astype(jnp.int32)
    blk_wslot = (ordinal[blk_expert] % 2).astype(jnp.int32)

    bg = b_gate_up[:, 0::2].reshape(N_EXPERTS, 1, D_EXPERT)
    bl = b_gate_up[:, 1::2].reshape(N_EXPERTS, 1, D_EXPERT)
    y_rows = _moe(blk_expert, blk_valid, blk_next, blk_wslot, tok_buf.reshape(nb, 1, BM_MOE), xn3, w_gate_up, bg, bl,
                  w_down,
                  b_down.reshape(N_EXPERTS, 1, D_MODEL))

    dest_blocks = (dest.reshape(n // TM_CMB, TM_CMB, TOP_K).transpose(0, 2, 1)
                   .reshape(n // TM_CMB, 1, TOP_K * TM_CMB).astype(jnp.int32))
    out = _combine(dest_blocks, x1, gate_pad, y_rows)
    return out.reshape(batch, seq, D_MODEL)
```

```python
import functools

import jax
import jax.numpy as jnp
import numpy as np
from jax import lax
from jax.experimental import pallas as pl
from jax.experimental.pallas import tpu as pltpu

F32 = jnp.float32
BF16 = jnp.bfloat16
HIGHEST = lax.Precision.HIGHEST
_NT = (((1,), (1,)), ((), ()))

D_MODEL = 1024
N_HEADS = 8
HEAD_DIM = 64
N_KV = 2
N_REP = N_HEADS // N_KV
D_ATTN = N_HEADS * HEAD_DIM
D_KV = N_KV * HEAD_DIM
N_GM_GROUPS = 8
GM_GROUP_DIM = 64
D_GM = N_GM_GROUPS * GM_GROUP_DIM
N_GATE = 3 * N_HEADS
CMP_LEN = 32
CMP_STRIDE = 16
CMP_HIDDEN = 128
SEL_BLOCK = 64
N_SEL = 16
WINDOW = 512
Q_BLOCK = 128
FORCE_BONUS = 1.0e4
GM_CHUNK = 128
N_EXPERTS = 32
TOP_K = 4
D_EXPERT = 1024
SWIGLU_LIMIT = 7.0
SWIGLU_ALPHA = 1.702
EPS = 1e-6
NEG = -1.0e30
LOG2E = 1.4426950408889634

LANE = 128
ROW_SUB = D_MODEL // LANE
VMEM_LIMIT = 48 * 1024 * 1024
VMEM_LIMIT_MOE = 56 * 1024 * 1024

_C_Q = 0
_C_KC = _C_Q + D_ATTN
_C_VC = _C_KC + D_KV
_C_KS = _C_VC + D_KV
_C_VS = _C_KS + D_KV
_C_KW = _C_VS + D_KV
_C_VW = _C_KW + D_KV
_C_U = _C_VW + D_KV
_C_V = _C_U + D_GM
_C_G = _C_V + D_GM
D_IN_PAD = _C_G + LANE

TM_IN = 256
TM_MIX = 256
KC_SEL = 512
BM_MOE = 256
MOE_AHEAD = 2
TM_CMB = 128


def _rms(x, eps=EPS):
    return lax.rsqrt(jnp.mean(x * x, axis=-1, keepdims=True) + eps)


def _inproj_kernel(x_ref, g1_ref, w_ref, qg_ref, kg_ref, vg_ref, fs_ref, fw_ref,
                   q_ref, kc_ref, vc_ref, ks_ref, vs_ref, kw_ref, vw_ref, gate_ref, u_ref, v_ref):
    x = x_ref[...]
    h = (x * _rms(x)) * g1_ref[...]
    z = jnp.dot(h.astype(BF16), w_ref[...], preferred_element_type=F32)

    def head_norm(col0, n, gain, scale):
        outs = []
        for i in range(n):
            sl = z[:, col0 + i * HEAD_DIM: col0 + (i + 1) * HEAD_DIM]
            outs.append((sl * _rms(sl)) * gain * scale)
        return jnp.concatenate(outs, axis=-1)

    q_ref[...] = head_norm(_C_Q, N_HEADS, qg_ref[...], HEAD_DIM ** -0.5 * LOG2E)
    kc_ref[...] = z[:, _C_KC:_C_KC + D_KV]
    vc_ref[...] = z[:, _C_VC:_C_VC + D_KV]
    for col, gain, feat_ref, k_ref in ((_C_KS, kg_ref[1:2, :], fs_ref, ks_ref), (_C_KW, kg_ref[2:3, :], fw_ref, kw_ref)):
        kn = head_norm(col, N_KV, gain, 1.0).astype(BF16)
        for g in range(N_KV):
            k_ref[0, g] = jnp.concatenate([kn[:, g * HEAD_DIM:(g + 1) * HEAD_DIM], feat_ref[...]], axis=1)
    for col, vt_ref in ((_C_VS, vs_ref), (_C_VW, vw_ref)):
        vt = z[:, col:col + D_KV].T
        for g in range(N_KV):
            vt_ref[0, g] = vt[g * HEAD_DIM:(g + 1) * HEAD_DIM, :].astype(BF16)
    gate_ref[...] = jax.nn.sigmoid(z[:, _C_G:_C_G + LANE])
    u_ref[...] = jax.nn.gelu(z[:, _C_U:_C_U + D_GM])
    gv = jax.nn.gelu(z[:, _C_V:_C_V + D_GM])
    v_ref[...] = (gv * _rms(gv)) * vg_ref[...]


def _inproj(x2, norm1_g, w_r, q_norm_g, k_norm_g, gm_v_norm_g, feat_s, feat_w, batch, seq):
    n = x2.shape[0]
    tps = seq // TM_IN
    row = lambda c: pl.BlockSpec((TM_IN, c), lambda i: (i, 0))
    full = lambda a: pl.BlockSpec(a.shape, lambda i: (0,) * a.ndim)
    per_seq = lambda a: pl.BlockSpec((TM_IN, a.shape[1]), lambda i: (i % tps, 0))
    keys = lambda w: pl.BlockSpec((1, N_KV, TM_IN, w), lambda i: (i // tps, 0, i % tps, 0))
    vals_t = pl.BlockSpec((1, N_KV, HEAD_DIM, TM_IN), lambda i: (i // tps, 0, 0, i % tps))
    g1 = norm1_g.reshape(1, D_MODEL)
    qg = q_norm_g.reshape(1, HEAD_DIM)
    vg = gm_v_norm_g.reshape(1, D_GM)
    ws, ww = HEAD_DIM + feat_s.shape[1], HEAD_DIM + feat_w.shape[1]
    tok = lambda c: jax.ShapeDtypeStruct((n, c), F32)
    return pl.pallas_call(
        _inproj_kernel,
        grid=(n // TM_IN,),
        in_specs=[row(D_MODEL), full(g1), full(w_r), full(qg), full(k_norm_g), full(vg), per_seq(feat_s),
                  per_seq(feat_w)],
        out_specs=[row(D_ATTN), row(D_KV), row(D_KV), keys(ws), vals_t, keys(ww), vals_t, row(LANE), row(D_GM),
                   row(D_GM)],
        out_shape=[tok(D_ATTN), tok(D_KV), tok(D_KV),
                   jax.ShapeDtypeStruct((batch, N_KV, seq, ws), BF16),
                   jax.ShapeDtypeStruct((batch, N_KV, HEAD_DIM, seq), BF16),
                   jax.ShapeDtypeStruct((batch, N_KV, seq, ww), BF16),
                   jax.ShapeDtypeStruct((batch, N_KV, HEAD_DIM, seq), BF16),
                   tok(LANE), tok(D_GM), tok(D_GM)],
        compiler_params=pltpu.CompilerParams(dimension_semantics=("arbitrary",), vmem_limit_bytes=VMEM_LIMIT),
        name="inproj",
    )(x2, g1, w_r, qg, k_norm_g, vg, feat_s, feat_w)


def _compress_kernel(a_ref, pos_ref, w1_ref, w1a_ref, w1b_ref, b1_ref, w2_ref, b2_ref, kg_ref, o_ref, *, norm):
    a = a_ref[0]
    nseg = a.shape[0]
    c = jnp.dot(pos_ref[...], w1_ref[...], precision=HIGHEST, preferred_element_type=F32)[0:1] + b1_ref[...]
    row = lax.broadcasted_iota(jnp.int32, (nseg, 1), 0)
    for g in range(N_KV):
        pa = jnp.dot(a, w1a_ref[g], precision=HIGHEST, preferred_element_type=F32)
        pb = jnp.dot(a, w1b_ref[g], precision=HIGHEST, preferred_element_type=F32)
        hid = jax.nn.gelu(pa + pltpu.roll(pb, nseg - 1, 0) + c)
        out = jnp.dot(hid, w2_ref[...], precision=HIGHEST, preferred_element_type=F32) + b2_ref[...]
        if norm:
            out = (out * _rms(out)) * kg_ref[...]
        o_ref[0, g] = jnp.where(row < nseg - 1, out, 0.0)


def _compress(raw, pos, w1, b1, w2, b2, gain, batch, seq, norm):
    nseg = seq // CMP_STRIDE
    half = CMP_STRIDE * HEAD_DIM
    a = raw.reshape(batch, nseg, CMP_STRIDE * D_KV)
    pos8 = jnp.broadcast_to(pos.reshape(1, CMP_LEN * HEAD_DIM), (8, CMP_LEN * HEAD_DIM))

    def expand(wh):
        wh = wh.reshape(CMP_STRIDE, HEAD_DIM, CMP_HIDDEN)
        z = jnp.zeros((N_KV, CMP_STRIDE, N_KV, HEAD_DIM, CMP_HIDDEN), F32)
        for g in range(N_KV):
            z = z.at[g, :, g].set(wh)
        return z.reshape(N_KV, CMP_STRIDE * D_KV, CMP_HIDDEN)

    w1a, w1b = expand(w1[:half]), expand(w1[half:])
    b1r, b2r, gr = b1.reshape(1, CMP_HIDDEN), b2.reshape(1, HEAD_DIM), gain.reshape(1, HEAD_DIM)
    full = lambda t: pl.BlockSpec(t.shape, lambda i: (0,) * t.ndim)
    return pl.pallas_call(
        functools.partial(_compress_kernel, norm=norm),
        grid=(batch,),
        in_specs=[pl.BlockSpec((1, nseg, CMP_STRIDE * D_KV), lambda i: (i, 0, 0)),
                  full(pos8), full(w1), full(w1a), full(w1b), full(b1r), full(w2), full(b2r), full(gr)],
        out_specs=pl.BlockSpec((1, N_KV, nseg, HEAD_DIM), lambda i: (i, 0, 0, 0)),
        out_shape=jax.ShapeDtypeStruct((batch, N_KV, nseg, HEAD_DIM), F32),
        compiler_params=pltpu.CompilerParams(dimension_semantics=("arbitrary",), vmem_limit_bytes=VMEM_LIMIT),
        name="compress_k" if norm else "compress_v",
    )(a, pos8, w1, w1a, w1b, b1r, w2, b2r, gr)


def _attn_kernel(qt_ref, kc_ref, vct_ref, ks_ref, vst_ref, kw_ref, vwt_ref, g_ref, ovt_ref, o_ref,
                 qs_ref, s0_ref, s1_ref, p0_ref, p1_ref, st_ref, acc_ref):
    qb = pl.program_id(1)
    nq = N_REP * Q_BLOCK
    q0 = qb * Q_BLOCK
    qt = qt_ref[0, 0]
    ql = lax.broadcasted_iota(jnp.int32, (1, nq), 1) % Q_BLOCK
    t_row = (q0 + ql).astype(F32)
    m_init = 0.5 * NEG

    def online(s, m, l):
        m_new = jnp.maximum(m, jnp.max(s, axis=0, keepdims=True))
        alpha = jnp.exp2(m - m_new)
        p = jnp.exp2(s - m_new)
        return p, m_new, alpha, alpha * l + jnp.sum(p, axis=0, keepdims=True)

    def inv(l):
        return jnp.where(l > 0.0, 1.0 / l, 0.0)

    m0 = jnp.full((1, nq), m_init, F32)
    l0 = jnp.zeros((1, nq), F32)
    a0 = jnp.zeros((HEAD_DIM, nq), F32)

    ncmp = kc_ref.shape[1]
    s = jnp.dot(kc_ref[0], qt, preferred_element_type=F32)
    c_end = (lax.broadcasted_iota(jnp.int32, (ncmp, 1), 0) * CMP_STRIDE + (CMP_LEN - 1)).astype(F32)
    p, _, _, l = online(jnp.where(c_end <= t_row, s, NEG), m0, l0)
    p = p * inv(l)
    o_cmp = jnp.dot(vct_ref[0], p.astype(BF16), preferred_element_type=F32)

    psum = p[:, 0:Q_BLOCK]
    for r in range(1, N_REP):
        psum = psum + p[:, r * Q_BLOCK:(r + 1) * Q_BLOCK]
    nsel = ovt_ref.shape[0]
    p_hi = psum.astype(BF16)
    p_lo = (psum - p_hi.astype(F32)).astype(BF16)
    imp = (jnp.dot(ovt_ref[...], p_hi, preferred_element_type=F32)
           + jnp.dot(ovt_ref[...], p_lo, preferred_element_type=F32))
    n_col = lax.broadcasted_iota(jnp.int32, (nsel, 1), 0).astype(F32)
    n_start = n_col * SEL_BLOCK
    tq = t_row[:, 0:Q_BLOCK]
    cur = jnp.floor(tq * (1.0 / SEL_BLOCK)) * SEL_BLOCK
    forced = (n_start == cur) | (n_start == 0.0)
    valid = n_start <= tq
    imp = jnp.where(forced, imp + FORCE_BONUS, imp)
    imp = jnp.where(valid, imp, NEG)
    sel = jnp.zeros((nsel, Q_BLOCK), F32)
    for _ in range(min(N_SEL, nsel)):
        mx = jnp.max(imp, axis=0, keepdims=True)
        first = jnp.min(jnp.where(imp == mx, n_col, float(nsel)), axis=0, keepdims=True)
        hit = n_col == first
        sel = jnp.where(hit, 1.0, sel)
        imp = jnp.where(hit, -jnp.inf, imp)
    selb = jnp.where(valid & (sel > 0.0), 0.0, NEG).astype(BF16)
    qs_ref[0:LANE, :] = qt
    qs_ref[LANE:LANE + nsel, :] = jnp.concatenate([selb] * N_REP, axis=1)
    if qs_ref.shape[0] > LANE + nsel:
        qs_ref[LANE + nsel:, :] = jnp.zeros((qs_ref.shape[0] - LANE - nsel, nq), BF16)

    def attend(k_blk, vt_blk, q_op, bias, carry):
        m, l, acc = carry
        s = jnp.dot(k_blk, q_op, preferred_element_type=F32)
        if bias is not None:
            s = s + bias
        p, m, alpha, l = online(s, m, l)
        pv = jnp.dot(vt_blk, p.astype(BF16), preferred_element_type=F32)
        return m, l, alpha * acc + pv

    seq = ks_ref.shape[1]

    def scores(j):
        k0 = pl.multiple_of(jnp.minimum(j * KC_SEL, seq - KC_SEL), KC_SEL)
        s = jnp.dot(ks_ref[0, pl.ds(k0, KC_SEL), :], qs_ref[...], preferred_element_type=F32)
        return s, jnp.max(s, axis=0, keepdims=True)

    def values(j, p):
        k0 = pl.multiple_of(jnp.maximum(j, 0) * KC_SEL, KC_SEL)
        return jnp.dot(vst_ref[0, :, pl.ds(k0, KC_SEL)], p, preferred_element_type=F32)

    def stage(j, s_cur, s_nxt, p_cur, p_prv):
        m, l, alpha_prev, mx = st_ref[0:1, :], st_ref[1:2, :], st_ref[2:3, :], st_ref[3:4, :]
        m_new = jnp.maximum(m, mx)
        alpha = jnp.exp2(m - m_new)
        k0 = pl.multiple_of(jnp.minimum((j + 1) * KC_SEL, seq - KC_SEL), KC_SEL)
        k0p = pl.multiple_of(jnp.maximum(j - 1, 0) * KC_SEL, KC_SEL)
        sub = KC_SEL // 4
        psum, mx_next, zeros = None, None, []
        for q in range(4):
            rows = slice(q * sub, (q + 1) * sub)
            k_q = ks_ref[0, pl.ds(k0 + q * sub, sub), :]
            if q >= 1:
                k_q = k_q + jnp.concatenate([zeros[q - 1]] * (ks_ref.shape[2] // LANE), axis=1)
            s_q = jnp.dot(k_q, qs_ref[...], preferred_element_type=F32)
            s_nxt[rows, :] = s_q
            mx_q = jnp.max(s_q, axis=0, keepdims=True)
            mx_next = mx_q if mx_next is None else jnp.maximum(mx_next, mx_q)
            p_q = jnp.exp2(s_cur[rows, :] - m_new)
            ps_q = jnp.sum(p_q, axis=0, keepdims=True)
            psum = ps_q if psum is None else psum + ps_q
            p_q = p_q.astype(BF16)
            p_cur[rows, :] = p_q
            dep = ps_q[:, 0:LANE]
            for r in range(1, N_REP):
                dep = dep + ps_q[:, r * LANE:(r + 1) * LANE]
            bits = pltpu.bitcast(dep, jnp.int32)
            zeros.append(lax.shift_right_logical(lax.shift_right_logical(bits, 16), 16).astype(F32).astype(BF16))
            if q == 1:
                vt_prev = vst_ref[0, :, pl.ds(k0p, KC_SEL)] + jnp.concatenate([zeros[1]] * (KC_SEL // LANE), axis=1)
                acc_ref[...] = alpha_prev * acc_ref[...] + jnp.dot(vt_prev, p_prv[...], preferred_element_type=F32)
        st_ref[0:1, :] = m_new
        st_ref[1:2, :] = alpha * l + psum
        st_ref[2:3, :] = alpha
        st_ref[3:4, :] = mx_next

    n_full = q0 // KC_SEL
    s_first, mx_first = scores(0)

    @pl.when(n_full % 2 == 0)
    def _():
        s0_ref[...] = s_first

    @pl.when(n_full % 2 == 1)
    def _():
        s1_ref[...] = s_first

    p0_ref[...] = jnp.zeros(p0_ref.shape, BF16)
    p1_ref[...] = jnp.zeros(p1_ref.shape, BF16)
    st_ref[0:1, :] = m0
    st_ref[1:2, :] = l0
    st_ref[2:3, :] = jnp.ones((1, nq), F32)
    st_ref[3:4, :] = mx_first
    acc_ref[...] = a0

    def sel_body(j, carry):
        @pl.when((n_full - j) % 2 == 0)
        def _():
            stage(j, s0_ref, s1_ref, p0_ref, p1_ref)

        @pl.when((n_full - j) % 2 == 1)
        def _():
            stage(j, s1_ref, s0_ref, p1_ref, p0_ref)
        return carry

    lax.fori_loop(0, n_full, sel_body, 0)
    pos_last = (n_full * KC_SEL + lax.broadcasted_iota(jnp.int32, (KC_SEL, 1), 0)).astype(F32)
    p, _, alpha, l_sel = online(s0_ref[...] + jnp.where(pos_last <= t_row, 0.0, NEG), st_ref[0:1, :], st_ref[1:2, :])
    acc = st_ref[2:3, :] * acc_ref[...] + values(n_full - 1, p1_ref[...])
    o_sel = alpha * acc + values(n_full, p.astype(BF16))

    n_wk = WINDOW + Q_BLOCK
    w0 = pl.multiple_of(jnp.maximum(q0 - WINDOW, 0), Q_BLOCK)
    kk = lax.broadcasted_iota(jnp.int32, (n_wk, 1), 0) - (q0 - w0)
    in_win = (kk <= ql) & (kk + WINDOW > ql)
    _, l_win, o_win = attend(kw_ref[0, pl.ds(w0, n_wk), :], vwt_ref[0, :, pl.ds(w0, n_wk)], qt,
                             jnp.where(in_win, 0.0, NEG), (m0, l0, a0))

    gt = g_ref[0, 0]
    o_ref[0, 0] = (gt[0:1] * o_cmp + gt[1:2] * (o_sel * inv(l_sel)) + gt[2:3] * (o_win * inv(l_win)))


def _attention(qt, kc, vct, ks, vst, kw, vwt, gt, ovt):
    bgn, nqb = qt.shape[0], qt.shape[1]
    seq = ks.shape[1]
    nq = N_REP * Q_BLOCK
    per_bg = lambda a: pl.BlockSpec((1,) + a.shape[1:], lambda b, i: (b,) + (0,) * (a.ndim - 1))
    return pl.pallas_call(
        _attn_kernel,
        grid=(bgn, nqb),
        in_specs=[pl.BlockSpec((1, 1, LANE, nq), lambda b, i: (b, i, 0, 0)),
                  per_bg(kc), per_bg(vct), per_bg(ks), per_bg(vst), per_bg(kw), per_bg(vwt),
                  pl.BlockSpec((1, 1, 3, nq), lambda b, i: (b, i, 0, 0)),
                  pl.BlockSpec(ovt.shape, lambda b, i: (0, 0))],
        out_specs=pl.BlockSpec((1, 1, HEAD_DIM, nq), lambda b, i: (b, i, 0, 0)),
        out_shape=jax.ShapeDtypeStruct((bgn, nqb, HEAD_DIM, nq), F32),
        scratch_shapes=[pltpu.VMEM((ks.shape[2], nq), BF16),
                        pltpu.VMEM((KC_SEL, nq), F32), pltpu.VMEM((KC_SEL, nq), F32),
                        pltpu.VMEM((KC_SEL, nq), BF16), pltpu.VMEM((KC_SEL, nq), BF16),
                        pltpu.VMEM((8, nq), F32), pltpu.VMEM((HEAD_DIM, nq), F32)],
        compiler_params=pltpu.CompilerParams(dimension_semantics=("arbitrary", "arbitrary"),
                                             vmem_limit_bytes=VMEM_LIMIT),
        name="nsa_attention",
    )(qt, kc, vct, ks, vst, kw, vwt, gt, ovt)


def _mix_kernel(x_ref, oa_ref, u_ref, v_ref, ws_ref, bs_ref, ga_ref, gg_ref, wo_ref, g2_ref, wr_ref, br_ref,
                x1_ref, xn_ref, idx_ref, gate_ref, rank_ref, cnt_out_ref, cnt_ref):
    tm = x_ref.shape[0]
    rr = lax.broadcasted_iota(jnp.int32, (GM_CHUNK, GM_CHUNK), 0)
    cc = lax.broadcasted_iota(jnp.int32, (GM_CHUNK, GM_CHUNK), 1)
    grp = lax.broadcasted_iota(jnp.int32, (1, D_GM), 1) // GM_GROUP_DIM
    ws = [jnp.where(rr >= cc, ws_ref[g], 0.0).astype(BF16) for g in range(N_GM_GROUPS)]
    ys = []
    for c in range(tm // GM_CHUNK):
        vch = v_ref[c * GM_CHUNK:(c + 1) * GM_CHUNK, :].astype(BF16)
        y = bs_ref[...]
        for g in range(N_GM_GROUPS):
            y = y + jnp.where(grp == g, jnp.dot(ws[g], vch, preferred_element_type=F32), 0.0)
        ys.append(y)
    o_gm = u_ref[...] * jnp.concatenate(ys, axis=0)
    o_at = oa_ref[...]
    mixed = jnp.concatenate([(o_at * _rms(o_at)) * ga_ref[...], (o_gm * _rms(o_gm)) * gg_ref[...]], axis=-1)
    x1 = x_ref[...] + jnp.dot(mixed.astype(BF16), wo_ref[...], preferred_element_type=F32)
    x1_ref[...] = x1
    xn = (x1 * _rms(x1)) * g2_ref[...]
    for s in range(ROW_SUB):
        xn_ref[:, s, :] = xn[:, s * LANE:(s + 1) * LANE]
    logits = jnp.dot(xn, wr_ref[...], precision=HIGHEST, preferred_element_type=F32) + br_ref[...]
    lane = lax.broadcasted_iota(jnp.int32, (1, LANE), 1).astype(F32)
    idx_out = jnp.zeros((tm, LANE), F32)
    val_out = jnp.zeros((tm, LANE), F32)
    vals, firsts = [], []
    for k in range(TOP_K):
        mx = jnp.max(logits, axis=-1, keepdims=True)
        first = jnp.min(jnp.where(logits == mx, lane, float(LANE)), axis=-1, keepdims=True)
        logits = jnp.where(lane == first, -jnp.inf, logits)
        idx_out = jnp.where(lane == float(k), first, idx_out)
        vals.append(mx)
        firsts.append(first)
    es = [jnp.exp(v - vals[0]) for v in vals]
    den = es[0] + es[1] + es[2] + es[3]
    for k in range(TOP_K):
        val_out = jnp.where(lane == float(k), es[k] / den, val_out)
    idx_ref[...] = idx_out.astype(jnp.int32)
    gate_ref[...] = val_out

    @pl.when(pl.program_id(0) == 0)
    def _():
        cnt_ref[...] = jnp.zeros(cnt_ref.shape, F32)

    hit = (lane == firsts[0]) | (lane == firsts[1]) | (lane == firsts[2]) | (lane == firsts[3])
    hit_b = jnp.where(hit, 1.0, 0.0).astype(BF16)
    tr = lax.broadcasted_iota(jnp.int32, (tm, tm), 0)
    tc = lax.broadcasted_iota(jnp.int32, (tm, tm), 1)
    before = jnp.where(tr > tc, 1.0, 0.0).astype(BF16)
    ranks = jnp.dot(before, hit_b, preferred_element_type=F32) + cnt_ref[0:1, :]
    rank_out = jnp.zeros((tm, LANE), F32)
    for k in range(TOP_K):
        r_k = jnp.sum(jnp.where(lane == firsts[k], ranks, 0.0), axis=-1, keepdims=True)
        rank_out = jnp.where(lane == float(k), r_k, rank_out)
    rank_ref[...] = rank_out.astype(jnp.int32)
    cnt_ref[0:1, :] = cnt_ref[0:1, :] + jnp.sum(hit_b.astype(F32), axis=0, keepdims=True)
    cnt_out_ref[...] = cnt_ref[...].astype(jnp.int32)


def _mix(x2, o_attn, u_act, v_act, gm_w_s, bias_full, ga, gg, w_out_b, g2, wr_pad, br_pad):
    n = x2.shape[0]
    row = lambda c: pl.BlockSpec((TM_MIX, c), lambda i: (i, 0))
    full = lambda a: pl.BlockSpec(a.shape, lambda i: (0,) * a.ndim)
    return pl.pallas_call(
        _mix_kernel,
        grid=(n // TM_MIX,),
        in_specs=[row(D_MODEL), row(D_ATTN), row(D_GM), row(D_GM), full(gm_w_s), full(bias_full), full(ga), full(gg),
                  full(w_out_b), full(g2), full(wr_pad), full(br_pad)],
        out_specs=[row(D_MODEL), pl.BlockSpec((TM_MIX, ROW_SUB, LANE), lambda i: (i, 0, 0)), row(LANE), row(LANE),
                   row(LANE), pl.BlockSpec((ROW_SUB, LANE), lambda i: (0, 0))],
        out_shape=[jax.ShapeDtypeStruct((n, D_MODEL), F32), jax.ShapeDtypeStruct((n, ROW_SUB, LANE), F32),
                   jax.ShapeDtypeStruct((n, LANE), jnp.int32), jax.ShapeDtypeStruct((n, LANE), F32),
                   jax.ShapeDtypeStruct((n, LANE), jnp.int32), jax.ShapeDtypeStruct((ROW_SUB, LANE), jnp.int32)],
        scratch_shapes=[pltpu.VMEM((ROW_SUB, LANE), F32)],
        compiler_params=pltpu.CompilerParams(dimension_semantics=("arbitrary",), vmem_limit_bytes=VMEM_LIMIT),
        name="mix_outproj_router",
    )(x2, o_attn, u_act, v_act, gm_w_s, bias_full, ga, gg, w_out_b, g2, wr_pad, br_pad)


def _row_gather(idx_ref, n_rows, src_hbm, dst_ref, sem):
    def start():
        for r in range(n_rows):
            pltpu.make_async_copy(src_hbm.at[pl.ds(idx_ref[0, 0, r], 1), :], dst_ref.at[pl.ds(r, 1), :], sem).start()

    def wait():
        pltpu.make_async_copy(src_hbm.at[pl.ds(0, n_rows), :], dst_ref, sem).wait()

    return start, wait


def _tile_row_gather(idx_ref, n_rows, src_hbm, dst_ref, sem):
    def start():
        for r in range(n_rows):
            t = idx_ref[0, 0, r]
            pltpu.make_async_copy(src_hbm.at[lax.shift_right_logical(t, 3), t & (ROW_SUB - 1)],
                                  dst_ref.at[r // ROW_SUB, :, r % ROW_SUB, :], sem).start(priority=r % 2)

    def wait():
        pltpu.make_async_copy(src_hbm.at[pl.ds(0, n_rows // ROW_SUB)], dst_ref, sem).wait()

    return start, wait


def _tiles_to_matrix(ref):
    rows = ref.shape[0] * ROW_SUB
    return jnp.concatenate([ref[:, c].reshape(rows, LANE) for c in range(ROW_SUB)], axis=1)


def _moe_kernel(be_ref, bv_ref, bn_ref, bs_ref, tok_ref, tok_n1_ref, tok_n2_ref, x_hbm, wgu_hbm, bg_ref, bl_ref,
                wd_hbm, bd_ref, o_ref, xbuf, sems, wt_s, wg_s, wl_s, wd_s, wgu_buf, wd_buf, wsems):
    i = pl.program_id(0)
    slot = i % (MOE_AHEAD + 1)
    slot_n2 = (i + MOE_AHEAD) % (MOE_AHEAD + 1)
    start_cur, wait_cur = _tile_row_gather(tok_ref, BM_MOE, x_hbm, xbuf.at[slot], sems.at[slot])
    start_n1, _ = _tile_row_gather(tok_n1_ref, BM_MOE, x_hbm, xbuf.at[1], sems.at[1])
    start_n2, _ = _tile_row_gather(tok_n2_ref, BM_MOE, x_hbm, xbuf.at[slot_n2], sems.at[slot_n2])

    prev = jnp.maximum(i - 1, 0)

    def fetch_weights(e, ws):
        n_piece = 4
        rows = D_MODEL // n_piece
        copies = [pltpu.make_async_copy(wgu_hbm.at[e, pl.ds(c * rows, rows)], wgu_buf.at[ws, pl.ds(c * rows, rows)],
                                        wsems.at[ws]) for c in range(n_piece)]
        copies += [pltpu.make_async_copy(wd_hbm.at[e, pl.ds(c * rows, rows)], wd_buf.at[ws, pl.ds(c * rows, rows)],
                                         wsems.at[ws]) for c in range(n_piece)]
        return copies

    @pl.when(i == 0)
    def _():
        start_cur()
        start_n1()
        for cp in fetch_weights(be_ref[0], 0):
            cp.start()

    @pl.when((bv_ref[i] == 1) & ((i == 0) | (be_ref[i] != be_ref[prev])))
    def _():
        ws = bs_ref[i]
        for cp in fetch_weights(be_ref[i], ws):
            cp.wait()

        @pl.when(bn_ref[i] >= 0)
        def _():
            for cp in fetch_weights(bn_ref[i], 1 - ws):
                cp.start()

        tc = wt_s.shape[1]
        for c in range(2 * D_EXPERT // tc):
            wt = wgu_buf[ws, :, c * tc:(c + 1) * tc].T
            for j in range(ROW_SUB):
                wt_s[j] = wt[:, j * LANE:(j + 1) * LANE]
            for first, dst in ((0, wg_s), (1, wl_s)):
                half = jnp.concatenate([wt_s[j, pl.ds(first, tc // 2, stride=2), :] for j in range(ROW_SUB)], axis=1)
                dst[c * tc // 2:(c + 1) * tc // 2, :] = half.astype(BF16)
        wd_s[...] = wd_buf[ws].astype(BF16)

    @pl.when(bv_ref[i] == 1)
    def _():
        wait_cur()
        start_n2()
        xb = _tiles_to_matrix(xbuf.at[slot]).astype(BF16)
        hg = lax.dot_general(xb, wg_s[...], _NT, preferred_element_type=F32) + bg_ref[0]
        hl = lax.dot_general(xb, wl_s[...], _NT, preferred_element_type=F32) + bl_ref[0]
        hg = jnp.minimum(hg, SWIGLU_LIMIT)
        hl = jnp.clip(hl, -SWIGLU_LIMIT, SWIGLU_LIMIT)
        a = hg * jax.nn.sigmoid(SWIGLU_ALPHA * hg) * (hl + 1.0)
        o_ref[...] = jnp.dot(a.astype(BF16), wd_s[...], preferred_element_type=F32) + bd_ref[0]

    @pl.when((bv_ref[i] == 0) & ((i == 1) | ((i >= MOE_AHEAD) & (bv_ref[jnp.maximum(i - MOE_AHEAD, 0)] == 1))))
    def _():
        wait_cur()

    @pl.when(bv_ref[i] == 0)
    def _():
        o_ref[...] = jnp.zeros(o_ref.shape, F32)


def _moe(blk_expert, blk_valid, blk_next, blk_wslot, tok_blocks, xn3, w_gate_up, bg, bl, w_down, bd):
    nb = blk_expert.shape[0]
    per_e = lambda a: pl.BlockSpec((1,) + a.shape[1:], lambda i, be, *_: (be[i],) + (0,) * (a.ndim - 1))

    def tok_spec(ahead):
        return pl.BlockSpec((1, 1, BM_MOE), lambda i, *_: (jnp.minimum(i + ahead, nb - 1), 0, 0),
                            memory_space=pltpu.SMEM)

    grid_spec = pltpu.PrefetchScalarGridSpec(
        num_scalar_prefetch=4,
        grid=(nb,),
        in_specs=[tok_spec(0), tok_spec(1), tok_spec(MOE_AHEAD),
                  pl.BlockSpec(memory_space=pl.ANY),
                  pl.BlockSpec(memory_space=pl.ANY), per_e(bg), per_e(bl),
                  pl.BlockSpec(memory_space=pl.ANY), per_e(bd)],
        out_specs=pl.BlockSpec((BM_MOE, D_MODEL), lambda i, *_: (i, 0)),
        scratch_shapes=[pltpu.VMEM((MOE_AHEAD + 1, BM_MOE // ROW_SUB, ROW_SUB, ROW_SUB, LANE), F32),
                        pltpu.SemaphoreType.DMA((MOE_AHEAD + 1,)),
                        pltpu.VMEM((ROW_SUB, 256, LANE), F32), pltpu.VMEM((D_EXPERT, D_MODEL), BF16),
                        pltpu.VMEM((D_EXPERT, D_MODEL), BF16), pltpu.VMEM((D_EXPERT, D_MODEL), BF16),
                        pltpu.VMEM((2, D_MODEL, 2 * D_EXPERT), F32), pltpu.VMEM((2, D_EXPERT, D_MODEL), F32),
                        pltpu.SemaphoreType.DMA((2,))],
    )
    return pl.pallas_call(
        _moe_kernel,
        grid_spec=grid_spec,
        out_shape=jax.ShapeDtypeStruct((nb * BM_MOE, D_MODEL), F32),
        compiler_params=pltpu.CompilerParams(dimension_semantics=("arbitrary",), vmem_limit_bytes=VMEM_LIMIT_MOE),
        name="moe_experts",
    )(blk_expert, blk_valid, blk_next, blk_wslot, tok_blocks, tok_blocks, tok_blocks,
      xn3.reshape(xn3.shape[0] // ROW_SUB, ROW_SUB, ROW_SUB, LANE), w_gate_up, bg, bl, w_down, bd)


def _slot_table_kernel(lo_ref, hi_ref, dest_ref, tok_ref):
    i = pl.program_id(0)
    n_chunk = dest_ref.shape[2]

    @pl.when(i == 0)
    def _():
        def clear(s, carry):
            tok_ref[s] = 0
            return carry
        for e in range(lo_ref.shape[0]):
            lax.fori_loop(lo_ref[e], hi_ref[e], clear, 0)

    def put(s, carry):
        tok_ref[dest_ref[0, 0, s]] = lax.shift_right_logical(i * n_chunk + s, TOP_K.bit_length() - 1)
        return carry
    lax.fori_loop(0, n_chunk, put, 0, unroll=8)


def _slot_table(pad_lo, pad_hi, dest, n_slots):
    n_chunk = 8192
    s_tot = dest.shape[0]
    grid_spec = pltpu.PrefetchScalarGridSpec(
        num_scalar_prefetch=2,
        grid=(s_tot // n_chunk,),
        in_specs=[pl.BlockSpec((1, 1, n_chunk), lambda i, *_: (i, 0, 0), memory_space=pltpu.SMEM)],
        out_specs=pl.BlockSpec(memory_space=pltpu.SMEM),
    )
    return pl.pallas_call(
        _slot_table_kernel,
        grid_spec=grid_spec,
        out_shape=jax.ShapeDtypeStruct((n_slots,), jnp.int32),
        compiler_params=pltpu.CompilerParams(dimension_semantics=("arbitrary",)),
        name="moe_slot_table",
    )(pad_lo, pad_hi, dest.reshape(s_tot // n_chunk, 1, n_chunk))


def _combine_kernel(dest_ref, dest_next_ref, x1_ref, gate_ref, y_hbm, o_ref, buf, sems):
    i = pl.program_id(0)
    slot = i % 2
    n_rows = TOP_K * TM_CMB
    start_cur, wait_cur = _row_gather(dest_ref, n_rows, y_hbm, buf.at[slot], sems.at[slot])
    start_next, _ = _row_gather(dest_next_ref, n_rows, y_hbm, buf.at[1 - slot], sems.at[1 - slot])

    @pl.when(i == 0)
    def _():
        start_cur()

    @pl.when(i + 1 < pl.num_programs(0))
    def _():
        start_next()

    wait_cur()
    gate = gate_ref[...]
    acc = x1_ref[...]
    for k in range(TOP_K):
        acc = acc + gate[:, k:k + 1] * buf[slot, k * TM_CMB:(k + 1) * TM_CMB, :]
    o_ref[...] = acc


def _combine(dest_blocks, x1, gate_pad, y_rows):
    n = x1.shape[0]
    nt = n // TM_CMB
    n_rows = TOP_K * TM_CMB
    return pl.pallas_call(
        _combine_kernel,
        grid=(nt,),
        in_specs=[pl.BlockSpec((1, 1, n_rows), lambda i: (i, 0, 0), memory_space=pltpu.SMEM),
                  pl.BlockSpec((1, 1, n_rows), lambda i: (jnp.minimum(i + 1, nt - 1), 0, 0),
                               memory_space=pltpu.SMEM),
                  pl.BlockSpec((TM_CMB, D_MODEL), lambda i: (i, 0)),
                  pl.BlockSpec((TM_CMB, LANE), lambda i: (i, 0)),
                  pl.BlockSpec(memory_space=pl.ANY)],
        out_specs=pl.BlockSpec((TM_CMB, D_MODEL), lambda i: (i, 0)),
        out_shape=jax.ShapeDtypeStruct((n, D_MODEL), F32),
        scratch_shapes=[pltpu.VMEM((2, n_rows, D_MODEL), F32), pltpu.SemaphoreType.DMA((2,))],
        compiler_params=pltpu.CompilerParams(dimension_semantics=("arbitrary",), vmem_limit_bytes=VMEM_LIMIT),
        name="moe_combine",
    )(dest_blocks, dest_blocks, x1, gate_pad, y_rows)


def kernel(x, norm1_g, w_in, q_norm_g, k_norm_g, cmp_pos, w_cmp1, b_cmp1, w_cmp2, b_cmp2, gm_v_norm_g, gm_w_s,
           gm_b_s, out_norm_attn_g, out_norm_gm_g, w_out, norm2_g, w_router, b_router, w_gate_up, b_gate_up,
           w_down, b_down):
    batch, seq, _ = x.shape
    n = batch * seq
    nqb = seq // Q_BLOCK
    bgn = batch * N_KV
    x2 = x.reshape(n, D_MODEL)

    c_gate = D_ATTN + 6 * D_KV
    w_r = jnp.concatenate([w_in[:, :c_gate], w_in[:, c_gate + N_GATE:], w_in[:, c_gate:c_gate + N_GATE],
                           jnp.zeros((D_MODEL, LANE - N_GATE), F32)], axis=1).astype(BF16)
    nsel = seq // SEL_BLOCK
    oh_w = -(-nsel // LANE) * LANE

    def pos_features(pos, one_hot=False):
        feat = np.zeros((pos.shape[0], LANE - HEAD_DIM + (oh_w if one_hot else 0)), np.float32)
        feat[:, 0] = feat[:, 1] = pos // SEL_BLOCK
        feat[:, 2] = feat[:, 3] = pos % SEL_BLOCK
        if one_hot:
            feat[np.arange(pos.shape[0]), LANE - HEAD_DIM + pos // SEL_BLOCK] = 1.0
        return jnp.asarray(feat, BF16)

    pos_t = np.arange(seq)
    q, kc_raw, vc_raw, ks_aug, vst, kw_aug, vwt, gates, u_act, v_act = _inproj(
        x2, norm1_g, w_r, q_norm_g, k_norm_g, gm_v_norm_g, pos_features(pos_t, one_hot=True), pos_features(pos_t),
        batch, seq)

    kc = _compress(kc_raw, cmp_pos[0], w_cmp1[0], b_cmp1[0], w_cmp2[0], b_cmp2[0], k_norm_g[0], batch, seq, True)
    vc = _compress(vc_raw, cmp_pos[1], w_cmp1[1], b_cmp1[1], w_cmp2[1], b_cmp2[1], k_norm_g[0], batch, seq, False)

    nq = N_REP * Q_BLOCK
    ncmp = seq // CMP_STRIDE


    head = np.arange(N_KV)[:, None] * N_REP + np.arange(nq)[None, :] // Q_BLOCK
    coef = np.exp2(-(head + 1.0)) * LOG2E
    c_hi = coef.astype(BF16).astype(np.float64)
    c_lo = (coef - c_hi).astype(BF16).astype(np.float64)
    qrows = np.zeros((N_KV, LANE - HEAD_DIM, nq), np.float32)
    qrows[:, 0], qrows[:, 1], qrows[:, 2], qrows[:, 3] = SEL_BLOCK * c_hi, SEL_BLOCK * c_lo, c_hi, c_lo
    qrows = jnp.broadcast_to(jnp.asarray(qrows, BF16)[None, :, None], (batch, N_KV, nqb, LANE - HEAD_DIM, nq))
    qt = (q.reshape(batch, nqb, Q_BLOCK, N_KV, N_REP, HEAD_DIM).transpose(0, 3, 1, 5, 4, 2)
          .reshape(batch, N_KV, nqb, HEAD_DIM, nq).astype(BF16))
    qt = jnp.concatenate([qt, qrows], axis=3).reshape(bgn, nqb, LANE, nq)
    gt = (gates[:, :N_GATE].reshape(batch, nqb, Q_BLOCK, N_KV, N_REP, 3).transpose(0, 3, 1, 5, 4, 2)
          .reshape(bgn, nqb, 3, nq))
    pos_c = np.arange(ncmp) * CMP_STRIDE + (CMP_LEN - 1)
    kc_b = jnp.concatenate([kc.reshape(bgn, ncmp, HEAD_DIM).astype(BF16),
                            jnp.broadcast_to(pos_features(pos_c)[None], (bgn, ncmp, LANE - HEAD_DIM))], axis=-1)
    vct = vc.reshape(bgn, ncmp, HEAD_DIM).transpose(0, 2, 1).astype(BF16)
    c0 = np.arange(ncmp)[None, :] * CMP_STRIDE
    n0 = np.arange(seq // SEL_BLOCK)[:, None] * SEL_BLOCK
    ovt = np.clip(np.minimum(c0 + CMP_LEN, n0 + SEL_BLOCK) - np.maximum(c0, n0), 0, None) / CMP_LEN
    per_group = lambda a: a.reshape((bgn,) + a.shape[2:])
    ot = _attention(qt, kc_b, vct, per_group(ks_aug), per_group(vst), per_group(kw_aug), per_group(vwt), gt,
                    jnp.asarray(ovt, BF16))
    o_attn = (ot.reshape(batch, N_KV, nqb, HEAD_DIM, N_REP, Q_BLOCK).transpose(0, 2, 5, 1, 4, 3)
              .reshape(n, D_ATTN))

    bias_full = jnp.repeat(gm_b_s.T, GM_GROUP_DIM, axis=1)
    wr_pad = jnp.concatenate([w_router, jnp.zeros((D_MODEL, LANE - N_EXPERTS), F32)], axis=1)
    br_pad = jnp.concatenate([b_router, jnp.full((LANE - N_EXPERTS,), NEG, F32)]).reshape(1, LANE)
    x1, xn3, idx_pad, gate_pad, rank_pad, cnt_pad = _mix(
        x2, o_attn, u_act, v_act, gm_w_s, bias_full, out_norm_attn_g.reshape(1, D_ATTN),
        out_norm_gm_g.reshape(1, D_GM), w_out.astype(BF16), norm2_g.reshape(1, D_MODEL), wr_pad, br_pad)

    s_tot = n * TOP_K
    nb = s_tot // BM_MOE + N_EXPERTS - 1 + MOE_AHEAD
    e_flat = idx_pad[:, :TOP_K].reshape(s_tot)
    rank = rank_pad[:, :TOP_K].reshape(s_tot)
    counts = cnt_pad[0, :N_EXPERTS]
    padded = ((counts + BM_MOE - 1) // BM_MOE) * BM_MOE
    pad_end = jnp.cumsum(padded)
    pad_start = pad_end - padded
    dest = pad_start[e_flat] + rank
    pad_lo = jnp.concatenate([pad_start + counts, pad_end[-1:]]).astype(jnp.int32)
    pad_hi = jnp.concatenate([pad_end, jnp.full((1,), nb * BM_MOE)]).astype(jnp.int32)
    tok_buf = _slot_table(pad_lo, pad_hi, dest.astype(jnp.int32), nb * BM_MOE)
    blk_start = jnp.arange(nb, dtype=jnp.int32) * BM_MOE
    blk_expert = jnp.minimum(jnp.sum((blk_start[:, None] >= pad_end[None, :]).astype(jnp.int32), axis=1),
                             N_EXPERTS - 1)
    blk_valid = (blk_start < pad_end[-1]).astype(jnp.int32)
    e_ids = jnp.arange(N_EXPERTS, dtype=jnp.int32)
    present = counts > 0
    ordinal = jnp.cumsum(present.astype(jnp.int32)) - 1
    later = jnp.where(present[None, :] & (e_ids[None, :] > e_ids[:, None]), e_ids[None, :], N_EXPERTS)
    nxt = jnp.min(later, axis=1)
    blk_next = jnp.where(nxt < N_EXPERTS, nxt, -1)[blk_expert].astype(jnp.int32)
    blk_wslot = (ordinal[blk_expert] % 2).astype(jnp.int32)

    bg = b_gate_up[:, 0::2].reshape(N_EXPERTS, 1, D_EXPERT)
    bl = b_gate_up[:, 1::2].reshape(N_EXPERTS, 1, D_EXPERT)
    y_rows = _moe(blk_expert, blk_valid, blk_next, blk_wslot, tok_buf.reshape(nb, 1, BM_MOE), xn3, w_gate_up, bg, bl,
                  w_down,
                  b_down.reshape(N_EXPERTS, 1, D_MODEL))

    dest_blocks = (dest.reshape(n // TM_CMB, TM_CMB, TOP_K).transpose(0, 2, 1)
                   .reshape(n // TM_CMB, 1, TOP_K * TM_CMB).astype(jnp.int32))
    out = _combine(dest_blocks, x1, gate_pad, y_rows)
    return out.reshape(batch, seq, D_MODEL)
```

```python
import functools

import jax
import jax.numpy as jnp
import numpy as np
from jax import lax
from jax.experimental import pallas as pl
from jax.experimental.pallas import tpu as pltpu

F32 = jnp.float32
BF16 = jnp.bfloat16
HIGHEST = lax.Precision.HIGHEST
_NT = (((1,), (1,)), ((), ()))

D_MODEL = 1024
N_HEADS = 8
HEAD_DIM = 64
N_KV = 2
N_REP = N_HEADS // N_KV
D_ATTN = N_HEADS * HEAD_DIM
D_KV = N_KV * HEAD_DIM
N_GM_GROUPS = 8
GM_GROUP_DIM = 64
D_GM = N_GM_GROUPS * GM_GROUP_DIM
N_GATE = 3 * N_HEADS
CMP_LEN = 32
CMP_STRIDE = 16
CMP_HIDDEN = 128
SEL_BLOCK = 64
N_SEL = 16
WINDOW = 512
Q_BLOCK = 128
FORCE_BONUS = 1.0e4
GM_CHUNK = 128
N_EXPERTS = 32
TOP_K = 4
D_EXPERT = 1024
SWIGLU_LIMIT = 7.0
SWIGLU_ALPHA = 1.702
EPS = 1e-6
NEG = -1.0e30
LOG2E = 1.4426950408889634

LANE = 128
ROW_SUB = D_MODEL // LANE
GATE_ROWS = 16
VMEM_LIMIT = 48 * 1024 * 1024
VMEM_LIMIT_MOE = 56 * 1024 * 1024

_C_Q = 0
_C_KC = _C_Q + D_ATTN
_C_VC = _C_KC + D_KV
_C_KS = _C_VC + D_KV
_C_VS = _C_KS + D_KV
_C_KW = _C_VS + D_KV
_C_VW = _C_KW + D_KV
_C_U = _C_VW + D_KV
_C_V = _C_U + D_GM
_C_G = _C_V + D_GM
D_IN_PAD = _C_G + LANE

TM_IN = 256
TM_MIX = 256
KC_SEL = 512
BM_MOE = 256
MOE_AHEAD = 2
TM_CMB = 128


def _rms(x, eps=EPS):
    return lax.rsqrt(jnp.mean(x * x, axis=-1, keepdims=True) + eps)


def _inproj_kernel(x_ref, g1_ref, w_ref, qg_ref, kg_ref, vg_ref, fs_ref, fw_ref, qc_ref,
                   q_ref, kc_ref, vc_ref, ks_ref, vs_ref, kw_ref, vw_ref, gate_ref, u_ref, v_ref):
    x = x_ref[...]
    h = (x * _rms(x)) * g1_ref[...]
    z = jnp.dot(h.astype(BF16), w_ref[...], preferred_element_type=F32)

    def head_norm(col0, n, gain, scale):
        outs = []
        for i in range(n):
            sl = z[:, col0 + i * HEAD_DIM: col0 + (i + 1) * HEAD_DIM]
            outs.append((sl * _rms(sl)) * gain * scale)
        return jnp.concatenate(outs, axis=-1)

    qn_t = head_norm(_C_Q, N_HEADS, qg_ref[...], HEAD_DIM ** -0.5 * LOG2E).T.astype(BF16)
    gate_t = jax.nn.sigmoid(z[:, _C_G:_C_G + LANE]).T
    for g in range(N_KV):
        for b in range(x.shape[0] // Q_BLOCK):
            cols = slice(b * Q_BLOCK, (b + 1) * Q_BLOCK)
            q_ref[0, g, b, 0:HEAD_DIM, :] = jnp.concatenate(
                [qn_t[(g * N_REP + r) * HEAD_DIM:(g * N_REP + r + 1) * HEAD_DIM, cols] for r in range(N_REP)], axis=1)
            q_ref[0, g, b, HEAD_DIM:, :] = qc_ref[g]
            gate_ref[0, g, b] = gate_t[g * GATE_ROWS:(g + 1) * GATE_ROWS, cols]
    kc_ref[...] = z[:, _C_KC:_C_KC + D_KV]
    vc_ref[...] = z[:, _C_VC:_C_VC + D_KV]
    for col, gain, feat_ref, k_ref in ((_C_KS, kg_ref[1:2, :], fs_ref, ks_ref), (_C_KW, kg_ref[2:3, :], fw_ref, kw_ref)):
        kn = head_norm(col, N_KV, gain, 1.0).astype(BF16)
        for g in range(N_KV):
            k_ref[0, g] = jnp.concatenate([kn[:, g * HEAD_DIM:(g + 1) * HEAD_DIM], feat_ref[...]], axis=1)
    for col, vt_ref in ((_C_VS, vs_ref), (_C_VW, vw_ref)):
        vt = z[:, col:col + D_KV].T
        for g in range(N_KV):
            vt_ref[0, g] = vt[g * HEAD_DIM:(g + 1) * HEAD_DIM, :].astype(BF16)
    u_ref[...] = jax.nn.gelu(z[:, _C_U:_C_U + D_GM])
    gv = jax.nn.gelu(z[:, _C_V:_C_V + D_GM])
    v_ref[...] = (gv * _rms(gv)) * vg_ref[...]


def _inproj(x2, norm1_g, w_r, q_norm_g, k_norm_g, gm_v_norm_g, feat_s, feat_w, q_coef, batch, seq):
    n = x2.shape[0]
    tps = seq // TM_IN
    qpt = TM_IN // Q_BLOCK
    nq = N_REP * Q_BLOCK
    q_blocks = lambda r, c: pl.BlockSpec((1, N_KV, qpt, r, c), lambda i: (i // tps, 0, i % tps, 0, 0))
    row = lambda c: pl.BlockSpec((TM_IN, c), lambda i: (i, 0))
    full = lambda a: pl.BlockSpec(a.shape, lambda i: (0,) * a.ndim)
    per_seq = lambda a: pl.BlockSpec((TM_IN, a.shape[1]), lambda i: (i % tps, 0))
    keys = lambda w: pl.BlockSpec((1, N_KV, TM_IN, w), lambda i: (i // tps, 0, i % tps, 0))
    vals_t = pl.BlockSpec((1, N_KV, HEAD_DIM, TM_IN), lambda i: (i // tps, 0, 0, i % tps))
    g1 = norm1_g.reshape(1, D_MODEL)
    qg = q_norm_g.reshape(1, HEAD_DIM)
    vg = gm_v_norm_g.reshape(1, D_GM)
    ws, ww = HEAD_DIM + feat_s.shape[1], HEAD_DIM + feat_w.shape[1]
    tok = lambda c: jax.ShapeDtypeStruct((n, c), F32)
    return pl.pallas_call(
        _inproj_kernel,
        grid=(n // TM_IN,),
        in_specs=[row(D_MODEL), full(g1), full(w_r), full(qg), full(k_norm_g), full(vg), per_seq(feat_s),
                  per_seq(feat_w), full(q_coef)],
        out_specs=[q_blocks(LANE, nq), row(D_KV), row(D_KV), keys(ws), vals_t, keys(ww), vals_t,
                   q_blocks(GATE_ROWS, Q_BLOCK), row(D_GM), row(D_GM)],
        out_shape=[jax.ShapeDtypeStruct((batch, N_KV, seq // Q_BLOCK, LANE, nq), BF16), tok(D_KV), tok(D_KV),
                   jax.ShapeDtypeStruct((batch, N_KV, seq, ws), BF16),
                   jax.ShapeDtypeStruct((batch, N_KV, HEAD_DIM, seq), BF16),
                   jax.ShapeDtypeStruct((batch, N_KV, seq, ww), BF16),
                   jax.ShapeDtypeStruct((batch, N_KV, HEAD_DIM, seq), BF16),
                   jax.ShapeDtypeStruct((batch, N_KV, seq // Q_BLOCK, GATE_ROWS, Q_BLOCK), F32),
                   tok(D_GM), tok(D_GM)],
        compiler_params=pltpu.CompilerParams(dimension_semantics=("arbitrary",), vmem_limit_bytes=VMEM_LIMIT),
        name="inproj",
    )(x2, g1, w_r, qg, k_norm_g, vg, feat_s, feat_w, q_coef)


def _compress_kernel(a_ref, pos_ref, w1_ref, w1a_ref, w1b_ref, b1_ref, w2_ref, b2_ref, kg_ref, o_ref, *, norm):
    a = a_ref[0]
    nseg = a.shape[0]
    c = jnp.dot(pos_ref[...], w1_ref[...], precision=HIGHEST, preferred_element_type=F32)[0:1] + b1_ref[...]
    row = lax.broadcasted_iota(jnp.int32, (nseg, 1), 0)
    for g in range(N_KV):
        pa = jnp.dot(a, w1a_ref[g], precision=HIGHEST, preferred_element_type=F32)
        pb = jnp.dot(a, w1b_ref[g], precision=HIGHEST, preferred_element_type=F32)
        hid = jax.nn.gelu(pa + pltpu.roll(pb, nseg - 1, 0) + c)
        out = jnp.dot(hid, w2_ref[...], precision=HIGHEST, preferred_element_type=F32) + b2_ref[...]
        if norm:
            out = (out * _rms(out)) * kg_ref[...]
        o_ref[0, g] = jnp.where(row < nseg - 1, out, 0.0)


def _compress(raw, pos, w1, b1, w2, b2, gain, batch, seq, norm):
    nseg = seq // CMP_STRIDE
    half = CMP_STRIDE * HEAD_DIM
    a = raw.reshape(batch, nseg, CMP_STRIDE * D_KV)
    pos8 = jnp.broadcast_to(pos.reshape(1, CMP_LEN * HEAD_DIM), (8, CMP_LEN * HEAD_DIM))

    def expand(wh):
        wh = wh.reshape(CMP_STRIDE, HEAD_DIM, CMP_HIDDEN)
        z = jnp.zeros((N_KV, CMP_STRIDE, N_KV, HEAD_DIM, CMP_HIDDEN), F32)
        for g in range(N_KV):
            z = z.at[g, :, g].set(wh)
        return z.reshape(N_KV, CMP_STRIDE * D_KV, CMP_HIDDEN)

    w1a, w1b = expand(w1[:half]), expand(w1[half:])
    b1r, b2r, gr = b1.reshape(1, CMP_HIDDEN), b2.reshape(1, HEAD_DIM), gain.reshape(1, HEAD_DIM)
    full = lambda t: pl.BlockSpec(t.shape, lambda i: (0,) * t.ndim)
    return pl.pallas_call(
        functools.partial(_compress_kernel, norm=norm),
        grid=(batch,),
        in_specs=[pl.BlockSpec((1, nseg, CMP_STRIDE * D_KV), lambda i: (i, 0, 0)),
                  full(pos8), full(w1), full(w1a), full(w1b), full(b1r), full(w2), full(b2r), full(gr)],
        out_specs=pl.BlockSpec((1, N_KV, nseg, HEAD_DIM), lambda i: (i, 0, 0, 0)),
        out_shape=jax.ShapeDtypeStruct((batch, N_KV, nseg, HEAD_DIM), F32),
        compiler_params=pltpu.CompilerParams(dimension_semantics=("arbitrary",), vmem_limit_bytes=VMEM_LIMIT),
        name="compress_k" if norm else "compress_v",
    )(a, pos8, w1, w1a, w1b, b1r, w2, b2r, gr)


def _attn_kernel(qt_ref, kc_ref, vct_ref, ks_ref, vst_ref, kw_ref, vwt_ref, g_ref, ovt_ref, o_ref,
                 qs_ref, s0_ref, s1_ref, p0_ref, p1_ref, st_ref, acc_ref):
    qb = pl.program_id(1)
    nq = N_REP * Q_BLOCK
    q0 = qb * Q_BLOCK
    qt = qt_ref[0, 0]
    ql = lax.broadcasted_iota(jnp.int32, (1, nq), 1) % Q_BLOCK
    t_row = (q0 + ql).astype(F32)
    m_init = 0.5 * NEG

    def online(s, m, l):
        m_new = jnp.maximum(m, jnp.max(s, axis=0, keepdims=True))
        alpha = jnp.exp2(m - m_new)
        p = jnp.exp2(s - m_new)
        return p, m_new, alpha, alpha * l + jnp.sum(p, axis=0, keepdims=True)

    def inv(l):
        return jnp.where(l > 0.0, 1.0 / l, 0.0)

    m0 = jnp.full((1, nq), m_init, F32)
    l0 = jnp.zeros((1, nq), F32)
    a0 = jnp.zeros((HEAD_DIM, nq), F32)

    ncmp = kc_ref.shape[1]
    s = jnp.dot(kc_ref[0], qt, preferred_element_type=F32)
    c_end = (lax.broadcasted_iota(jnp.int32, (ncmp, 1), 0) * CMP_STRIDE + (CMP_LEN - 1)).astype(F32)
    p, _, _, l = online(jnp.where(c_end <= t_row, s, NEG), m0, l0)
    p = p * inv(l)
    o_cmp = jnp.dot(vct_ref[0], p.astype(BF16), preferred_element_type=F32)

    psum = p[:, 0:Q_BLOCK]
    for r in range(1, N_REP):
        psum = psum + p[:, r * Q_BLOCK:(r + 1) * Q_BLOCK]
    nsel = ovt_ref.shape[0]
    p_hi = psum.astype(BF16)
    p_lo = (psum - p_hi.astype(F32)).astype(BF16)
    imp = (jnp.dot(ovt_ref[...], p_hi, preferred_element_type=F32)
           + jnp.dot(ovt_ref[...], p_lo, preferred_element_type=F32))
    n_col = lax.broadcasted_iota(jnp.int32, (nsel, 1), 0).astype(F32)
    n_start = n_col * SEL_BLOCK
    tq = t_row[:, 0:Q_BLOCK]
    cur = jnp.floor(tq * (1.0 / SEL_BLOCK)) * SEL_BLOCK
    forced = (n_start == cur) | (n_start == 0.0)
    valid = n_start <= tq
    imp = jnp.where(forced, imp + FORCE_BONUS, imp)
    imp = jnp.where(valid, imp, NEG)
    sel = jnp.zeros((nsel, Q_BLOCK), F32)
    for _ in range(min(N_SEL, nsel)):
        mx = jnp.max(imp, axis=0, keepdims=True)
        first = jnp.min(jnp.where(imp == mx, n_col, float(nsel)), axis=0, keepdims=True)
        hit = n_col == first
        sel = jnp.where(hit, 1.0, sel)
        imp = jnp.where(hit, -jnp.inf, imp)
    selb = jnp.where(valid & (sel > 0.0), 0.0, NEG).astype(BF16)
    qs_ref[0:LANE, :] = qt
    qs_ref[LANE:LANE + nsel, :] = jnp.concatenate([selb] * N_REP, axis=1)
    if qs_ref.shape[0] > LANE + nsel:
        qs_ref[LANE + nsel:, :] = jnp.zeros((qs_ref.shape[0] - LANE - nsel, nq), BF16)

    def attend(k_blk, vt_blk, q_op, bias, carry):
        m, l, acc = carry
        s = jnp.dot(k_blk, q_op, preferred_element_type=F32)
        if bias is not None:
            s = s + bias
        p, m, alpha, l = online(s, m, l)
        pv = jnp.dot(vt_blk, p.astype(BF16), preferred_element_type=F32)
        return m, l, alpha * acc + pv

    seq = ks_ref.shape[1]

    def scores(j):
        k0 = pl.multiple_of(jnp.minimum(j * KC_SEL, seq - KC_SEL), KC_SEL)
        s = jnp.dot(ks_ref[0, pl.ds(k0, KC_SEL), :], qs_ref[...], preferred_element_type=F32)
        return s, jnp.max(s, axis=0, keepdims=True)

    def values(j, p):
        k0 = pl.multiple_of(jnp.maximum(j, 0) * KC_SEL, KC_SEL)
        return jnp.dot(vst_ref[0, :, pl.ds(k0, KC_SEL)], p, preferred_element_type=F32)

    def stage(j, s_cur, s_nxt, p_cur, p_prv):
        m, l, alpha_prev, mx = st_ref[0:1, :], st_ref[1:2, :], st_ref[2:3, :], st_ref[3:4, :]
        m_new = jnp.maximum(m, mx)
        alpha = jnp.exp2(m - m_new)
        k0 = pl.multiple_of(jnp.minimum((j + 1) * KC_SEL, seq - KC_SEL), KC_SEL)
        k0p = pl.multiple_of(jnp.maximum(j - 1, 0) * KC_SEL, KC_SEL)
        sub = KC_SEL // 4
        psum, mx_next, zeros = None, None, []
        for q in range(4):
            rows = slice(q * sub, (q + 1) * sub)
            k_q = ks_ref[0, pl.ds(k0 + q * sub, sub), :]
            if q >= 1:
                k_q = k_q + jnp.concatenate([zeros[q - 1]] * (ks_ref.shape[2] // LANE), axis=1)
            s_q = jnp.dot(k_q, qs_ref[...], preferred_element_type=F32)
            s_nxt[rows, :] = s_q
            mx_q = jnp.max(s_q, axis=0, keepdims=True)
            mx_next = mx_q if mx_next is None else jnp.maximum(mx_next, mx_q)
            p_q = jnp.exp2(s_cur[rows, :] - m_new)
            ps_q = jnp.sum(p_q, axis=0, keepdims=True)
            psum = ps_q if psum is None else psum + ps_q
            p_q = p_q.astype(BF16)
            p_cur[rows, :] = p_q
            dep = ps_q[:, 0:LANE]
            for r in range(1, N_REP):
                dep = dep + ps_q[:, r * LANE:(r + 1) * LANE]
            bits = pltpu.bitcast(dep, jnp.int32)
            zeros.append(lax.shift_right_logical(lax.shift_right_logical(bits, 16), 16).astype(F32).astype(BF16))
            if q == 1:
                vt_prev = vst_ref[0, :, pl.ds(k0p, KC_SEL)] + jnp.concatenate([zeros[1]] * (KC_SEL // LANE), axis=1)
                acc_ref[...] = alpha_prev * acc_ref[...] + jnp.dot(vt_prev, p_prv[...], preferred_element_type=F32)
        st_ref[0:1, :] = m_new
        st_ref[1:2, :] = alpha * l + psum
        st_ref[2:3, :] = alpha
        st_ref[3:4, :] = mx_next

    n_full = q0 // KC_SEL
    s_first, mx_first = scores(0)

    @pl.when(n_full % 2 == 0)
    def _():
        s0_ref[...] = s_first

    @pl.when(n_full % 2 == 1)
    def _():
        s1_ref[...] = s_first

    p0_ref[...] = jnp.zeros(p0_ref.shape, BF16)
    p1_ref[...] = jnp.zeros(p1_ref.shape, BF16)
    st_ref[0:1, :] = m0
    st_ref[1:2, :] = l0
    st_ref[2:3, :] = jnp.ones((1, nq), F32)
    st_ref[3:4, :] = mx_first
    acc_ref[...] = a0

    def sel_body(j, carry):
        @pl.when((n_full - j) % 2 == 0)
        def _():
            stage(j, s0_ref, s1_ref, p0_ref, p1_ref)

        @pl.when((n_full - j) % 2 == 1)
        def _():
            stage(j, s1_ref, s0_ref, p1_ref, p0_ref)
        return carry

    lax.fori_loop(0, n_full, sel_body, 0)
    pos_last = (n_full * KC_SEL + lax.broadcasted_iota(jnp.int32, (KC_SEL, 1), 0)).astype(F32)
    p, _, alpha, l_sel = online(s0_ref[...] + jnp.where(pos_last <= t_row, 0.0, NEG), st_ref[0:1, :], st_ref[1:2, :])
    acc = st_ref[2:3, :] * acc_ref[...] + values(n_full - 1, p1_ref[...])
    o_sel = alpha * acc + values(n_full, p.astype(BF16))

    n_wk = WINDOW + Q_BLOCK
    w0 = pl.multiple_of(jnp.maximum(q0 - WINDOW, 0), Q_BLOCK)
    kk = lax.broadcasted_iota(jnp.int32, (n_wk, 1), 0) - (q0 - w0)
    in_win = (kk <= ql) & (kk + WINDOW > ql)
    _, l_win, o_win = attend(kw_ref[0, pl.ds(w0, n_wk), :], vwt_ref[0, :, pl.ds(w0, n_wk)], qt,
                             jnp.where(in_win, 0.0, NEG), (m0, l0, a0))

    gt = g_ref[0, 0]

    def gate(j):
        return jnp.concatenate([gt[j * N_REP + r:j * N_REP + r + 1, :] for r in range(N_REP)], axis=1)

    o_ref[0, 0] = (gate(0) * o_cmp + gate(1) * (o_sel * inv(l_sel)) + gate(2) * (o_win * inv(l_win)))


def _attention(qt, kc, vct, ks, vst, kw, vwt, gt, ovt):
    bgn, nqb = qt.shape[0], qt.shape[1]
    seq = ks.shape[1]
    nq = N_REP * Q_BLOCK
    per_bg = lambda a: pl.BlockSpec((1,) + a.shape[1:], lambda b, i: (b,) + (0,) * (a.ndim - 1))
    return pl.pallas_call(
        _attn_kernel,
        grid=(bgn, nqb),
        in_specs=[pl.BlockSpec((1, 1, LANE, nq), lambda b, i: (b, i, 0, 0)),
                  per_bg(kc), per_bg(vct), per_bg(ks), per_bg(vst), per_bg(kw), per_bg(vwt),
                  pl.BlockSpec((1, 1, GATE_ROWS, Q_BLOCK), lambda b, i: (b, i, 0, 0)),
                  pl.BlockSpec(ovt.shape, lambda b, i: (0, 0))],
        out_specs=pl.BlockSpec((1, 1, HEAD_DIM, nq), lambda b, i: (b, i, 0, 0)),
        out_shape=jax.ShapeDtypeStruct((bgn, nqb, HEAD_DIM, nq), F32),
        scratch_shapes=[pltpu.VMEM((ks.shape[2], nq), BF16),
                        pltpu.VMEM((KC_SEL, nq), F32), pltpu.VMEM((KC_SEL, nq), F32),
                        pltpu.VMEM((KC_SEL, nq), BF16), pltpu.VMEM((KC_SEL, nq), BF16),
                        pltpu.VMEM((8, nq), F32), pltpu.VMEM((HEAD_DIM, nq), F32)],
        compiler_params=pltpu.CompilerParams(dimension_semantics=("arbitrary", "arbitrary"),
                                             vmem_limit_bytes=VMEM_LIMIT),
        name="nsa_attention",
    )(qt, kc, vct, ks, vst, kw, vwt, gt, ovt)


def _mix_kernel(x_ref, oa_ref, u_ref, v_ref, ws_ref, bs_ref, ga_ref, gg_ref, wo_ref, g2_ref, wr_ref, br_ref,
                x1_ref, xn_ref, idx_ref, gate_ref, rank_ref, cnt_out_ref, cnt_ref):
    tm = x_ref.shape[0]
    rr = lax.broadcasted_iota(jnp.int32, (GM_CHUNK, GM_CHUNK), 0)
    cc = lax.broadcasted_iota(jnp.int32, (GM_CHUNK, GM_CHUNK), 1)
    grp = lax.broadcasted_iota(jnp.int32, (1, D_GM), 1) // GM_GROUP_DIM
    ws = [jnp.where(rr >= cc, ws_ref[g], 0.0).astype(BF16) for g in range(N_GM_GROUPS)]
    ys = []
    for c in range(tm // GM_CHUNK):
        vch = v_ref[c * GM_CHUNK:(c + 1) * GM_CHUNK, :].astype(BF16)
        y = bs_ref[...]
        for g in range(N_GM_GROUPS):
            y = y + jnp.where(grp == g, jnp.dot(ws[g], vch, preferred_element_type=F32), 0.0)
        ys.append(y)
    o_gm = u_ref[...] * jnp.concatenate(ys, axis=0)
    o_at = jnp.concatenate(
        [jnp.concatenate([oa_ref[0, g, b, :, r * Q_BLOCK:(r + 1) * Q_BLOCK]
                          for g in range(N_KV) for r in range(N_REP)], axis=0).T
         for b in range(tm // Q_BLOCK)], axis=0)
    mixed =jnp.concatenate([(o_at * _rms(o_at)) * ga_ref[...], (o_gm * _rms(o_gm)) * gg_ref[...]], axis=-1)
    x1 = x_ref[...] + jnp.dot(mixed.astype(BF16), wo_ref[...], preferred_element_type=F32)
    x1_ref[...] = x1
    xn = (x1 * _rms(x1)) * g2_ref[...]
    for s in range(ROW_SUB):
        xn_ref[:, s, :] = xn[:, s * LANE:(s + 1) * LANE]
    logits = jnp.dot(xn, wr_ref[...], precision=HIGHEST, preferred_element_type=F32) + br_ref[...]
    lane = lax.broadcasted_iota(jnp.int32, (1, LANE), 1).astype(F32)
    idx_out = jnp.zeros((tm, LANE), F32)
    val_out = jnp.zeros((tm, LANE), F32)
    vals, firsts = [], []
    for k in range(TOP_K):
        mx = jnp.max(logits, axis=-1, keepdims=True)
        first = jnp.min(jnp.where(logits == mx, lane, float(LANE)), axis=-1, keepdims=True)
        logits = jnp.where(lane == first, -jnp.inf, logits)
        idx_out = jnp.where(lane == float(k), first, idx_out)
        vals.append(mx)
        firsts.append(first)
    es = [jnp.exp(v - vals[0]) for v in vals]
    den = es[0] + es[1] + es[2] + es[3]
    for k in range(TOP_K):
        val_out = jnp.where(lane == float(k), es[k] / den, val_out)
    idx_ref[...] = idx_out.astype(jnp.int32)
    gate_ref[...] = val_out

    @pl.when(pl.program_id(0) == 0)
    def _():
        cnt_ref[...] = jnp.zeros(cnt_ref.shape, F32)

    hit = (lane == firsts[0]) | (lane == firsts[1]) | (lane == firsts[2]) | (lane == firsts[3])
    hit_b = jnp.where(hit, 1.0, 0.0).astype(BF16)
    tr = lax.broadcasted_iota(jnp.int32, (tm, tm), 0)
    tc = lax.broadcasted_iota(jnp.int32, (tm, tm), 1)
    before = jnp.where(tr > tc, 1.0, 0.0).astype(BF16)
    ranks = jnp.dot(before, hit_b, preferred_element_type=F32) + cnt_ref[0:1, :]
    rank_out = jnp.zeros((tm, LANE), F32)
    for k in range(TOP_K):
        r_k = jnp.sum(jnp.where(lane == firsts[k], ranks, 0.0), axis=-1, keepdims=True)
        rank_out = jnp.where(lane == float(k), r_k, rank_out)
    rank_ref[...] = rank_out.astype(jnp.int32)
    cnt_ref[0:1, :] = cnt_ref[0:1, :] + jnp.sum(hit_b.astype(F32), axis=0, keepdims=True)
    cnt_out_ref[...] = cnt_ref[...].astype(jnp.int32)


def _mix(x2, ot, u_act, v_act, gm_w_s, bias_full, ga, gg, w_out_b, g2, wr_pad, br_pad):
    n = x2.shape[0]
    tps = ot.shape[2] * Q_BLOCK // TM_MIX
    row = lambda c: pl.BlockSpec((TM_MIX, c), lambda i: (i, 0))
    full = lambda a: pl.BlockSpec(a.shape, lambda i: (0,) * a.ndim)
    attn_blocks = pl.BlockSpec((1, N_KV, TM_MIX // Q_BLOCK) + ot.shape[3:], lambda i: (i // tps, 0, i % tps, 0, 0))
    return pl.pallas_call(
        _mix_kernel,
        grid=(n // TM_MIX,),
        in_specs=[row(D_MODEL), attn_blocks, row(D_GM), row(D_GM), full(gm_w_s), full(bias_full), full(ga), full(gg),
                  full(w_out_b), full(g2), full(wr_pad), full(br_pad)],
        out_specs=[row(D_MODEL), pl.BlockSpec((TM_MIX, ROW_SUB, LANE), lambda i: (i, 0, 0)), row(LANE), row(LANE),
                   row(LANE), pl.BlockSpec((ROW_SUB, LANE), lambda i: (0, 0))],
        out_shape=[jax.ShapeDtypeStruct((n, D_MODEL), F32), jax.ShapeDtypeStruct((n, ROW_SUB, LANE), F32),
                   jax.ShapeDtypeStruct((n, LANE), jnp.int32), jax.ShapeDtypeStruct((n, LANE), F32),
                   jax.ShapeDtypeStruct((n, LANE), jnp.int32), jax.ShapeDtypeStruct((ROW_SUB, LANE), jnp.int32)],
        scratch_shapes=[pltpu.VMEM((ROW_SUB, LANE), F32)],
        compiler_params=pltpu.CompilerParams(dimension_semantics=("arbitrary",), vmem_limit_bytes=VMEM_LIMIT),
        name="mix_outproj_router",
    )(x2, ot, u_act, v_act, gm_w_s, bias_full, ga, gg, w_out_b, g2, wr_pad, br_pad)


def _row_gather(idx_ref, n_rows, src_hbm, dst_ref, sem):
    def start():
        for r in range(n_rows):
            pltpu.make_async_copy(src_hbm.at[pl.ds(idx_ref[0, 0, r], 1), :], dst_ref.at[pl.ds(r, 1), :], sem).start()

    def wait():
        pltpu.make_async_copy(src_hbm.at[pl.ds(0, n_rows), :], dst_ref, sem).wait()

    return start, wait


def _tile_row_gather(idx_ref, n_rows, src_hbm, dst_ref, sem):
    def start():
        for r in range(n_rows):
            t = idx_ref[0, 0, r]
            pltpu.make_async_copy(src_hbm.at[lax.shift_right_logical(t, 3), t & (ROW_SUB - 1)],
                                  dst_ref.at[r // ROW_SUB, :, r % ROW_SUB, :], sem).start(priority=r % 2)

    def wait():
        pltpu.make_async_copy(src_hbm.at[pl.ds(0, n_rows // ROW_SUB)], dst_ref, sem).wait()

    return start, wait


def _tiles_to_matrix(ref):
    rows = ref.shape[0] * ROW_SUB
    return jnp.concatenate([ref[:, c].reshape(rows, LANE) for c in range(ROW_SUB)], axis=1)


def _moe_kernel(be_ref, bv_ref, bn_ref, bs_ref, tok_ref, tok_n1_ref, tok_n2_ref, x_hbm, wgu_hbm, bg_ref, bl_ref,
                wd_hbm, bd_ref, o_ref, xbuf, sems, wt_s, wg_s, wl_s, wd_s, wgu_buf, wd_buf, wsems):
    i = pl.program_id(0)
    slot = i % (MOE_AHEAD + 1)
    slot_n2 = (i + MOE_AHEAD) % (MOE_AHEAD + 1)
    start_cur, wait_cur = _tile_row_gather(tok_ref, BM_MOE, x_hbm, xbuf.at[slot], sems.at[slot])
    start_n1, _ = _tile_row_gather(tok_n1_ref, BM_MOE, x_hbm, xbuf.at[1], sems.at[1])
    start_n2, _ = _tile_row_gather(tok_n2_ref, BM_MOE, x_hbm, xbuf.at[slot_n2], sems.at[slot_n2])

    prev = jnp.maximum(i - 1, 0)

    def fetch_weights(e, ws):
        n_piece = 4
        rows = D_MODEL // n_piece
        copies = [pltpu.make_async_copy(wgu_hbm.at[e, pl.ds(c * rows, rows)], wgu_buf.at[ws, pl.ds(c * rows, rows)],
                                        wsems.at[ws]) for c in range(n_piece)]
        copies += [pltpu.make_async_copy(wd_hbm.at[e, pl.ds(c * rows, rows)], wd_buf.at[ws, pl.ds(c * rows, rows)],
                                         wsems.at[ws]) for c in range(n_piece)]
        return copies

    @pl.when(i == 0)
    def _():
        start_cur()
        start_n1()
        for cp in fetch_weights(be_ref[0], 0):
            cp.start()

    @pl.when((bv_ref[i] == 1) & ((i == 0) | (be_ref[i] != be_ref[prev])))
    def _():
        ws = bs_ref[i]
        for cp in fetch_weights(be_ref[i], ws):
            cp.wait()

        @pl.when(bn_ref[i] >= 0)
        def _():
            for cp in fetch_weights(bn_ref[i], 1 - ws):
                cp.start()

        tc = wt_s.shape[1]
        for c in range(2 * D_EXPERT // tc):
            wt = wgu_buf[ws, :, c * tc:(c + 1) * tc].T
            for j in range(ROW_SUB):
                wt_s[j] = wt[:, j * LANE:(j + 1) * LANE]
            for first, dst in ((0, wg_s), (1, wl_s)):
                half = jnp.concatenate([wt_s[j, pl.ds(first, tc // 2, stride=2), :] for j in range(ROW_SUB)], axis=1)
                dst[c * tc // 2:(c + 1) * tc // 2, :] = half.astype(BF16)
        wd_s[...] = wd_buf[ws].astype(BF16)

    @pl.when(bv_ref[i] == 1)
    def _():
        wait_cur()
        start_n2()
        xb = _tiles_to_matrix(xbuf.at[slot]).astype(BF16)
        hg = lax.dot_general(xb, wg_s[...], _NT, preferred_element_type=F32) + bg_ref[0]
        hl = lax.dot_general(xb, wl_s[...], _NT, preferred_element_type=F32) + bl_ref[0]
        hg = jnp.minimum(hg, SWIGLU_LIMIT)
        hl = jnp.clip(hl, -SWIGLU_LIMIT, SWIGLU_LIMIT)
        a = hg * jax.nn.sigmoid(SWIGLU_ALPHA * hg) * (hl + 1.0)
        o_ref[...] = jnp.dot(a.astype(BF16), wd_s[...], preferred_element_type=F32) + bd_ref[0]

    @pl.when((bv_ref[i] == 0) & ((i == 1) | ((i >= MOE_AHEAD) & (bv_ref[jnp.maximum(i - MOE_AHEAD, 0)] == 1))))
    def _():
        wait_cur()

    @pl.when(bv_ref[i] == 0)
    def _():
        o_ref[...] = jnp.zeros(o_ref.shape, F32)


def _moe(blk_expert, blk_valid, blk_next, blk_wslot, tok_blocks, xn3, w_gate_up, bg, bl, w_down, bd):
    nb = blk_expert.shape[0]
    per_e = lambda a: pl.BlockSpec((1,) + a.shape[1:], lambda i, be, *_: (be[i],) + (0,) * (a.ndim - 1))

    def tok_spec(ahead):
        return pl.BlockSpec((1, 1, BM_MOE), lambda i, *_: (jnp.minimum(i + ahead, nb - 1), 0, 0),
                            memory_space=pltpu.SMEM)

    grid_spec = pltpu.PrefetchScalarGridSpec(
        num_scalar_prefetch=4,
        grid=(nb,),
        in_specs=[tok_spec(0), tok_spec(1), tok_spec(MOE_AHEAD),
                  pl.BlockSpec(memory_space=pl.ANY),
                  pl.BlockSpec(memory_space=pl.ANY), per_e(bg), per_e(bl),
                  pl.BlockSpec(memory_space=pl.ANY), per_e(bd)],
        out_specs=pl.BlockSpec((BM_MOE, D_MODEL), lambda i, *_: (i, 0)),
        scratch_shapes=[pltpu.VMEM((MOE_AHEAD + 1, BM_MOE // ROW_SUB, ROW_SUB, ROW_SUB, LANE), F32),
                        pltpu.SemaphoreType.DMA((MOE_AHEAD + 1,)),
                        pltpu.VMEM((ROW_SUB, 256, LANE), F32), pltpu.VMEM((D_EXPERT, D_MODEL), BF16),
                        pltpu.VMEM((D_EXPERT, D_MODEL), BF16), pltpu.VMEM((D_EXPERT, D_MODEL), BF16),
                        pltpu.VMEM((2, D_MODEL, 2 * D_EXPERT), F32), pltpu.VMEM((2, D_EXPERT, D_MODEL), F32),
                        pltpu.SemaphoreType.DMA((2,))],
    )
    return pl.pallas_call(
        _moe_kernel,
        grid_spec=grid_spec,
        out_shape=jax.ShapeDtypeStruct((nb * BM_MOE, D_MODEL), F32),
        compiler_params=pltpu.CompilerParams(dimension_semantics=("arbitrary",), vmem_limit_bytes=VMEM_LIMIT_MOE),
        name="moe_experts",
    )(blk_expert, blk_valid, blk_next, blk_wslot, tok_blocks, tok_blocks, tok_blocks,
      xn3.reshape(xn3.shape[0] // ROW_SUB, ROW_SUB, ROW_SUB, LANE), w_gate_up, bg, bl, w_down, bd)


def _slot_table_kernel(lo_ref, hi_ref, dest_ref, tok_ref):
    i = pl.program_id(0)
    n_chunk = dest_ref.shape[2]

    @pl.when(i == 0)
    def _():
        def clear(s, carry):
            tok_ref[s] = 0
            return carry
        for e in range(lo_ref.shape[0]):
            lax.fori_loop(lo_ref[e], hi_ref[e], clear, 0)

    def put(s, carry):
        tok_ref[dest_ref[0, 0, s]] = lax.shift_right_logical(i * n_chunk + s, TOP_K.bit_length() - 1)
        return carry
    lax.fori_loop(0, n_chunk, put, 0, unroll=8)


def _slot_table(pad_lo, pad_hi, dest, n_slots):
    n_chunk = 8192
    s_tot = dest.shape[0]
    grid_spec = pltpu.PrefetchScalarGridSpec(
        num_scalar_prefetch=2,
        grid=(s_tot // n_chunk,),
        in_specs=[pl.BlockSpec((1, 1, n_chunk), lambda i, *_: (i, 0, 0), memory_space=pltpu.SMEM)],
        out_specs=pl.BlockSpec(memory_space=pltpu.SMEM),
    )
    return pl.pallas_call(
        _slot_table_kernel,
        grid_spec=grid_spec,
        out_shape=jax.ShapeDtypeStruct((n_slots,), jnp.int32),
        compiler_params=pltpu.CompilerParams(dimension_semantics=("arbitrary",)),
        name="moe_slot_table",
    )(pad_lo, pad_hi, dest.reshape(s_tot // n_chunk, 1, n_chunk))


def _combine_kernel(dest_ref, dest_next_ref, x1_ref, gate_ref, y_hbm, o_ref, buf, sems):
    i = pl.program_id(0)
    slot = i % 2
    n_rows = TOP_K * TM_CMB
    start_cur, wait_cur = _row_gather(dest_ref, n_rows, y_hbm, buf.at[slot], sems.at[slot])
    start_next, _ = _row_gather(dest_next_ref, n_rows, y_hbm, buf.at[1 - slot], sems.at[1 - slot])

    @pl.when(i == 0)
    def _():
        start_cur()

    @pl.when(i + 1 < pl.num_programs(0))
    def _():
        start_next()

    wait_cur()
    gate = gate_ref[...]
    acc = x1_ref[...]
    for k in range(TOP_K):
        acc = acc + gate[:, k:k + 1] * buf[slot, k * TM_CMB:(k + 1) * TM_CMB, :]
    o_ref[...] = acc


def _combine(dest_blocks, x1, gate_pad, y_rows):
    n = x1.shape[0]
    nt = n // TM_CMB
    n_rows = TOP_K * TM_CMB
    return pl.pallas_call(
        _combine_kernel,
        grid=(nt,),
        in_specs=[pl.BlockSpec((1, 1, n_rows), lambda i: (i, 0, 0), memory_space=pltpu.SMEM),
                  pl.BlockSpec((1, 1, n_rows), lambda i: (jnp.minimum(i + 1, nt - 1), 0, 0),
                               memory_space=pltpu.SMEM),
                  pl.BlockSpec((TM_CMB, D_MODEL), lambda i: (i, 0)),
                  pl.BlockSpec((TM_CMB, LANE), lambda i: (i, 0)),
                  pl.BlockSpec(memory_space=pl.ANY)],
        out_specs=pl.BlockSpec((TM_CMB, D_MODEL), lambda i: (i, 0)),
        out_shape=jax.ShapeDtypeStruct((n, D_MODEL), F32),
        scratch_shapes=[pltpu.VMEM((2, n_rows, D_MODEL), F32), pltpu.SemaphoreType.DMA((2,))],
        compiler_params=pltpu.CompilerParams(dimension_semantics=("arbitrary",), vmem_limit_bytes=VMEM_LIMIT),
        name="moe_combine",
    )(dest_blocks, dest_blocks, x1, gate_pad, y_rows)


def kernel(x, norm1_g, w_in, q_norm_g, k_norm_g, cmp_pos, w_cmp1, b_cmp1, w_cmp2, b_cmp2, gm_v_norm_g, gm_w_s,
           gm_b_s, out_norm_attn_g, out_norm_gm_g, w_out, norm2_g, w_router, b_router, w_gate_up, b_gate_up,
           w_down, b_down):
    batch, seq, _ = x.shape
    n = batch * seq
    nqb = seq // Q_BLOCK
    bgn = batch * N_KV
    x2 = x.reshape(n, D_MODEL)

    c_gate = D_ATTN + 6 * D_KV
    gate_src = np.full((LANE,), N_GATE, np.int32)
    for h in range(N_HEADS):
        for j in range(3):
            gate_src[(h // N_REP) * GATE_ROWS + j * N_REP + h % N_REP] = h * 3 + j
    w_gate = jnp.concatenate([w_in[:, c_gate:c_gate + N_GATE], jnp.zeros((D_MODEL, 1), F32)], axis=1)[:, gate_src]
    w_r = jnp.concatenate([w_in[:, :c_gate], w_in[:, c_gate + N_GATE:], w_gate], axis=1).astype(BF16)
    nq = N_REP * Q_BLOCK
    head = np.arange(N_KV)[:, None] * N_REP + np.arange(nq)[None, :] // Q_BLOCK
    coef = np.exp2(-(head + 1.0)) * LOG2E
    c_hi = coef.astype(BF16).astype(np.float64)
    c_lo = (coef - c_hi).astype(BF16).astype(np.float64)
    qrows = np.zeros((N_KV, LANE - HEAD_DIM, nq), np.float32)
    qrows[:, 0], qrows[:, 1], qrows[:, 2], qrows[:, 3] = SEL_BLOCK * c_hi, SEL_BLOCK * c_lo, c_hi, c_lo
    nsel = seq // SEL_BLOCK
    oh_w = -(-nsel // LANE) * LANE

    def pos_features(pos, one_hot=False):
        feat = np.zeros((pos.shape[0], LANE - HEAD_DIM + (oh_w if one_hot else 0)), np.float32)
        feat[:, 0] = feat[:, 1] = pos // SEL_BLOCK
        feat[:, 2] = feat[:, 3] = pos % SEL_BLOCK
        if one_hot:
            feat[np.arange(pos.shape[0]), LANE - HEAD_DIM + pos // SEL_BLOCK] = 1.0
        return jnp.asarray(feat, BF16)

    pos_t = np.arange(seq)
    qt, kc_raw, vc_raw, ks_aug, vst, kw_aug, vwt, gt, u_act, v_act = _inproj(
        x2, norm1_g, w_r, q_norm_g, k_norm_g, gm_v_norm_g, pos_features(pos_t, one_hot=True), pos_features(pos_t),
        jnp.asarray(qrows, BF16), batch, seq)

    kc = _compress(kc_raw, cmp_pos[0], w_cmp1[0], b_cmp1[0], w_cmp2[0], b_cmp2[0], k_norm_g[0], batch, seq, True)
    vc = _compress(vc_raw, cmp_pos[1], w_cmp1[1], b_cmp1[1], w_cmp2[1], b_cmp2[1], k_norm_g[0], batch, seq, False)

    ncmp = seq // CMP_STRIDE
    pos_c = np.arange(ncmp) * CMP_STRIDE + (CMP_LEN - 1)
    kc_b = jnp.concatenate([kc.reshape(bgn, ncmp, HEAD_DIM).astype(BF16),
                            jnp.broadcast_to(pos_features(pos_c)[None], (bgn, ncmp, LANE - HEAD_DIM))], axis=-1)
    vct = vc.reshape(bgn, ncmp, HEAD_DIM).transpose(0, 2, 1).astype(BF16)
    c0 = np.arange(ncmp)[None, :] * CMP_STRIDE
    n0 = np.arange(seq // SEL_BLOCK)[:, None] * SEL_BLOCK
    ovt = np.clip(np.minimum(c0 + CMP_LEN, n0 + SEL_BLOCK) - np.maximum(c0, n0), 0, None) / CMP_LEN
    per_group = lambda a: a.reshape((bgn,) + a.shape[2:])
    ot = _attention(per_group(qt), kc_b, vct, per_group(ks_aug), per_group(vst), per_group(kw_aug),
                    per_group(vwt), per_group(gt), jnp.asarray(ovt, BF16))
    ot = ot.reshape((batch, N_KV) + ot.shape[1:])

    bias_full = jnp.repeat(gm_b_s.T, GM_GROUP_DIM, axis=1)
    wr_pad = jnp.concatenate([w_router, jnp.zeros((D_MODEL, LANE - N_EXPERTS), F32)], axis=1)
    br_pad = jnp.concatenate([b_router, jnp.full((LANE - N_EXPERTS,), NEG, F32)]).reshape(1, LANE)
    x1, xn3, idx_pad, gate_pad, rank_pad, cnt_pad = _mix(
        x2, ot, u_act, v_act, gm_w_s, bias_full, out_norm_attn_g.reshape(1, D_ATTN),
        out_norm_gm_g.reshape(1, D_GM), w_out.astype(BF16), norm2_g.reshape(1, D_MODEL), wr_pad, br_pad)

    s_tot = n * TOP_K
    nb = s_tot // BM_MOE + N_EXPERTS - 1 + MOE_AHEAD
    e_flat = idx_pad[:, :TOP_K].reshape(s_tot)
    rank = rank_pad[:, :TOP_K].reshape(s_tot)
    counts = cnt_pad[0, :N_EXPERTS]
    padded = ((counts + BM_MOE - 1) // BM_MOE) * BM_MOE
    pad_end = jnp.cumsum(padded)
    pad_start = pad_end - padded
    dest = pad_start[e_flat] + rank
    pad_lo = jnp.concatenate([pad_start + counts, pad_end[-1:]]).astype(jnp.int32)
    pad_hi = jnp.concatenate([pad_end, jnp.full((1,), nb * BM_MOE)]).astype(jnp.int32)
    tok_buf = _slot_table(pad_lo, pad_hi, dest.astype(jnp.int32), nb * BM_MOE)
    blk_start = jnp.arange(nb, dtype=jnp.int32) * BM_MOE
    blk_expert = jnp.minimum(jnp.sum((blk_start[:, None] >= pad_end[None, :]).astype(jnp.int32), axis=1),
                             N_EXPERTS - 1)
    blk_valid = (blk_start < pad_end[-1]).astype(jnp.int32)
    e_ids = jnp.arange(N_EXPERTS, dtype=jnp.int32)
    present = counts > 0
    ordinal = jnp.cumsum(present.astype(jnp.int32)) - 1
    later = jnp.where(present[None, :] & (e_ids[None, :] > e_ids[:, None]), e_ids[None, :], N_EXPERTS)
    nxt = jnp.min(later, axis=1)
    blk_next = jnp.where(nxt < N_EXPERTS, nxt, -1)[blk_expert].astype(jnp.int32)
    blk_wslot = (ordinal[blk_expert] % 2).astype(jnp.int32)

    bg = b_gate_up[:, 0::2].reshape(N_EXPERTS, 1, D_EXPERT)
    bl = b_gate_up[:, 1::2].reshape(N_EXPERTS, 1, D_EXPERT)
    y_rows = _moe(blk_expert, blk_valid, blk_next, blk_wslot, tok_buf.reshape(nb, 1, BM_MOE), xn3, w_gate_up, bg, bl,
                  w_down,
                  b_down.reshape(N_EXPERTS, 1, D_MODEL))

    dest_blocks = (dest.reshape(n // TM_CMB, TM_CMB, TOP_K).transpose(0, 2, 1)
                   .reshape(n // TM_CMB, 1, TOP_K * TM_CMB).astype(jnp.int32))
    out = _combine(dest_blocks, x1, gate_pad, y_rows)
    return out.reshape(batch, seq, D_MODEL)
```

```python
import functools

import jax
import jax.numpy as jnp
import numpy as np
from jax import lax
from jax.experimental import pallas as pl
from jax.experimental.pallas import tpu as pltpu

F32 = jnp.float32
BF16 = jnp.bfloat16
HIGHEST = lax.Precision.HIGHEST
_NT = (((1,), (1,)), ((), ()))

D_MODEL = 1024
N_HEADS = 8
HEAD_DIM = 64
N_KV = 2
N_REP = N_HEADS // N_KV
D_ATTN = N_HEADS * HEAD_DIM
D_KV = N_KV * HEAD_DIM
N_GM_GROUPS = 8
GM_GROUP_DIM = 64
D_GM = N_GM_GROUPS * GM_GROUP_DIM
N_GATE = 3 * N_HEADS
CMP_LEN = 32
CMP_STRIDE = 16
CMP_HIDDEN = 128
SEL_BLOCK = 64
N_SEL = 16
WINDOW = 512
Q_BLOCK = 128
FORCE_BONUS = 1.0e4
GM_CHUNK = 128
N_EXPERTS = 32
TOP_K = 4
D_EXPERT = 1024
SWIGLU_LIMIT = 7.0
SWIGLU_ALPHA = 1.702
EPS = 1e-6
NEG = -1.0e30
LOG2E = 1.4426950408889634

LANE = 128
ROW_SUB = D_MODEL // LANE
GATE_ROWS = 16
VMEM_LIMIT = 48 * 1024 * 1024
VMEM_LIMIT_MOE = 56 * 1024 * 1024

_C_Q = 0
_C_KC = _C_Q + D_ATTN
_C_VC = _C_KC + D_KV
_C_KS = _C_VC + D_KV
_C_VS = _C_KS + D_KV
_C_KW = _C_VS + D_KV
_C_VW = _C_KW + D_KV
_C_U = _C_VW + D_KV
_C_V = _C_U + D_GM
_C_G = _C_V + D_GM
D_IN_PAD = _C_G + LANE

TM_IN = 256
TM_MIX = 256
KC_SEL = 512
BM_MOE = 512
MOE_AHEAD = 2
TM_CMB = 128


def _rms(x, eps=EPS):
    return lax.rsqrt(jnp.mean(x * x, axis=-1, keepdims=True) + eps)


def _inproj_kernel(x_ref, g1_ref, w_ref, qg_ref, kg_ref, vg_ref, fs_ref, fw_ref, qc_ref,
                   q_ref, kc_ref, vc_ref, ks_ref, vs_ref, kw_ref, vw_ref, gate_ref, u_ref, v_ref):
    x = x_ref[...]
    h = (x * _rms(x)) * g1_ref[...]
    z = jnp.dot(h.astype(BF16), w_ref[...], preferred_element_type=F32)

    def head_norm(col0, n, gain, scale):
        outs = []
        for i in range(n):
            sl = z[:, col0 + i * HEAD_DIM: col0 + (i + 1) * HEAD_DIM]
            outs.append((sl * _rms(sl)) * gain * scale)
        return jnp.concatenate(outs, axis=-1)

    qn_t = head_norm(_C_Q, N_HEADS, qg_ref[...], HEAD_DIM ** -0.5 * LOG2E).T.astype(BF16)
    gate_t = jax.nn.sigmoid(z[:, _C_G:_C_G + LANE]).T
    for g in range(N_KV):
        for b in range(x.shape[0] // Q_BLOCK):
            cols = slice(b * Q_BLOCK, (b + 1) * Q_BLOCK)
            q_ref[0, g, b, 0:HEAD_DIM, :] = jnp.concatenate(
                [qn_t[(g * N_REP + r) * HEAD_DIM:(g * N_REP + r + 1) * HEAD_DIM, cols] for r in range(N_REP)], axis=1)
            q_ref[0, g, b, HEAD_DIM:, :] = qc_ref[g]
            gate_ref[0, g, b] = gate_t[g * GATE_ROWS:(g + 1) * GATE_ROWS, cols]
    kc_ref[...] = z[:, _C_KC:_C_KC + D_KV]
    vc_ref[...] = z[:, _C_VC:_C_VC + D_KV]
    for col, gain, feat_ref, k_ref in ((_C_KS, kg_ref[1:2, :], fs_ref, ks_ref), (_C_KW, kg_ref[2:3, :], fw_ref, kw_ref)):
        kn = head_norm(col, N_KV, gain, 1.0).astype(BF16)
        for g in range(N_KV):
            k_ref[0, g] = jnp.concatenate([kn[:, g * HEAD_DIM:(g + 1) * HEAD_DIM], feat_ref[...]], axis=1)
    for col, vt_ref in ((_C_VS, vs_ref), (_C_VW, vw_ref)):
        vt = z[:, col:col + D_KV].T
        for g in range(N_KV):
            vt_ref[0, g] = vt[g * HEAD_DIM:(g + 1) * HEAD_DIM, :].astype(BF16)
    u_ref[...] = jax.nn.gelu(z[:, _C_U:_C_U + D_GM])
    gv = jax.nn.gelu(z[:, _C_V:_C_V + D_GM])
    v_ref[...] = (gv * _rms(gv)) * vg_ref[...]


def _inproj(x2, norm1_g, w_r, q_norm_g, k_norm_g, gm_v_norm_g, feat_s, feat_w, q_coef, batch, seq):
    n = x2.shape[0]
    tps = seq // TM_IN
    qpt = TM_IN // Q_BLOCK
    nq = N_REP * Q_BLOCK
    q_blocks = lambda r, c: pl.BlockSpec((1, N_KV, qpt, r, c), lambda i: (i // tps, 0, i % tps, 0, 0))
    row = lambda c: pl.BlockSpec((TM_IN, c), lambda i: (i, 0))
    full = lambda a: pl.BlockSpec(a.shape, lambda i: (0,) * a.ndim)
    per_seq = lambda a: pl.BlockSpec((TM_IN, a.shape[1]), lambda i: (i % tps, 0))
    keys = lambda w: pl.BlockSpec((1, N_KV, TM_IN, w), lambda i: (i // tps, 0, i % tps, 0))
    vals_t = pl.BlockSpec((1, N_KV, HEAD_DIM, TM_IN), lambda i: (i // tps, 0, 0, i % tps))
    g1 = norm1_g.reshape(1, D_MODEL)
    qg = q_norm_g.reshape(1, HEAD_DIM)
    vg = gm_v_norm_g.reshape(1, D_GM)
    ws, ww = HEAD_DIM + feat_s.shape[1], HEAD_DIM + feat_w.shape[1]
    tok = lambda c: jax.ShapeDtypeStruct((n, c), F32)
    return pl.pallas_call(
        _inproj_kernel,
        grid=(n // TM_IN,),
        in_specs=[row(D_MODEL), full(g1), full(w_r), full(qg), full(k_norm_g), full(vg), per_seq(feat_s),
                  per_seq(feat_w), full(q_coef)],
        out_specs=[q_blocks(LANE, nq), row(D_KV), row(D_KV), keys(ws), vals_t, keys(ww), vals_t,
                   q_blocks(GATE_ROWS, Q_BLOCK), row(D_GM), row(D_GM)],
        out_shape=[jax.ShapeDtypeStruct((batch, N_KV, seq // Q_BLOCK, LANE, nq), BF16), tok(D_KV), tok(D_KV),
                   jax.ShapeDtypeStruct((batch, N_KV, seq, ws), BF16),
                   jax.ShapeDtypeStruct((batch, N_KV, HEAD_DIM, seq), BF16),
                   jax.ShapeDtypeStruct((batch, N_KV, seq, ww), BF16),
                   jax.ShapeDtypeStruct((batch, N_KV, HEAD_DIM, seq), BF16),
                   jax.ShapeDtypeStruct((batch, N_KV, seq // Q_BLOCK, GATE_ROWS, Q_BLOCK), F32),
                   tok(D_GM), tok(D_GM)],
        compiler_params=pltpu.CompilerParams(dimension_semantics=("arbitrary",), vmem_limit_bytes=VMEM_LIMIT),
        name="inproj",
    )(x2, g1, w_r, qg, k_norm_g, vg, feat_s, feat_w, q_coef)


def _compress_kernel(a_ref, pos_ref, w1_ref, w1a_ref, w1b_ref, b1_ref, w2_ref, b2_ref, kg_ref, o_ref, *, norm):
    a = a_ref[0]
    nseg = a.shape[0]
    c = jnp.dot(pos_ref[...], w1_ref[...], precision=HIGHEST, preferred_element_type=F32)[0:1] + b1_ref[...]
    row = lax.broadcasted_iota(jnp.int32, (nseg, 1), 0)
    for g in range(N_KV):
        pa = jnp.dot(a, w1a_ref[g], precision=HIGHEST, preferred_element_type=F32)
        pb = jnp.dot(a, w1b_ref[g], precision=HIGHEST, preferred_element_type=F32)
        hid = jax.nn.gelu(pa + pltpu.roll(pb, nseg - 1, 0) + c)
        out = jnp.dot(hid, w2_ref[...], precision=HIGHEST, preferred_element_type=F32) + b2_ref[...]
        if norm:
            out = (out * _rms(out)) * kg_ref[...]
        o_ref[0, g] = jnp.where(row < nseg - 1, out, 0.0)


def _compress(raw, pos, w1, b1, w2, b2, gain, batch, seq, norm):
    nseg = seq // CMP_STRIDE
    half = CMP_STRIDE * HEAD_DIM
    a = raw.reshape(batch, nseg, CMP_STRIDE * D_KV)
    pos8 = jnp.broadcast_to(pos.reshape(1, CMP_LEN * HEAD_DIM), (8, CMP_LEN * HEAD_DIM))

    def expand(wh):
        wh = wh.reshape(CMP_STRIDE, HEAD_DIM, CMP_HIDDEN)
        z = jnp.zeros((N_KV, CMP_STRIDE, N_KV, HEAD_DIM, CMP_HIDDEN), F32)
        for g in range(N_KV):
            z = z.at[g, :, g].set(wh)
        return z.reshape(N_KV, CMP_STRIDE * D_KV, CMP_HIDDEN)

    w1a, w1b = expand(w1[:half]), expand(w1[half:])
    b1r, b2r, gr = b1.reshape(1, CMP_HIDDEN), b2.reshape(1, HEAD_DIM), gain.reshape(1, HEAD_DIM)
    full = lambda t: pl.BlockSpec(t.shape, lambda i: (0,) * t.ndim)
    return pl.pallas_call(
        functools.partial(_compress_kernel, norm=norm),
        grid=(batch,),
        in_specs=[pl.BlockSpec((1, nseg, CMP_STRIDE * D_KV), lambda i: (i, 0, 0)),
                  full(pos8), full(w1), full(w1a), full(w1b), full(b1r), full(w2), full(b2r), full(gr)],
        out_specs=pl.BlockSpec((1, N_KV, nseg, HEAD_DIM), lambda i: (i, 0, 0, 0)),
        out_shape=jax.ShapeDtypeStruct((batch, N_KV, nseg, HEAD_DIM), F32),
        compiler_params=pltpu.CompilerParams(dimension_semantics=("arbitrary",), vmem_limit_bytes=VMEM_LIMIT),
        name="compress_k" if norm else "compress_v",
    )(a, pos8, w1, w1a, w1b, b1r, w2, b2r, gr)


def _attn_kernel(qt_ref, kc_ref, vct_ref, ks_ref, vst_ref, kw_ref, vwt_ref, g_ref, ovt_ref, o_ref,
                 qs_ref, s0_ref, s1_ref, p0_ref, p1_ref, st_ref, acc_ref):
    qb = pl.program_id(1)
    nq = N_REP * Q_BLOCK
    q0 = qb * Q_BLOCK
    qt = qt_ref[0, 0]
    ql = lax.broadcasted_iota(jnp.int32, (1, nq), 1) % Q_BLOCK
    t_row = (q0 + ql).astype(F32)
    m_init = 0.5 * NEG

    def online(s, m, l):
        m_new = jnp.maximum(m, jnp.max(s, axis=0, keepdims=True))
        alpha = jnp.exp2(m - m_new)
        p = jnp.exp2(s - m_new)
        return p, m_new, alpha, alpha * l + jnp.sum(p, axis=0, keepdims=True)

    def inv(l):
        return jnp.where(l > 0.0, 1.0 / l, 0.0)

    m0 = jnp.full((1, nq), m_init, F32)
    l0 = jnp.zeros((1, nq), F32)
    a0 = jnp.zeros((HEAD_DIM, nq), F32)

    ncmp = kc_ref.shape[1]
    s = jnp.dot(kc_ref[0], qt, preferred_element_type=F32)
    c_end = (lax.broadcasted_iota(jnp.int32, (ncmp, 1), 0) * CMP_STRIDE + (CMP_LEN - 1)).astype(F32)
    p, _, _, l = online(jnp.where(c_end <= t_row, s, NEG), m0, l0)
    p = p * inv(l)
    o_cmp = jnp.dot(vct_ref[0], p.astype(BF16), preferred_element_type=F32)

    psum = p[:, 0:Q_BLOCK]
    for r in range(1, N_REP):
        psum = psum + p[:, r * Q_BLOCK:(r + 1) * Q_BLOCK]
    nsel = ovt_ref.shape[0]
    p_hi = psum.astype(BF16)
    p_lo = (psum - p_hi.astype(F32)).astype(BF16)
    imp = (jnp.dot(ovt_ref[...], p_hi, preferred_element_type=F32)
           + jnp.dot(ovt_ref[...], p_lo, preferred_element_type=F32))
    n_col = lax.broadcasted_iota(jnp.int32, (nsel, 1), 0).astype(F32)
    n_start = n_col * SEL_BLOCK
    tq = t_row[:, 0:Q_BLOCK]
    cur = jnp.floor(tq * (1.0 / SEL_BLOCK)) * SEL_BLOCK
    forced = (n_start == cur) | (n_start == 0.0)
    valid = n_start <= tq
    imp = jnp.where(forced, imp + FORCE_BONUS, imp)
    imp = jnp.where(valid, imp, NEG)
    sel = jnp.zeros((nsel, Q_BLOCK), F32)
    for _ in range(min(N_SEL, nsel)):
        mx = jnp.max(imp, axis=0, keepdims=True)
        first = jnp.min(jnp.where(imp == mx, n_col, float(nsel)), axis=0, keepdims=True)
        hit = n_col == first
        sel = jnp.where(hit, 1.0, sel)
        imp = jnp.where(hit, -jnp.inf, imp)
    selb = jnp.where(valid & (sel > 0.0), 0.0, NEG).astype(BF16)
    qs_ref[0:LANE, :] = qt
    qs_ref[LANE:LANE + nsel, :] = jnp.concatenate([selb] * N_REP, axis=1)
    if qs_ref.shape[0] > LANE + nsel:
        qs_ref[LANE + nsel:, :] = jnp.zeros((qs_ref.shape[0] - LANE - nsel, nq), BF16)

    def attend(k_blk, vt_blk, q_op, bias, carry):
        m, l, acc = carry
        s = jnp.dot(k_blk, q_op, preferred_element_type=F32)
        if bias is not None:
            s = s + bias
        p, m, alpha, l = online(s, m, l)
        pv = jnp.dot(vt_blk, p.astype(BF16), preferred_element_type=F32)
        return m, l, alpha * acc + pv

    seq = ks_ref.shape[1]

    def scores(j):
        k0 = pl.multiple_of(jnp.minimum(j * KC_SEL, seq - KC_SEL), KC_SEL)
        s = jnp.dot(ks_ref[0, pl.ds(k0, KC_SEL), :], qs_ref[...], preferred_element_type=F32)
        return s, jnp.max(s, axis=0, keepdims=True)

    def values(j, p):
        k0 = pl.multiple_of(jnp.maximum(j, 0) * KC_SEL, KC_SEL)
        return jnp.dot(vst_ref[0, :, pl.ds(k0, KC_SEL)], p, preferred_element_type=F32)

    def stage(j, s_cur, s_nxt, p_cur, p_prv):
        m, l, alpha_prev, mx = st_ref[0:1, :], st_ref[1:2, :], st_ref[2:3, :], st_ref[3:4, :]
        m_new = jnp.maximum(m, mx)
        alpha = jnp.exp2(m - m_new)
        k0 = pl.multiple_of(jnp.minimum((j + 1) * KC_SEL, seq - KC_SEL), KC_SEL)
        k0p = pl.multiple_of(jnp.maximum(j - 1, 0) * KC_SEL, KC_SEL)
        sub = KC_SEL // 4
        psum, mx_next, zeros = None, None, []
        for q in range(4):
            rows = slice(q * sub, (q + 1) * sub)
            k_q = ks_ref[0, pl.ds(k0 + q * sub, sub), :]
            if q >= 1:
                k_q = k_q + jnp.concatenate([zeros[q - 1]] * (ks_ref.shape[2] // LANE), axis=1)
            s_q = jnp.dot(k_q, qs_ref[...], preferred_element_type=F32)
            s_nxt[rows, :] = s_q
            mx_q = jnp.max(s_q, axis=0, keepdims=True)
            mx_next = mx_q if mx_next is None else jnp.maximum(mx_next, mx_q)
            p_q = jnp.exp2(s_cur[rows, :] - m_new)
            ps_q = jnp.sum(p_q, axis=0, keepdims=True)
            psum = ps_q if psum is None else psum + ps_q
            p_q = p_q.astype(BF16)
            p_cur[rows, :] = p_q
            dep = ps_q[:, 0:LANE]
            for r in range(1, N_REP):
                dep = dep + ps_q[:, r * LANE:(r + 1) * LANE]
            bits = pltpu.bitcast(dep, jnp.int32)
            zeros.append(lax.shift_right_logical(lax.shift_right_logical(bits, 16), 16).astype(F32).astype(BF16))
            if q == 1:
                vt_prev = vst_ref[0, :, pl.ds(k0p, KC_SEL)] + jnp.concatenate([zeros[1]] * (KC_SEL // LANE), axis=1)
                acc_ref[...] = alpha_prev * acc_ref[...] + jnp.dot(vt_prev, p_prv[...], preferred_element_type=F32)
        st_ref[0:1, :] = m_new
        st_ref[1:2, :] = alpha * l + psum
        st_ref[2:3, :] = alpha
        st_ref[3:4, :] = mx_next

    n_full = q0 // KC_SEL
    s_first, mx_first = scores(0)

    @pl.when(n_full % 2 == 0)
    def _():
        s0_ref[...] = s_first

    @pl.when(n_full % 2 == 1)
    def _():
        s1_ref[...] = s_first

    p0_ref[...] = jnp.zeros(p0_ref.shape, BF16)
    p1_ref[...] = jnp.zeros(p1_ref.shape, BF16)
    st_ref[0:1, :] = m0
    st_ref[1:2, :] = l0
    st_ref[2:3, :] = jnp.ones((1, nq), F32)
    st_ref[3:4, :] = mx_first
    acc_ref[...] = a0

    def sel_body(j, carry):
        @pl.when((n_full - j) % 2 == 0)
        def _():
            stage(j, s0_ref, s1_ref, p0_ref, p1_ref)

        @pl.when((n_full - j) % 2 == 1)
        def _():
            stage(j, s1_ref, s0_ref, p1_ref, p0_ref)
        return carry

    lax.fori_loop(0, n_full, sel_body, 0)
    pos_last = (n_full * KC_SEL + lax.broadcasted_iota(jnp.int32, (KC_SEL, 1), 0)).astype(F32)
    p, _, alpha, l_sel = online(s0_ref[...] + jnp.where(pos_last <= t_row, 0.0, NEG), st_ref[0:1, :], st_ref[1:2, :])
    acc = st_ref[2:3, :] * acc_ref[...] + values(n_full - 1, p1_ref[...])
    o_sel = alpha * acc + values(n_full, p.astype(BF16))

    n_wk = WINDOW + Q_BLOCK
    w0 = pl.multiple_of(jnp.maximum(q0 - WINDOW, 0), Q_BLOCK)
    kk = lax.broadcasted_iota(jnp.int32, (n_wk, 1), 0) - (q0 - w0)
    in_win = (kk <= ql) & (kk + WINDOW > ql)
    _, l_win, o_win = attend(kw_ref[0, pl.ds(w0, n_wk), :], vwt_ref[0, :, pl.ds(w0, n_wk)], qt,
                             jnp.where(in_win, 0.0, NEG), (m0, l0, a0))

    gt = g_ref[0, 0]

    def gate(j):
        return jnp.concatenate([gt[j * N_REP + r:j * N_REP + r + 1, :] for r in range(N_REP)], axis=1)

    o_ref[0, 0] = (gate(0) * o_cmp + gate(1) * (o_sel * inv(l_sel)) + gate(2) * (o_win * inv(l_win)))


def _attention(qt, kc, vct, ks, vst, kw, vwt, gt, ovt):
    bgn, nqb = qt.shape[0], qt.shape[1]
    seq = ks.shape[1]
    nq = N_REP * Q_BLOCK
    per_bg = lambda a: pl.BlockSpec((1,) + a.shape[1:], lambda b, i: (b,) + (0,) * (a.ndim - 1))
    return pl.pallas_call(
        _attn_kernel,
        grid=(bgn, nqb),
        in_specs=[pl.BlockSpec((1, 1, LANE, nq), lambda b, i: (b, i, 0, 0)),
                  per_bg(kc), per_bg(vct), per_bg(ks), per_bg(vst), per_bg(kw), per_bg(vwt),
                  pl.BlockSpec((1, 1, GATE_ROWS, Q_BLOCK), lambda b, i: (b, i, 0, 0)),
                  pl.BlockSpec(ovt.shape, lambda b, i: (0, 0))],
        out_specs=pl.BlockSpec((1, 1, HEAD_DIM, nq), lambda b, i: (b, i, 0, 0)),
        out_shape=jax.ShapeDtypeStruct((bgn, nqb, HEAD_DIM, nq), F32),
        scratch_shapes=[pltpu.VMEM((ks.shape[2], nq), BF16),
                        pltpu.VMEM((KC_SEL, nq), F32), pltpu.VMEM((KC_SEL, nq), F32),
                        pltpu.VMEM((KC_SEL, nq), BF16), pltpu.VMEM((KC_SEL, nq), BF16),
                        pltpu.VMEM((8, nq), F32), pltpu.VMEM((HEAD_DIM, nq), F32)],
        compiler_params=pltpu.CompilerParams(dimension_semantics=("arbitrary", "arbitrary"),
                                             vmem_limit_bytes=VMEM_LIMIT),
        name="nsa_attention",
    )(qt, kc, vct, ks, vst, kw, vwt, gt, ovt)


def _mix_kernel(x_ref, oa_ref, u_ref, v_ref, ws_ref, bs_ref, ga_ref, gg_ref, wo_ref, g2_ref, wr_ref, br_ref,
                x1_ref, xn_ref, idx_ref, gate_ref, rank_ref, cnt_out_ref, cnt_ref):
    tm = x_ref.shape[0]
    rr = lax.broadcasted_iota(jnp.int32, (GM_CHUNK, GM_CHUNK), 0)
    cc = lax.broadcasted_iota(jnp.int32, (GM_CHUNK, GM_CHUNK), 1)
    grp = lax.broadcasted_iota(jnp.int32, (1, D_GM), 1) // GM_GROUP_DIM
    ws = [jnp.where(rr >= cc, ws_ref[g], 0.0).astype(BF16) for g in range(N_GM_GROUPS)]
    ys = []
    for c in range(tm // GM_CHUNK):
        vch = v_ref[c * GM_CHUNK:(c + 1) * GM_CHUNK, :].astype(BF16)
        y = bs_ref[...]
        for g in range(N_GM_GROUPS):
            y = y + jnp.where(grp == g, jnp.dot(ws[g], vch, preferred_element_type=F32), 0.0)
        ys.append(y)
    o_gm = u_ref[...] * jnp.concatenate(ys, axis=0)
    o_at = jnp.concatenate(
        [jnp.concatenate([oa_ref[0, g, b, :, r * Q_BLOCK:(r + 1) * Q_BLOCK]
                          for g in range(N_KV) for r in range(N_REP)], axis=0).T
         for b in range(tm // Q_BLOCK)], axis=0)
    mixed =jnp.concatenate([(o_at * _rms(o_at)) * ga_ref[...], (o_gm * _rms(o_gm)) * gg_ref[...]], axis=-1)
    x1 = x_ref[...] + jnp.dot(mixed.astype(BF16), wo_ref[...], preferred_element_type=F32)
    x1_ref[...] = x1
    xn = (x1 * _rms(x1)) * g2_ref[...]
    for s in range(ROW_SUB):
        xn_ref[:, s, :] = xn[:, s * LANE:(s + 1) * LANE]
    logits = jnp.dot(xn, wr_ref[...], precision=HIGHEST, preferred_element_type=F32) + br_ref[...]
    lane = lax.broadcasted_iota(jnp.int32, (1, LANE), 1).astype(F32)
    idx_out = jnp.zeros((tm, LANE), F32)
    val_out = jnp.zeros((tm, LANE), F32)
    vals, firsts = [], []
    for k in range(TOP_K):
        mx = jnp.max(logits, axis=-1, keepdims=True)
        first = jnp.min(jnp.where(logits == mx, lane, float(LANE)), axis=-1, keepdims=True)
        logits = jnp.where(lane == first, -jnp.inf, logits)
        idx_out = jnp.where(lane == float(k), first, idx_out)
        vals.append(mx)
        firsts.append(first)
    es = [jnp.exp(v - vals[0]) for v in vals]
    den = es[0] + es[1] + es[2] + es[3]
    for k in range(TOP_K):
        val_out = jnp.where(lane == float(k), es[k] / den, val_out)
    idx_ref[...] = idx_out.astype(jnp.int32)
    gate_ref[...] = val_out

    @pl.when(pl.program_id(0) == 0)
    def _():
        cnt_ref[...] = jnp.zeros(cnt_ref.shape, F32)

    hit = (lane == firsts[0]) | (lane == firsts[1]) | (lane == firsts[2]) | (lane == firsts[3])
    hit_b = jnp.where(hit, 1.0, 0.0).astype(BF16)
    tr = lax.broadcasted_iota(jnp.int32, (tm, tm), 0)
    tc = lax.broadcasted_iota(jnp.int32, (tm, tm), 1)
    before = jnp.where(tr > tc, 1.0, 0.0).astype(BF16)
    ranks = jnp.dot(before, hit_b, preferred_element_type=F32) + cnt_ref[0:1, :]
    rank_out = jnp.zeros((tm, LANE), F32)
    for k in range(TOP_K):
        r_k = jnp.sum(jnp.where(lane == firsts[k], ranks, 0.0), axis=-1, keepdims=True)
        rank_out = jnp.where(lane == float(k), r_k, rank_out)
    rank_ref[...] = rank_out.astype(jnp.int32)
    cnt_ref[0:1, :] = cnt_ref[0:1, :] + jnp.sum(hit_b.astype(F32), axis=0, keepdims=True)
    cnt_out_ref[...] = cnt_ref[...].astype(jnp.int32)


def _mix(x2, ot, u_act, v_act, gm_w_s, bias_full, ga, gg, w_out_b, g2, wr_pad, br_pad):
    n = x2.shape[0]
    tps = ot.shape[2] * Q_BLOCK // TM_MIX
    row = lambda c: pl.BlockSpec((TM_MIX, c), lambda i: (i, 0))
    full = lambda a: pl.BlockSpec(a.shape, lambda i: (0,) * a.ndim)
    attn_blocks = pl.BlockSpec((1, N_KV, TM_MIX // Q_BLOCK) + ot.shape[3:], lambda i: (i // tps, 0, i % tps, 0, 0))
    return pl.pallas_call(
        _mix_kernel,
        grid=(n // TM_MIX,),
        in_specs=[row(D_MODEL), attn_blocks, row(D_GM), row(D_GM), full(gm_w_s), full(bias_full), full(ga), full(gg),
                  full(w_out_b), full(g2), full(wr_pad), full(br_pad)],
        out_specs=[row(D_MODEL), pl.BlockSpec((TM_MIX, ROW_SUB, LANE), lambda i: (i, 0, 0)), row(LANE), row(LANE),
                   row(LANE), pl.BlockSpec((ROW_SUB, LANE), lambda i: (0, 0))],
        out_shape=[jax.ShapeDtypeStruct((n, D_MODEL), F32), jax.ShapeDtypeStruct((n, ROW_SUB, LANE), F32),
                   jax.ShapeDtypeStruct((n, LANE), jnp.int32), jax.ShapeDtypeStruct((n, LANE), F32),
                   jax.ShapeDtypeStruct((n, LANE), jnp.int32), jax.ShapeDtypeStruct((ROW_SUB, LANE), jnp.int32)],
        scratch_shapes=[pltpu.VMEM((ROW_SUB, LANE), F32)],
        compiler_params=pltpu.CompilerParams(dimension_semantics=("arbitrary",), vmem_limit_bytes=VMEM_LIMIT),
        name="mix_outproj_router",
    )(x2, ot, u_act, v_act, gm_w_s, bias_full, ga, gg, w_out_b, g2, wr_pad, br_pad)


def _row_gather(idx_ref, n_rows, src_hbm, dst_ref, sem):
    def start():
        for r in range(n_rows):
            pltpu.make_async_copy(src_hbm.at[pl.ds(idx_ref[0, 0, r], 1), :], dst_ref.at[pl.ds(r, 1), :], sem).start()

    def wait():
        pltpu.make_async_copy(src_hbm.at[pl.ds(0, n_rows), :], dst_ref, sem).wait()

    return start, wait


def _tile_row_gather(idx_ref, n_rows, src_hbm, dst_ref, sem):
    def start():
        for r in range(n_rows):
            t = idx_ref[0, 0, r]
            pltpu.make_async_copy(src_hbm.at[lax.shift_right_logical(t, 3), t & (ROW_SUB - 1)],
                                  dst_ref.at[r // ROW_SUB, :, r % ROW_SUB, :], sem).start(priority=r % 2)

    def wait():
        pltpu.make_async_copy(src_hbm.at[pl.ds(0, n_rows // ROW_SUB)], dst_ref, sem).wait()

    return start, wait


def _tiles_to_matrix(ref):
    rows = ref.shape[0] * ROW_SUB
    return jnp.concatenate([ref[:, c].reshape(rows, LANE) for c in range(ROW_SUB)], axis=1)


def _moe_kernel(be_ref, bv_ref, bn_ref, bs_ref, tok_ref, tok_n1_ref, tok_n2_ref, x_hbm, wgu_hbm, bg_ref, bl_ref,
                wd_hbm, bd_ref, o_ref, xbuf, sems, wt_s, wg_s, wl_s, wd_s, wgu_buf, wd_buf, wsems):
    i = pl.program_id(0)
    slot = i % (MOE_AHEAD + 1)
    slot_n2 = (i + MOE_AHEAD) % (MOE_AHEAD + 1)
    start_cur, wait_cur = _tile_row_gather(tok_ref, BM_MOE, x_hbm, xbuf.at[slot], sems.at[slot])
    start_n1, _ = _tile_row_gather(tok_n1_ref, BM_MOE, x_hbm, xbuf.at[1], sems.at[1])
    start_n2, _ = _tile_row_gather(tok_n2_ref, BM_MOE, x_hbm, xbuf.at[slot_n2], sems.at[slot_n2])

    prev = jnp.maximum(i - 1, 0)

    def fetch_weights(e, ws):
        n_piece = 4
        rows = D_MODEL // n_piece
        copies = [pltpu.make_async_copy(wgu_hbm.at[e, pl.ds(c * rows, rows)], wgu_buf.at[ws, pl.ds(c * rows, rows)],
                                        wsems.at[ws]) for c in range(n_piece)]
        copies += [pltpu.make_async_copy(wd_hbm.at[e, pl.ds(c * rows, rows)], wd_buf.at[ws, pl.ds(c * rows, rows)],
                                         wsems.at[ws]) for c in range(n_piece)]
        return copies

    @pl.when(i == 0)
    def _():
        start_cur()
        start_n1()
        for cp in fetch_weights(be_ref[0], 0):
            cp.start()

    @pl.when((bv_ref[i] == 1) & ((i == 0) | (be_ref[i] != be_ref[prev])))
    def _():
        ws = bs_ref[i]
        for cp in fetch_weights(be_ref[i], ws):
            cp.wait()

        @pl.when(bn_ref[i] >= 0)
        def _():
            for cp in fetch_weights(bn_ref[i], 1 - ws):
                cp.start()

        tc = wt_s.shape[1]
        for c in range(2 * D_EXPERT // tc):
            wt = wgu_buf[ws, :, c * tc:(c + 1) * tc].T
            for j in range(ROW_SUB):
                wt_s[j] = wt[:, j * LANE:(j + 1) * LANE]
            for first, dst in ((0, wg_s), (1, wl_s)):
                half = jnp.concatenate([wt_s[j, pl.ds(first, tc // 2, stride=2), :] for j in range(ROW_SUB)], axis=1)
                dst[c * tc // 2:(c + 1) * tc // 2, :] = half.astype(BF16)
        wd_s[...] = wd_buf[ws].astype(BF16)

    @pl.when(bv_ref[i] == 1)
    def _():
        wait_cur()
        start_n2()
        xb = _tiles_to_matrix(xbuf.at[slot]).astype(BF16)
        hg = lax.dot_general(xb, wg_s[...], _NT, preferred_element_type=F32) + bg_ref[0]
        hl = lax.dot_general(xb, wl_s[...], _NT, preferred_element_type=F32) + bl_ref[0]
        hg = jnp.minimum(hg, SWIGLU_LIMIT)
        hl = jnp.clip(hl, -SWIGLU_LIMIT, SWIGLU_LIMIT)
        a = hg * jax.nn.sigmoid(SWIGLU_ALPHA * hg) * (hl + 1.0)
        o_ref[...] = jnp.dot(a.astype(BF16), wd_s[...], preferred_element_type=F32) + bd_ref[0]

    @pl.when((bv_ref[i] == 0) & ((i == 1) | ((i >= MOE_AHEAD) & (bv_ref[jnp.maximum(i - MOE_AHEAD, 0)] == 1))))
    def _():
        wait_cur()

    @pl.when(bv_ref[i] == 0)
    def _():
        o_ref[...] = jnp.zeros(o_ref.shape, F32)


def _moe(blk_expert, blk_valid, blk_next, blk_wslot, tok_blocks, xn3, w_gate_up, bg, bl, w_down, bd):
    nb = blk_expert.shape[0]
    per_e = lambda a: pl.BlockSpec((1,) + a.shape[1:], lambda i, be, *_: (be[i],) + (0,) * (a.ndim - 1))

    def tok_spec(ahead):
        return pl.BlockSpec((1, 1, BM_MOE), lambda i, *_: (jnp.minimum(i + ahead, nb - 1), 0, 0),
                            memory_space=pltpu.SMEM)

    grid_spec = pltpu.PrefetchScalarGridSpec(
        num_scalar_prefetch=4,
        grid=(nb,),
        in_specs=[tok_spec(0), tok_spec(1), tok_spec(MOE_AHEAD),
                  pl.BlockSpec(memory_space=pl.ANY),
                  pl.BlockSpec(memory_space=pl.ANY), per_e(bg), per_e(bl),
                  pl.BlockSpec(memory_space=pl.ANY), per_e(bd)],
        out_specs=pl.BlockSpec((BM_MOE, D_MODEL), lambda i, *_: (i, 0)),
        scratch_shapes=[pltpu.VMEM((MOE_AHEAD + 1, BM_MOE // ROW_SUB, ROW_SUB, ROW_SUB, LANE), F32),
                        pltpu.SemaphoreType.DMA((MOE_AHEAD + 1,)),
                        pltpu.VMEM((ROW_SUB, 256, LANE), F32), pltpu.VMEM((D_EXPERT, D_MODEL), BF16),
                        pltpu.VMEM((D_EXPERT, D_MODEL), BF16), pltpu.VMEM((D_EXPERT, D_MODEL), BF16),
                        pltpu.VMEM((2, D_MODEL, 2 * D_EXPERT), F32), pltpu.VMEM((2, D_EXPERT, D_MODEL), F32),
                        pltpu.SemaphoreType.DMA((2,))],
    )
    return pl.pallas_call(
        _moe_kernel,
        grid_spec=grid_spec,
        out_shape=jax.ShapeDtypeStruct((nb * BM_MOE, D_MODEL), F32),
        compiler_params=pltpu.CompilerParams(dimension_semantics=("arbitrary",), vmem_limit_bytes=VMEM_LIMIT_MOE),
        name="moe_experts",
    )(blk_expert, blk_valid, blk_next, blk_wslot, tok_blocks, tok_blocks, tok_blocks,
      xn3.reshape(xn3.shape[0] // ROW_SUB, ROW_SUB, ROW_SUB, LANE), w_gate_up, bg, bl, w_down, bd)


def _slot_table_kernel(lo_ref, hi_ref, dest_ref, tok_ref):
    i = pl.program_id(0)
    n_chunk = dest_ref.shape[2]

    @pl.when(i == 0)
    def _():
        def clear(s, carry):
            tok_ref[s] = 0
            return carry
        for e in range(lo_ref.shape[0]):
            lax.fori_loop(lo_ref[e], hi_ref[e], clear, 0)

    def put(s, carry):
        tok_ref[dest_ref[0, 0, s]] = lax.shift_right_logical(i * n_chunk + s, TOP_K.bit_length() - 1)
        return carry
    lax.fori_loop(0, n_chunk, put, 0, unroll=8)


def _slot_table(pad_lo, pad_hi, dest, n_slots):
    n_chunk = 8192
    s_tot = dest.shape[0]
    grid_spec = pltpu.PrefetchScalarGridSpec(
        num_scalar_prefetch=2,
        grid=(s_tot // n_chunk,),
        in_specs=[pl.BlockSpec((1, 1, n_chunk), lambda i, *_: (i, 0, 0), memory_space=pltpu.SMEM)],
        out_specs=pl.BlockSpec(memory_space=pltpu.SMEM),
    )
    return pl.pallas_call(
        _slot_table_kernel,
        grid_spec=grid_spec,
        out_shape=jax.ShapeDtypeStruct((n_slots,), jnp.int32),
        compiler_params=pltpu.CompilerParams(dimension_semantics=("arbitrary",)),
        name="moe_slot_table",
    )(pad_lo, pad_hi, dest.reshape(s_tot // n_chunk, 1, n_chunk))


def _combine_kernel(dest_ref, dest_next_ref, x1_ref, gate_ref, y_hbm, o_ref, buf, sems):
    i = pl.program_id(0)
    slot = i % 2
    n_rows = TOP_K * TM_CMB
    start_cur, wait_cur = _row_gather(dest_ref, n_rows, y_hbm, buf.at[slot], sems.at[slot])
    start_next, _ = _row_gather(dest_next_ref, n_rows, y_hbm, buf.at[1 - slot], sems.at[1 - slot])

    @pl.when(i == 0)
    def _():
        start_cur()

    @pl.when(i + 1 < pl.num_programs(0))
    def _():
        start_next()

    wait_cur()
    gate = gate_ref[...]
    acc = x1_ref[...]
    for k in range(TOP_K):
        acc = acc + gate[:, k:k + 1] * buf[slot, k * TM_CMB:(k + 1) * TM_CMB, :]
    o_ref[...] = acc


def _combine(dest_blocks, x1, gate_pad, y_rows):
    n = x1.shape[0]
    nt = n // TM_CMB
    n_rows = TOP_K * TM_CMB
    return pl.pallas_call(
        _combine_kernel,
        grid=(nt,),
        in_specs=[pl.BlockSpec((1, 1, n_rows), lambda i: (i, 0, 0), memory_space=pltpu.SMEM),
                  pl.BlockSpec((1, 1, n_rows), lambda i: (jnp.minimum(i + 1, nt - 1), 0, 0),
                               memory_space=pltpu.SMEM),
                  pl.BlockSpec((TM_CMB, D_MODEL), lambda i: (i, 0)),
                  pl.BlockSpec((TM_CMB, LANE), lambda i: (i, 0)),
                  pl.BlockSpec(memory_space=pl.ANY)],
        out_specs=pl.BlockSpec((TM_CMB, D_MODEL), lambda i: (i, 0)),
        out_shape=jax.ShapeDtypeStruct((n, D_MODEL), F32),
        scratch_shapes=[pltpu.VMEM((2, n_rows, D_MODEL), F32), pltpu.SemaphoreType.DMA((2,))],
        compiler_params=pltpu.CompilerParams(dimension_semantics=("arbitrary",), vmem_limit_bytes=VMEM_LIMIT),
        name="moe_combine",
    )(dest_blocks, dest_blocks, x1, gate_pad, y_rows)


def kernel(x, norm1_g, w_in, q_norm_g, k_norm_g, cmp_pos, w_cmp1, b_cmp1, w_cmp2, b_cmp2, gm_v_norm_g, gm_w_s,
           gm_b_s, out_norm_attn_g, out_norm_gm_g, w_out, norm2_g, w_router, b_router, w_gate_up, b_gate_up,
           w_down, b_down):
    batch, seq, _ = x.shape
    n = batch * seq
    nqb = seq // Q_BLOCK
    bgn = batch * N_KV
    x2 = x.reshape(n, D_MODEL)

    c_gate = D_ATTN + 6 * D_KV
    gate_src = np.full((LANE,), N_GATE, np.int32)
    for h in range(N_HEADS):
        for j in range(3):
            gate_src[(h // N_REP) * GATE_ROWS + j * N_REP + h % N_REP] = h * 3 + j
    w_gate = jnp.concatenate([w_in[:, c_gate:c_gate + N_GATE], jnp.zeros((D_MODEL, 1), F32)], axis=1)[:, gate_src]
    w_r = jnp.concatenate([w_in[:, :c_gate], w_in[:, c_gate + N_GATE:], w_gate], axis=1).astype(BF16)
    nq = N_REP * Q_BLOCK
    head = np.arange(N_KV)[:, None] * N_REP + np.arange(nq)[None, :] // Q_BLOCK
    coef = np.exp2(-(head + 1.0)) * LOG2E
    c_hi = coef.astype(BF16).astype(np.float64)
    c_lo = (coef - c_hi).astype(BF16).astype(np.float64)
    qrows = np.zeros((N_KV, LANE - HEAD_DIM, nq), np.float32)
    qrows[:, 0], qrows[:, 1], qrows[:, 2], qrows[:, 3] = SEL_BLOCK * c_hi, SEL_BLOCK * c_lo, c_hi, c_lo
    nsel = seq // SEL_BLOCK
    oh_w = -(-nsel // LANE) * LANE

    def pos_features(pos, one_hot=False):
        feat = np.zeros((pos.shape[0], LANE - HEAD_DIM + (oh_w if one_hot else 0)), np.float32)
        feat[:, 0] = feat[:, 1] = pos // SEL_BLOCK
        feat[:, 2] = feat[:, 3] = pos % SEL_BLOCK
        if one_hot:
            feat[np.arange(pos.shape[0]), LANE - HEAD_DIM + pos // SEL_BLOCK] = 1.0
        return jnp.asarray(feat, BF16)

    pos_t = np.arange(seq)
    qt, kc_raw, vc_raw, ks_aug, vst, kw_aug, vwt, gt, u_act, v_act = _inproj(
        x2, norm1_g, w_r, q_norm_g, k_norm_g, gm_v_norm_g, pos_features(pos_t, one_hot=True), pos_features(pos_t),
        jnp.asarray(qrows, BF16), batch, seq)

    kc = _compress(kc_raw, cmp_pos[0], w_cmp1[0], b_cmp1[0], w_cmp2[0], b_cmp2[0], k_norm_g[0], batch, seq, True)
    vc = _compress(vc_raw, cmp_pos[1], w_cmp1[1], b_cmp1[1], w_cmp2[1], b_cmp2[1], k_norm_g[0], batch, seq, False)

    ncmp = seq // CMP_STRIDE
    pos_c = np.arange(ncmp) * CMP_STRIDE + (CMP_LEN - 1)
    kc_b = jnp.concatenate([kc.reshape(bgn, ncmp, HEAD_DIM).astype(BF16),
                            jnp.broadcast_to(pos_features(pos_c)[None], (bgn, ncmp, LANE - HEAD_DIM))], axis=-1)
    vct = vc.reshape(bgn, ncmp, HEAD_DIM).transpose(0, 2, 1).astype(BF16)
    c0 = np.arange(ncmp)[None, :] * CMP_STRIDE
    n0 = np.arange(seq // SEL_BLOCK)[:, None] * SEL_BLOCK
    ovt = np.clip(np.minimum(c0 + CMP_LEN, n0 + SEL_BLOCK) - np.maximum(c0, n0), 0, None) / CMP_LEN
    per_group = lambda a: a.reshape((bgn,) + a.shape[2:])
    ot = _attention(per_group(qt), kc_b, vct, per_group(ks_aug), per_group(vst), per_group(kw_aug),
                    per_group(vwt), per_group(gt), jnp.asarray(ovt, BF16))
    ot = ot.reshape((batch, N_KV) + ot.shape[1:])

    bias_full = jnp.repeat(gm_b_s.T, GM_GROUP_DIM, axis=1)
    wr_pad = jnp.concatenate([w_router, jnp.zeros((D_MODEL, LANE - N_EXPERTS), F32)], axis=1)
    br_pad = jnp.concatenate([b_router, jnp.full((LANE - N_EXPERTS,), NEG, F32)]).reshape(1, LANE)
    x1, xn3, idx_pad, gate_pad, rank_pad, cnt_pad = _mix(
        x2, ot, u_act, v_act, gm_w_s, bias_full, out_norm_attn_g.reshape(1, D_ATTN),
        out_norm_gm_g.reshape(1, D_GM), w_out.astype(BF16), norm2_g.reshape(1, D_MODEL), wr_pad, br_pad)

    s_tot = n * TOP_K
    nb = s_tot // BM_MOE + N_EXPERTS - 1 + MOE_AHEAD
    e_flat = idx_pad[:, :TOP_K].reshape(s_tot)
    rank = rank_pad[:, :TOP_K].reshape(s_tot)
    counts = cnt_pad[0, :N_EXPERTS]
    padded = ((counts + BM_MOE - 1) // BM_MOE) * BM_MOE
    pad_end = jnp.cumsum(padded)
    pad_start = pad_end - padded
    dest = pad_start[e_flat] + rank
    pad_lo = jnp.concatenate([pad_start + counts, pad_end[-1:]]).astype(jnp.int32)
    pad_hi = jnp.concatenate([pad_end, jnp.full((1,), nb * BM_MOE)]).astype(jnp.int32)
    tok_buf = _slot_table(pad_lo, pad_hi, dest.astype(jnp.int32), nb * BM_MOE)
    blk_start = jnp.arange(nb, dtype=jnp.int32) * BM_MOE
    blk_expert = jnp.minimum(jnp.sum((blk_start[:, None] >= pad_end[None, :]).astype(jnp.int32), axis=1),
                             N_EXPERTS - 1)
    blk_valid = (blk_start < pad_end[-1]).astype(jnp.int32)
    e_ids = jnp.arange(N_EXPERTS, dtype=jnp.int32)
    present = counts > 0
    ordinal = jnp.cumsum(present.astype(jnp.int32)) - 1
    later = jnp.where(present[None, :] & (e_ids[None, :] > e_ids[:, None]), e_ids[None, :], N_EXPERTS)
    nxt = jnp.min(later, axis=1)
    blk_next = jnp.where(nxt < N_EXPERTS, nxt, -1)[blk_expert].astype(jnp.int32)
    blk_wslot = (ordinal[blk_expert] % 2).astype(jnp.int32)

    bg = b_gate_up[:, 0::2].reshape(N_EXPERTS, 1, D_EXPERT)
    bl = b_gate_up[:, 1::2].reshape(N_EXPERTS, 1, D_EXPERT)
    y_rows = _moe(blk_expert, blk_valid, blk_next, blk_wslot, tok_buf.reshape(nb, 1, BM_MOE), xn3, w_gate_up, bg, bl,
                  w_down,
                  b_down.reshape(N_EXPERTS, 1, D_MODEL))

    dest_blocks = (dest.reshape(n // TM_CMB, TM_CMB, TOP_K).transpose(0, 2, 1)
                   .reshape(n // TM_CMB, 1, TOP_K * TM_CMB).astype(jnp.int32))
    out = _combine(dest_blocks, x1, gate_pad, y_rows)
    return out.reshape(batch, seq, D_MODEL)
```

```python
import functools

import jax
import jax.numpy as jnp
import numpy as np
from jax import lax
from jax.experimental import pallas as pl
from jax.experimental.pallas import tpu as pltpu

F32 = jnp.float32
BF16 = jnp.bfloat16
HIGHEST = lax.Precision.HIGHEST
_NT = (((1,), (1,)), ((), ()))

D_MODEL = 1024
N_HEADS = 8
HEAD_DIM = 64
N_KV = 2
N_REP = N_HEADS // N_KV
D_ATTN = N_HEADS * HEAD_DIM
D_KV = N_KV * HEAD_DIM
N_GM_GROUPS = 8
GM_GROUP_DIM = 64
D_GM = N_GM_GROUPS * GM_GROUP_DIM
N_GATE = 3 * N_HEADS
CMP_LEN = 32
CMP_STRIDE = 16
CMP_HIDDEN = 128
SEL_BLOCK = 64
N_SEL = 16
WINDOW = 512
Q_BLOCK = 128
FORCE_BONUS = 1.0e4
GM_CHUNK = 128
N_EXPERTS = 32
TOP_K = 4
D_EXPERT = 1024
SWIGLU_LIMIT = 7.0
SWIGLU_ALPHA = 1.702
EPS = 1e-6
NEG = -1.0e30
LOG2E = 1.4426950408889634

LANE = 128
ROW_SUB = D_MODEL // LANE
GATE_ROWS = 16
VMEM_LIMIT = 48 * 1024 * 1024
VMEM_LIMIT_MOE = 56 * 1024 * 1024

_C_Q = 0
_C_KC = _C_Q + D_ATTN
_C_VC = _C_KC + D_KV
_C_KS = _C_VC + D_KV
_C_VS = _C_KS + D_KV
_C_KW = _C_VS + D_KV
_C_VW = _C_KW + D_KV
_C_U = _C_VW + D_KV
_C_V = _C_U + D_GM
_C_G = _C_V + D_GM
D_IN_PAD = _C_G + LANE

TM_IN = 256
TM_MIX = 256
KC_SEL = 512
SEL_PARTS = 2
BM_MOE = 256
MOE_AHEAD = 2
TM_CMB = 128


def _rms(x, eps=EPS):
    return lax.rsqrt(jnp.mean(x * x, axis=-1, keepdims=True) + eps)


def _inproj_kernel(x_ref, g1_ref, w_ref, qg_ref, kg_ref, vg_ref, fs_ref, fw_ref, qc_ref,
                   q_ref, kc_ref, vc_ref, ks_ref, vs_ref, kw_ref, vw_ref, gate_ref, u_ref, v_ref):
    x = x_ref[...]
    h = (x * _rms(x)) * g1_ref[...]
    z = jnp.dot(h.astype(BF16), w_ref[...], preferred_element_type=F32)

    def head_norm(col0, n, gain, scale):
        outs = []
        for i in range(n):
            sl = z[:, col0 + i * HEAD_DIM: col0 + (i + 1) * HEAD_DIM]
            outs.append((sl * _rms(sl)) * gain * scale)
        return jnp.concatenate(outs, axis=-1)

    qn_t = head_norm(_C_Q, N_HEADS, qg_ref[...], HEAD_DIM ** -0.5 * LOG2E).T.astype(BF16)
    gate_t = jax.nn.sigmoid(z[:, _C_G:_C_G + LANE]).T
    for g in range(N_KV):
        for b in range(x.shape[0] // Q_BLOCK):
            cols = slice(b * Q_BLOCK, (b + 1) * Q_BLOCK)
            q_ref[0, g, b, 0:HEAD_DIM, :] = jnp.concatenate(
                [qn_t[(g * N_REP + r) * HEAD_DIM:(g * N_REP + r + 1) * HEAD_DIM, cols] for r in range(N_REP)], axis=1)
            q_ref[0, g, b, HEAD_DIM:, :] = qc_ref[g]
            gate_ref[0, g, b] = gate_t[g * GATE_ROWS:(g + 1) * GATE_ROWS, cols]
    kc_ref[...] = z[:, _C_KC:_C_KC + D_KV]
    vc_ref[...] = z[:, _C_VC:_C_VC + D_KV]
    for col, gain, feat_ref, k_ref in ((_C_KS, kg_ref[1:2, :], fs_ref, ks_ref), (_C_KW, kg_ref[2:3, :], fw_ref, kw_ref)):
        kn = head_norm(col, N_KV, gain, 1.0).astype(BF16)
        for g in range(N_KV):
            k_ref[0, g] = jnp.concatenate([kn[:, g * HEAD_DIM:(g + 1) * HEAD_DIM], feat_ref[...]], axis=1)
    for col, vt_ref in ((_C_VS, vs_ref), (_C_VW, vw_ref)):
        vt = z[:, col:col + D_KV].T
        for g in range(N_KV):
            vt_ref[0, g] = vt[g * HEAD_DIM:(g + 1) * HEAD_DIM, :].astype(BF16)
    u_ref[...] = jax.nn.gelu(z[:, _C_U:_C_U + D_GM])
    gv = jax.nn.gelu(z[:, _C_V:_C_V + D_GM])
    v_ref[...] = (gv * _rms(gv)) * vg_ref[...]


def _inproj(x2, norm1_g, w_r, q_norm_g, k_norm_g, gm_v_norm_g, feat_s, feat_w, q_coef, batch, seq):
    n = x2.shape[0]
    tps = seq // TM_IN
    qpt = TM_IN // Q_BLOCK
    nq = N_REP * Q_BLOCK
    q_blocks = lambda r, c: pl.BlockSpec((1, N_KV, qpt, r, c), lambda i: (i // tps, 0, i % tps, 0, 0))
    row = lambda c: pl.BlockSpec((TM_IN, c), lambda i: (i, 0))
    full = lambda a: pl.BlockSpec(a.shape, lambda i: (0,) * a.ndim)
    per_seq = lambda a: pl.BlockSpec((TM_IN, a.shape[1]), lambda i: (i % tps, 0))
    keys = lambda w: pl.BlockSpec((1, N_KV, TM_IN, w), lambda i: (i // tps, 0, i % tps, 0))
    vals_t = pl.BlockSpec((1, N_KV, HEAD_DIM, TM_IN), lambda i: (i // tps, 0, 0, i % tps))
    g1 = norm1_g.reshape(1, D_MODEL)
    qg = q_norm_g.reshape(1, HEAD_DIM)
    vg = gm_v_norm_g.reshape(1, D_GM)
    ws, ww = HEAD_DIM + feat_s.shape[1], HEAD_DIM + feat_w.shape[1]
    tok = lambda c: jax.ShapeDtypeStruct((n, c), F32)
    return pl.pallas_call(
        _inproj_kernel,
        grid=(n // TM_IN,),
        in_specs=[row(D_MODEL), full(g1), full(w_r), full(qg), full(k_norm_g), full(vg), per_seq(feat_s),
                  per_seq(feat_w), full(q_coef)],
        out_specs=[q_blocks(LANE, nq), row(D_KV), row(D_KV), keys(ws), vals_t, keys(ww), vals_t,
                   q_blocks(GATE_ROWS, Q_BLOCK), row(D_GM), row(D_GM)],
        out_shape=[jax.ShapeDtypeStruct((batch, N_KV, seq // Q_BLOCK, LANE, nq), BF16), tok(D_KV), tok(D_KV),
                   jax.ShapeDtypeStruct((batch, N_KV, seq, ws), BF16),
                   jax.ShapeDtypeStruct((batch, N_KV, HEAD_DIM, seq), BF16),
                   jax.ShapeDtypeStruct((batch, N_KV, seq, ww), BF16),
                   jax.ShapeDtypeStruct((batch, N_KV, HEAD_DIM, seq), BF16),
                   jax.ShapeDtypeStruct((batch, N_KV, seq // Q_BLOCK, GATE_ROWS, Q_BLOCK), F32),
                   tok(D_GM), tok(D_GM)],
        compiler_params=pltpu.CompilerParams(dimension_semantics=("arbitrary",), vmem_limit_bytes=VMEM_LIMIT),
        name="inproj",
    )(x2, g1, w_r, qg, k_norm_g, vg, feat_s, feat_w, q_coef)


def _compress_kernel(a_ref, pos_ref, w1_ref, w1a_ref, w1b_ref, b1_ref, w2_ref, b2_ref, kg_ref, o_ref, *, norm):
    a = a_ref[0]
    nseg = a.shape[0]
    c = jnp.dot(pos_ref[...], w1_ref[...], precision=HIGHEST, preferred_element_type=F32)[0:1] + b1_ref[...]
    row = lax.broadcasted_iota(jnp.int32, (nseg, 1), 0)
    for g in range(N_KV):
        pa = jnp.dot(a, w1a_ref[g], precision=HIGHEST, preferred_element_type=F32)
        pb = jnp.dot(a, w1b_ref[g], precision=HIGHEST, preferred_element_type=F32)
        hid = jax.nn.gelu(pa + pltpu.roll(pb, nseg - 1, 0) + c)
        out = jnp.dot(hid, w2_ref[...], precision=HIGHEST, preferred_element_type=F32) + b2_ref[...]
        if norm:
            out = (out * _rms(out)) * kg_ref[...]
        o_ref[0, g] = jnp.where(row < nseg - 1, out, 0.0)


def _compress(raw, pos, w1, b1, w2, b2, gain, batch, seq, norm):
    nseg = seq // CMP_STRIDE
    half = CMP_STRIDE * HEAD_DIM
    a = raw.reshape(batch, nseg, CMP_STRIDE * D_KV)
    pos8 = jnp.broadcast_to(pos.reshape(1, CMP_LEN * HEAD_DIM), (8, CMP_LEN * HEAD_DIM))

    def expand(wh):
        wh = wh.reshape(CMP_STRIDE, HEAD_DIM, CMP_HIDDEN)
        z = jnp.zeros((N_KV, CMP_STRIDE, N_KV, HEAD_DIM, CMP_HIDDEN), F32)
        for g in range(N_KV):
            z = z.at[g, :, g].set(wh)
        return z.reshape(N_KV, CMP_STRIDE * D_KV, CMP_HIDDEN)

    w1a, w1b = expand(w1[:half]), expand(w1[half:])
    b1r, b2r, gr = b1.reshape(1, CMP_HIDDEN), b2.reshape(1, HEAD_DIM), gain.reshape(1, HEAD_DIM)
    full = lambda t: pl.BlockSpec(t.shape, lambda i: (0,) * t.ndim)
    return pl.pallas_call(
        functools.partial(_compress_kernel, norm=norm),
        grid=(batch,),
        in_specs=[pl.BlockSpec((1, nseg, CMP_STRIDE * D_KV), lambda i: (i, 0, 0)),
                  full(pos8), full(w1), full(w1a), full(w1b), full(b1r), full(w2), full(b2r), full(gr)],
        out_specs=pl.BlockSpec((1, N_KV, nseg, HEAD_DIM), lambda i: (i, 0, 0, 0)),
        out_shape=jax.ShapeDtypeStruct((batch, N_KV, nseg, HEAD_DIM), F32),
        compiler_params=pltpu.CompilerParams(dimension_semantics=("arbitrary",), vmem_limit_bytes=VMEM_LIMIT),
        name="compress_k" if norm else "compress_v",
    )(a, pos8, w1, w1a, w1b, b1r, w2, b2r, gr)


def _attn_kernel(qt_ref, kc_ref, vct_ref, ks_ref, vst_ref, kw_ref, vwt_ref, g_ref, ovt_ref, o_ref,
                 qs_ref, s0_ref, s1_ref, p0_ref, p1_ref, st_ref, acc_ref):
    qb = pl.program_id(1)
    nq = N_REP * Q_BLOCK
    q0 = qb * Q_BLOCK
    qt = qt_ref[0, 0]
    ql = lax.broadcasted_iota(jnp.int32, (1, nq), 1) % Q_BLOCK
    t_row = (q0 + ql).astype(F32)
    m_init = 0.5 * NEG

    def online(s, m, l):
        m_new = jnp.maximum(m, jnp.max(s, axis=0, keepdims=True))
        alpha = jnp.exp2(m - m_new)
        p = jnp.exp2(s - m_new)
        return p, m_new, alpha, alpha * l + jnp.sum(p, axis=0, keepdims=True)

    def inv(l):
        return jnp.where(l > 0.0, 1.0 / l, 0.0)

    m0 = jnp.full((1, nq), m_init, F32)
    l0 = jnp.zeros((1, nq), F32)
    a0 = jnp.zeros((HEAD_DIM, nq), F32)

    ncmp = kc_ref.shape[1]
    s = jnp.dot(kc_ref[0], qt, preferred_element_type=F32)
    c_end = (lax.broadcasted_iota(jnp.int32, (ncmp, 1), 0) * CMP_STRIDE + (CMP_LEN - 1)).astype(F32)
    p, _, _, l = online(jnp.where(c_end <= t_row, s, NEG), m0, l0)
    p = p * inv(l)
    o_cmp = jnp.dot(vct_ref[0], p.astype(BF16), preferred_element_type=F32)

    psum = p[:, 0:Q_BLOCK]
    for r in range(1, N_REP):
        psum = psum + p[:, r * Q_BLOCK:(r + 1) * Q_BLOCK]
    nsel = ovt_ref.shape[0]
    p_hi = psum.astype(BF16)
    p_lo = (psum - p_hi.astype(F32)).astype(BF16)
    imp = (jnp.dot(ovt_ref[...], p_hi, preferred_element_type=F32)
           + jnp.dot(ovt_ref[...], p_lo, preferred_element_type=F32))
    n_col = lax.broadcasted_iota(jnp.int32, (nsel, 1), 0).astype(F32)
    n_start = n_col * SEL_BLOCK
    tq = t_row[:, 0:Q_BLOCK]
    cur = jnp.floor(tq * (1.0 / SEL_BLOCK)) * SEL_BLOCK
    forced = (n_start == cur) | (n_start == 0.0)
    valid = n_start <= tq
    imp = jnp.where(forced, imp + FORCE_BONUS, imp)
    imp = jnp.where(valid, imp, NEG)
    sel = jnp.zeros((nsel, Q_BLOCK), F32)
    for _ in range(min(N_SEL, nsel)):
        mx = jnp.max(imp, axis=0, keepdims=True)
        first = jnp.min(jnp.where(imp == mx, n_col, float(nsel)), axis=0, keepdims=True)
        hit = n_col == first
        sel = jnp.where(hit, 1.0, sel)
        imp = jnp.where(hit, -jnp.inf, imp)
    selb = jnp.where(valid & (sel > 0.0), 0.0, NEG).astype(BF16)
    qs_ref[0:LANE, :] = qt
    qs_ref[LANE:LANE + nsel, :] = jnp.concatenate([selb] * N_REP, axis=1)
    if qs_ref.shape[0] > LANE + nsel:
        qs_ref[LANE + nsel:, :] = jnp.zeros((qs_ref.shape[0] - LANE - nsel, nq), BF16)

    def attend(k_blk, vt_blk, q_op, bias, carry):
        m, l, acc = carry
        s = jnp.dot(k_blk, q_op, preferred_element_type=F32)
        if bias is not None:
            s = s + bias
        p, m, alpha, l = online(s, m, l)
        pv = jnp.dot(vt_blk, p.astype(BF16), preferred_element_type=F32)
        return m, l, alpha * acc + pv

    seq = ks_ref.shape[1]

    def scores(j):
        k0 = pl.multiple_of(jnp.minimum(j * KC_SEL, seq - KC_SEL), KC_SEL)
        s = jnp.dot(ks_ref[0, pl.ds(k0, KC_SEL), :], qs_ref[...], preferred_element_type=F32)
        return s, jnp.max(s, axis=0, keepdims=True)

    def values(j, p):
        k0 = pl.multiple_of(jnp.maximum(j, 0) * KC_SEL, KC_SEL)
        return jnp.dot(vst_ref[0, :, pl.ds(k0, KC_SEL)], p, preferred_element_type=F32)

    def stage(j, s_cur, s_nxt, p_cur, p_prv):
        m, l, alpha_prev, mx = st_ref[0:1, :], st_ref[1:2, :], st_ref[2:3, :], st_ref[3:4, :]
        m_new = jnp.maximum(m, mx)
        alpha = jnp.exp2(m - m_new)
        k0 = pl.multiple_of(jnp.minimum((j + 1) * KC_SEL, seq - KC_SEL), KC_SEL)
        k0p = pl.multiple_of(jnp.maximum(j - 1, 0) * KC_SEL, KC_SEL)
        sub = KC_SEL // SEL_PARTS
        psum, mx_next, zeros = None, None, []
        for q in range(SEL_PARTS):
            rows = slice(q * sub, (q + 1) * sub)
            k_q = ks_ref[0, pl.ds(k0 + q * sub, sub), :]
            if q >= 1:
                k_q = k_q + jnp.concatenate([zeros[q - 1]] * (ks_ref.shape[2] // LANE), axis=1)
            s_q = jnp.dot(k_q, qs_ref[...], preferred_element_type=F32)
            s_nxt[rows, :] = s_q
            mx_q = jnp.max(s_q, axis=0, keepdims=True)
            mx_next = mx_q if mx_next is None else jnp.maximum(mx_next, mx_q)
            p_q = jnp.exp2(s_cur[rows, :] - m_new)
            ps_q = jnp.sum(p_q, axis=0, keepdims=True)
            psum = ps_q if psum is None else psum + ps_q
            p_q = p_q.astype(BF16)
            p_cur[rows, :] = p_q
            dep = ps_q[:, 0:LANE]
            for r in range(1, N_REP):
                dep = dep + ps_q[:, r * LANE:(r + 1) * LANE]
            bits = pltpu.bitcast(dep, jnp.int32)
            zeros.append(lax.shift_right_logical(lax.shift_right_logical(bits, 16), 16).astype(F32).astype(BF16))
            if q == SEL_PARTS // 2 - 1:
                vt_prev = vst_ref[0, :, pl.ds(k0p, KC_SEL)] + jnp.concatenate([zeros[q]] * (KC_SEL // LANE), axis=1)
                acc_ref[...] = alpha_prev * acc_ref[...] + jnp.dot(vt_prev, p_prv[...], preferred_element_type=F32)
        st_ref[0:1, :] = m_new
        st_ref[1:2, :] = alpha * l + psum
        st_ref[2:3, :] = alpha
        st_ref[3:4, :] = mx_next

    n_full = q0 // KC_SEL
    s_first, mx_first = scores(0)

    @pl.when(n_full % 2 == 0)
    def _():
        s0_ref[...] = s_first

    @pl.when(n_full % 2 == 1)
    def _():
        s1_ref[...] = s_first

    p0_ref[...] = jnp.zeros(p0_ref.shape, BF16)
    p1_ref[...] = jnp.zeros(p1_ref.shape, BF16)
    st_ref[0:1, :] = m0
    st_ref[1:2, :] = l0
    st_ref[2:3, :] = jnp.ones((1, nq), F32)
    st_ref[3:4, :] = mx_first
    acc_ref[...] = a0

    def sel_body(j, carry):
        @pl.when((n_full - j) % 2 == 0)
        def _():
            stage(j, s0_ref, s1_ref, p0_ref, p1_ref)

        @pl.when((n_full - j) % 2 == 1)
        def _():
            stage(j, s1_ref, s0_ref, p1_ref, p0_ref)
        return carry

    lax.fori_loop(0, n_full, sel_body, 0)
    pos_last = (n_full * KC_SEL + lax.broadcasted_iota(jnp.int32, (KC_SEL, 1), 0)).astype(F32)
    p, _, alpha, l_sel = online(s0_ref[...] + jnp.where(pos_last <= t_row, 0.0, NEG), st_ref[0:1, :], st_ref[1:2, :])
    acc = st_ref[2:3, :] * acc_ref[...] + values(n_full - 1, p1_ref[...])
    o_sel = alpha * acc + values(n_full, p.astype(BF16))

    n_wk = WINDOW + Q_BLOCK
    w0 = pl.multiple_of(jnp.maximum(q0 - WINDOW, 0), Q_BLOCK)
    kk = lax.broadcasted_iota(jnp.int32, (n_wk, 1), 0) - (q0 - w0)
    in_win = (kk <= ql) & (kk + WINDOW > ql)
    _, l_win, o_win = attend(kw_ref[0, pl.ds(w0, n_wk), :], vwt_ref[0, :, pl.ds(w0, n_wk)], qt,
                             jnp.where(in_win, 0.0, NEG), (m0, l0, a0))

    gt = g_ref[0, 0]

    def gate(j):
        return jnp.concatenate([gt[j * N_REP + r:j * N_REP + r + 1, :] for r in range(N_REP)], axis=1)

    o_ref[0, 0] = (gate(0) * o_cmp + gate(1) * (o_sel * inv(l_sel)) + gate(2) * (o_win * inv(l_win)))


def _attention(qt, kc, vct, ks, vst, kw, vwt, gt, ovt):
    bgn, nqb = qt.shape[0], qt.shape[1]
    seq = ks.shape[1]
    nq = N_REP * Q_BLOCK
    per_bg = lambda a: pl.BlockSpec((1,) + a.shape[1:], lambda b, i: (b,) + (0,) * (a.ndim - 1))
    return pl.pallas_call(
        _attn_kernel,
        grid=(bgn, nqb),
        in_specs=[pl.BlockSpec((1, 1, LANE, nq), lambda b, i: (b, i, 0, 0)),
                  per_bg(kc), per_bg(vct), per_bg(ks), per_bg(vst), per_bg(kw), per_bg(vwt),
                  pl.BlockSpec((1, 1, GATE_ROWS, Q_BLOCK), lambda b, i: (b, i, 0, 0)),
                  pl.BlockSpec(ovt.shape, lambda b, i: (0, 0))],
        out_specs=pl.BlockSpec((1, 1, HEAD_DIM, nq), lambda b, i: (b, i, 0, 0)),
        out_shape=jax.ShapeDtypeStruct((bgn, nqb, HEAD_DIM, nq), F32),
        scratch_shapes=[pltpu.VMEM((ks.shape[2], nq), BF16),
                        pltpu.VMEM((KC_SEL, nq), F32), pltpu.VMEM((KC_SEL, nq), F32),
                        pltpu.VMEM((KC_SEL, nq), BF16), pltpu.VMEM((KC_SEL, nq), BF16),
                        pltpu.VMEM((8, nq), F32), pltpu.VMEM((HEAD_DIM, nq), F32)],
        compiler_params=pltpu.CompilerParams(dimension_semantics=("arbitrary", "arbitrary"),
                                             vmem_limit_bytes=VMEM_LIMIT),
        name="nsa_attention",
    )(qt, kc, vct, ks, vst, kw, vwt, gt, ovt)


def _mix_kernel(x_ref, oa_ref, u_ref, v_ref, ws_ref, bs_ref, ga_ref, gg_ref, wo_ref, g2_ref, wr_ref, br_ref,
                x1_ref, xn_ref, idx_ref, gate_ref, rank_ref, cnt_out_ref, cnt_ref):
    tm = x_ref.shape[0]
    rr = lax.broadcasted_iota(jnp.int32, (GM_CHUNK, GM_CHUNK), 0)
    cc = lax.broadcasted_iota(jnp.int32, (GM_CHUNK, GM_CHUNK), 1)
    grp = lax.broadcasted_iota(jnp.int32, (1, D_GM), 1) // GM_GROUP_DIM
    ws = [jnp.where(rr >= cc, ws_ref[g], 0.0).astype(BF16) for g in range(N_GM_GROUPS)]
    ys = []
    for c in range(tm // GM_CHUNK):
        vch = v_ref[c * GM_CHUNK:(c + 1) * GM_CHUNK, :].astype(BF16)
        y = bs_ref[...]
        for g in range(N_GM_GROUPS):
            y = y + jnp.where(grp == g, jnp.dot(ws[g], vch, preferred_element_type=F32), 0.0)
        ys.append(y)
    o_gm = u_ref[...] * jnp.concatenate(ys, axis=0)
    o_at = jnp.concatenate(
        [jnp.concatenate([oa_ref[0, g, b, :, r * Q_BLOCK:(r + 1) * Q_BLOCK]
                          for g in range(N_KV) for r in range(N_REP)], axis=0).T
         for b in range(tm // Q_BLOCK)], axis=0)
    mixed =jnp.concatenate([(o_at * _rms(o_at)) * ga_ref[...], (o_gm * _rms(o_gm)) * gg_ref[...]], axis=-1)
    x1 = x_ref[...] + jnp.dot(mixed.astype(BF16), wo_ref[...], preferred_element_type=F32)
    x1_ref[...] = x1
    xn = (x1 * _rms(x1)) * g2_ref[...]
    for s in range(ROW_SUB):
        xn_ref[:, s, :] = xn[:, s * LANE:(s + 1) * LANE]
    logits = jnp.dot(xn, wr_ref[...], precision=HIGHEST, preferred_element_type=F32) + br_ref[...]
    lane = lax.broadcasted_iota(jnp.int32, (1, LANE), 1).astype(F32)
    idx_out = jnp.zeros((tm, LANE), F32)
    val_out = jnp.zeros((tm, LANE), F32)
    vals, firsts = [], []
    for k in range(TOP_K):
        mx = jnp.max(logits, axis=-1, keepdims=True)
        first = jnp.min(jnp.where(logits == mx, lane, float(LANE)), axis=-1, keepdims=True)
        logits = jnp.where(lane == first, -jnp.inf, logits)
        idx_out = jnp.where(lane == float(k), first, idx_out)
        vals.append(mx)
        firsts.append(first)
    es = [jnp.exp(v - vals[0]) for v in vals]
    den = es[0] + es[1] + es[2] + es[3]
    for k in range(TOP_K):
        val_out = jnp.where(lane == float(k), es[k] / den, val_out)
    idx_ref[...] = idx_out.astype(jnp.int32)
    gate_ref[...] = val_out

    @pl.when(pl.program_id(0) == 0)
    def _():
        cnt_ref[...] = jnp.zeros(cnt_ref.shape, F32)

    hit = (lane == firsts[0]) | (lane == firsts[1]) | (lane == firsts[2]) | (lane == firsts[3])
    hit_b = jnp.where(hit, 1.0, 0.0).astype(BF16)
    tr = lax.broadcasted_iota(jnp.int32, (tm, tm), 0)
    tc = lax.broadcasted_iota(jnp.int32, (tm, tm), 1)
    before = jnp.where(tr > tc, 1.0, 0.0).astype(BF16)
    ranks = jnp.dot(before, hit_b, preferred_element_type=F32) + cnt_ref[0:1, :]
    rank_out = jnp.zeros((tm, LANE), F32)
    for k in range(TOP_K):
        r_k = jnp.sum(jnp.where(lane == firsts[k], ranks, 0.0), axis=-1, keepdims=True)
        rank_out = jnp.where(lane == float(k), r_k, rank_out)
    rank_ref[...] = rank_out.astype(jnp.int32)
    cnt_ref[0:1, :] = cnt_ref[0:1, :] + jnp.sum(hit_b.astype(F32), axis=0, keepdims=True)
    cnt_out_ref[...] = cnt_ref[...].astype(jnp.int32)


def _mix(x2, ot, u_act, v_act, gm_w_s, bias_full, ga, gg, w_out_b, g2, wr_pad, br_pad):
    n = x2.shape[0]
    tps = ot.shape[2] * Q_BLOCK // TM_MIX
    row = lambda c: pl.BlockSpec((TM_MIX, c), lambda i: (i, 0))
    full = lambda a: pl.BlockSpec(a.shape, lambda i: (0,) * a.ndim)
    attn_blocks = pl.BlockSpec((1, N_KV, TM_MIX // Q_BLOCK) + ot.shape[3:], lambda i: (i // tps, 0, i % tps, 0, 0))
    return pl.pallas_call(
        _mix_kernel,
        grid=(n // TM_MIX,),
        in_specs=[row(D_MODEL), attn_blocks, row(D_GM), row(D_GM), full(gm_w_s), full(bias_full), full(ga), full(gg),
                  full(w_out_b), full(g2), full(wr_pad), full(br_pad)],
        out_specs=[row(D_MODEL), pl.BlockSpec((TM_MIX, ROW_SUB, LANE), lambda i: (i, 0, 0)), row(LANE), row(LANE),
                   row(LANE), pl.BlockSpec((ROW_SUB, LANE), lambda i: (0, 0))],
        out_shape=[jax.ShapeDtypeStruct((n, D_MODEL), F32), jax.ShapeDtypeStruct((n, ROW_SUB, LANE), F32),
                   jax.ShapeDtypeStruct((n, LANE), jnp.int32), jax.ShapeDtypeStruct((n, LANE), F32),
                   jax.ShapeDtypeStruct((n, LANE), jnp.int32), jax.ShapeDtypeStruct((ROW_SUB, LANE), jnp.int32)],
        scratch_shapes=[pltpu.VMEM((ROW_SUB, LANE), F32)],
        compiler_params=pltpu.CompilerParams(dimension_semantics=("arbitrary",), vmem_limit_bytes=VMEM_LIMIT),
        name="mix_outproj_router",
    )(x2, ot, u_act, v_act, gm_w_s, bias_full, ga, gg, w_out_b, g2, wr_pad, br_pad)


def _row_gather(idx_ref, n_rows, src_hbm, dst_ref, sem):
    def start():
        for r in range(n_rows):
            pltpu.make_async_copy(src_hbm.at[pl.ds(idx_ref[0, 0, r], 1), :], dst_ref.at[pl.ds(r, 1), :], sem).start()

    def wait():
        pltpu.make_async_copy(src_hbm.at[pl.ds(0, n_rows), :], dst_ref, sem).wait()

    return start, wait


def _tile_row_gather(idx_ref, n_rows, src_hbm, dst_ref, sem):
    def start():
        for r in range(n_rows):
            t = idx_ref[0, 0, r]
            pltpu.make_async_copy(src_hbm.at[lax.shift_right_logical(t, 3), t & (ROW_SUB - 1)],
                                  dst_ref.at[r // ROW_SUB, :, r % ROW_SUB, :], sem).start(priority=r % 2)

    def wait():
        pltpu.make_async_copy(src_hbm.at[pl.ds(0, n_rows // ROW_SUB)], dst_ref, sem).wait()

    return start, wait


def _tiles_to_matrix(ref):
    rows = ref.shape[0] * ROW_SUB
    return jnp.concatenate([ref[:, c].reshape(rows, LANE) for c in range(ROW_SUB)], axis=1)


def _moe_kernel(be_ref, bv_ref, bn_ref, bs_ref, tok_ref, tok_n1_ref, tok_n2_ref, x_hbm, wgu_hbm, bg_ref, bl_ref,
                wd_hbm, bd_ref, o_ref, xbuf, sems, wt_s, wg_s, wl_s, wd_s, wgu_buf, wd_buf, wsems):
    i = pl.program_id(0)
    slot = i % (MOE_AHEAD + 1)
    slot_n2 = (i + MOE_AHEAD) % (MOE_AHEAD + 1)
    start_cur, wait_cur = _tile_row_gather(tok_ref, BM_MOE, x_hbm, xbuf.at[slot], sems.at[slot])
    start_n1, _ = _tile_row_gather(tok_n1_ref, BM_MOE, x_hbm, xbuf.at[1], sems.at[1])
    start_n2, _ = _tile_row_gather(tok_n2_ref, BM_MOE, x_hbm, xbuf.at[slot_n2], sems.at[slot_n2])

    prev = jnp.maximum(i - 1, 0)

    def fetch_weights(e, ws):
        n_piece = 4
        rows = D_MODEL // n_piece
        copies = [pltpu.make_async_copy(wgu_hbm.at[e, pl.ds(c * rows, rows)], wgu_buf.at[ws, pl.ds(c * rows, rows)],
                                        wsems.at[ws]) for c in range(n_piece)]
        copies += [pltpu.make_async_copy(wd_hbm.at[e, pl.ds(c * rows, rows)], wd_buf.at[ws, pl.ds(c * rows, rows)],
                                         wsems.at[ws]) for c in range(n_piece)]
        return copies

    @pl.when(i == 0)
    def _():
        start_cur()
        start_n1()
        for cp in fetch_weights(be_ref[0], 0):
            cp.start()

    @pl.when((bv_ref[i] == 1) & ((i == 0) | (be_ref[i] != be_ref[prev])))
    def _():
        ws = bs_ref[i]
        for cp in fetch_weights(be_ref[i], ws):
            cp.wait()

        @pl.when(bn_ref[i] >= 0)
        def _():
            for cp in fetch_weights(bn_ref[i], 1 - ws):
                cp.start()

        tc = wt_s.shape[1]
        for c in range(2 * D_EXPERT // tc):
            wt = wgu_buf[ws, :, c * tc:(c + 1) * tc].T
            for j in range(ROW_SUB):
                wt_s[j] = wt[:, j * LANE:(j + 1) * LANE]
            for first, dst in ((0, wg_s), (1, wl_s)):
                half = jnp.concatenate([wt_s[j, pl.ds(first, tc // 2, stride=2), :] for j in range(ROW_SUB)], axis=1)
                dst[c * tc // 2:(c + 1) * tc // 2, :] = half.astype(BF16)
        wd_s[...] = wd_buf[ws].astype(BF16)

    @pl.when(bv_ref[i] == 1)
    def _():
        wait_cur()
        start_n2()
        xb = _tiles_to_matrix(xbuf.at[slot]).astype(BF16)
        hg = lax.dot_general(xb, wg_s[...], _NT, preferred_element_type=F32) + bg_ref[0]
        hl = lax.dot_general(xb, wl_s[...], _NT, preferred_element_type=F32) + bl_ref[0]
        hg = jnp.minimum(hg, SWIGLU_LIMIT)
        hl = jnp.clip(hl, -SWIGLU_LIMIT, SWIGLU_LIMIT)
        a = hg * jax.nn.sigmoid(SWIGLU_ALPHA * hg) * (hl + 1.0)
        o_ref[...] = jnp.dot(a.astype(BF16), wd_s[...], preferred_element_type=F32) + bd_ref[0]

    @pl.when((bv_ref[i] == 0) & ((i == 1) | ((i >= MOE_AHEAD) & (bv_ref[jnp.maximum(i - MOE_AHEAD, 0)] == 1))))
    def _():
        wait_cur()

    @pl.when(bv_ref[i] == 0)
    def _():
        o_ref[...] = jnp.zeros(o_ref.shape, F32)


def _moe(blk_expert, blk_valid, blk_next, blk_wslot, tok_blocks, xn3, w_gate_up, bg, bl, w_down, bd):
    nb = blk_expert.shape[0]
    per_e = lambda a: pl.BlockSpec((1,) + a.shape[1:], lambda i, be, *_: (be[i],) + (0,) * (a.ndim - 1))

    def tok_spec(ahead):
        return pl.BlockSpec((1, 1, BM_MOE), lambda i, *_: (jnp.minimum(i + ahead, nb - 1), 0, 0),
                            memory_space=pltpu.SMEM)

    grid_spec = pltpu.PrefetchScalarGridSpec(
        num_scalar_prefetch=4,
        grid=(nb,),
        in_specs=[tok_spec(0), tok_spec(1), tok_spec(MOE_AHEAD),
                  pl.BlockSpec(memory_space=pl.ANY),
                  pl.BlockSpec(memory_space=pl.ANY), per_e(bg), per_e(bl),
                  pl.BlockSpec(memory_space=pl.ANY), per_e(bd)],
        out_specs=pl.BlockSpec((BM_MOE, D_MODEL), lambda i, *_: (i, 0)),
        scratch_shapes=[pltpu.VMEM((MOE_AHEAD + 1, BM_MOE // ROW_SUB, ROW_SUB, ROW_SUB, LANE), F32),
                        pltpu.SemaphoreType.DMA((MOE_AHEAD + 1,)),
                        pltpu.VMEM((ROW_SUB, 256, LANE), F32), pltpu.VMEM((D_EXPERT, D_MODEL), BF16),
                        pltpu.VMEM((D_EXPERT, D_MODEL), BF16), pltpu.VMEM((D_EXPERT, D_MODEL), BF16),
                        pltpu.VMEM((2, D_MODEL, 2 * D_EXPERT), F32), pltpu.VMEM((2, D_EXPERT, D_MODEL), F32),
                        pltpu.SemaphoreType.DMA((2,))],
    )
    return pl.pallas_call(
        _moe_kernel,
        grid_spec=grid_spec,
        out_shape=jax.ShapeDtypeStruct((nb * BM_MOE, D_MODEL), F32),
        compiler_params=pltpu.CompilerParams(dimension_semantics=("arbitrary",), vmem_limit_bytes=VMEM_LIMIT_MOE),
        name="moe_experts",
    )(blk_expert, blk_valid, blk_next, blk_wslot, tok_blocks, tok_blocks, tok_blocks,
      xn3.reshape(xn3.shape[0] // ROW_SUB, ROW_SUB, ROW_SUB, LANE), w_gate_up, bg, bl, w_down, bd)


def _slot_table_kernel(lo_ref, hi_ref, dest_ref, tok_ref):
    i = pl.program_id(0)
    n_chunk = dest_ref.shape[2]

    @pl.when(i == 0)
    def _():
        def clear(s, carry):
            tok_ref[s] = 0
            return carry
        for e in range(lo_ref.shape[0]):
            lax.fori_loop(lo_ref[e], hi_ref[e], clear, 0)

    def put(s, carry):
        tok_ref[dest_ref[0, 0, s]] = lax.shift_right_logical(i * n_chunk + s, TOP_K.bit_length() - 1)
        return carry
    lax.fori_loop(0, n_chunk, put, 0, unroll=8)


def _slot_table(pad_lo, pad_hi, dest, n_slots):
    n_chunk = 8192
    s_tot = dest.shape[0]
    grid_spec = pltpu.PrefetchScalarGridSpec(
        num_scalar_prefetch=2,
        grid=(s_tot // n_chunk,),
        in_specs=[pl.BlockSpec((1, 1, n_chunk), lambda i, *_: (i, 0, 0), memory_space=pltpu.SMEM)],
        out_specs=pl.BlockSpec(memory_space=pltpu.SMEM),
    )
    return pl.pallas_call(
        _slot_table_kernel,
        grid_spec=grid_spec,
        out_shape=jax.ShapeDtypeStruct((n_slots,), jnp.int32),
        compiler_params=pltpu.CompilerParams(dimension_semantics=("arbitrary",)),
        name="moe_slot_table",
    )(pad_lo, pad_hi, dest.reshape(s_tot // n_chunk, 1, n_chunk))


def _combine_kernel(dest_ref, dest_next_ref, x1_ref, gate_ref, y_hbm, o_ref, buf, sems):
    i = pl.program_id(0)
    slot = i % 2
    n_rows = TOP_K * TM_CMB
    start_cur, wait_cur = _row_gather(dest_ref, n_rows, y_hbm, buf.at[slot], sems.at[slot])
    start_next, _ = _row_gather(dest_next_ref, n_rows, y_hbm, buf.at[1 - slot], sems.at[1 - slot])

    @pl.when(i == 0)
    def _():
        start_cur()

    @pl.when(i + 1 < pl.num_programs(0))
    def _():
        start_next()

    wait_cur()
    gate = gate_ref[...]
    acc = x1_ref[...]
    for k in range(TOP_K):
        acc = acc + gate[:, k:k + 1] * buf[slot, k * TM_CMB:(k + 1) * TM_CMB, :]
    o_ref[...] = acc


def _combine(dest_blocks, x1, gate_pad, y_rows):
    n = x1.shape[0]
    nt = n // TM_CMB
    n_rows = TOP_K * TM_CMB
    return pl.pallas_call(
        _combine_kernel,
        grid=(nt,),
        in_specs=[pl.BlockSpec((1, 1, n_rows), lambda i: (i, 0, 0), memory_space=pltpu.SMEM),
                  pl.BlockSpec((1, 1, n_rows), lambda i: (jnp.minimum(i + 1, nt - 1), 0, 0),
                               memory_space=pltpu.SMEM),
                  pl.BlockSpec((TM_CMB, D_MODEL), lambda i: (i, 0)),
                  pl.BlockSpec((TM_CMB, LANE), lambda i: (i, 0)),
                  pl.BlockSpec(memory_space=pl.ANY)],
        out_specs=pl.BlockSpec((TM_CMB, D_MODEL), lambda i: (i, 0)),
        out_shape=jax.ShapeDtypeStruct((n, D_MODEL), F32),
        scratch_shapes=[pltpu.VMEM((2, n_rows, D_MODEL), F32), pltpu.SemaphoreType.DMA((2,))],
        compiler_params=pltpu.CompilerParams(dimension_semantics=("arbitrary",), vmem_limit_bytes=VMEM_LIMIT),
        name="moe_combine",
    )(dest_blocks, dest_blocks, x1, gate_pad, y_rows)


def kernel(x, norm1_g, w_in, q_norm_g, k_norm_g, cmp_pos, w_cmp1, b_cmp1, w_cmp2, b_cmp2, gm_v_norm_g, gm_w_s,
           gm_b_s, out_norm_attn_g, out_norm_gm_g, w_out, norm2_g, w_router, b_router, w_gate_up, b_gate_up,
           w_down, b_down):
    batch, seq, _ = x.shape
    n = batch * seq
    nqb = seq // Q_BLOCK
    bgn = batch * N_KV
    x2 = x.reshape(n, D_MODEL)

    c_gate = D_ATTN + 6 * D_KV
    gate_src = np.full((LANE,), N_GATE, np.int32)
    for h in range(N_HEADS):
        for j in range(3):
            gate_src[(h // N_REP) * GATE_ROWS + j * N_REP + h % N_REP] = h * 3 + j
    w_gate = jnp.concatenate([w_in[:, c_gate:c_gate + N_GATE], jnp.zeros((D_MODEL, 1), F32)], axis=1)[:, gate_src]
    w_r = jnp.concatenate([w_in[:, :c_gate], w_in[:, c_gate + N_GATE:], w_gate], axis=1).astype(BF16)
    nq = N_REP * Q_BLOCK
    head = np.arange(N_KV)[:, None] * N_REP + np.arange(nq)[None, :] // Q_BLOCK
    coef = np.exp2(-(head + 1.0)) * LOG2E
    c_hi = coef.astype(BF16).astype(np.float64)
    c_lo = (coef - c_hi).astype(BF16).astype(np.float64)
    qrows = np.zeros((N_KV, LANE - HEAD_DIM, nq), np.float32)
    qrows[:, 0], qrows[:, 1], qrows[:, 2], qrows[:, 3] = SEL_BLOCK * c_hi, SEL_BLOCK * c_lo, c_hi, c_lo
    nsel = seq // SEL_BLOCK
    oh_w = -(-nsel // LANE) * LANE

    def pos_features(pos, one_hot=False):
        feat = np.zeros((pos.shape[0], LANE - HEAD_DIM + (oh_w if one_hot else 0)), np.float32)
        feat[:, 0] = feat[:, 1] = pos // SEL_BLOCK
        feat[:, 2] = feat[:, 3] = pos % SEL_BLOCK
        if one_hot:
            feat[np.arange(pos.shape[0]), LANE - HEAD_DIM + pos // SEL_BLOCK] = 1.0
        return jnp.asarray(feat, BF16)

    pos_t = np.arange(seq)
    qt, kc_raw, vc_raw, ks_aug, vst, kw_aug, vwt, gt, u_act, v_act = _inproj(
        x2, norm1_g, w_r, q_norm_g, k_norm_g, gm_v_norm_g, pos_features(pos_t, one_hot=True), pos_features(pos_t),
        jnp.asarray(qrows, BF16), batch, seq)

    kc = _compress(kc_raw, cmp_pos[0], w_cmp1[0], b_cmp1[0], w_cmp2[0], b_cmp2[0], k_norm_g[0], batch, seq, True)
    vc = _compress(vc_raw, cmp_pos[1], w_cmp1[1], b_cmp1[1], w_cmp2[1], b_cmp2[1], k_norm_g[0], batch, seq, False)

    ncmp = seq // CMP_STRIDE
    pos_c = np.arange(ncmp) * CMP_STRIDE + (CMP_LEN - 1)
    kc_b = jnp.concatenate([kc.reshape(bgn, ncmp, HEAD_DIM).astype(BF16),
                            jnp.broadcast_to(pos_features(pos_c)[None], (bgn, ncmp, LANE - HEAD_DIM))], axis=-1)
    vct = vc.reshape(bgn, ncmp, HEAD_DIM).transpose(0, 2, 1).astype(BF16)
    c0 = np.arange(ncmp)[None, :] * CMP_STRIDE
    n0 = np.arange(seq // SEL_BLOCK)[:, None] * SEL_BLOCK
    ovt = np.clip(np.minimum(c0 + CMP_LEN, n0 + SEL_BLOCK) - np.maximum(c0, n0), 0, None) / CMP_LEN
    per_group = lambda a: a.reshape((bgn,) + a.shape[2:])
    ot = _attention(per_group(qt), kc_b, vct, per_group(ks_aug), per_group(vst), per_group(kw_aug),
                    per_group(vwt), per_group(gt), jnp.asarray(ovt, BF16))
    ot = ot.reshape((batch, N_KV) + ot.shape[1:])

    bias_full = jnp.repeat(gm_b_s.T, GM_GROUP_DIM, axis=1)
    wr_pad = jnp.concatenate([w_router, jnp.zeros((D_MODEL, LANE - N_EXPERTS), F32)], axis=1)
    br_pad = jnp.concatenate([b_router, jnp.full((LANE - N_EXPERTS,), NEG, F32)]).reshape(1, LANE)
    x1, xn3, idx_pad, gate_pad, rank_pad, cnt_pad = _mix(
        x2, ot, u_act, v_act, gm_w_s, bias_full, out_norm_attn_g.reshape(1, D_ATTN),
        out_norm_gm_g.reshape(1, D_GM), w_out.astype(BF16), norm2_g.reshape(1, D_MODEL), wr_pad, br_pad)

    s_tot = n * TOP_K
    nb = s_tot // BM_MOE + N_EXPERTS - 1 + MOE_AHEAD
    e_flat = idx_pad[:, :TOP_K].reshape(s_tot)
    rank = rank_pad[:, :TOP_K].reshape(s_tot)
    counts = cnt_pad[0, :N_EXPERTS]
    padded = ((counts + BM_MOE - 1) // BM_MOE) * BM_MOE
    pad_end = jnp.cumsum(padded)
    pad_start = pad_end - padded
    dest = pad_start[e_flat] + rank
    pad_lo = jnp.concatenate([pad_start + counts, pad_end[-1:]]).astype(jnp.int32)
    pad_hi = jnp.concatenate([pad_end, jnp.full((1,), nb * BM_MOE)]).astype(jnp.int32)
    tok_buf = _slot_table(pad_lo, pad_hi, dest.astype(jnp.int32), nb * BM_MOE)
    blk_start = jnp.arange(nb, dtype=jnp.int32) * BM_MOE
    blk_expert = jnp.minimum(jnp.sum((blk_start[:, None] >= pad_end[None, :]).astype(jnp.int32), axis=1),
                             N_EXPERTS - 1)
    blk_valid = (blk_start < pad_end[-1]).astype(jnp.int32)
    e_ids = jnp.arange(N_EXPERTS, dtype=jnp.int32)
    present = counts > 0
    ordinal = jnp.cumsum(present.astype(jnp.int32)) - 1
    later = jnp.where(present[None, :] & (e_ids[None, :] > e_ids[:, None]), e_ids[None, :], N_EXPERTS)
    nxt = jnp.min(later, axis=1)
    blk_next = jnp.where(nxt < N_EXPERTS, nxt, -1)[blk_expert].astype(jnp.int32)
    blk_wslot = (ordinal[blk_expert] % 2).astype(jnp.int32)

    bg = b_gate_up[:, 0::2].reshape(N_EXPERTS, 1, D_EXPERT)
    bl = b_gate_up[:, 1::2].reshape(N_EXPERTS, 1, D_EXPERT)
    y_rows = _moe(blk_expert, blk_valid, blk_next, blk_wslot, tok_buf.reshape(nb, 1, BM_MOE), xn3, w_gate_up, bg, bl,
                  w_down,
                  b_down.reshape(N_EXPERTS, 1, D_MODEL))

    dest_blocks = (dest.reshape(n // TM_CMB, TM_CMB, TOP_K).transpose(0, 2, 1)
                   .reshape(n // TM_CMB, 1, TOP_K * TM_CMB).astype(jnp.int32))
    out = _combine(dest_blocks, x1, gate_pad, y_rows)
    return out.reshape(batch, seq, D_MODEL)
```

```python
import functools

import jax
import jax.numpy as jnp
import numpy as np
from jax import lax
from jax.experimental import pallas as pl
from jax.experimental.pallas import tpu as pltpu

F32 = jnp.float32
BF16 = jnp.bfloat16
HIGHEST = lax.Precision.HIGHEST
_NT = (((1,), (1,)), ((), ()))

D_MODEL = 1024
N_HEADS = 8
HEAD_DIM = 64
N_KV = 2
N_REP = N_HEADS // N_KV
D_ATTN = N_HEADS * HEAD_DIM
D_KV = N_KV * HEAD_DIM
N_GM_GROUPS = 8
GM_GROUP_DIM = 64
D_GM = N_GM_GROUPS * GM_GROUP_DIM
N_GATE = 3 * N_HEADS
CMP_LEN = 32
CMP_STRIDE = 16
CMP_HIDDEN = 128
SEL_BLOCK = 64
N_SEL = 16
WINDOW = 512
Q_BLOCK = 128
FORCE_BONUS = 1.0e4
GM_CHUNK = 128
N_EXPERTS = 32
TOP_K = 4
D_EXPERT = 1024
SWIGLU_LIMIT = 7.0
SWIGLU_ALPHA = 1.702
EPS = 1e-6
NEG = -1.0e30
LOG2E = 1.4426950408889634

LANE = 128
ROW_SUB = D_MODEL // LANE
GATE_ROWS = 16
VMEM_LIMIT = 48 * 1024 * 1024
VMEM_LIMIT_MOE = 56 * 1024 * 1024

_C_Q = 0
_C_KC = _C_Q + D_ATTN
_C_VC = _C_KC + D_KV
_C_KS = _C_VC + D_KV
_C_VS = _C_KS + D_KV
_C_KW = _C_VS + D_KV
_C_VW = _C_KW + D_KV
_C_U = _C_VW + D_KV
_C_V = _C_U + D_GM
_C_G = _C_V + D_GM
D_IN_PAD = _C_G + LANE

TM_IN = 256
TM_MIX = 256
KC_SEL = 512
SEL_PARTS = 8
BM_MOE = 256
MOE_AHEAD = 2
TM_CMB = 128


def _rms(x, eps=EPS):
    return lax.rsqrt(jnp.mean(x * x, axis=-1, keepdims=True) + eps)


def _inproj_kernel(x_ref, g1_ref, w_ref, qg_ref, kg_ref, vg_ref, fs_ref, fw_ref, qc_ref,
                   q_ref, kc_ref, vc_ref, ks_ref, vs_ref, kw_ref, vw_ref, gate_ref, u_ref, v_ref):
    x = x_ref[...]
    h = (x * _rms(x)) * g1_ref[...]
    z = jnp.dot(h.astype(BF16), w_ref[...], preferred_element_type=F32)

    def head_norm(col0, n, gain, scale):
        outs = []
        for i in range(n):
            sl = z[:, col0 + i * HEAD_DIM: col0 + (i + 1) * HEAD_DIM]
            outs.append((sl * _rms(sl)) * gain * scale)
        return jnp.concatenate(outs, axis=-1)

    qn_t = head_norm(_C_Q, N_HEADS, qg_ref[...], HEAD_DIM ** -0.5 * LOG2E).T.astype(BF16)
    gate_t = jax.nn.sigmoid(z[:, _C_G:_C_G + LANE]).T
    for g in range(N_KV):
        for b in range(x.shape[0] // Q_BLOCK):
            cols = slice(b * Q_BLOCK, (b + 1) * Q_BLOCK)
            q_ref[0, g, b, 0:HEAD_DIM, :] = jnp.concatenate(
                [qn_t[(g * N_REP + r) * HEAD_DIM:(g * N_REP + r + 1) * HEAD_DIM, cols] for r in range(N_REP)], axis=1)
            q_ref[0, g, b, HEAD_DIM:, :] = qc_ref[g]
            gate_ref[0, g, b] = gate_t[g * GATE_ROWS:(g + 1) * GATE_ROWS, cols]
    kc_ref[...] = z[:, _C_KC:_C_KC + D_KV]
    vc_ref[...] = z[:, _C_VC:_C_VC + D_KV]
    for col, gain, feat_ref, k_ref in ((_C_KS, kg_ref[1:2, :], fs_ref, ks_ref), (_C_KW, kg_ref[2:3, :], fw_ref, kw_ref)):
        kn = head_norm(col, N_KV, gain, 1.0).astype(BF16)
        for g in range(N_KV):
            k_ref[0, g] = jnp.concatenate([kn[:, g * HEAD_DIM:(g + 1) * HEAD_DIM], feat_ref[...]], axis=1)
    for col, vt_ref in ((_C_VS, vs_ref), (_C_VW, vw_ref)):
        vt = z[:, col:col + D_KV].T
        for g in range(N_KV):
            vt_ref[0, g] = vt[g * HEAD_DIM:(g + 1) * HEAD_DIM, :].astype(BF16)
    u_ref[...] = jax.nn.gelu(z[:, _C_U:_C_U + D_GM])
    gv = jax.nn.gelu(z[:, _C_V:_C_V + D_GM])
    v_ref[...] = (gv * _rms(gv)) * vg_ref[...]


def _inproj(x2, norm1_g, w_r, q_norm_g, k_norm_g, gm_v_norm_g, feat_s, feat_w, q_coef, batch, seq):
    n = x2.shape[0]
    tps = seq // TM_IN
    qpt = TM_IN // Q_BLOCK
    nq = N_REP * Q_BLOCK
    q_blocks = lambda r, c: pl.BlockSpec((1, N_KV, qpt, r, c), lambda i: (i // tps, 0, i % tps, 0, 0))
    row = lambda c: pl.BlockSpec((TM_IN, c), lambda i: (i, 0))
    full = lambda a: pl.BlockSpec(a.shape, lambda i: (0,) * a.ndim)
    per_seq = lambda a: pl.BlockSpec((TM_IN, a.shape[1]), lambda i: (i % tps, 0))
    keys = lambda w: pl.BlockSpec((1, N_KV, TM_IN, w), lambda i: (i // tps, 0, i % tps, 0))
    vals_t = pl.BlockSpec((1, N_KV, HEAD_DIM, TM_IN), lambda i: (i // tps, 0, 0, i % tps))
    g1 = norm1_g.reshape(1, D_MODEL)
    qg = q_norm_g.reshape(1, HEAD_DIM)
    vg = gm_v_norm_g.reshape(1, D_GM)
    ws, ww = HEAD_DIM + feat_s.shape[1], HEAD_DIM + feat_w.shape[1]
    tok = lambda c: jax.ShapeDtypeStruct((n, c), F32)
    return pl.pallas_call(
        _inproj_kernel,
        grid=(n // TM_IN,),
        in_specs=[row(D_MODEL), full(g1), full(w_r), full(qg), full(k_norm_g), full(vg), per_seq(feat_s),
                  per_seq(feat_w), full(q_coef)],
        out_specs=[q_blocks(LANE, nq), row(D_KV), row(D_KV), keys(ws), vals_t, keys(ww), vals_t,
                   q_blocks(GATE_ROWS, Q_BLOCK), row(D_GM), row(D_GM)],
        out_shape=[jax.ShapeDtypeStruct((batch, N_KV, seq // Q_BLOCK, LANE, nq), BF16), tok(D_KV), tok(D_KV),
                   jax.ShapeDtypeStruct((batch, N_KV, seq, ws), BF16),
                   jax.ShapeDtypeStruct((batch, N_KV, HEAD_DIM, seq), BF16),
                   jax.ShapeDtypeStruct((batch, N_KV, seq, ww), BF16),
                   jax.ShapeDtypeStruct((batch, N_KV, HEAD_DIM, seq), BF16),
                   jax.ShapeDtypeStruct((batch, N_KV, seq // Q_BLOCK, GATE_ROWS, Q_BLOCK), F32),
                   tok(D_GM), tok(D_GM)],
        compiler_params=pltpu.CompilerParams(dimension_semantics=("arbitrary",), vmem_limit_bytes=VMEM_LIMIT),
        name="inproj",
    )(x2, g1, w_r, qg, k_norm_g, vg, feat_s, feat_w, q_coef)


def _compress_kernel(a_ref, pos_ref, w1_ref, w1a_ref, w1b_ref, b1_ref, w2_ref, b2_ref, kg_ref, o_ref, *, norm):
    a = a_ref[0]
    nseg = a.shape[0]
    c = jnp.dot(pos_ref[...], w1_ref[...], precision=HIGHEST, preferred_element_type=F32)[0:1] + b1_ref[...]
    row = lax.broadcasted_iota(jnp.int32, (nseg, 1), 0)
    for g in range(N_KV):
        pa = jnp.dot(a, w1a_ref[g], precision=HIGHEST, preferred_element_type=F32)
        pb = jnp.dot(a, w1b_ref[g], precision=HIGHEST, preferred_element_type=F32)
        hid = jax.nn.gelu(pa + pltpu.roll(pb, nseg - 1, 0) + c)
        out = jnp.dot(hid, w2_ref[...], precision=HIGHEST, preferred_element_type=F32) + b2_ref[...]
        if norm:
            out = (out * _rms(out)) * kg_ref[...]
        o_ref[0, g] = jnp.where(row < nseg - 1, out, 0.0)


def _compress(raw, pos, w1, b1, w2, b2, gain, batch, seq, norm):
    nseg = seq // CMP_STRIDE
    half = CMP_STRIDE * HEAD_DIM
    a = raw.reshape(batch, nseg, CMP_STRIDE * D_KV)
    pos8 = jnp.broadcast_to(pos.reshape(1, CMP_LEN * HEAD_DIM), (8, CMP_LEN * HEAD_DIM))

    def expand(wh):
        wh = wh.reshape(CMP_STRIDE, HEAD_DIM, CMP_HIDDEN)
        z = jnp.zeros((N_KV, CMP_STRIDE, N_KV, HEAD_DIM, CMP_HIDDEN), F32)
        for g in range(N_KV):
            z = z.at[g, :, g].set(wh)
        return z.reshape(N_KV, CMP_STRIDE * D_KV, CMP_HIDDEN)

    w1a, w1b = expand(w1[:half]), expand(w1[half:])
    b1r, b2r, gr = b1.reshape(1, CMP_HIDDEN), b2.reshape(1, HEAD_DIM), gain.reshape(1, HEAD_DIM)
    full = lambda t: pl.BlockSpec(t.shape, lambda i: (0,) * t.ndim)
    return pl.pallas_call(
        functools.partial(_compress_kernel, norm=norm),
        grid=(batch,),
        in_specs=[pl.BlockSpec((1, nseg, CMP_STRIDE * D_KV), lambda i: (i, 0, 0)),
                  full(pos8), full(w1), full(w1a), full(w1b), full(b1r), full(w2), full(b2r), full(gr)],
        out_specs=pl.BlockSpec((1, N_KV, nseg, HEAD_DIM), lambda i: (i, 0, 0, 0)),
        out_shape=jax.ShapeDtypeStruct((batch, N_KV, nseg, HEAD_DIM), F32),
        compiler_params=pltpu.CompilerParams(dimension_semantics=("arbitrary",), vmem_limit_bytes=VMEM_LIMIT),
        name="compress_k" if norm else "compress_v",
    )(a, pos8, w1, w1a, w1b, b1r, w2, b2r, gr)


def _attn_kernel(qt_ref, kc_ref, vct_ref, ks_ref, vst_ref, kw_ref, vwt_ref, g_ref, ovt_ref, o_ref,
                 qs_ref, s0_ref, s1_ref, p0_ref, p1_ref, st_ref, acc_ref):
    qb = pl.program_id(1)
    nq = N_REP * Q_BLOCK
    q0 = qb * Q_BLOCK
    qt = qt_ref[0, 0]
    ql = lax.broadcasted_iota(jnp.int32, (1, nq), 1) % Q_BLOCK
    t_row = (q0 + ql).astype(F32)
    m_init = 0.5 * NEG

    def online(s, m, l):
        m_new = jnp.maximum(m, jnp.max(s, axis=0, keepdims=True))
        alpha = jnp.exp2(m - m_new)
        p = jnp.exp2(s - m_new)
        return p, m_new, alpha, alpha * l + jnp.sum(p, axis=0, keepdims=True)

    def inv(l):
        return jnp.where(l > 0.0, 1.0 / l, 0.0)

    m0 = jnp.full((1, nq), m_init, F32)
    l0 = jnp.zeros((1, nq), F32)
    a0 = jnp.zeros((HEAD_DIM, nq), F32)

    ncmp = kc_ref.shape[1]
    s = jnp.dot(kc_ref[0], qt, preferred_element_type=F32)
    c_end = (lax.broadcasted_iota(jnp.int32, (ncmp, 1), 0) * CMP_STRIDE + (CMP_LEN - 1)).astype(F32)
    p, _, _, l = online(jnp.where(c_end <= t_row, s, NEG), m0, l0)
    p = p * inv(l)
    o_cmp = jnp.dot(vct_ref[0], p.astype(BF16), preferred_element_type=F32)

    psum = p[:, 0:Q_BLOCK]
    for r in range(1, N_REP):
        psum = psum + p[:, r * Q_BLOCK:(r + 1) * Q_BLOCK]
    nsel = ovt_ref.shape[0]
    p_hi = psum.astype(BF16)
    p_lo = (psum - p_hi.astype(F32)).astype(BF16)
    imp = (jnp.dot(ovt_ref[...], p_hi, preferred_element_type=F32)
           + jnp.dot(ovt_ref[...], p_lo, preferred_element_type=F32))
    n_col = lax.broadcasted_iota(jnp.int32, (nsel, 1), 0).astype(F32)
    n_start = n_col * SEL_BLOCK
    tq = t_row[:, 0:Q_BLOCK]
    cur = jnp.floor(tq * (1.0 / SEL_BLOCK)) * SEL_BLOCK
    forced = (n_start == cur) | (n_start == 0.0)
    valid = n_start <= tq
    imp = jnp.where(forced, imp + FORCE_BONUS, imp)
    imp = jnp.where(valid, imp, NEG)
    sel = jnp.zeros((nsel, Q_BLOCK), F32)
    for _ in range(min(N_SEL, nsel)):
        mx = jnp.max(imp, axis=0, keepdims=True)
        first = jnp.min(jnp.where(imp == mx, n_col, float(nsel)), axis=0, keepdims=True)
        hit = n_col == first
        sel = jnp.where(hit, 1.0, sel)
        imp = jnp.where(hit, -jnp.inf, imp)
    selb = jnp.where(valid & (sel > 0.0), 0.0, NEG).astype(BF16)
    qs_ref[0:LANE, :] = qt
    qs_ref[LANE:LANE + nsel, :] = jnp.concatenate([selb] * N_REP, axis=1)
    if qs_ref.shape[0] > LANE + nsel:
        qs_ref[LANE + nsel:, :] = jnp.zeros((qs_ref.shape[0] - LANE - nsel, nq), BF16)

    def attend(k_blk, vt_blk, q_op, bias, carry):
        m, l, acc = carry
        s = jnp.dot(k_blk, q_op, preferred_element_type=F32)
        if bias is not None:
            s = s + bias
        p, m, alpha, l = online(s, m, l)
        pv = jnp.dot(vt_blk, p.astype(BF16), preferred_element_type=F32)
        return m, l, alpha * acc + pv

    seq = ks_ref.shape[1]

    def scores(j):
        k0 = pl.multiple_of(jnp.minimum(j * KC_SEL, seq - KC_SEL), KC_SEL)
        s = jnp.dot(ks_ref[0, pl.ds(k0, KC_SEL), :], qs_ref[...], preferred_element_type=F32)
        return s, jnp.max(s, axis=0, keepdims=True)

    def values(j, p):
        k0 = pl.multiple_of(jnp.maximum(j, 0) * KC_SEL, KC_SEL)
        return jnp.dot(vst_ref[0, :, pl.ds(k0, KC_SEL)], p, preferred_element_type=F32)

    def stage(j, s_cur, s_nxt, p_cur, p_prv):
        m, l, alpha_prev, mx = st_ref[0:1, :], st_ref[1:2, :], st_ref[2:3, :], st_ref[3:4, :]
        m_new = jnp.maximum(m, mx)
        alpha = jnp.exp2(m - m_new)
        k0 = pl.multiple_of(jnp.minimum((j + 1) * KC_SEL, seq - KC_SEL), KC_SEL)
        k0p = pl.multiple_of(jnp.maximum(j - 1, 0) * KC_SEL, KC_SEL)
        sub = KC_SEL // SEL_PARTS
        psum, mx_next, zeros = None, None, []
        for q in range(SEL_PARTS):
            rows = slice(q * sub, (q + 1) * sub)
            k_q = ks_ref[0, pl.ds(k0 + q * sub, sub), :]
            if q >= 1:
                k_q = k_q + jnp.concatenate([zeros[q - 1]] * (ks_ref.shape[2] // LANE), axis=1)
            s_q = jnp.dot(k_q, qs_ref[...], preferred_element_type=F32)
            s_nxt[rows, :] = s_q
            mx_q = jnp.max(s_q, axis=0, keepdims=True)
            mx_next = mx_q if mx_next is None else jnp.maximum(mx_next, mx_q)
            p_q = jnp.exp2(s_cur[rows, :] - m_new)
            ps_q = jnp.sum(p_q, axis=0, keepdims=True)
            psum = ps_q if psum is None else psum + ps_q
            p_q = p_q.astype(BF16)
            p_cur[rows, :] = p_q
            dep = ps_q[:, 0:LANE]
            for r in range(1, N_REP):
                dep = dep + ps_q[:, r * LANE:(r + 1) * LANE]
            bits = pltpu.bitcast(dep, jnp.int32)
            zeros.append(lax.shift_right_logical(lax.shift_right_logical(bits, 16), 16).astype(F32).astype(BF16))
            if q == SEL_PARTS // 2 - 1:
                vt_prev = vst_ref[0, :, pl.ds(k0p, KC_SEL)] + jnp.concatenate([zeros[q]] * (KC_SEL // LANE), axis=1)
                acc_ref[...] = alpha_prev * acc_ref[...] + jnp.dot(vt_prev, p_prv[...], preferred_element_type=F32)
        st_ref[0:1, :] = m_new
        st_ref[1:2, :] = alpha * l + psum
        st_ref[2:3, :] = alpha
        st_ref[3:4, :] = mx_next

    n_full = q0 // KC_SEL
    s_first, mx_first = scores(0)

    @pl.when(n_full % 2 == 0)
    def _():
        s0_ref[...] = s_first

    @pl.when(n_full % 2 == 1)
    def _():
        s1_ref[...] = s_first

    p0_ref[...] = jnp.zeros(p0_ref.shape, BF16)
    p1_ref[...] = jnp.zeros(p1_ref.shape, BF16)
    st_ref[0:1, :] = m0
    st_ref[1:2, :] = l0
    st_ref[2:3, :] = jnp.ones((1, nq), F32)
    st_ref[3:4, :] = mx_first
    acc_ref[...] = a0

    def sel_body(j, carry):
        @pl.when((n_full - j) % 2 == 0)
        def _():
            stage(j, s0_ref, s1_ref, p0_ref, p1_ref)

        @pl.when((n_full - j) % 2 == 1)
        def _():
            stage(j, s1_ref, s0_ref, p1_ref, p0_ref)
        return carry

    lax.fori_loop(0, n_full, sel_body, 0)
    pos_last = (n_full * KC_SEL + lax.broadcasted_iota(jnp.int32, (KC_SEL, 1), 0)).astype(F32)
    p, _, alpha, l_sel = online(s0_ref[...] + jnp.where(pos_last <= t_row, 0.0, NEG), st_ref[0:1, :], st_ref[1:2, :])
    acc = st_ref[2:3, :] * acc_ref[...] + values(n_full - 1, p1_ref[...])
    o_sel = alpha * acc + values(n_full, p.astype(BF16))

    n_wk = WINDOW + Q_BLOCK
    w0 = pl.multiple_of(jnp.maximum(q0 - WINDOW, 0), Q_BLOCK)
    kk = lax.broadcasted_iota(jnp.int32, (n_wk, 1), 0) - (q0 - w0)
    in_win = (kk <= ql) & (kk + WINDOW > ql)
    _, l_win, o_win = attend(kw_ref[0, pl.ds(w0, n_wk), :], vwt_ref[0, :, pl.ds(w0, n_wk)], qt,
                             jnp.where(in_win, 0.0, NEG), (m0, l0, a0))

    gt = g_ref[0, 0]

    def gate(j):
        return jnp.concatenate([gt[j * N_REP + r:j * N_REP + r + 1, :] for r in range(N_REP)], axis=1)

    o_ref[0, 0] = (gate(0) * o_cmp + gate(1) * (o_sel * inv(l_sel)) + gate(2) * (o_win * inv(l_win)))


def _attention(qt, kc, vct, ks, vst, kw, vwt, gt, ovt):
    bgn, nqb = qt.shape[0], qt.shape[1]
    seq = ks.shape[1]
    nq = N_REP * Q_BLOCK
    per_bg = lambda a: pl.BlockSpec((1,) + a.shape[1:], lambda b, i: (b,) + (0,) * (a.ndim - 1))
    return pl.pallas_call(
        _attn_kernel,
        grid=(bgn, nqb),
        in_specs=[pl.BlockSpec((1, 1, LANE, nq), lambda b, i: (b, i, 0, 0)),
                  per_bg(kc), per_bg(vct), per_bg(ks), per_bg(vst), per_bg(kw), per_bg(vwt),
                  pl.BlockSpec((1, 1, GATE_ROWS, Q_BLOCK), lambda b, i: (b, i, 0, 0)),
                  pl.BlockSpec(ovt.shape, lambda b, i: (0, 0))],
        out_specs=pl.BlockSpec((1, 1, HEAD_DIM, nq), lambda b, i: (b, i, 0, 0)),
        out_shape=jax.ShapeDtypeStruct((bgn, nqb, HEAD_DIM, nq), F32),
        scratch_shapes=[pltpu.VMEM((ks.shape[2], nq), BF16),
                        pltpu.VMEM((KC_SEL, nq), F32), pltpu.VMEM((KC_SEL, nq), F32),
                        pltpu.VMEM((KC_SEL, nq), BF16), pltpu.VMEM((KC_SEL, nq), BF16),
                        pltpu.VMEM((8, nq), F32), pltpu.VMEM((HEAD_DIM, nq), F32)],
        compiler_params=pltpu.CompilerParams(dimension_semantics=("arbitrary", "arbitrary"),
                                             vmem_limit_bytes=VMEM_LIMIT),
        name="nsa_attention",
    )(qt, kc, vct, ks, vst, kw, vwt, gt, ovt)


def _mix_kernel(x_ref, oa_ref, u_ref, v_ref, ws_ref, bs_ref, ga_ref, gg_ref, wo_ref, g2_ref, wr_ref, br_ref,
                x1_ref, xn_ref, idx_ref, gate_ref, rank_ref, cnt_out_ref, cnt_ref):
    tm = x_ref.shape[0]
    rr = lax.broadcasted_iota(jnp.int32, (GM_CHUNK, GM_CHUNK), 0)
    cc = lax.broadcasted_iota(jnp.int32, (GM_CHUNK, GM_CHUNK), 1)
    grp = lax.broadcasted_iota(jnp.int32, (1, D_GM), 1) // GM_GROUP_DIM
    ws = [jnp.where(rr >= cc, ws_ref[g], 0.0).astype(BF16) for g in range(N_GM_GROUPS)]
    ys = []
    for c in range(tm // GM_CHUNK):
        vch = v_ref[c * GM_CHUNK:(c + 1) * GM_CHUNK, :].astype(BF16)
        y = bs_ref[...]
        for g in range(N_GM_GROUPS):
            y = y + jnp.where(grp == g, jnp.dot(ws[g], vch, preferred_element_type=F32), 0.0)
        ys.append(y)
    o_gm = u_ref[...] * jnp.concatenate(ys, axis=0)
    o_at = jnp.concatenate(
        [jnp.concatenate([oa_ref[0, g, b, :, r * Q_BLOCK:(r + 1) * Q_BLOCK]
                          for g in range(N_KV) for r in range(N_REP)], axis=0).T
         for b in range(tm // Q_BLOCK)], axis=0)
    mixed =jnp.concatenate([(o_at * _rms(o_at)) * ga_ref[...], (o_gm * _rms(o_gm)) * gg_ref[...]], axis=-1)
    x1 = x_ref[...] + jnp.dot(mixed.astype(BF16), wo_ref[...], preferred_element_type=F32)
    x1_ref[...] = x1
    xn = (x1 * _rms(x1)) * g2_ref[...]
    for s in range(ROW_SUB):
        xn_ref[:, s, :] = xn[:, s * LANE:(s + 1) * LANE]
    logits = jnp.dot(xn, wr_ref[...], precision=HIGHEST, preferred_element_type=F32) + br_ref[...]
    lane = lax.broadcasted_iota(jnp.int32, (1, LANE), 1).astype(F32)
    idx_out = jnp.zeros((tm, LANE), F32)
    val_out = jnp.zeros((tm, LANE), F32)
    vals, firsts = [], []
    for k in range(TOP_K):
        mx = jnp.max(logits, axis=-1, keepdims=True)
        first = jnp.min(jnp.where(logits == mx, lane, float(LANE)), axis=-1, keepdims=True)
        logits = jnp.where(lane == first, -jnp.inf, logits)
        idx_out = jnp.where(lane == float(k), first, idx_out)
        vals.append(mx)
        firsts.append(first)
    es = [jnp.exp(v - vals[0]) for v in vals]
    den = es[0] + es[1] + es[2] + es[3]
    for k in range(TOP_K):
        val_out = jnp.where(lane == float(k), es[k] / den, val_out)
    idx_ref[...] = idx_out.astype(jnp.int32)
    gate_ref[...] = val_out

    @pl.when(pl.program_id(0) == 0)
    def _():
        cnt_ref[...] = jnp.zeros(cnt_ref.shape, F32)

    hit = (lane == firsts[0]) | (lane == firsts[1]) | (lane == firsts[2]) | (lane == firsts[3])
    hit_b = jnp.where(hit, 1.0, 0.0).astype(BF16)
    tr = lax.broadcasted_iota(jnp.int32, (tm, tm), 0)
    tc = lax.broadcasted_iota(jnp.int32, (tm, tm), 1)
    before = jnp.where(tr > tc, 1.0, 0.0).astype(BF16)
    ranks = jnp.dot(before, hit_b, preferred_element_type=F32) + cnt_ref[0:1, :]
    rank_out = jnp.zeros((tm, LANE), F32)
    for k in range(TOP_K):
        r_k = jnp.sum(jnp.where(lane == firsts[k], ranks, 0.0), axis=-1, keepdims=True)
        rank_out = jnp.where(lane == float(k), r_k, rank_out)
    rank_ref[...] = rank_out.astype(jnp.int32)
    cnt_ref[0:1, :] = cnt_ref[0:1, :] + jnp.sum(hit_b.astype(F32), axis=0, keepdims=True)
    cnt_out_ref[...] = cnt_ref[...].astype(jnp.int32)


def _mix(x2, ot, u_act, v_act, gm_w_s, bias_full, ga, gg, w_out_b, g2, wr_pad, br_pad):
    n = x2.shape[0]
    tps = ot.shape[2] * Q_BLOCK // TM_MIX
    row = lambda c: pl.BlockSpec((TM_MIX, c), lambda i: (i, 0))
    full = lambda a: pl.BlockSpec(a.shape, lambda i: (0,) * a.ndim)
    attn_blocks = pl.BlockSpec((1, N_KV, TM_MIX // Q_BLOCK) + ot.shape[3:], lambda i: (i // tps, 0, i % tps, 0, 0))
    return pl.pallas_call(
        _mix_kernel,
        grid=(n // TM_MIX,),
        in_specs=[row(D_MODEL), attn_blocks, row(D_GM), row(D_GM), full(gm_w_s), full(bias_full), full(ga), full(gg),
                  full(w_out_b), full(g2), full(wr_pad), full(br_pad)],
        out_specs=[row(D_MODEL), pl.BlockSpec((TM_MIX, ROW_SUB, LANE), lambda i: (i, 0, 0)), row(LANE), row(LANE),
                   row(LANE), pl.BlockSpec((ROW_SUB, LANE), lambda i: (0, 0))],
        out_shape=[jax.ShapeDtypeStruct((n, D_MODEL), F32), jax.ShapeDtypeStruct((n, ROW_SUB, LANE), F32),
                   jax.ShapeDtypeStruct((n, LANE), jnp.int32), jax.ShapeDtypeStruct((n, LANE), F32),
                   jax.ShapeDtypeStruct((n, LANE), jnp.int32), jax.ShapeDtypeStruct((ROW_SUB, LANE), jnp.int32)],
        scratch_shapes=[pltpu.VMEM((ROW_SUB, LANE), F32)],
        compiler_params=pltpu.CompilerParams(dimension_semantics=("arbitrary",), vmem_limit_bytes=VMEM_LIMIT),
        name="mix_outproj_router",
    )(x2, ot, u_act, v_act, gm_w_s, bias_full, ga, gg, w_out_b, g2, wr_pad, br_pad)


def _row_gather(idx_ref, n_rows, src_hbm, dst_ref, sem):
    def start():
        for r in range(n_rows):
            pltpu.make_async_copy(src_hbm.at[pl.ds(idx_ref[0, 0, r], 1), :], dst_ref.at[pl.ds(r, 1), :], sem).start()

    def wait():
        pltpu.make_async_copy(src_hbm.at[pl.ds(0, n_rows), :], dst_ref, sem).wait()

    return start, wait


def _tile_row_gather(idx_ref, n_rows, src_hbm, dst_ref, sem):
    def start():
        for r in range(n_rows):
            t = idx_ref[0, 0, r]
            pltpu.make_async_copy(src_hbm.at[lax.shift_right_logical(t, 3), t & (ROW_SUB - 1)],
                                  dst_ref.at[r // ROW_SUB, :, r % ROW_SUB, :], sem).start(priority=r % 2)

    def wait():
        pltpu.make_async_copy(src_hbm.at[pl.ds(0, n_rows // ROW_SUB)], dst_ref, sem).wait()

    return start, wait


def _tiles_to_matrix(ref):
    rows = ref.shape[0] * ROW_SUB
    return jnp.concatenate([ref[:, c].reshape(rows, LANE) for c in range(ROW_SUB)], axis=1)


def _moe_kernel(be_ref, bv_ref, bn_ref, bs_ref, tok_ref, tok_n1_ref, tok_n2_ref, x_hbm, wgu_hbm, bg_ref, bl_ref,
                wd_hbm, bd_ref, o_ref, xbuf, sems, wt_s, wg_s, wl_s, wd_s, wgu_buf, wd_buf, wsems):
    i = pl.program_id(0)
    slot = i % (MOE_AHEAD + 1)
    slot_n2 = (i + MOE_AHEAD) % (MOE_AHEAD + 1)
    start_cur, wait_cur = _tile_row_gather(tok_ref, BM_MOE, x_hbm, xbuf.at[slot], sems.at[slot])
    start_n1, _ = _tile_row_gather(tok_n1_ref, BM_MOE, x_hbm, xbuf.at[1], sems.at[1])
    start_n2, _ = _tile_row_gather(tok_n2_ref, BM_MOE, x_hbm, xbuf.at[slot_n2], sems.at[slot_n2])

    prev = jnp.maximum(i - 1, 0)

    def fetch_weights(e, ws):
        n_piece = 4
        rows = D_MODEL // n_piece
        copies = [pltpu.make_async_copy(wgu_hbm.at[e, pl.ds(c * rows, rows)], wgu_buf.at[ws, pl.ds(c * rows, rows)],
                                        wsems.at[ws]) for c in range(n_piece)]
        copies += [pltpu.make_async_copy(wd_hbm.at[e, pl.ds(c * rows, rows)], wd_buf.at[ws, pl.ds(c * rows, rows)],
                                         wsems.at[ws]) for c in range(n_piece)]
        return copies

    @pl.when(i == 0)
    def _():
        start_cur()
        start_n1()
        for cp in fetch_weights(be_ref[0], 0):
            cp.start()

    @pl.when((bv_ref[i] == 1) & ((i == 0) | (be_ref[i] != be_ref[prev])))
    def _():
        ws = bs_ref[i]
        for cp in fetch_weights(be_ref[i], ws):
            cp.wait()

        @pl.when(bn_ref[i] >= 0)
        def _():
            for cp in fetch_weights(bn_ref[i], 1 - ws):
                cp.start()

        tc = wt_s.shape[1]
        for c in range(2 * D_EXPERT // tc):
            wt = wgu_buf[ws, :, c * tc:(c + 1) * tc].T
            for j in range(ROW_SUB):
                wt_s[j] = wt[:, j * LANE:(j + 1) * LANE]
            for first, dst in ((0, wg_s), (1, wl_s)):
                half = jnp.concatenate([wt_s[j, pl.ds(first, tc // 2, stride=2), :] for j in range(ROW_SUB)], axis=1)
                dst[c * tc // 2:(c + 1) * tc // 2, :] = half.astype(BF16)
        wd_s[...] = wd_buf[ws].astype(BF16)

    @pl.when(bv_ref[i] == 1)
    def _():
        wait_cur()
        start_n2()
        xb = _tiles_to_matrix(xbuf.at[slot]).astype(BF16)
        hg = lax.dot_general(xb, wg_s[...], _NT, preferred_element_type=F32) + bg_ref[0]
        hl = lax.dot_general(xb, wl_s[...], _NT, preferred_element_type=F32) + bl_ref[0]
        hg = jnp.minimum(hg, SWIGLU_LIMIT)
        hl = jnp.clip(hl, -SWIGLU_LIMIT, SWIGLU_LIMIT)
        a = hg * jax.nn.sigmoid(SWIGLU_ALPHA * hg) * (hl + 1.0)
        o_ref[...] = jnp.dot(a.astype(BF16), wd_s[...], preferred_element_type=F32) + bd_ref[0]

    @pl.when((bv_ref[i] == 0) & ((i == 1) | ((i >= MOE_AHEAD) & (bv_ref[jnp.maximum(i - MOE_AHEAD, 0)] == 1))))
    def _():
        wait_cur()

    @pl.when(bv_ref[i] == 0)
    def _():
        o_ref[...] = jnp.zeros(o_ref.shape, F32)


def _moe(blk_expert, blk_valid, blk_next, blk_wslot, tok_blocks, xn3, w_gate_up, bg, bl, w_down, bd):
    nb = blk_expert.shape[0]
    per_e = lambda a: pl.BlockSpec((1,) + a.shape[1:], lambda i, be, *_: (be[i],) + (0,) * (a.ndim - 1))

    def tok_spec(ahead):
        return pl.BlockSpec((1, 1, BM_MOE), lambda i, *_: (jnp.minimum(i + ahead, nb - 1), 0, 0),
                            memory_space=pltpu.SMEM)

    grid_spec = pltpu.PrefetchScalarGridSpec(
        num_scalar_prefetch=4,
        grid=(nb,),
        in_specs=[tok_spec(0), tok_spec(1), tok_spec(MOE_AHEAD),
                  pl.BlockSpec(memory_space=pl.ANY),
                  pl.BlockSpec(memory_space=pl.ANY), per_e(bg), per_e(bl),
                  pl.BlockSpec(memory_space=pl.ANY), per_e(bd)],
        out_specs=pl.BlockSpec((BM_MOE, D_MODEL), lambda i, *_: (i, 0)),
        scratch_shapes=[pltpu.VMEM((MOE_AHEAD + 1, BM_MOE // ROW_SUB, ROW_SUB, ROW_SUB, LANE), F32),
                        pltpu.SemaphoreType.DMA((MOE_AHEAD + 1,)),
                        pltpu.VMEM((ROW_SUB, 256, LANE), F32), pltpu.VMEM((D_EXPERT, D_MODEL), BF16),
                        pltpu.VMEM((D_EXPERT, D_MODEL), BF16), pltpu.VMEM((D_EXPERT, D_MODEL), BF16),
                        pltpu.VMEM((2, D_MODEL, 2 * D_EXPERT), F32), pltpu.VMEM((2, D_EXPERT, D_MODEL), F32),
                        pltpu.SemaphoreType.DMA((2,))],
    )
    return pl.pallas_call(
        _moe_kernel,
        grid_spec=grid_spec,
        out_shape=jax.ShapeDtypeStruct((nb * BM_MOE, D_MODEL), F32),
        compiler_params=pltpu.CompilerParams(dimension_semantics=("arbitrary",), vmem_limit_bytes=VMEM_LIMIT_MOE),
        name="moe_experts",
    )(blk_expert, blk_valid, blk_next, blk_wslot, tok_blocks, tok_blocks, tok_blocks,
      xn3.reshape(xn3.shape[0] // ROW_SUB, ROW_SUB, ROW_SUB, LANE), w_gate_up, bg, bl, w_down, bd)


def _slot_table_kernel(lo_ref, hi_ref, dest_ref, tok_ref):
    i = pl.program_id(0)
    n_chunk = dest_ref.shape[2]

    @pl.when(i == 0)
    def _():
        def clear(s, carry):
            tok_ref[s] = 0
            return carry
        for e in range(lo_ref.shape[0]):
            lax.fori_loop(lo_ref[e], hi_ref[e], clear, 0)

    def put(s, carry):
        tok_ref[dest_ref[0, 0, s]] = lax.shift_right_logical(i * n_chunk + s, TOP_K.bit_length() - 1)
        return carry
    lax.fori_loop(0, n_chunk, put, 0, unroll=8)


def _slot_table(pad_lo, pad_hi, dest, n_slots):
    n_chunk = 8192
    s_tot = dest.shape[0]
    grid_spec = pltpu.PrefetchScalarGridSpec(
        num_scalar_prefetch=2,
        grid=(s_tot // n_chunk,),
        in_specs=[pl.BlockSpec((1, 1, n_chunk), lambda i, *_: (i, 0, 0), memory_space=pltpu.SMEM)],
        out_specs=pl.BlockSpec(memory_space=pltpu.SMEM),
    )
    return pl.pallas_call(
        _slot_table_kernel,
        grid_spec=grid_spec,
        out_shape=jax.ShapeDtypeStruct((n_slots,), jnp.int32),
        compiler_params=pltpu.CompilerParams(dimension_semantics=("arbitrary",)),
        name="moe_slot_table",
    )(pad_lo, pad_hi, dest.reshape(s_tot // n_chunk, 1, n_chunk))


def _combine_kernel(dest_ref, dest_next_ref, x1_ref, gate_ref, y_hbm, o_ref, buf, sems):
    i = pl.program_id(0)
    slot = i % 2
    n_rows = TOP_K * TM_CMB
    start_cur, wait_cur = _row_gather(dest_ref, n_rows, y_hbm, buf.at[slot], sems.at[slot])
    start_next, _ = _row_gather(dest_next_ref, n_rows, y_hbm, buf.at[1 - slot], sems.at[1 - slot])

    @pl.when(i == 0)
    def _():
        start_cur()

    @pl.when(i + 1 < pl.num_programs(0))
    def _():
        start_next()

    wait_cur()
    gate = gate_ref[...]
    acc = x1_ref[...]
    for k in range(TOP_K):
        acc = acc + gate[:, k:k + 1] * buf[slot, k * TM_CMB:(k + 1) * TM_CMB, :]
    o_ref[...] = acc


def _combine(dest_blocks, x1, gate_pad, y_rows):
    n = x1.shape[0]
    nt = n // TM_CMB
    n_rows = TOP_K * TM_CMB
    return pl.pallas_call(
        _combine_kernel,
        grid=(nt,),
        in_specs=[pl.BlockSpec((1, 1, n_rows), lambda i: (i, 0, 0), memory_space=pltpu.SMEM),
                  pl.BlockSpec((1, 1, n_rows), lambda i: (jnp.minimum(i + 1, nt - 1), 0, 0),
                               memory_space=pltpu.SMEM),
                  pl.BlockSpec((TM_CMB, D_MODEL), lambda i: (i, 0)),
                  pl.BlockSpec((TM_CMB, LANE), lambda i: (i, 0)),
                  pl.BlockSpec(memory_space=pl.ANY)],
        out_specs=pl.BlockSpec((TM_CMB, D_MODEL), lambda i: (i, 0)),
        out_shape=jax.ShapeDtypeStruct((n, D_MODEL), F32),
        scratch_shapes=[pltpu.VMEM((2, n_rows, D_MODEL), F32), pltpu.SemaphoreType.DMA((2,))],
        compiler_params=pltpu.CompilerParams(dimension_semantics=("arbitrary",), vmem_limit_bytes=VMEM_LIMIT),
        name="moe_combine",
    )(dest_blocks, dest_blocks, x1, gate_pad, y_rows)


def kernel(x, norm1_g, w_in, q_norm_g, k_norm_g, cmp_pos, w_cmp1, b_cmp1, w_cmp2, b_cmp2, gm_v_norm_g, gm_w_s,
           gm_b_s, out_norm_attn_g, out_norm_gm_g, w_out, norm2_g, w_router, b_router, w_gate_up, b_gate_up,
           w_down, b_down):
    batch, seq, _ = x.shape
    n = batch * seq
    nqb = seq // Q_BLOCK
    bgn = batch * N_KV
    x2 = x.reshape(n, D_MODEL)

    c_gate = D_ATTN + 6 * D_KV
    gate_src = np.full((LANE,), N_GATE, np.int32)
    for h in range(N_HEADS):
        for j in range(3):
            gate_src[(h // N_REP) * GATE_ROWS + j * N_REP + h % N_REP] = h * 3 + j
    w_gate = jnp.concatenate([w_in[:, c_gate:c_gate + N_GATE], jnp.zeros((D_MODEL, 1), F32)], axis=1)[:, gate_src]
    w_r = jnp.concatenate([w_in[:, :c_gate], w_in[:, c_gate + N_GATE:], w_gate], axis=1).astype(BF16)
    nq = N_REP * Q_BLOCK
    head = np.arange(N_KV)[:, None] * N_REP + np.arange(nq)[None, :] // Q_BLOCK
    coef = np.exp2(-(head + 1.0)) * LOG2E
    c_hi = coef.astype(BF16).astype(np.float64)
    c_lo = (coef - c_hi).astype(BF16).astype(np.float64)
    qrows = np.zeros((N_KV, LANE - HEAD_DIM, nq), np.float32)
    qrows[:, 0], qrows[:, 1], qrows[:, 2], qrows[:, 3] = SEL_BLOCK * c_hi, SEL_BLOCK * c_lo, c_hi, c_lo
    nsel = seq // SEL_BLOCK
    oh_w = -(-nsel // LANE) * LANE

    def pos_features(pos, one_hot=False):
        feat = np.zeros((pos.shape[0], LANE - HEAD_DIM + (oh_w if one_hot else 0)), np.float32)
        feat[:, 0] = feat[:, 1] = pos // SEL_BLOCK
        feat[:, 2] = feat[:, 3] = pos % SEL_BLOCK
        if one_hot:
            feat[np.arange(pos.shape[0]), LANE - HEAD_DIM + pos // SEL_BLOCK] = 1.0
        return jnp.asarray(feat, BF16)

    pos_t = np.arange(seq)
    qt, kc_raw, vc_raw, ks_aug, vst, kw_aug, vwt, gt, u_act, v_act = _inproj(
        x2, norm1_g, w_r, q_norm_g, k_norm_g, gm_v_norm_g, pos_features(pos_t, one_hot=True), pos_features(pos_t),
        jnp.asarray(qrows, BF16), batch, seq)

    kc = _compress(kc_raw, cmp_pos[0], w_cmp1[0], b_cmp1[0], w_cmp2[0], b_cmp2[0], k_norm_g[0], batch, seq, True)
    vc = _compress(vc_raw, cmp_pos[1], w_cmp1[1], b_cmp1[1], w_cmp2[1], b_cmp2[1], k_norm_g[0], batch, seq, False)

    ncmp = seq // CMP_STRIDE
    pos_c = np.arange(ncmp) * CMP_STRIDE + (CMP_LEN - 1)
    kc_b = jnp.concatenate([kc.reshape(bgn, ncmp, HEAD_DIM).astype(BF16),
                            jnp.broadcast_to(pos_features(pos_c)[None], (bgn, ncmp, LANE - HEAD_DIM))], axis=-1)
    vct = vc.reshape(bgn, ncmp, HEAD_DIM).transpose(0, 2, 1).astype(BF16)
    c0 = np.arange(ncmp)[None, :] * CMP_STRIDE
    n0 = np.arange(seq // SEL_BLOCK)[:, None] * SEL_BLOCK
    ovt = np.clip(np.minimum(c0 + CMP_LEN, n0 + SEL_BLOCK) - np.maximum(c0, n0), 0, None) / CMP_LEN
    per_group = lambda a: a.reshape((bgn,) + a.shape[2:])
    ot = _attention(per_group(qt), kc_b, vct, per_group(ks_aug), per_group(vst), per_group(kw_aug),
                    per_group(vwt), per_group(gt), jnp.asarray(ovt, BF16))
    ot = ot.reshape((batch, N_KV) + ot.shape[1:])

    bias_full = jnp.repeat(gm_b_s.T, GM_GROUP_DIM, axis=1)
    wr_pad = jnp.concatenate([w_router, jnp.zeros((D_MODEL, LANE - N_EXPERTS), F32)], axis=1)
    br_pad = jnp.concatenate([b_router, jnp.full((LANE - N_EXPERTS,), NEG, F32)]).reshape(1, LANE)
    x1, xn3, idx_pad, gate_pad, rank_pad, cnt_pad = _mix(
        x2, ot, u_act, v_act, gm_w_s, bias_full, out_norm_attn_g.reshape(1, D_ATTN),
        out_norm_gm_g.reshape(1, D_GM), w_out.astype(BF16), norm2_g.reshape(1, D_MODEL), wr_pad, br_pad)

    s_tot = n * TOP_K
    nb = s_tot // BM_MOE + N_EXPERTS - 1 + MOE_AHEAD
    e_flat = idx_pad[:, :TOP_K].reshape(s_tot)
    rank = rank_pad[:, :TOP_K].reshape(s_tot)
    counts = cnt_pad[0, :N_EXPERTS]
    padded = ((counts + BM_MOE - 1) // BM_MOE) * BM_MOE
    pad_end = jnp.cumsum(padded)
    pad_start = pad_end - padded
    dest = pad_start[e_flat] + rank
    pad_lo = jnp.concatenate([pad_start + counts, pad_end[-1:]]).astype(jnp.int32)
    pad_hi = jnp.concatenate([pad_end, jnp.full((1,), nb * BM_MOE)]).astype(jnp.int32)
    tok_buf = _slot_table(pad_lo, pad_hi, dest.astype(jnp.int32), nb * BM_MOE)
    blk_start = jnp.arange(nb, dtype=jnp.int32) * BM_MOE
    blk_expert = jnp.minimum(jnp.sum((blk_start[:, None] >= pad_end[None, :]).astype(jnp.int32), axis=1),
                             N_EXPERTS - 1)
    blk_valid = (blk_start < pad_end[-1]).astype(jnp.int32)
    e_ids = jnp.arange(N_EXPERTS, dtype=jnp.int32)
    present = counts > 0
    ordinal = jnp.cumsum(present.astype(jnp.int32)) - 1
    later = jnp.where(present[None, :] & (e_ids[None, :] > e_ids[:, None]), e_ids[None, :], N_EXPERTS)
    nxt = jnp.min(later, axis=1)
    blk_next = jnp.where(nxt < N_EXPERTS, nxt, -1)[blk_expert].astype(jnp.int32)
    blk_wslot = (ordinal[blk_expert] % 2).astype(jnp.int32)

    bg = b_gate_up[:, 0::2].reshape(N_EXPERTS, 1, D_EXPERT)
    bl = b_gate_up[:, 1::2].reshape(N_EXPERTS, 1, D_EXPERT)
    y_rows = _moe(blk_expert, blk_valid, blk_next, blk_wslot, tok_buf.reshape(nb, 1, BM_MOE), xn3, w_gate_up, bg, bl,
                  w_down,
                  b_down.reshape(N_EXPERTS, 1, D_MODEL))

    dest_blocks = (dest.reshape(n // TM_CMB, TM_CMB, TOP_K).transpose(0, 2, 1)
                   .reshape(n // TM_CMB, 1, TOP_K * TM_CMB).astype(jnp.int32))
    out = _combine(dest_blocks, x1, gate_pad, y_rows)
    return out.reshape(batch, seq, D_MODEL)
```

```python
import functools

import jax
import jax.numpy as jnp
import numpy as np
from jax import lax
from jax.experimental import pallas as pl
from jax.experimental.pallas import tpu as pltpu

F32 = jnp.float32
BF16 = jnp.bfloat16
HIGHEST = lax.Precision.HIGHEST
_NT = (((1,), (1,)), ((), ()))

D_MODEL = 1024
N_HEADS = 8
HEAD_DIM = 64
N_KV = 2
N_REP = N_HEADS // N_KV
D_ATTN = N_HEADS * HEAD_DIM
D_KV = N_KV * HEAD_DIM
N_GM_GROUPS = 8
GM_GROUP_DIM = 64
D_GM = N_GM_GROUPS * GM_GROUP_DIM
N_GATE = 3 * N_HEADS
CMP_LEN = 32
CMP_STRIDE = 16
CMP_HIDDEN = 128
SEL_BLOCK = 64
N_SEL = 16
WINDOW = 512
Q_BLOCK = 128
FORCE_BONUS = 1.0e4
GM_CHUNK = 128
N_EXPERTS = 32
TOP_K = 4
D_EXPERT = 1024
SWIGLU_LIMIT = 7.0
SWIGLU_ALPHA = 1.702
EPS = 1e-6
NEG = -1.0e30
LOG2E = 1.4426950408889634

LANE = 128
ROW_SUB = D_MODEL // LANE
GATE_ROWS = 16
VMEM_LIMIT = 48 * 1024 * 1024
VMEM_LIMIT_MOE = 56 * 1024 * 1024

_C_Q = 0
_C_KC = _C_Q + D_ATTN
_C_VC = _C_KC + D_KV
_C_KS = _C_VC + D_KV
_C_VS = _C_KS + D_KV
_C_KW = _C_VS + D_KV
_C_VW = _C_KW + D_KV
_C_U = _C_VW + D_KV
_C_V = _C_U + D_GM
_C_G = _C_V + D_GM
D_IN_PAD = _C_G + LANE

TM_IN = 256
TM_MIX = 256
KC_SEL = 512
BM_MOE = 256
MOE_AHEAD = 2
TM_CMB = 128


def _rms(x, eps=EPS):
    return lax.rsqrt(jnp.mean(x * x, axis=-1, keepdims=True) + eps)


def _inproj_kernel(x_ref, g1_ref, w_ref, qg_ref, kg_ref, vg_ref, fs_ref, fw_ref, qc_ref,
                   q_ref, kc_ref, vc_ref, ks_ref, vs_ref, kw_ref, vw_ref, gate_ref, u_ref, v_ref):
    x = x_ref[...]
    h = (x * _rms(x)) * g1_ref[...]
    z = jnp.dot(h.astype(BF16), w_ref[...], preferred_element_type=F32)

    def head_norm(col0, n, gain, scale):
        outs = []
        for i in range(n):
            sl = z[:, col0 + i * HEAD_DIM: col0 + (i + 1) * HEAD_DIM]
            outs.append((sl * _rms(sl)) * gain * scale)
        return jnp.concatenate(outs, axis=-1)

    qn_t = head_norm(_C_Q, N_HEADS, qg_ref[...], HEAD_DIM ** -0.5 * LOG2E).T.astype(BF16)
    gate_t = jax.nn.sigmoid(z[:, _C_G:_C_G + LANE]).T
    for g in range(N_KV):
        for b in range(x.shape[0] // Q_BLOCK):
            cols = slice(b * Q_BLOCK, (b + 1) * Q_BLOCK)
            q_ref[0, g, b, 0:HEAD_DIM, :] = jnp.concatenate(
                [qn_t[(g * N_REP + r) * HEAD_DIM:(g * N_REP + r + 1) * HEAD_DIM, cols] for r in range(N_REP)], axis=1)
            q_ref[0, g, b, HEAD_DIM:, :] = qc_ref[g]
            gate_ref[0, g, b] = gate_t[g * GATE_ROWS:(g + 1) * GATE_ROWS, cols]
    kc_ref[...] = z[:, _C_KC:_C_KC + D_KV]
    vc_ref[...] = z[:, _C_VC:_C_VC + D_KV]
    for col, gain, feat_ref, k_ref in ((_C_KS, kg_ref[1:2, :], fs_ref, ks_ref), (_C_KW, kg_ref[2:3, :], fw_ref, kw_ref)):
        kn = head_norm(col, N_KV, gain, 1.0).astype(BF16)
        for g in range(N_KV):
            k_ref[0, g] = jnp.concatenate([kn[:, g * HEAD_DIM:(g + 1) * HEAD_DIM], feat_ref[...]], axis=1)
    for col, vt_ref in ((_C_VS, vs_ref), (_C_VW, vw_ref)):
        vt = z[:, col:col + D_KV].T
        for g in range(N_KV):
            vt_ref[0, g] = vt[g * HEAD_DIM:(g + 1) * HEAD_DIM, :].astype(BF16)
    u_ref[...] = jax.nn.gelu(z[:, _C_U:_C_U + D_GM])
    gv = jax.nn.gelu(z[:, _C_V:_C_V + D_GM])
    v_ref[...] = (gv * _rms(gv)) * vg_ref[...]


def _inproj(x2, norm1_g, w_r, q_norm_g, k_norm_g, gm_v_norm_g, feat_s, feat_w, q_coef, batch, seq):
    n = x2.shape[0]
    tps = seq // TM_IN
    qpt = TM_IN // Q_BLOCK
    nq = N_REP * Q_BLOCK
    q_blocks = lambda r, c: pl.BlockSpec((1, N_KV, qpt, r, c), lambda i: (i // tps, 0, i % tps, 0, 0))
    row = lambda c: pl.BlockSpec((TM_IN, c), lambda i: (i, 0))
    full = lambda a: pl.BlockSpec(a.shape, lambda i: (0,) * a.ndim)
    per_seq = lambda a: pl.BlockSpec((TM_IN, a.shape[1]), lambda i: (i % tps, 0))
    keys = lambda w: pl.BlockSpec((1, N_KV, TM_IN, w), lambda i: (i // tps, 0, i % tps, 0))
    vals_t = pl.BlockSpec((1, N_KV, HEAD_DIM, TM_IN), lambda i: (i // tps, 0, 0, i % tps))
    g1 = norm1_g.reshape(1, D_MODEL)
    qg = q_norm_g.reshape(1, HEAD_DIM)
    vg = gm_v_norm_g.reshape(1, D_GM)
    ws, ww = HEAD_DIM + feat_s.shape[1], HEAD_DIM + feat_w.shape[1]
    tok = lambda c: jax.ShapeDtypeStruct((n, c), F32)
    return pl.pallas_call(
        _inproj_kernel,
        grid=(n // TM_IN,),
        in_specs=[row(D_MODEL), full(g1), full(w_r), full(qg), full(k_norm_g), full(vg), per_seq(feat_s),
                  per_seq(feat_w), full(q_coef)],
        out_specs=[q_blocks(LANE, nq), row(D_KV), row(D_KV), keys(ws), vals_t, keys(ww), vals_t,
                   q_blocks(GATE_ROWS, Q_BLOCK), row(D_GM), row(D_GM)],
        out_shape=[jax.ShapeDtypeStruct((batch, N_KV, seq // Q_BLOCK, LANE, nq), BF16), tok(D_KV), tok(D_KV),
                   jax.ShapeDtypeStruct((batch, N_KV, seq, ws), BF16),
                   jax.ShapeDtypeStruct((batch, N_KV, HEAD_DIM, seq), BF16),
                   jax.ShapeDtypeStruct((batch, N_KV, seq, ww), BF16),
                   jax.ShapeDtypeStruct((batch, N_KV, HEAD_DIM, seq), BF16),
                   jax.ShapeDtypeStruct((batch, N_KV, seq // Q_BLOCK, GATE_ROWS, Q_BLOCK), F32),
                   tok(D_GM), tok(D_GM)],
        compiler_params=pltpu.CompilerParams(dimension_semantics=("arbitrary",), vmem_limit_bytes=VMEM_LIMIT),
        name="inproj",
    )(x2, g1, w_r, qg, k_norm_g, vg, feat_s, feat_w, q_coef)


def _compress_kernel(a_ref, pos_ref, w1_ref, w1a_ref, w1b_ref, b1_ref, w2_ref, b2_ref, kg_ref, o_ref, *, norm):
    a = a_ref[0]
    nseg = a.shape[0]
    c = jnp.dot(pos_ref[...], w1_ref[...], precision=HIGHEST, preferred_element_type=F32)[0:1] + b1_ref[...]
    row = lax.broadcasted_iota(jnp.int32, (nseg, 1), 0)
    for g in range(N_KV):
        pa = jnp.dot(a, w1a_ref[g], precision=HIGHEST, preferred_element_type=F32)
        pb = jnp.dot(a, w1b_ref[g], precision=HIGHEST, preferred_element_type=F32)
        hid = jax.nn.gelu(pa + pltpu.roll(pb, nseg - 1, 0) + c)
        out = jnp.dot(hid, w2_ref[...], precision=HIGHEST, preferred_element_type=F32) + b2_ref[...]
        if norm:
            out = (out * _rms(out)) * kg_ref[...]
        o_ref[0, g] = jnp.where(row < nseg - 1, out, 0.0)


def _compress(raw, pos, w1, b1, w2, b2, gain, batch, seq, norm):
    nseg = seq // CMP_STRIDE
    half = CMP_STRIDE * HEAD_DIM
    a = raw.reshape(batch, nseg, CMP_STRIDE * D_KV)
    pos8 = jnp.broadcast_to(pos.reshape(1, CMP_LEN * HEAD_DIM), (8, CMP_LEN * HEAD_DIM))

    def expand(wh):
        wh = wh.reshape(CMP_STRIDE, HEAD_DIM, CMP_HIDDEN)
        z = jnp.zeros((N_KV, CMP_STRIDE, N_KV, HEAD_DIM, CMP_HIDDEN), F32)
        for g in range(N_KV):
            z = z.at[g, :, g].set(wh)
        return z.reshape(N_KV, CMP_STRIDE * D_KV, CMP_HIDDEN)

    w1a, w1b = expand(w1[:half]), expand(w1[half:])
    b1r, b2r, gr = b1.reshape(1, CMP_HIDDEN), b2.reshape(1, HEAD_DIM), gain.reshape(1, HEAD_DIM)
    full = lambda t: pl.BlockSpec(t.shape, lambda i: (0,) * t.ndim)
    return pl.pallas_call(
        functools.partial(_compress_kernel, norm=norm),
        grid=(batch,),
        in_specs=[pl.BlockSpec((1, nseg, CMP_STRIDE * D_KV), lambda i: (i, 0, 0)),
                  full(pos8), full(w1), full(w1a), full(w1b), full(b1r), full(w2), full(b2r), full(gr)],
        out_specs=pl.BlockSpec((1, N_KV, nseg, HEAD_DIM), lambda i: (i, 0, 0, 0)),
        out_shape=jax.ShapeDtypeStruct((batch, N_KV, nseg, HEAD_DIM), F32),
        compiler_params=pltpu.CompilerParams(dimension_semantics=("arbitrary",), vmem_limit_bytes=VMEM_LIMIT),
        name="compress_k" if norm else "compress_v",
    )(a, pos8, w1, w1a, w1b, b1r, w2, b2r, gr)


def _attn_kernel(qt_ref, kc_ref, vct_ref, ks_ref, vst_ref, kw_ref, vwt_ref, g_ref, ovt_ref, o_ref,
                 qs_ref, s0_ref, s1_ref, p0_ref, p1_ref, st_ref, acc_ref, ocmp_ref, imp_ref):
    qb = pl.program_id(1)
    nq = N_REP * Q_BLOCK
    q0 = qb * Q_BLOCK
    qt = qt_ref[0, 0]
    ql = lax.broadcasted_iota(jnp.int32, (1, nq), 1) % Q_BLOCK
    t_row = (q0 + ql).astype(F32)
    m_init = 0.5 * NEG

    def online(s, m, l):
        m_new = jnp.maximum(m, jnp.max(s, axis=0, keepdims=True))
        alpha = jnp.exp2(m - m_new)
        p = jnp.exp2(s - m_new)
        return p, m_new, alpha, alpha * l + jnp.sum(p, axis=0, keepdims=True)

    def inv(l):
        return jnp.where(l > 0.0, 1.0 / l, 0.0)

    m0 = jnp.full((1, nq), m_init, F32)
    l0 = jnp.zeros((1, nq), F32)
    a0 = jnp.zeros((HEAD_DIM, nq), F32)

    ncmp = kc_ref.shape[1]
    nsel = ovt_ref.shape[0]
    n_var = 4
    nqb_total = pl.num_programs(1)

    def cmp_branch(rows):
        s = jnp.dot(kc_ref[0, 0:rows, :], qt, preferred_element_type=F32)
        c_end = (lax.broadcasted_iota(jnp.int32, (rows, 1), 0) * CMP_STRIDE + (CMP_LEN - 1)).astype(F32)
        p, _, _, l = online(jnp.where(c_end <= t_row, s, NEG), m0, l0)
        p = p * inv(l)
        ocmp_ref[...] = jnp.dot(vct_ref[0, :, 0:rows], p.astype(BF16), preferred_element_type=F32)
        psum = p[:, 0:Q_BLOCK]
        for r in range(1, N_REP):
            psum = psum + p[:, r * Q_BLOCK:(r + 1) * Q_BLOCK]
        p_hi = psum.astype(BF16)
        p_lo = (psum - p_hi.astype(F32)).astype(BF16)
        imp_ref[...] = (jnp.dot(ovt_ref[:, 0:rows], p_hi, preferred_element_type=F32)
                        + jnp.dot(ovt_ref[:, 0:rows], p_lo, preferred_element_type=F32))

    if ncmp % (n_var * LANE) == 0:
        part = (qb * n_var) // nqb_total
        for v in range(n_var):
            pl.when(part == v)(functools.partial(cmp_branch, (v + 1) * ncmp // n_var))
    else:
        cmp_branch(ncmp)
    o_cmp = ocmp_ref[...]
    imp = imp_ref[...]
    n_col = lax.broadcasted_iota(jnp.int32, (nsel, 1), 0).astype(F32)
    n_start = n_col * SEL_BLOCK
    tq = t_row[:, 0:Q_BLOCK]
    cur = jnp.floor(tq * (1.0 / SEL_BLOCK)) * SEL_BLOCK
    forced = (n_start == cur) | (n_start == 0.0)
    valid = n_start <= tq
    imp = jnp.where(forced, imp + FORCE_BONUS, imp)
    imp = jnp.where(valid, imp, NEG)
    sel = jnp.zeros((nsel, Q_BLOCK), F32)
    for _ in range(min(N_SEL, nsel)):
        mx = jnp.max(imp, axis=0, keepdims=True)
        first = jnp.min(jnp.where(imp == mx, n_col, float(nsel)), axis=0, keepdims=True)
        hit = n_col == first
        sel = jnp.where(hit, 1.0, sel)
        imp = jnp.where(hit, -jnp.inf, imp)
    selb = jnp.where(valid & (sel > 0.0), 0.0, NEG).astype(BF16)
    qs_ref[0:LANE, :] = qt
    qs_ref[LANE:LANE + nsel, :] = jnp.concatenate([selb] * N_REP, axis=1)
    if qs_ref.shape[0] > LANE + nsel:
        qs_ref[LANE + nsel:, :] = jnp.zeros((qs_ref.shape[0] - LANE - nsel, nq), BF16)

    def attend(k_blk, vt_blk, q_op, bias, carry):
        m, l, acc = carry
        s = jnp.dot(k_blk, q_op, preferred_element_type=F32)
        if bias is not None:
            s = s + bias
        p, m, alpha, l = online(s, m, l)
        pv = jnp.dot(vt_blk, p.astype(BF16), preferred_element_type=F32)
        return m, l, alpha * acc + pv

    seq = ks_ref.shape[1]

    def scores(j):
        k0 = pl.multiple_of(jnp.minimum(j * KC_SEL, seq - KC_SEL), KC_SEL)
        s = jnp.dot(ks_ref[0, pl.ds(k0, KC_SEL), :], qs_ref[...], preferred_element_type=F32)
        return s, jnp.max(s, axis=0, keepdims=True)

    def values(j, p):
        k0 = pl.multiple_of(jnp.maximum(j, 0) * KC_SEL, KC_SEL)
        return jnp.dot(vst_ref[0, :, pl.ds(k0, KC_SEL)], p, preferred_element_type=F32)

    def stage(j, s_cur, s_nxt, p_cur, p_prv):
        m, l, alpha_prev, mx = st_ref[0:1, :], st_ref[1:2, :], st_ref[2:3, :], st_ref[3:4, :]
        m_new = jnp.maximum(m, mx)
        alpha = jnp.exp2(m - m_new)
        k0 = pl.multiple_of(jnp.minimum((j + 1) * KC_SEL, seq - KC_SEL), KC_SEL)
        k0p = pl.multiple_of(jnp.maximum(j - 1, 0) * KC_SEL, KC_SEL)
        sub = KC_SEL // 4
        psum, mx_next, zeros = None, None, []
        for q in range(4):
            rows = slice(q * sub, (q + 1) * sub)
            k_q = ks_ref[0, pl.ds(k0 + q * sub, sub), :]
            if q >= 1:
                k_q = k_q + jnp.concatenate([zeros[q - 1]] * (ks_ref.shape[2] // LANE), axis=1)
            s_q = jnp.dot(k_q, qs_ref[...], preferred_element_type=F32)
            s_nxt[rows, :] = s_q
            mx_q = jnp.max(s_q, axis=0, keepdims=True)
            mx_next = mx_q if mx_next is None else jnp.maximum(mx_next, mx_q)
            p_q = jnp.exp2(s_cur[rows, :] - m_new)
            ps_q = jnp.sum(p_q, axis=0, keepdims=True)
            psum = ps_q if psum is None else psum + ps_q
            p_q = p_q.astype(BF16)
            p_cur[rows, :] = p_q
            dep = ps_q[:, 0:LANE]
            for r in range(1, N_REP):
                dep = dep + ps_q[:, r * LANE:(r + 1) * LANE]
            bits = pltpu.bitcast(dep, jnp.int32)
            zeros.append(lax.shift_right_logical(lax.shift_right_logical(bits, 16), 16).astype(F32).astype(BF16))
            if q == 1:
                vt_prev = vst_ref[0, :, pl.ds(k0p, KC_SEL)] + jnp.concatenate([zeros[1]] * (KC_SEL // LANE), axis=1)
                acc_ref[...] = alpha_prev * acc_ref[...] + jnp.dot(vt_prev, p_prv[...], preferred_element_type=F32)
        st_ref[0:1, :] = m_new
        st_ref[1:2, :] = alpha * l + psum
        st_ref[2:3, :] = alpha
        st_ref[3:4, :] = mx_next

    n_full = q0 // KC_SEL
    s_first, mx_first = scores(0)

    @pl.when(n_full % 2 == 0)
    def _():
        s0_ref[...] = s_first

    @pl.when(n_full % 2 == 1)
    def _():
        s1_ref[...] = s_first

    p0_ref[...] = jnp.zeros(p0_ref.shape, BF16)
    p1_ref[...] = jnp.zeros(p1_ref.shape, BF16)
    st_ref[0:1, :] = m0
    st_ref[1:2, :] = l0
    st_ref[2:3, :] = jnp.ones((1, nq), F32)
    st_ref[3:4, :] = mx_first
    acc_ref[...] = a0

    def sel_body(j, carry):
        @pl.when((n_full - j) % 2 == 0)
        def _():
            stage(j, s0_ref, s1_ref, p0_ref, p1_ref)

        @pl.when((n_full - j) % 2 == 1)
        def _():
            stage(j, s1_ref, s0_ref, p1_ref, p0_ref)
        return carry

    lax.fori_loop(0, n_full, sel_body, 0)
    pos_last = (n_full * KC_SEL + lax.broadcasted_iota(jnp.int32, (KC_SEL, 1), 0)).astype(F32)
    p, _, alpha, l_sel = online(s0_ref[...] + jnp.where(pos_last <= t_row, 0.0, NEG), st_ref[0:1, :], st_ref[1:2, :])
    acc = st_ref[2:3, :] * acc_ref[...] + values(n_full - 1, p1_ref[...])
    o_sel = alpha * acc + values(n_full, p.astype(BF16))

    n_wk = WINDOW + Q_BLOCK
    w0 = pl.multiple_of(jnp.maximum(q0 - WINDOW, 0), Q_BLOCK)
    kk = lax.broadcasted_iota(jnp.int32, (n_wk, 1), 0) - (q0 - w0)
    in_win = (kk <= ql) & (kk + WINDOW > ql)
    _, l_win, o_win = attend(kw_ref[0, pl.ds(w0, n_wk), :], vwt_ref[0, :, pl.ds(w0, n_wk)], qt,
                             jnp.where(in_win, 0.0, NEG), (m0, l0, a0))

    gt = g_ref[0, 0]

    def gate(j):
        return jnp.concatenate([gt[j * N_REP + r:j * N_REP + r + 1, :] for r in range(N_REP)], axis=1)

    o_ref[0, 0] = (gate(0) * o_cmp + gate(1) * (o_sel * inv(l_sel)) + gate(2) * (o_win * inv(l_win)))


def _attention(qt, kc, vct, ks, vst, kw, vwt, gt, ovt):
    bgn, nqb = qt.shape[0], qt.shape[1]
    seq = ks.shape[1]
    nq = N_REP * Q_BLOCK
    per_bg = lambda a: pl.BlockSpec((1,) + a.shape[1:], lambda b, i: (b,) + (0,) * (a.ndim - 1))
    return pl.pallas_call(
        _attn_kernel,
        grid=(bgn, nqb),
        in_specs=[pl.BlockSpec((1, 1, LANE, nq), lambda b, i: (b, i, 0, 0)),
                  per_bg(kc), per_bg(vct), per_bg(ks), per_bg(vst), per_bg(kw), per_bg(vwt),
                  pl.BlockSpec((1, 1, GATE_ROWS, Q_BLOCK), lambda b, i: (b, i, 0, 0)),
                  pl.BlockSpec(ovt.shape, lambda b, i: (0, 0))],
        out_specs=pl.BlockSpec((1, 1, HEAD_DIM, nq), lambda b, i: (b, i, 0, 0)),
        out_shape=jax.ShapeDtypeStruct((bgn, nqb, HEAD_DIM, nq), F32),
        scratch_shapes=[pltpu.VMEM((ks.shape[2], nq), BF16),
                        pltpu.VMEM((KC_SEL, nq), F32), pltpu.VMEM((KC_SEL, nq), F32),
                        pltpu.VMEM((KC_SEL, nq), BF16), pltpu.VMEM((KC_SEL, nq), BF16),
                        pltpu.VMEM((8, nq), F32), pltpu.VMEM((HEAD_DIM, nq), F32),
                        pltpu.VMEM((HEAD_DIM, nq), F32), pltpu.VMEM((ovt.shape[0], Q_BLOCK), F32)],
        compiler_params=pltpu.CompilerParams(dimension_semantics=("arbitrary", "arbitrary"),
                                             vmem_limit_bytes=VMEM_LIMIT),
        name="nsa_attention",
    )(qt, kc, vct, ks, vst, kw, vwt, gt, ovt)


def _mix_kernel(x_ref, oa_ref, u_ref, v_ref, ws_ref, bs_ref, ga_ref, gg_ref, wo_ref, g2_ref, wr_ref, br_ref,
                x1_ref, xn_ref, idx_ref, gate_ref, rank_ref, cnt_out_ref, cnt_ref):
    tm = x_ref.shape[0]
    rr = lax.broadcasted_iota(jnp.int32, (GM_CHUNK, GM_CHUNK), 0)
    cc = lax.broadcasted_iota(jnp.int32, (GM_CHUNK, GM_CHUNK), 1)
    grp = lax.broadcasted_iota(jnp.int32, (1, D_GM), 1) // GM_GROUP_DIM
    ws = [jnp.where(rr >= cc, ws_ref[g], 0.0).astype(BF16) for g in range(N_GM_GROUPS)]
    ys = []
    for c in range(tm // GM_CHUNK):
        vch = v_ref[c * GM_CHUNK:(c + 1) * GM_CHUNK, :].astype(BF16)
        y = bs_ref[...]
        for g in range(N_GM_GROUPS):
            y = y + jnp.where(grp == g, jnp.dot(ws[g], vch, preferred_element_type=F32), 0.0)
        ys.append(y)
    o_gm = u_ref[...] * jnp.concatenate(ys, axis=0)
    o_at = jnp.concatenate(
        [jnp.concatenate([oa_ref[0, g, b, :, r * Q_BLOCK:(r + 1) * Q_BLOCK]
                          for g in range(N_KV) for r in range(N_REP)], axis=0).T
         for b in range(tm // Q_BLOCK)], axis=0)
    mixed =jnp.concatenate([(o_at * _rms(o_at)) * ga_ref[...], (o_gm * _rms(o_gm)) * gg_ref[...]], axis=-1)
    x1 = x_ref[...] + jnp.dot(mixed.astype(BF16), wo_ref[...], preferred_element_type=F32)
    x1_ref[...] = x1
    xn = (x1 * _rms(x1)) * g2_ref[...]
    for s in range(ROW_SUB):
        xn_ref[:, s, :] = xn[:, s * LANE:(s + 1) * LANE]
    logits = jnp.dot(xn, wr_ref[...], precision=HIGHEST, preferred_element_type=F32) + br_ref[...]
    lane = lax.broadcasted_iota(jnp.int32, (1, LANE), 1).astype(F32)
    idx_out = jnp.zeros((tm, LANE), F32)
    val_out = jnp.zeros((tm, LANE), F32)
    vals, firsts = [], []
    for k in range(TOP_K):
        mx = jnp.max(logits, axis=-1, keepdims=True)
        first = jnp.min(jnp.where(logits == mx, lane, float(LANE)), axis=-1, keepdims=True)
        logits = jnp.where(lane == first, -jnp.inf, logits)
        idx_out = jnp.where(lane == float(k), first, idx_out)
        vals.append(mx)
        firsts.append(first)
    es = [jnp.exp(v - vals[0]) for v in vals]
    den = es[0] + es[1] + es[2] + es[3]
    for k in range(TOP_K):
        val_out = jnp.where(lane == float(k), es[k] / den, val_out)
    idx_ref[...] = idx_out.astype(jnp.int32)
    gate_ref[...] = val_out

    @pl.when(pl.program_id(0) == 0)
    def _():
        cnt_ref[...] = jnp.zeros(cnt_ref.shape, F32)

    hit = (lane == firsts[0]) | (lane == firsts[1]) | (lane == firsts[2]) | (lane == firsts[3])
    hit_b = jnp.where(hit, 1.0, 0.0).astype(BF16)
    tr = lax.broadcasted_iota(jnp.int32, (tm, tm), 0)
    tc = lax.broadcasted_iota(jnp.int32, (tm, tm), 1)
    before = jnp.where(tr > tc, 1.0, 0.0).astype(BF16)
    ranks = jnp.dot(before, hit_b, preferred_element_type=F32) + cnt_ref[0:1, :]
    rank_out = jnp.zeros((tm, LANE), F32)
    for k in range(TOP_K):
        r_k = jnp.sum(jnp.where(lane == firsts[k], ranks, 0.0), axis=-1, keepdims=True)
        rank_out = jnp.where(lane == float(k), r_k, rank_out)
    rank_ref[...] = rank_out.astype(jnp.int32)
    cnt_ref[0:1, :] = cnt_ref[0:1, :] + jnp.sum(hit_b.astype(F32), axis=0, keepdims=True)
    cnt_out_ref[...] = cnt_ref[...].astype(jnp.int32)


def _mix(x2, ot, u_act, v_act, gm_w_s, bias_full, ga, gg, w_out_b, g2, wr_pad, br_pad):
    n = x2.shape[0]
    tps = ot.shape[2] * Q_BLOCK // TM_MIX
    row = lambda c: pl.BlockSpec((TM_MIX, c), lambda i: (i, 0))
    full = lambda a: pl.BlockSpec(a.shape, lambda i: (0,) * a.ndim)
    attn_blocks = pl.BlockSpec((1, N_KV, TM_MIX // Q_BLOCK) + ot.shape[3:], lambda i: (i // tps, 0, i % tps, 0, 0))
    return pl.pallas_call(
        _mix_kernel,
        grid=(n // TM_MIX,),
        in_specs=[row(D_MODEL), attn_blocks, row(D_GM), row(D_GM), full(gm_w_s), full(bias_full), full(ga), full(gg),
                  full(w_out_b), full(g2), full(wr_pad), full(br_pad)],
        out_specs=[row(D_MODEL), pl.BlockSpec((TM_MIX, ROW_SUB, LANE), lambda i: (i, 0, 0)), row(LANE), row(LANE),
                   row(LANE), pl.BlockSpec((ROW_SUB, LANE), lambda i: (0, 0))],
        out_shape=[jax.ShapeDtypeStruct((n, D_MODEL), F32), jax.ShapeDtypeStruct((n, ROW_SUB, LANE), F32),
                   jax.ShapeDtypeStruct((n, LANE), jnp.int32), jax.ShapeDtypeStruct((n, LANE), F32),
                   jax.ShapeDtypeStruct((n, LANE), jnp.int32), jax.ShapeDtypeStruct((ROW_SUB, LANE), jnp.int32)],
        scratch_shapes=[pltpu.VMEM((ROW_SUB, LANE), F32)],
        compiler_params=pltpu.CompilerParams(dimension_semantics=("arbitrary",), vmem_limit_bytes=VMEM_LIMIT),
        name="mix_outproj_router",
    )(x2, ot, u_act, v_act, gm_w_s, bias_full, ga, gg, w_out_b, g2, wr_pad, br_pad)


def _row_gather(idx_ref, n_rows, src_hbm, dst_ref, sem):
    def start():
        for r in range(n_rows):
            pltpu.make_async_copy(src_hbm.at[pl.ds(idx_ref[0, 0, r], 1), :], dst_ref.at[pl.ds(r, 1), :], sem).start()

    def wait():
        pltpu.make_async_copy(src_hbm.at[pl.ds(0, n_rows), :], dst_ref, sem).wait()

    return start, wait


def _tile_row_gather(idx_ref, n_rows, src_hbm, dst_ref, sem):
    def start():
        for r in range(n_rows):
            t = idx_ref[0, 0, r]
            pltpu.make_async_copy(src_hbm.at[lax.shift_right_logical(t, 3), t & (ROW_SUB - 1)],
                                  dst_ref.at[r // ROW_SUB, :, r % ROW_SUB, :], sem).start(priority=r % 2)

    def wait():
        pltpu.make_async_copy(src_hbm.at[pl.ds(0, n_rows // ROW_SUB)], dst_ref, sem).wait()

    return start, wait


def _tiles_to_matrix(ref):
    rows = ref.shape[0] * ROW_SUB
    return jnp.concatenate([ref[:, c].reshape(rows, LANE) for c in range(ROW_SUB)], axis=1)


def _moe_kernel(be_ref, bv_ref, bn_ref, bs_ref, tok_ref, tok_n1_ref, tok_n2_ref, x_hbm, wgu_hbm, bg_ref, bl_ref,
                wd_hbm, bd_ref, o_ref, xbuf, sems, wt_s, wg_s, wl_s, wd_s, wgu_buf, wd_buf, wsems):
    i = pl.program_id(0)
    slot = i % (MOE_AHEAD + 1)
    slot_n2 = (i + MOE_AHEAD) % (MOE_AHEAD + 1)
    start_cur, wait_cur = _tile_row_gather(tok_ref, BM_MOE, x_hbm, xbuf.at[slot], sems.at[slot])
    start_n1, _ = _tile_row_gather(tok_n1_ref, BM_MOE, x_hbm, xbuf.at[1], sems.at[1])
    start_n2, _ = _tile_row_gather(tok_n2_ref, BM_MOE, x_hbm, xbuf.at[slot_n2], sems.at[slot_n2])

    prev = jnp.maximum(i - 1, 0)

    def fetch_weights(e, ws):
        n_piece = 4
        rows = D_MODEL // n_piece
        copies = [pltpu.make_async_copy(wgu_hbm.at[e, pl.ds(c * rows, rows)], wgu_buf.at[ws, pl.ds(c * rows, rows)],
                                        wsems.at[ws]) for c in range(n_piece)]
        copies += [pltpu.make_async_copy(wd_hbm.at[e, pl.ds(c * rows, rows)], wd_buf.at[ws, pl.ds(c * rows, rows)],
                                         wsems.at[ws]) for c in range(n_piece)]
        return copies

    @pl.when(i == 0)
    def _():
        start_cur()
        start_n1()
        for cp in fetch_weights(be_ref[0], 0):
            cp.start()

    @pl.when((bv_ref[i] == 1) & ((i == 0) | (be_ref[i] != be_ref[prev])))
    def _():
        ws = bs_ref[i]
        for cp in fetch_weights(be_ref[i], ws):
            cp.wait()

        @pl.when(bn_ref[i] >= 0)
        def _():
            for cp in fetch_weights(bn_ref[i], 1 - ws):
                cp.start()

        tc = wt_s.shape[1]
        for c in range(2 * D_EXPERT // tc):
            wt = wgu_buf[ws, :, c * tc:(c + 1) * tc].T
            for j in range(ROW_SUB):
                wt_s[j] = wt[:, j * LANE:(j + 1) * LANE]
            for first, dst in ((0, wg_s), (1, wl_s)):
                half = jnp.concatenate([wt_s[j, pl.ds(first, tc // 2, stride=2), :] for j in range(ROW_SUB)], axis=1)
                dst[c * tc // 2:(c + 1) * tc // 2, :] = half.astype(BF16)
        wd_s[...] = wd_buf[ws].astype(BF16)

    @pl.when(bv_ref[i] == 1)
    def _():
        wait_cur()
        start_n2()
        xb = _tiles_to_matrix(xbuf.at[slot]).astype(BF16)
        hg = lax.dot_general(xb, wg_s[...], _NT, preferred_element_type=F32) + bg_ref[0]
        hl = lax.dot_general(xb, wl_s[...], _NT, preferred_element_type=F32) + bl_ref[0]
        hg = jnp.minimum(hg, SWIGLU_LIMIT)
        hl = jnp.clip(hl, -SWIGLU_LIMIT, SWIGLU_LIMIT)
        a = hg * jax.nn.sigmoid(SWIGLU_ALPHA * hg) * (hl + 1.0)
        o_ref[...] = jnp.dot(a.astype(BF16), wd_s[...], preferred_element_type=F32) + bd_ref[0]

    @pl.when((bv_ref[i] == 0) & ((i == 1) | ((i >= MOE_AHEAD) & (bv_ref[jnp.maximum(i - MOE_AHEAD, 0)] == 1))))
    def _():
        wait_cur()

    @pl.when(bv_ref[i] == 0)
    def _():
        o_ref[...] = jnp.zeros(o_ref.shape, F32)


def _moe(blk_expert, blk_valid, blk_next, blk_wslot, tok_blocks, xn3, w_gate_up, bg, bl, w_down, bd):
    nb = blk_expert.shape[0]
    per_e = lambda a: pl.BlockSpec((1,) + a.shape[1:], lambda i, be, *_: (be[i],) + (0,) * (a.ndim - 1))

    def tok_spec(ahead):
        return pl.BlockSpec((1, 1, BM_MOE), lambda i, *_: (jnp.minimum(i + ahead, nb - 1), 0, 0),
                            memory_space=pltpu.SMEM)

    grid_spec = pltpu.PrefetchScalarGridSpec(
        num_scalar_prefetch=4,
        grid=(nb,),
        in_specs=[tok_spec(0), tok_spec(1), tok_spec(MOE_AHEAD),
                  pl.BlockSpec(memory_space=pl.ANY),
                  pl.BlockSpec(memory_space=pl.ANY), per_e(bg), per_e(bl),
                  pl.BlockSpec(memory_space=pl.ANY), per_e(bd)],
        out_specs=pl.BlockSpec((BM_MOE, D_MODEL), lambda i, *_: (i, 0)),
        scratch_shapes=[pltpu.VMEM((MOE_AHEAD + 1, BM_MOE // ROW_SUB, ROW_SUB, ROW_SUB, LANE), F32),
                        pltpu.SemaphoreType.DMA((MOE_AHEAD + 1,)),
                        pltpu.VMEM((ROW_SUB, 256, LANE), F32), pltpu.VMEM((D_EXPERT, D_MODEL), BF16),
                        pltpu.VMEM((D_EXPERT, D_MODEL), BF16), pltpu.VMEM((D_EXPERT, D_MODEL), BF16),
                        pltpu.VMEM((2, D_MODEL, 2 * D_EXPERT), F32), pltpu.VMEM((2, D_EXPERT, D_MODEL), F32),
                        pltpu.SemaphoreType.DMA((2,))],
    )
    return pl.pallas_call(
        _moe_kernel,
        grid_spec=grid_spec,
        out_shape=jax.ShapeDtypeStruct((nb * BM_MOE, D_MODEL), F32),
        compiler_params=pltpu.CompilerParams(dimension_semantics=("arbitrary",), vmem_limit_bytes=VMEM_LIMIT_MOE),
        name="moe_experts",
    )(blk_expert, blk_valid, blk_next, blk_wslot, tok_blocks, tok_blocks, tok_blocks,
      xn3.reshape(xn3.shape[0] // ROW_SUB, ROW_SUB, ROW_SUB, LANE), w_gate_up, bg, bl, w_down, bd)


def _slot_table_kernel(lo_ref, hi_ref, dest_ref, tok_ref):
    i = pl.program_id(0)
    n_chunk = dest_ref.shape[2]

    @pl.when(i == 0)
    def _():
        def clear(s, carry):
            tok_ref[s] = 0
            return carry
        for e in range(lo_ref.shape[0]):
            lax.fori_loop(lo_ref[e], hi_ref[e], clear, 0)

    def put(s, carry):
        tok_ref[dest_ref[0, 0, s]] = lax.shift_right_logical(i * n_chunk + s, TOP_K.bit_length() - 1)
        return carry
    lax.fori_loop(0, n_chunk, put, 0, unroll=8)


def _slot_table(pad_lo, pad_hi, dest, n_slots):
    n_chunk = 8192
    s_tot = dest.shape[0]
    grid_spec = pltpu.PrefetchScalarGridSpec(
        num_scalar_prefetch=2,
        grid=(s_tot // n_chunk,),
        in_specs=[pl.BlockSpec((1, 1, n_chunk), lambda i, *_: (i, 0, 0), memory_space=pltpu.SMEM)],
        out_specs=pl.BlockSpec(memory_space=pltpu.SMEM),
    )
    return pl.pallas_call(
        _slot_table_kernel,
        grid_spec=grid_spec,
        out_shape=jax.ShapeDtypeStruct((n_slots,), jnp.int32),
        compiler_params=pltpu.CompilerParams(dimension_semantics=("arbitrary",)),
        name="moe_slot_table",
    )(pad_lo, pad_hi, dest.reshape(s_tot // n_chunk, 1, n_chunk))


def _combine_kernel(dest_ref, dest_next_ref, x1_ref, gate_ref, y_hbm, o_ref, buf, sems):
    i = pl.program_id(0)
    slot = i % 2
    n_rows = TOP_K * TM_CMB
    start_cur, wait_cur = _row_gather(dest_ref, n_rows, y_hbm, buf.at[slot], sems.at[slot])
    start_next, _ = _row_gather(dest_next_ref, n_rows, y_hbm, buf.at[1 - slot], sems.at[1 - slot])

    @pl.when(i == 0)
    def _():
        start_cur()

    @pl.when(i + 1 < pl.num_programs(0))
    def _():
        start_next()

    wait_cur()
    gate = gate_ref[...]
    acc = x1_ref[...]
    for k in range(TOP_K):
        acc = acc + gate[:, k:k + 1] * buf[slot, k * TM_CMB:(k + 1) * TM_CMB, :]
    o_ref[...] = acc


def _combine(dest_blocks, x1, gate_pad, y_rows):
    n = x1.shape[0]
    nt = n // TM_CMB
    n_rows = TOP_K * TM_CMB
    return pl.pallas_call(
        _combine_kernel,
        grid=(nt,),
        in_specs=[pl.BlockSpec((1, 1, n_rows), lambda i: (i, 0, 0), memory_space=pltpu.SMEM),
                  pl.BlockSpec((1, 1, n_rows), lambda i: (jnp.minimum(i + 1, nt - 1), 0, 0),
                               memory_space=pltpu.SMEM),
                  pl.BlockSpec((TM_CMB, D_MODEL), lambda i: (i, 0)),
                  pl.BlockSpec((TM_CMB, LANE), lambda i: (i, 0)),
                  pl.BlockSpec(memory_space=pl.ANY)],
        out_specs=pl.BlockSpec((TM_CMB, D_MODEL), lambda i: (i, 0)),
        out_shape=jax.ShapeDtypeStruct((n, D_MODEL), F32),
        scratch_shapes=[pltpu.VMEM((2, n_rows, D_MODEL), F32), pltpu.SemaphoreType.DMA((2,))],
        compiler_params=pltpu.CompilerParams(dimension_semantics=("arbitrary",), vmem_limit_bytes=VMEM_LIMIT),
        name="moe_combine",
    )(dest_blocks, dest_blocks, x1, gate_pad, y_rows)


def kernel(x, norm1_g, w_in, q_norm_g, k_norm_g, cmp_pos, w_cmp1, b_cmp1, w_cmp2, b_cmp2, gm_v_norm_g, gm_w_s,
           gm_b_s, out_norm_attn_g, out_norm_gm_g, w_out, norm2_g, w_router, b_router, w_gate_up, b_gate_up,
           w_down, b_down):
    batch, seq, _ = x.shape
    n = batch * seq
    nqb = seq // Q_BLOCK
    bgn = batch * N_KV
    x2 = x.reshape(n, D_MODEL)

    c_gate = D_ATTN + 6 * D_KV
    gate_src = np.full((LANE,), N_GATE, np.int32)
    for h in range(N_HEADS):
        for j in range(3):
            gate_src[(h // N_REP) * GATE_ROWS + j * N_REP + h % N_REP] = h * 3 + j
    w_gate = jnp.concatenate([w_in[:, c_gate:c_gate + N_GATE], jnp.zeros((D_MODEL, 1), F32)], axis=1)[:, gate_src]
    w_r = jnp.concatenate([w_in[:, :c_gate], w_in[:, c_gate + N_GATE:], w_gate], axis=1).astype(BF16)
    nq = N_REP * Q_BLOCK
    head = np.arange(N_KV)[:, None] * N_REP + np.arange(nq)[None, :] // Q_BLOCK
    coef = np.exp2(-(head + 1.0)) * LOG2E
    c_hi = coef.astype(BF16).astype(np.float64)
    c_lo = (coef - c_hi).astype(BF16).astype(np.float64)
    qrows = np.zeros((N_KV, LANE - HEAD_DIM, nq), np.float32)
    qrows[:, 0], qrows[:, 1], qrows[:, 2], qrows[:, 3] = SEL_BLOCK * c_hi, SEL_BLOCK * c_lo, c_hi, c_lo
    nsel = seq // SEL_BLOCK
    oh_w = -(-nsel // LANE) * LANE

    def pos_features(pos, one_hot=False):
        feat = np.zeros((pos.shape[0], LANE - HEAD_DIM + (oh_w if one_hot else 0)), np.float32)
        feat[:, 0] = feat[:, 1] = pos // SEL_BLOCK
        feat[:, 2] = feat[:, 3] = pos % SEL_BLOCK
        if one_hot:
            feat[np.arange(pos.shape[0]), LANE - HEAD_DIM + pos // SEL_BLOCK] = 1.0
        return jnp.asarray(feat, BF16)

    pos_t = np.arange(seq)
    qt, kc_raw, vc_raw, ks_aug, vst, kw_aug, vwt, gt, u_act, v_act = _inproj(
        x2, norm1_g, w_r, q_norm_g, k_norm_g, gm_v_norm_g, pos_features(pos_t, one_hot=True), pos_features(pos_t),
        jnp.asarray(qrows, BF16), batch, seq)

    kc = _compress(kc_raw, cmp_pos[0], w_cmp1[0], b_cmp1[0], w_cmp2[0], b_cmp2[0], k_norm_g[0], batch, seq, True)
    vc = _compress(vc_raw, cmp_pos[1], w_cmp1[1], b_cmp1[1], w_cmp2[1], b_cmp2[1], k_norm_g[0], batch, seq, False)

    ncmp = seq // CMP_STRIDE
    pos_c = np.arange(ncmp) * CMP_STRIDE + (CMP_LEN - 1)
    kc_b = jnp.concatenate([kc.reshape(bgn, ncmp, HEAD_DIM).astype(BF16),
                            jnp.broadcast_to(pos_features(pos_c)[None], (bgn, ncmp, LANE - HEAD_DIM))], axis=-1)
    vct = vc.reshape(bgn, ncmp, HEAD_DIM).transpose(0, 2, 1).astype(BF16)
    c0 = np.arange(ncmp)[None, :] * CMP_STRIDE
    n0 = np.arange(seq // SEL_BLOCK)[:, None] * SEL_BLOCK
    ovt = np.clip(np.minimum(c0 + CMP_LEN, n0 + SEL_BLOCK) - np.maximum(c0, n0), 0, None) / CMP_LEN
    per_group = lambda a: a.reshape((bgn,) + a.shape[2:])
    ot = _attention(per_group(qt), kc_b, vct, per_group(ks_aug), per_group(vst), per_group(kw_aug),
                    per_group(vwt), per_group(gt), jnp.asarray(ovt, BF16))
    ot = ot.reshape((batch, N_KV) + ot.shape[1:])

    bias_full = jnp.repeat(gm_b_s.T, GM_GROUP_DIM, axis=1)
    wr_pad = jnp.concatenate([w_router, jnp.zeros((D_MODEL, LANE - N_EXPERTS), F32)], axis=1)
    br_pad = jnp.concatenate([b_router, jnp.full((LANE - N_EXPERTS,), NEG, F32)]).reshape(1, LANE)
    x1, xn3, idx_pad, gate_pad, rank_pad, cnt_pad = _mix(
        x2, ot, u_act, v_act, gm_w_s, bias_full, out_norm_attn_g.reshape(1, D_ATTN),
        out_norm_gm_g.reshape(1, D_GM), w_out.astype(BF16), norm2_g.reshape(1, D_MODEL), wr_pad, br_pad)

    s_tot = n * TOP_K
    nb = s_tot // BM_MOE + N_EXPERTS - 1 + MOE_AHEAD
    e_flat = idx_pad[:, :TOP_K].reshape(s_tot)
    rank = rank_pad[:, :TOP_K].reshape(s_tot)
    counts = cnt_pad[0, :N_EXPERTS]
    padded = ((counts + BM_MOE - 1) // BM_MOE) * BM_MOE
    pad_end = jnp.cumsum(padded)
    pad_start = pad_end - padded
    dest = pad_start[e_flat] + rank
    pad_lo = jnp.concatenate([pad_start + counts, pad_end[-1:]]).astype(jnp.int32)
    pad_hi = jnp.concatenate([pad_end, jnp.full((1,), nb * BM_MOE)]).astype(jnp.int32)
    tok_buf = _slot_table(pad_lo, pad_hi, dest.astype(jnp.int32), nb * BM_MOE)
    blk_start = jnp.arange(nb, dtype=jnp.int32) * BM_MOE
    blk_expert = jnp.minimum(jnp.sum((blk_start[:, None] >= pad_end[None, :]).astype(jnp.int32), axis=1),
                             N_EXPERTS - 1)
    blk_valid = (blk_start < pad_end[-1]).astype(jnp.int32)
    e_ids = jnp.arange(N_EXPERTS, dtype=jnp.int32)
    present = counts > 0
    ordinal = jnp.cumsum(present.astype(jnp.int32)) - 1
    later = jnp.where(present[None, :] & (e_ids[None, :] > e_ids[:, None]), e_ids[None, :], N_EXPERTS)
    nxt = jnp.min(later, axis=1)
    blk_next = jnp.where(nxt < N_EXPERTS, nxt, -1)[blk_expert].astype(jnp.int32)
    blk_wslot = (ordinal[blk_expert] % 2).astype(jnp.int32)

    bg = b_gate_up[:, 0::2].reshape(N_EXPERTS, 1, D_EXPERT)
    bl = b_gate_up[:, 1::2].reshape(N_EXPERTS, 1, D_EXPERT)
    y_rows = _moe(blk_expert, blk_valid, blk_next, blk_wslot, tok_buf.reshape(nb, 1, BM_MOE), xn3, w_gate_up, bg, bl,
                  w_down,
                  b_down.reshape(N_EXPERTS, 1, D_MODEL))

    dest_blocks = (dest.reshape(n // TM_CMB, TM_CMB, TOP_K).transpose(0, 2, 1)
                   .reshape(n // TM_CMB, 1, TOP_K * TM_CMB).astype(jnp.int32))
    out = _combine(dest_blocks, x1, gate_pad, y_rows)
    return out.reshape(batch, seq, D_MODEL)
```
